```python
import math
import jax
import jax.numpy as jnp
from jax import lax

D_MODEL = 1024
BATCH = 32
SEQ = 256
DEPTH = 2
DEC_BATCH = 4
DEC_SEQ = 1024
PAST_LEN = 512

GRID_W = 64
DN_DK = 128
DN_DV = 128
DN_HEADS = (D_MODEL // 2) // DN_DV
DN_WIDTH = DN_HEADS * DN_DV
DN_CHUNK = 64
CONV_W = 3
SC_GROUPS = 4
SC_WIDTH = D_MODEL // 4
FO_GROUPS = 4
FO_GROUP_W = (D_MODEL // 4) // FO_GROUPS
FO_WIDTH = FO_GROUPS * FO_GROUP_W
D_MIX = DN_WIDTH + SC_WIDTH + FO_WIDTH
IN_SPLITS = (3 * DN_WIDTH, DN_WIDTH, DN_HEADS, DN_HEADS, DN_HEADS, DN_HEADS,
             SC_WIDTH, SC_WIDTH, SC_WIDTH, FO_WIDTH)
D_IN = sum(IN_SPLITS)
N_EXPERTS = 16
EC_CAPACITY = 2
D_EXPERT = 1024
N_MOD = 6
RMS_EPS = 1e-6
POS_BASE = 10000.0

kernel_name = 'hybrid_deltanet_conv_fourier_ec_diffusion_step'


def rms_norm(x, w):
    xf = x.astype(jnp.float32)
    y = xf * lax.rsqrt(jnp.mean(xf * xf, axis=-1, keepdims=True) + RMS_EPS)
    return (y * w.astype(jnp.float32)).astype(x.dtype)


def centred_conv(x, w):
    half = CONV_W // 2
    n = x.shape[1]
    xp = jnp.pad(x, ((0, 0), (half, half), (0, 0)))
    return sum(xp[:, j:j + n] * w[j] for j in range(CONV_W))


def l2_normalize(x):
    return x * lax.rsqrt(jnp.sum(x * x, axis=-1, keepdims=True) + 1e-6)


def chunk_gated_delta(q, k, v, g, beta, s0):
    b, h, n, dk = k.shape
    dv = v.shape[-1]
    nc = n // DN_CHUNK
    q = q * (dk ** -0.5)
    rs = lambda t: t.reshape((b, h, nc, DN_CHUNK) + t.shape[3:])
    q, k, v, g, beta = rs(q), rs(k), rs(v), rs(g), rs(beta)
    gc = jnp.cumsum(g, axis=-1)
    idx = jnp.arange(DN_CHUNK)
    causal = idx[:, None] >= idx[None, :]
    strict = idx[:, None] > idx[None, :]
    diff = gc[..., :, None] - gc[..., None, :]
    decay = jnp.where(causal, jnp.exp(jnp.where(causal, diff, 0.0)), 0.0)
    kb = k * beta[..., None]
    lmat = jnp.where(strict, jnp.einsum('bhcik,bhcjk->bhcij', kb, k) * decay, 0.0)
    amat = lmat + jnp.eye(DN_CHUNK, dtype=lmat.dtype)
    rhs = jnp.concatenate([v * beta[..., None], kb * jnp.exp(gc)[..., None]], axis=-1)
    sol = lax.linalg.triangular_solve(amat, rhs, left_side=True, lower=True, unit_diagonal=True)
    u, w = sol[..., :dv], sol[..., dv:]
    qk = jnp.einsum('bhcik,bhcjk->bhcij', q, k) * decay
    q_dec = q * jnp.exp(gc)[..., None]
    k_dec = k * jnp.exp(gc[..., -1:] - gc)[..., None]
    g_last = jnp.exp(gc[..., -1])

    def step(s, xs):
        u_c, w_c, qd_c, qk_c, kd_c, gl_c = xs
        v_new = u_c - jnp.einsum('bhik,bhkv->bhiv', w_c, s)
        o = jnp.einsum('bhik,bhkv->bhiv', qd_c, s) + jnp.einsum('bhij,bhjv->bhiv', qk_c, v_new)
        s = s * gl_c[..., None, None] + jnp.einsum('bhik,bhiv->bhkv', kd_c, v_new)
        return s, o

    xs = tuple(jnp.moveaxis(t, 2, 0) for t in (u, w, q_dec, qk, k_dec, g_last))
    s_fin, o = lax.scan(step, s0, xs)
    o = jnp.moveaxis(o, 0, 2).reshape(b, h, n, dv)
    return o, s_fin


def bidir_gated_deltanet(qkv, z, beta_raw, a_raw, conv_w, a_log, dt_bias, norm_w, s0):
    bsz, n, _ = qkv.shape
    f32 = jnp.float32
    qkv = jax.nn.silu(centred_conv(qkv, conv_w)).astype(f32)
    heads = lambda t: t.reshape(bsz, n, DN_HEADS, -1).transpose(0, 2, 1, 3)
    q, k, v = (heads(t) for t in jnp.split(qkv, 3, axis=-1))
    q, k = l2_normalize(q), l2_normalize(k)
    outs, finals = [], []
    for d in range(2):
        beta = jax.nn.sigmoid(beta_raw[d].astype(f32)).transpose(0, 2, 1)
        g = (-jnp.exp(a_log[d].astype(f32))
             * jax.nn.softplus(a_raw[d].astype(f32) + dt_bias[d].astype(f32))).transpose(0, 2, 1)
        seq = (q, k, v, g, beta)
        if d == 1:
            seq = tuple(jnp.flip(t, axis=2) for t in seq)
        o, s_fin = chunk_gated_delta(*seq, s0[:, d])
        if d == 1:
            o = jnp.flip(o, axis=2)
        outs.append(o)
        finals.append(s_fin)
    o = (outs[0] + outs[1]).transpose(0, 2, 1, 3)
    zf = z.astype(f32).reshape(bsz, n, DN_HEADS, DN_DV)
    o = (o * lax.rsqrt(jnp.mean(o * o, axis=-1, keepdims=True) + RMS_EPS)
         * norm_w.astype(f32) * jax.nn.silu(zf))
    return o.reshape(bsz, n, DN_WIDTH).astype(z.dtype), jnp.stack(finals, axis=1)


def short_gated_conv(b_gate, c_gate, u, conv_w):
    return b_gate * centred_conv(c_gate * u, conv_w)


def fourier_mix(u):
    bsz, n, _ = u.shape
    uf = u.astype(jnp.float32).reshape(bsz, n, FO_GROUPS, FO_GROUP_W)
    f = jnp.fft.fft2(uf, axes=(1, 3), norm='ortho').real
    return f.reshape(bsz, n, FO_WIDTH).astype(u.dtype)


def parallel_mixers(h, s0, w_in, dn_conv_w, dn_a_log, dn_dt_bias, dn_norm_w, sc_conv_w, w_out):
    offsets = []
    acc = 0
    for size in IN_SPLITS[:-1]:
        acc += size
        offsets.append(acc)
    qkv, z, bf, bb, af, ab, sb, sc, su, fu = jnp.split(h @ w_in, offsets, axis=-1)
    dn, s_fin = bidir_gated_deltanet(qkv, z, (bf, bb), (af, ab), dn_conv_w, dn_a_log,
                                     dn_dt_bias, dn_norm_w, s0)
    y = jnp.concatenate([dn, short_gated_conv(sb, sc, su, sc_conv_w), fourier_mix(fu)], axis=-1)
    return y @ w_out, s_fin


def expert_choice_ffn(h, w_router, w_gate, w_up, w_down):
    bsz, n, d = h.shape
    t = h.reshape(bsz * n, d)
    cap = max(1, EC_CAPACITY * t.shape[0] // N_EXPERTS)
    aff = jax.nn.softmax((t @ w_router).astype(jnp.float32), axis=-1)
    top_w, top_i = lax.top_k(aff.T, cap)
    xe = t[top_i]
    hid = jax.nn.silu(jnp.einsum('ecd,edf->ecf', xe, w_gate)) * jnp.einsum('ecd,edf->ecf', xe, w_up)
    ye = jnp.einsum('ecf,efd->ecd', hid, w_down) * top_w[..., None].astype(t.dtype)
    out = jnp.zeros_like(t).at[top_i.reshape(-1)].add(ye.reshape(-1, d))
    return out.reshape(bsz, n, d)


def trunk_layer(x, cond, s0, w_ada, b_ada, norm1_w, norm2_w, w_in, dn_conv_w, dn_a_log, dn_dt_bias,
                dn_norm_w, sc_conv_w, w_out, w_router, w_gate, w_up, w_down):
    mod = (jax.nn.silu(cond) @ w_ada + b_ada)[:, None, :]
    sh1, sc1, g1, sh2, sc2, g2 = jnp.split(mod, N_MOD, axis=-1)
    h = rms_norm(x, norm1_w) * (1 + sc1) + sh1
    mix, s_fin = parallel_mixers(h, s0, w_in, dn_conv_w, dn_a_log, dn_dt_bias, dn_norm_w, sc_conv_w, w_out)
    x = x + g1 * mix
    h = rms_norm(x, norm2_w) * (1 + sc2) + sh2
    x = x + g2 * expert_choice_ffn(h, w_router, w_gate, w_up, w_down)
    return x, s_fin


def grid_pos_embed(n, d, dtype):
    rows = n // GRID_W
    f32 = jnp.float32
    r = jnp.repeat(jnp.arange(rows, dtype=f32), GRID_W)
    col = jnp.tile(jnp.arange(GRID_W, dtype=f32), rows)
    quarter = d // 4
    omega = jnp.power(POS_BASE, -jnp.arange(quarter, dtype=f32) / quarter)
    ra = r[:, None] * omega
    ca = col[:, None] * omega
    return jnp.concatenate([jnp.sin(ra), jnp.cos(ra), jnp.sin(ca), jnp.cos(ca)], axis=-1).astype(dtype)


def setup_inputs(seed: int = 0) -> dict:
    key = jax.random.key(seed)
    ks = jax.random.split(key, 24)
    f32 = jnp.float32
    nrm = lambda k, shape, s: jax.random.normal(k, shape, f32) * s
    dt = jnp.exp(jax.random.uniform(ks[11], (DEPTH, 2, DN_HEADS), f32,
                                    minval=math.log(1e-3), maxval=math.log(1e-1)))
    return {
        'x_prompt': nrm(ks[0], (BATCH, SEQ, D_MODEL), 1.0),
        'x_sample': nrm(ks[1], (DEC_BATCH, DEC_SEQ, D_MODEL), 1.0),
        'state_delta': nrm(ks[2], (DEC_BATCH, DEPTH, 2, DN_HEADS, DN_DK, DN_DV), DN_DK ** -0.5),
        'c': nrm(ks[3], (DEC_BATCH, D_MODEL), 1.0),
        'c_ctx': nrm(ks[4], (D_MODEL,), 1.0),
        'w_ada': nrm(ks[5], (DEPTH, D_MODEL, N_MOD * D_MODEL), 0.5 * D_MODEL ** -0.5),
        'b_ada': nrm(ks[6], (DEPTH, N_MOD * D_MODEL), 0.02),
        'norm1_w': 1.0 + nrm(ks[7], (DEPTH, D_MODEL), 0.02),
        'norm2_w': 1.0 + nrm(ks[8], (DEPTH, D_MODEL), 0.02),
        'w_in': nrm(ks[9], (DEPTH, D_MODEL, D_IN), D_MODEL ** -0.5),
        'dn_conv_w': nrm(ks[10], (DEPTH, CONV_W, 3 * DN_WIDTH), CONV_W ** -0.5),
        'dn_a_log': jnp.log(jax.random.uniform(ks[12], (DEPTH, 2, DN_HEADS), f32, minval=1.0, maxval=16.0)),
        'dn_dt_bias': jnp.log(jnp.expm1(dt)),
        'dn_norm_w': 1.0 + nrm(ks[13], (DEPTH, DN_DV), 0.02),
        'sc_conv_w': nrm(ks[14], (DEPTH, CONV_W, SC_WIDTH), CONV_W ** -0.5),
        'w_out': nrm(ks[15], (DEPTH, D_MIX, D_MODEL), D_MIX ** -0.5),
        'w_router': nrm(ks[16], (DEPTH, D_MODEL, N_EXPERTS), D_MODEL ** -0.5),
        'w_gate': nrm(ks[17], (DEPTH, N_EXPERTS, D_MODEL, D_EXPERT), D_MODEL ** -0.5),
        'w_up': nrm(ks[18], (DEPTH, N_EXPERTS, D_MODEL, D_EXPERT), D_MODEL ** -0.5),
        'w_down': nrm(ks[19], (DEPTH, N_EXPERTS, D_EXPERT, D_MODEL), D_EXPERT ** -0.5),
        'final_norm_w': 1.0 + nrm(ks[20], (D_MODEL,), 0.02),
    }


def reference(x_prompt, x_sample, state_delta, c, c_ctx, w_ada, b_ada, norm1_w, norm2_w, w_in,
              dn_conv_w, dn_a_log, dn_dt_bias, dn_norm_w, sc_conv_w, w_out, w_router, w_gate, w_up,
              w_down, final_norm_w):
    def layer_weights(l):
        return (w_ada[l], b_ada[l], norm1_w[l], norm2_w[l], w_in[l], dn_conv_w[l], dn_a_log[l],
                dn_dt_bias[l], dn_norm_w[l], sc_conv_w[l], w_out[l], w_router[l], w_gate[l], w_up[l],
                w_down[l])

    s_zero = jnp.zeros((x_prompt.shape[0], 2, DN_HEADS, DN_DK, DN_DV), jnp.float32)
    xp = x_prompt
    ctx_states = []
    for l in range(DEPTH):
        xp, s_fin = trunk_layer(xp, c_ctx[None, :], s_zero, *layer_weights(l))
        ctx_states.append(s_fin.astype(x_prompt.dtype))
    y_prompt = rms_norm(xp, final_norm_w)
    new_state_delta = jnp.stack(ctx_states, axis=1)

    xs = x_sample + grid_pos_embed(x_sample.shape[1], x_sample.shape[2], x_sample.dtype)[None]
    for l in range(DEPTH):
        xs, _ = trunk_layer(xs, c, state_delta[:, l].astype(jnp.float32), *layer_weights(l))
    y_sample = rms_norm(xs, final_norm_w)
    return (y_prompt, y_sample, new_state_delta)
```

```python
import functools
import math

import jax
import jax.numpy as jnp
from jax import lax
from jax.experimental import pallas as pl
from jax.experimental.pallas import tpu as pltpu

F32 = jnp.float32
BF16 = jnp.bfloat16
I32 = jnp.int32
HI = lax.Precision.HIGHEST

D = 1024
H = 4
DH = 128
CH = 64
E = 16
NMOD = 6
EPS = 1e-6
GRID_W = 64
POS_BASE = 10000.0
P_COLS = 3072
VMEM_LIMIT = 56 * 1024 * 1024

NN = (((1,), (0,)), ((), ()))
NT = (((1,), (1,)), ((), ()))
TN = (((0,), (0,)), ((), ()))


def _mm(a, b, dims=NN, prec=None):
    return lax.dot_general(a, b, dims, precision=prec, preferred_element_type=F32)


def _mmb(a, b, dims=NN):
    return lax.dot_general(a.astype(BF16), b.astype(BF16), dims, preferred_element_type=F32)


def _silu(x):
    return x / (1.0 + jnp.exp(-x))


def _sigmoid(x):
    return 1.0 / (1.0 + jnp.exp(-x))


def _softplus(x):
    return jnp.maximum(x, 0.0) + jnp.log1p(jnp.exp(-jnp.abs(x)))


def _iota(shape, dim):
    return lax.broadcasted_iota(I32, shape, dim)


def _mod_row(t0, t1, n2):
    return jnp.where(t0 < t1, 0, 1 + jnp.maximum(t0 - t1, 0) // n2)


def _params(sem):
    return pltpu.CompilerParams(dimension_semantics=sem, vmem_limit_bytes=VMEM_LIMIT)


def _ada_kernel(c_ref, w_ref, b_ref, o_ref):
    s = _silu(c_ref[...])
    o_ref[0] = _mm(s, w_ref[0], prec=HI) + b_ref[0]


def _ada(cond8, w_ada, b_ada):
    depth = w_ada.shape[0]
    tn = 1536
    return pl.pallas_call(
        _ada_kernel,
        grid=(depth, NMOD * D // tn),
        in_specs=[pl.BlockSpec((8, D), lambda l, j: (0, 0)),
                  pl.BlockSpec((1, D, tn), lambda l, j: (l, 0, j)),
                  pl.BlockSpec((1, 1, tn), lambda l, j: (l, 0, j))],
        out_specs=pl.BlockSpec((1, 8, tn), lambda l, j: (l, 0, j)),
        out_shape=jax.ShapeDtypeStruct((depth, 8, NMOD * D), F32),
        compiler_params=_params(("arbitrary", "arbitrary")),
        name="ada",
    )(cond8, w_ada, b_ada.reshape(depth, 1, NMOD * D))


def _in_kernel(t1, n2, tm, x_ref, mod_ref, nw_ref, wm_ref, wg_ref, al_ref, dt_ref,
               p_ref, g_ref, gt_ref):
    r = _mod_row(pl.program_id(0) * tm, t1, n2)
    x = x_ref[...]
    y = x * lax.rsqrt(jnp.mean(x * x, axis=-1, keepdims=True) + EPS) * nw_ref[...]
    sh = mod_ref[pl.ds(r, 1), 0:D]
    sc = mod_ref[pl.ds(r, 1), D:2 * D]
    hb = (y * (1.0 + sc) + sh).astype(BF16)
    nsplit = 4
    wc = P_COLS // nsplit
    for j in range(nsplit):
        p_ref[:, j * wc:(j + 1) * wc] = _mm(hb, wm_ref[:, j * wc:(j + 1) * wc])
    raw = _mm(hb, wg_ref[...])
    lane = _iota((1, 128), 1)
    g = -jnp.exp(al_ref[...]) * _softplus(raw + dt_ref[...])
    act = jnp.where(lane < 2 * H, _sigmoid(raw), g)
    g_ref[...] = act[:, 0:4 * H]
    act_t = act.T
    for j in range(tm // CH):
        gt_ref[j] = act_t[0:4 * H, j * CH:(j + 1) * CH]


def _in_proj(x, mod, nw, wm, wg, al, dt, t1, n2, tm):
    t_all = x.shape[0]
    return pl.pallas_call(
        functools.partial(_in_kernel, t1, n2, tm),
        grid=(t_all // tm,),
        in_specs=[pl.BlockSpec((tm, D), lambda i: (i, 0)),
                  pl.BlockSpec((8, NMOD * D), lambda i: (0, 0)),
                  pl.BlockSpec((1, D), lambda i: (0, 0)),
                  pl.BlockSpec((D, P_COLS), lambda i: (0, 0)),
                  pl.BlockSpec((D, 128), lambda i: (0, 0)),
                  pl.BlockSpec((1, 128), lambda i: (0, 0)),
                  pl.BlockSpec((1, 128), lambda i: (0, 0))],
        out_specs=[pl.BlockSpec((tm, P_COLS), lambda i: (i, 0)),
                   pl.BlockSpec((tm, 4 * H), lambda i: (i, 0)),
                   pl.BlockSpec((tm // CH, 4 * H, CH), lambda i: (i, 0, 0))],
        out_shape=[jax.ShapeDtypeStruct((t_all, P_COLS), F32),
                   jax.ShapeDtypeStruct((t_all, 4 * H), F32),
                   jax.ShapeDtypeStruct((t_all // CH, 4 * H, CH), F32)],
        compiler_params=_params(("arbitrary",)),
        name="in_proj",
    )(x, mod, nw, wm, wg, al, dt)


def _shift_conv(x, w_ref, c0, c1):
    n = x.shape[0]
    row = _iota((n, 1), 0)
    prev = jnp.where(row == 0, 0.0, pltpu.roll(x, 1, 0))
    nxt = jnp.where(row == n - 1, 0.0, pltpu.roll(x, n - 1, 0))
    return prev * w_ref[0:1, c0:c1] + x * w_ref[1:2, c0:c1] + nxt * w_ref[2:3, c0:c1]


def _dn_kernel(n, q_ref, k_ref, v_ref, z_ref, g_ref, gt_ref, cw_ref, nw_ref, s0_ref,
               o_ref, sfin_ref, qs, ks, vs, gcc_s, gct_s, s_s, oacc):
    nc = n // CH
    width = H * DH

    for hh in range(H):
        c0, c1 = hh * DH, (hh + 1) * DH
        q = _silu(_shift_conv(q_ref[:, c0:c1], cw_ref, c0, c1))
        q = q * lax.rsqrt(jnp.sum(q * q, axis=-1, keepdims=True) + 1e-6)
        qs[:, c0:c1] = q * (DH ** -0.5)
        k = _silu(_shift_conv(k_ref[:, c0:c1], cw_ref, width + c0, width + c1))
        ks[:, c0:c1] = k * lax.rsqrt(jnp.sum(k * k, axis=-1, keepdims=True) + 1e-6)
        vs[:, c0:c1] = _silu(_shift_conv(v_ref[:, c0:c1], cw_ref, 2 * width + c0, 2 * width + c1))

    ii = _iota((CH, CH), 0)
    jj = _iota((CH, CH), 1)
    x = gt_ref[...].reshape(nc * 4 * H, CH)
    cf = _mm(x, (ii <= jj).astype(F32), prec=HI)
    cb = _mm(x, (ii >= jj).astype(F32), prec=HI)
    rsel = (_iota((nc * 4 * H, 1), 0) & (4 * H - 1)) < 3 * H
    gct_s[...] = jnp.where(rsel, cf, cb).reshape(nc, 4 * H, CH)
    i2 = _iota((2 * CH, 2 * CH), 0)
    j2 = _iota((2 * CH, 2 * CH), 1)
    same = (i2 >= CH) == (j2 >= CH)
    lf = (same & (i2 >= j2)).astype(F32)
    lb = (same & (i2 <= j2)).astype(F32)
    lsel = _iota((1, 4 * H), 1) < 3 * H
    for m in range(n // (2 * CH)):
        y = g_ref[m * 2 * CH:(m + 1) * 2 * CH, :]
        gcc_s[m * 2 * CH:(m + 1) * 2 * CH, :] = jnp.where(lsel, _mm(lf, y, prec=HI), _mm(lb, y, prec=HI))

    for hh in range(H):
        for d in range(2):
            s_s[hh * 2 + d] = s0_ref[0, d, hh]

    eye = (ii == jj).astype(F32)
    blk_same = [(ii >> s) == (jj >> s) for s in range(1, 7)]

    def chunk_step(c, carry):
        for hh in range(H):
            for d in range(2):
                cidx = c if d == 0 else nc - 1 - c
                r0 = pl.multiple_of(cidx * CH, CH)
                c0, c1 = hh * DH, (hh + 1) * DH
                gcol = 2 * H + H * d + hh
                bcol = H * d + hh
                incl = (ii >= jj) if d == 0 else (ii <= jj)
                strict = (ii > jj) if d == 0 else (ii < jj)
                qc = qs[pl.ds(r0, CH), c0:c1]
                kc = ks[pl.ds(r0, CH), c0:c1]
                vc = vs[pl.ds(r0, CH), c0:c1]
                beta = g_ref[pl.ds(r0, CH), bcol:bcol + 1]
                gc = gcc_s[pl.ds(r0, CH), gcol:gcol + 1]
                gr = gct_s[cidx, gcol:gcol + 1, :]
                tot = gr[:, CH - 1:CH] if d == 0 else gr[:, 0:1]
                decay = jnp.where(incl, jnp.exp(jnp.where(incl, gc - gr, 0.0)), 0.0)
                kb = kc * beta
                lmat = jnp.where(strict, _mmb(kb, kc, NT) * decay, 0.0)
                tinv = eye - jnp.where(blk_same[0], lmat, 0.0)
                for lvl in range(1, len(blk_same)):
                    off = jnp.where(blk_same[lvl] & ~blk_same[lvl - 1], lmat, 0.0)
                    tinv = tinv - _mm(tinv, _mm(off, tinv, prec=HI), prec=HI)
                egc = jnp.exp(gc)
                rhs = jnp.concatenate([vc * beta, kb * egc], axis=1)
                sol = _mm(tinv, rhs, prec=HI)
                u = sol[:, 0:DH]
                w = sol[:, DH:2 * DH]
                qk = _mmb(qc, kc, NT) * decay
                s = s_s[hh * 2 + d]
                sb = s.astype(BF16)
                v_new = u - _mmb(w, sb)
                o = _mmb(qc * egc, sb) + _mmb(qk, v_new)
                s_s[hh * 2 + d] = s * jnp.exp(tot) + _mmb(kc * jnp.exp(tot - gc), v_new, TN)
                oacc[d, pl.ds(r0, CH), c0:c1] = o
        return carry

    lax.fori_loop(0, nc, chunk_step, 0)

    for hh in range(H):
        c0, c1 = hh * DH, (hh + 1) * DH
        o = oacc[0, :, c0:c1] + oacc[1, :, c0:c1]
        o = o * lax.rsqrt(jnp.mean(o * o, axis=-1, keepdims=True) + EPS)
        o_ref[:, c0:c1] = o * nw_ref[...] * _silu(z_ref[:, c0:c1])
        for d in range(2):
            sfin_ref[0, d, hh] = s_s[hh * 2 + d]


def _deltanet(p, g, gt, conv_w, norm_w, s0, n, row_blk0):
    bsz = s0.shape[0]
    nc = n // CH
    width = H * DH
    return pl.pallas_call(
        functools.partial(_dn_kernel, n),
        grid=(bsz,),
        in_specs=[pl.BlockSpec((n, width), lambda b: (row_blk0 + b, 0)),
                  pl.BlockSpec((n, width), lambda b: (row_blk0 + b, 1)),
                  pl.BlockSpec((n, width), lambda b: (row_blk0 + b, 2)),
                  pl.BlockSpec((n, width), lambda b: (row_blk0 + b, 3)),
                  pl.BlockSpec((n, 4 * H), lambda b: (row_blk0 + b, 0)),
                  pl.BlockSpec((nc, 4 * H, CH), lambda b: (row_blk0 + b, 0, 0)),
                  pl.BlockSpec((3, 3 * width), lambda b: (0, 0)),
                  pl.BlockSpec((1, DH), lambda b: (0, 0)),
                  pl.BlockSpec((1, 2, H, DH, DH), lambda b: (b, 0, 0, 0, 0))],
        out_specs=[pl.BlockSpec((n, width), lambda b: (b, 0)),
                   pl.BlockSpec((1, 2, H, DH, DH), lambda b: (b, 0, 0, 0, 0))],
        out_shape=[jax.ShapeDtypeStruct((bsz * n, width), F32),
                   jax.ShapeDtypeStruct((bsz, 2, H, DH, DH), F32)],
        scratch_shapes=[pltpu.VMEM((n, width), F32), pltpu.VMEM((n, width), F32),
                        pltpu.VMEM((n, width), F32), pltpu.VMEM((n, 4 * H), F32),
                        pltpu.VMEM((nc, 4 * H, CH), F32), pltpu.VMEM((2 * H, DH, DH), F32),
                        pltpu.VMEM((2, n, width), F32)],
        compiler_params=_params(("arbitrary",)),
        name="deltanet_n%d" % n,
    )(p, p, p, p, g, gt, conv_w, norm_w, s0)


def _mix2_kernel(sb_ref, sc_ref, su_ref, fu_ref, cw_ref, bdc_ref, bds_ref, cs_ref, o_ref):
    w = sb_ref.shape[1]
    cu = sc_ref[...] * su_ref[...]
    o_ref[:, 0:w] = sb_ref[...] * _shift_conv(cu, cw_ref, 0, w)
    fu = fu_ref[...]
    a = _mm(fu, bdc_ref[...], prec=HI)
    b = _mm(fu, bds_ref[...], prec=HI)
    o_ref[:, w:2 * w] = _mm(cs_ref[...], jnp.concatenate([a, b], axis=0), prec=HI)


def _mix2(p, conv_w, bdc, bds, cs, bsz, n, row_blk0):
    w = 256
    return pl.pallas_call(
        _mix2_kernel,
        grid=(bsz,),
        in_specs=[pl.BlockSpec((n, w), lambda b: (row_blk0 + b, 8)),
                  pl.BlockSpec((n, w), lambda b: (row_blk0 + b, 9)),
                  pl.BlockSpec((n, w), lambda b: (row_blk0 + b, 10)),
                  pl.BlockSpec((n, w), lambda b: (row_blk0 + b, 11)),
                  pl.BlockSpec((3, w), lambda b: (0, 0)),
                  pl.BlockSpec((w, w), lambda b: (0, 0)),
                  pl.BlockSpec((w, w), lambda b: (0, 0)),
                  pl.BlockSpec((n, 2 * n), lambda b: (0, 0))],
        out_specs=pl.BlockSpec((n, 2 * w), lambda b: (b, 0)),
        out_shape=jax.ShapeDtypeStruct((bsz * n, 2 * w), F32),
        compiler_params=_params(("arbitrary",)),
        name="conv_fourier_n%d" % n,
    )(p, p, p, p, conv_w, bdc, bds, cs)


def _dft_tables(n):
    gw = 64
    k = jnp.arange(n, dtype=I32)
    ph = (k[:, None] * k[None, :]) % n
    ang = ph.astype(F32) * (2.0 * math.pi / n)
    scale = 1.0 / math.sqrt(n * gw)
    cs = jnp.concatenate([jnp.cos(ang), -jnp.sin(ang)], axis=1) * scale
    c = jnp.arange(256, dtype=I32)
    phc = ((c[:, None] % gw) * (c[None, :] % gw)) % gw
    angc = phc.astype(F32) * (2.0 * math.pi / gw)
    same = (c[:, None] // gw) == (c[None, :] // gw)
    bdc = jnp.where(same, jnp.cos(angc), 0.0)
    bds = jnp.where(same, jnp.sin(angc), 0.0)
    return cs, bdc, bds


def _out_kernel(t1, n2, tm, dn1_ref, dn2_ref, mx1_ref, mx2_ref, x_ref, mod_ref, nw_ref, wo_ref, wr_ref,
                x1_ref, h2_ref, aff_ref):
    t0 = pl.program_id(0) * tm
    r = _mod_row(t0, t1, n2)
    half = D // 2
    dn = jnp.where(t0 < t1, dn1_ref[...], dn2_ref[...])
    mx = jnp.where(t0 < t1, mx1_ref[...], mx2_ref[...])
    mix = _mmb(dn, wo_ref[0:half, :]) + _mmb(mx, wo_ref[half:D, :])
    x1 = x_ref[...] + mod_ref[pl.ds(r, 1), 2 * D:3 * D] * mix
    x1_ref[...] = x1
    y = x1 * lax.rsqrt(jnp.mean(x1 * x1, axis=-1, keepdims=True) + EPS) * nw_ref[...]
    h2 = y * (1.0 + mod_ref[pl.ds(r, 1), 4 * D:5 * D]) + mod_ref[pl.ds(r, 1), 3 * D:4 * D]
    h2_ref[...] = h2
    logits = _mm(h2, wr_ref[...], prec=HI)
    lt = logits.T[0:E, :]
    ex = jnp.exp(lt - jnp.max(lt, axis=0, keepdims=True))
    aff_ref[...] = ex / jnp.sum(ex, axis=0, keepdims=True)


def _out_proj(dn1, dn2, mx1, mx2, x, mod, nw, wo, wr, t1, n2, tm):
    t_all = x.shape[0]
    nt1 = t1 // tm
    spec1 = pl.BlockSpec((tm, D // 2), lambda i: (jnp.minimum(i, nt1 - 1), 0))
    spec2 = pl.BlockSpec((tm, D // 2), lambda i: (jnp.maximum(i - nt1, 0), 0))
    return pl.pallas_call(
        functools.partial(_out_kernel, t1, n2, tm),
        grid=(t_all // tm,),
        in_specs=[spec1, spec2, spec1, spec2,
                  pl.BlockSpec((tm, D), lambda i: (i, 0)),
                  pl.BlockSpec((8, NMOD * D), lambda i: (0, 0)),
                  pl.BlockSpec((1, D), lambda i: (0, 0)),
                  pl.BlockSpec((D, D), lambda i: (0, 0)),
                  pl.BlockSpec((D, 128), lambda i: (0, 0))],
        out_specs=[pl.BlockSpec((tm, D), lambda i: (i, 0)),
                   pl.BlockSpec((tm, D), lambda i: (i, 0)),
                   pl.BlockSpec((E, tm), lambda i: (0, i))],
        out_shape=[jax.ShapeDtypeStruct((t_all, D), F32),
                   jax.ShapeDtypeStruct((t_all, D), F32),
                   jax.ShapeDtypeStruct((E, t_all), F32)],
        compiler_params=_params(("arbitrary",)),
        name="out_proj",
    )(dn1, dn2, mx1, mx2, x, mod, nw, wo, wr)


SEL_TB = 256
N_SLOT_ROWS = 16


def _sel_kernel(t, cap, aff_ref, slots_ref, lo_ref, hi_ref, a3, c3):
    ntb = t // SEL_TB
    aff = aff_ref[...]

    def search(i, thr):
        cand = thr | jnp.left_shift(jnp.int32(1), 30 - i)
        cnt = jnp.sum((aff >= pltpu.bitcast(cand, F32)).astype(F32), axis=1, keepdims=True)
        return jnp.where(cnt >= cap, cand, thr)

    thr = lax.fori_loop(0, 31, search, jnp.zeros((E, 1), I32))
    gt = (aff >= pltpu.bitcast(thr + 1, F32)).astype(F32)
    eq = (aff >= pltpu.bitcast(thr, F32)).astype(F32) - gt
    need = cap - jnp.sum(gt, axis=1, keepdims=True)

    ui = _iota((SEL_TB, SEL_TB), 0)
    uj = _iota((SEL_TB, SEL_TB), 1)
    upper = (ui < uj).astype(BF16)

    def excl_cumsum(rows, dst, r0, r1):
        carry = jnp.zeros((rows.shape[0], 1), F32)
        for j in range(ntb):
            blk = rows[:, j * SEL_TB:(j + 1) * SEL_TB]
            dst[j, r0:r1, :] = _mm(blk.astype(BF16), upper) + carry
            carry = carry + jnp.sum(blk, axis=1, keepdims=True)

    excl_cumsum(eq, c3, 0, E)
    rank_eq = jnp.concatenate([c3[j, 0:E, :] for j in range(ntb)], axis=1)
    sel = jnp.maximum(gt, jnp.where(rank_eq < need, eq, 0.0))
    n_tok = jnp.sum(sel, axis=0, keepdims=True)
    excl_cumsum(jnp.concatenate([sel, jnp.broadcast_to(n_tok, (8, t))], axis=0), c3, 0, E + 8)
    ei = _iota((E, E), 0)
    ej = _iota((E, E), 1)
    below = _mm((ej < ei).astype(BF16), sel.astype(BF16))
    for j in range(ntb):
        sl = slice(j * SEL_TB, (j + 1) * SEL_TB)
        off = c3[j, E:E + 1, :]
        lo_ref[:, sl] = off
        hi_ref[:, sl] = off + n_tok[:, sl]
        rank = off + below[:, sl]
        a3[j, 0] = sel[:, sl]
        a3[j, 1] = aff[:, sl]
        a3[j, 2] = rank

    slot_iota = _iota((cap, SEL_TB), 0).astype(F32)
    tok_iota = _iota((1, SEL_TB), 1)

    def per_expert(e, carry):
        def per_block(j, acc):
            pos = c3[j, pl.ds(e, 1), :]
            chosen = a3[j, 0, pl.ds(e, 1), :]
            w = a3[j, 1, pl.ds(e, 1), :]
            rank = a3[j, 2, pl.ds(e, 1), :].astype(I32)
            tok = tok_iota + j * SEL_TB
            w_hi = w.astype(BF16).astype(F32)
            w_mid = (w - w_hi).astype(BF16).astype(F32)
            w_lo = w - w_hi - w_mid
            vals = jnp.concatenate(
                [(tok >> 7).astype(F32), (tok & 127).astype(F32), w_hi, w_mid, w_lo,
                 (rank >> 7).astype(F32), (rank & 127).astype(F32),
                 jnp.zeros((N_SLOT_ROWS - 7, SEL_TB), F32)], axis=0)
            onehot = jnp.where((slot_iota == pos) & (chosen > 0.0), 1.0, 0.0)
            return acc + _mmb(vals, onehot, NT)
        slots_ref[e] = lax.fori_loop(0, ntb, per_block, jnp.zeros((N_SLOT_ROWS, cap), F32))
        return carry

    lax.fori_loop(0, E, per_expert, 0)


def _select(aff_t, t, cap, col_blk):
    ntb = t // SEL_TB
    return pl.pallas_call(
        functools.partial(_sel_kernel, t, cap),
        grid=(1,),
        in_specs=[pl.BlockSpec((E, t), lambda i: (0, col_blk))],
        out_specs=[pl.BlockSpec((E, N_SLOT_ROWS, cap), lambda i: (0, 0, 0)),
                   pl.BlockSpec((1, t), lambda i: (0, 0)),
                   pl.BlockSpec((1, t), lambda i: (0, 0))],
        out_shape=[jax.ShapeDtypeStruct((E, N_SLOT_ROWS, cap), F32),
                   jax.ShapeDtypeStruct((1, t), F32),
                   jax.ShapeDtypeStruct((1, t), F32)],
        scratch_shapes=[pltpu.VMEM((ntb, 3, E, SEL_TB), F32),
                        pltpu.VMEM((ntb, E + 8, SEL_TB), F32)],
        compiler_params=_params(("arbitrary",)),
        name="select_t%d" % t,
    )(aff_t)


def _ffn_kernel(rc, nch, idx_ref, dst_ref, h2_hbm, wcol_ref, wg_ref, wu_ref, wd_ref, z_hbm,
                xbuf, ybuf, wgb, wub, wdb, gsem, ssem):
    e = pl.program_id(0)
    c = pl.program_id(1)
    step = e * nch + c
    base = step * rc

    def gather(j, carry):
        tok = idx_ref[base + j]
        pltpu.make_async_copy(h2_hbm.at[pl.ds(tok, 1)], xbuf.at[pl.ds(j, 1)], gsem).start()
        return carry

    lax.fori_loop(0, rc, gather, 0)

    @pl.when(c == 0)
    def _():
        wgb[...] = wg_ref[0].astype(BF16)
        wub[...] = wu_ref[0].astype(BF16)
        wdb[...] = wd_ref[0].astype(BF16)

    pltpu.make_async_copy(h2_hbm.at[pl.ds(0, rc)], xbuf, gsem).wait()
    xb = xbuf[...].astype(BF16)
    hid = _silu(_mm(xb, wgb[...])) * _mm(xb, wub[...])
    y = _mm(hid.astype(BF16), wdb[...]) * wcol_ref[...]

    @pl.when(step > 0)
    def _():
        pltpu.make_async_copy(ybuf, z_hbm.at[pl.ds(0, rc)], ssem).wait()

    ybuf[...] = y

    def scatter(j, carry):
        row = dst_ref[base + j]
        pltpu.make_async_copy(ybuf.at[pl.ds(j, 1)], z_hbm.at[pl.ds(row, 1)], ssem).start()
        return carry

    lax.fori_loop(0, rc, scatter, 0)

    @pl.when(step == E * nch - 1)
    def _():
        pltpu.make_async_copy(ybuf, z_hbm.at[pl.ds(0, rc)], ssem).wait()


def _expert_ffn(idx, dst, h2, wcol, w_gate, w_up, w_down, rc, nch):
    zrows = E * nch * rc
    wspec = pl.BlockSpec((1, D, D), lambda e, c, *_: (e, 0, 0))
    return pl.pallas_call(
        functools.partial(_ffn_kernel, rc, nch),
        grid_spec=pltpu.PrefetchScalarGridSpec(
            num_scalar_prefetch=2,
            grid=(E, nch),
            in_specs=[pl.BlockSpec(memory_space=pl.ANY),
                      pl.BlockSpec((rc, 1), lambda e, c, *_: (e * nch + c, 0)),
                      wspec, wspec, wspec],
            out_specs=pl.BlockSpec(memory_space=pl.ANY),
            scratch_shapes=[pltpu.VMEM((rc, D), F32), pltpu.VMEM((rc, D), F32),
                            pltpu.VMEM((D, D), BF16), pltpu.VMEM((D, D), BF16),
                            pltpu.VMEM((D, D), BF16),
                            pltpu.SemaphoreType.DMA(()), pltpu.SemaphoreType.DMA(())]),
        out_shape=jax.ShapeDtypeStruct((zrows, D), F32),
        compiler_params=_params(("arbitrary", "arbitrary")),
        name="expert_ffn",
    )(idx, dst, h2, wcol, w_gate, w_up, w_down)


CMB_TB = 256
CMB_RC = 512


def _comb_kernel(t1, n2, zrows, final, bs_ref, be_ref, x1_ref, lo_ref, hi_ref, mod_ref, fw_ref, z_hbm,
                 o_ref, zbuf, sem):
    i = pl.program_id(0)
    r = _mod_row(i * CMB_TB, t1, n2)
    start = bs_ref[i]
    end = be_ref[i]
    base0 = (start // 8) * 8
    nchunk = (end - base0 + CMB_RC - 1) // CMB_RC
    eye = _iota((CMB_TB, CMB_TB), 0) == _iota((CMB_TB, CMB_TB), 1)
    lo = jnp.sum(jnp.where(eye, lo_ref[...], 0.0), axis=1, keepdims=True)
    hi = jnp.sum(jnp.where(eye, hi_ref[...], 0.0), axis=1, keepdims=True)
    col = _iota((1, CMB_RC), 1)

    def chunk(ci, acc):
        nominal = base0 + ci * CMB_RC
        b = pl.multiple_of(jnp.minimum(nominal, zrows - CMB_RC), 8)
        cp = pltpu.make_async_copy(z_hbm.at[pl.ds(b, CMB_RC)], zbuf, sem)
        cp.start()
        cp.wait()
        rows = (col + b).astype(F32)
        pick = (rows >= lo) & (rows < hi) & (rows >= nominal.astype(F32))
        s = jnp.where(pick, 1.0, 0.0).astype(BF16)
        z = zbuf[...]
        z_hi = z.astype(BF16)
        z_lo = (z - z_hi.astype(F32)).astype(BF16)
        return acc + _mm(s, z_hi) + _mm(s, z_lo)

    moe = lax.fori_loop(0, nchunk, chunk, jnp.zeros((CMB_TB, D), F32))
    x2 = x1_ref[...] + mod_ref[pl.ds(r, 1), 5 * D:6 * D] * moe
    if final:
        x2 = x2 * lax.rsqrt(jnp.mean(x2 * x2, axis=-1, keepdims=True) + EPS) * fw_ref[...]
    o_ref[...] = x2


def _combine(bstart, bend, x1, lo, hi, mod, fw, z, t1, n2, final):
    t_all = x1.shape[0]
    zrows = z.shape[0]
    return pl.pallas_call(
        functools.partial(_comb_kernel, t1, n2, zrows, final),
        grid_spec=pltpu.PrefetchScalarGridSpec(
            num_scalar_prefetch=2,
            grid=(t_all // CMB_TB,),
            in_specs=[pl.BlockSpec((CMB_TB, D), lambda i, *_: (i, 0)),
                      pl.BlockSpec((1, CMB_TB), lambda i, *_: (0, i)),
                      pl.BlockSpec((1, CMB_TB), lambda i, *_: (0, i)),
                      pl.BlockSpec((8, NMOD * D), lambda i, *_: (0, 0)),
                      pl.BlockSpec((1, D), lambda i, *_: (0, 0)),
                      pl.BlockSpec(memory_space=pl.ANY)],
            out_specs=pl.BlockSpec((CMB_TB, D), lambda i, *_: (i, 0)),
            scratch_shapes=[pltpu.VMEM((CMB_RC, D), F32), pltpu.SemaphoreType.DMA(())]),
        out_shape=jax.ShapeDtypeStruct((t_all, D), F32),
        compiler_params=_params(("arbitrary",)),
        name="combine",
    )(bstart, bend, x1, lo, hi, mod, fw, z)


def _grid_pos_embed(n, d):
    rows = n // GRID_W
    r = jnp.repeat(jnp.arange(rows, dtype=F32), GRID_W)
    col = jnp.tile(jnp.arange(GRID_W, dtype=F32), rows)
    quarter = d // 4
    omega = jnp.power(POS_BASE, -jnp.arange(quarter, dtype=F32) / quarter)
    ra = r[:, None] * omega
    ca = col[:, None] * omega
    return jnp.concatenate([jnp.sin(ra), jnp.cos(ra), jnp.sin(ca), jnp.cos(ca)], axis=-1)


def _decode_slots(slots):
    idx = (slots[:, 0] * 128.0 + slots[:, 1]).astype(I32)
    w = slots[:, 2] + slots[:, 3] + slots[:, 4]
    rank = (slots[:, 5] * 128.0 + slots[:, 6]).astype(I32)
    return idx, w, rank


def kernel(x_prompt, x_sample, state_delta, c, c_ctx, w_ada, b_ada, norm1_w, norm2_w, w_in, dn_conv_w,
           dn_a_log, dn_dt_bias, dn_norm_w, sc_conv_w, w_out, w_router, w_gate, w_up, w_down,
           final_norm_w):
    b1, n1, _ = x_prompt.shape
    b2, n2, _ = x_sample.shape
    depth = w_ada.shape[0]
    t1, t2 = b1 * n1, b2 * n2
    t_all = t1 + t2
    cap1 = max(1, 2 * t1 // E)
    cap2 = max(1, 2 * t2 // E)
    tm = 512
    rc = 512 if (cap1 % 512 == 0 and cap2 % 512 == 0) else 128
    assert t1 % n2 == 0 and t1 % tm == 0 and n2 % tm == 0 and n1 % (2 * CH) == 0
    assert t1 % SEL_TB == 0 and t2 % SEL_TB == 0 and t1 % t2 == 0
    assert cap1 % rc == 0 and cap2 % rc == 0 and 2 * t_all >= CMB_RC
    nch = (cap1 + cap2) // rc
    width = H * DH

    cond8 = jnp.zeros((8, D), F32).at[0].set(c_ctx).at[1:1 + b2].set(c)
    mod = _ada(cond8, w_ada, b_ada)

    x = jnp.concatenate([x_prompt.reshape(t1, D),
                         (x_sample + _grid_pos_embed(n2, D)[None]).reshape(t2, D)], axis=0)
    s_zero = jnp.zeros((b1, 2, H, DH, DH), F32)
    tabs1 = _dft_tables(n1)
    tabs2 = _dft_tables(n2)
    ctx_states = []

    for l in range(depth):
        wl = w_in[l]
        wm = jnp.concatenate([wl[:, 0:4 * width], wl[:, 4 * width + 4 * H:]], axis=1).astype(BF16)
        wg = jnp.pad(wl[:, 4 * width:4 * width + 4 * H], ((0, 0), (0, 128 - 4 * H))).astype(BF16)
        al = jnp.pad(dn_a_log[l].reshape(1, 2 * H), ((0, 0), (2 * H, 128 - 4 * H)))
        dt = jnp.pad(dn_dt_bias[l].reshape(1, 2 * H), ((0, 0), (2 * H, 128 - 4 * H)))
        p, g, gt = _in_proj(x, mod[l], norm1_w[l].reshape(1, D), wm, wg, al, dt, t1, n2, tm)

        nwd = dn_norm_w[l].reshape(1, DH)
        dn1, s_ctx = _deltanet(p, g, gt, dn_conv_w[l], nwd, s_zero, n1, 0)
        dn2, _ = _deltanet(p, g, gt, dn_conv_w[l], nwd, state_delta[:, l], n2, t1 // n2)
        ctx_states.append(s_ctx)
        mx1 = _mix2(p, sc_conv_w[l], tabs1[1], tabs1[2], tabs1[0], b1, n1, 0)
        mx2 = _mix2(p, sc_conv_w[l], tabs2[1], tabs2[2], tabs2[0], b2, n2, t1 // n2)

        wr = jnp.pad(w_router[l], ((0, 0), (0, 128 - E)))
        x1, h2, aff_t = _out_proj(dn1, dn2, mx1, mx2, x, mod[l], norm2_w[l].reshape(1, D),
                                  w_out[l].astype(BF16), wr, t1, n2, tm)

        slots1, lo1, hi1 = _select(aff_t, t1, cap1, 0)
        slots2, lo2, hi2 = _select(aff_t, t2, cap2, t1 // t2)
        idx1, wsel1, rank1 = _decode_slots(slots1)
        idx2, wsel2, rank2 = _decode_slots(slots2)
        idx = jnp.concatenate([idx1, idx2 + t1], axis=1).reshape(-1)
        dst = jnp.concatenate([rank1, rank2 + 2 * t1], axis=1).reshape(-1)
        wcol = jnp.concatenate([wsel1, wsel2], axis=1).reshape(-1, 1)
        z = _expert_ffn(idx, dst, h2, wcol, w_gate[l], w_up[l], w_down[l], rc, nch)

        lo = jnp.concatenate([lo1, lo2 + 2.0 * t1], axis=1)
        hi = jnp.concatenate([hi1, hi2 + 2.0 * t1], axis=1)
        bstart = lo[0, ::CMB_TB].astype(I32)
        bend = hi[0, CMB_TB - 1::CMB_TB].astype(I32)
        x = _combine(bstart, bend, x1, lo, hi, mod[l], final_norm_w.reshape(1, D), z, t1, n2,
                     l == depth - 1)

    y_prompt = x[:t1].reshape(b1, n1, D)
    y_sample = x[t1:].reshape(b2, n2, D)
    return y_prompt, y_sample, jnp.stack(ctx_states, axis=1)
```

```python
import functools
import math

import jax
import jax.numpy as jnp
from jax import lax
from jax.experimental import pallas as pl
from jax.experimental.pallas import tpu as pltpu

F32 = jnp.float32
BF16 = jnp.bfloat16
I32 = jnp.int32
HI = lax.Precision.HIGHEST

D = 1024
H = 4
DH = 128
CH = 64
E = 16
NMOD = 6
EPS = 1e-6
GRID_W = 64
POS_BASE = 10000.0
P_COLS = 3072
VMEM_LIMIT = 56 * 1024 * 1024

NN = (((1,), (0,)), ((), ()))
NT = (((1,), (1,)), ((), ()))
TN = (((0,), (0,)), ((), ()))


def _mm(a, b, dims=NN, prec=None):
    return lax.dot_general(a, b, dims, precision=prec, preferred_element_type=F32)


def _mmb(a, b, dims=NN):
    return lax.dot_general(a.astype(BF16), b.astype(BF16), dims, preferred_element_type=F32)


def _mm_inv(a, b):
    return _mmb(a, b)


def _silu(x):
    return x / (1.0 + jnp.exp(-x))


def _sigmoid(x):
    return 1.0 / (1.0 + jnp.exp(-x))


def _softplus(x):
    return jnp.maximum(x, 0.0) + jnp.log1p(jnp.exp(-jnp.abs(x)))


def _iota(shape, dim):
    return lax.broadcasted_iota(I32, shape, dim)


def _mod_row(t0, t1, n2):
    return jnp.where(t0 < t1, 0, 1 + jnp.maximum(t0 - t1, 0) // n2)


def _params(sem):
    return pltpu.CompilerParams(dimension_semantics=sem, vmem_limit_bytes=VMEM_LIMIT)


def _ada_kernel(c_ref, w_ref, b_ref, o_ref):
    s = _silu(c_ref[...])
    o_ref[0] = _mm(s, w_ref[0], prec=HI) + b_ref[0]


def _ada(cond8, w_ada, b_ada):
    depth = w_ada.shape[0]
    tn = 1536
    return pl.pallas_call(
        _ada_kernel,
        grid=(depth, NMOD * D // tn),
        in_specs=[pl.BlockSpec((8, D), lambda l, j: (0, 0)),
                  pl.BlockSpec((1, D, tn), lambda l, j: (l, 0, j)),
                  pl.BlockSpec((1, 1, tn), lambda l, j: (l, 0, j))],
        out_specs=pl.BlockSpec((1, 8, tn), lambda l, j: (l, 0, j)),
        out_shape=jax.ShapeDtypeStruct((depth, 8, NMOD * D), F32),
        compiler_params=_params(("arbitrary", "arbitrary")),
        name="ada",
    )(cond8, w_ada, b_ada.reshape(depth, 1, NMOD * D))


def _in_kernel(t1, n2, tm, x_ref, mod_ref, nw_ref, wm_ref, wg_ref, al_ref, dt_ref,
               p_ref, g_ref, gt_ref):
    r = _mod_row(pl.program_id(0) * tm, t1, n2)
    x = x_ref[...]
    y = x * lax.rsqrt(jnp.mean(x * x, axis=-1, keepdims=True) + EPS) * nw_ref[...]
    sh = mod_ref[pl.ds(r, 1), 0:D]
    sc = mod_ref[pl.ds(r, 1), D:2 * D]
    hb = (y * (1.0 + sc) + sh).astype(BF16)
    nsplit = 4
    wc = P_COLS // nsplit
    for j in range(nsplit):
        p_ref[:, j * wc:(j + 1) * wc] = _mm(hb, wm_ref[:, j * wc:(j + 1) * wc])
    raw = _mm(hb, wg_ref[...])
    lane = _iota((1, 128), 1)
    g = -jnp.exp(al_ref[...]) * _softplus(raw + dt_ref[...])
    act = jnp.where(lane < 2 * H, _sigmoid(raw), g)
    g_ref[...] = act[:, 0:4 * H]
    act_t = act.T
    for j in range(tm // CH):
        gt_ref[j] = act_t[0:4 * H, j * CH:(j + 1) * CH]


def _in_proj(x, mod, nw, wm, wg, al, dt, t1, n2, tm):
    t_all = x.shape[0]
    return pl.pallas_call(
        functools.partial(_in_kernel, t1, n2, tm),
        grid=(t_all // tm,),
        in_specs=[pl.BlockSpec((tm, D), lambda i: (i, 0)),
                  pl.BlockSpec((8, NMOD * D), lambda i: (0, 0)),
                  pl.BlockSpec((1, D), lambda i: (0, 0)),
                  pl.BlockSpec((D, P_COLS), lambda i: (0, 0)),
                  pl.BlockSpec((D, 128), lambda i: (0, 0)),
                  pl.BlockSpec((1, 128), lambda i: (0, 0)),
                  pl.BlockSpec((1, 128), lambda i: (0, 0))],
        out_specs=[pl.BlockSpec((tm, P_COLS), lambda i: (i, 0)),
                   pl.BlockSpec((tm, 4 * H), lambda i: (i, 0)),
                   pl.BlockSpec((tm // CH, 4 * H, CH), lambda i: (i, 0, 0))],
        out_shape=[jax.ShapeDtypeStruct((t_all, P_COLS), F32),
                   jax.ShapeDtypeStruct((t_all, 4 * H), F32),
                   jax.ShapeDtypeStruct((t_all // CH, 4 * H, CH), F32)],
        compiler_params=_params(("arbitrary",)),
        name="in_proj",
    )(x, mod, nw, wm, wg, al, dt)


def _shift_conv(x, w_ref, c0, c1):
    n = x.shape[0]
    row = _iota((n, 1), 0)
    prev = jnp.where(row == 0, 0.0, pltpu.roll(x, 1, 0))
    nxt = jnp.where(row == n - 1, 0.0, pltpu.roll(x, n - 1, 0))
    return prev * w_ref[0:1, c0:c1] + x * w_ref[1:2, c0:c1] + nxt * w_ref[2:3, c0:c1]


def _dn_kernel(n, q_ref, k_ref, v_ref, z_ref, g_ref, gt_ref, cw_ref, nw_ref, s0_ref,
               o_ref, sfin_ref, qs, ks, vs, gcc_s, gct_s, s_s, oacc):
    nc = n // CH
    width = H * DH

    for hh in range(H):
        c0, c1 = hh * DH, (hh + 1) * DH
        q = _silu(_shift_conv(q_ref[:, c0:c1], cw_ref, c0, c1))
        q = q * lax.rsqrt(jnp.sum(q * q, axis=-1, keepdims=True) + 1e-6)
        qs[:, c0:c1] = q * (DH ** -0.5)
        k = _silu(_shift_conv(k_ref[:, c0:c1], cw_ref, width + c0, width + c1))
        ks[:, c0:c1] = k * lax.rsqrt(jnp.sum(k * k, axis=-1, keepdims=True) + 1e-6)
        vs[:, c0:c1] = _silu(_shift_conv(v_ref[:, c0:c1], cw_ref, 2 * width + c0, 2 * width + c1))

    ii = _iota((CH, CH), 0)
    jj = _iota((CH, CH), 1)
    x = gt_ref[...].reshape(nc * 4 * H, CH)
    cf = _mm(x, (ii <= jj).astype(F32), prec=HI)
    cb = _mm(x, (ii >= jj).astype(F32), prec=HI)
    rsel = (_iota((nc * 4 * H, 1), 0) & (4 * H - 1)) < 3 * H
    gct_s[...] = jnp.where(rsel, cf, cb).reshape(nc, 4 * H, CH)
    i2 = _iota((2 * CH, 2 * CH), 0)
    j2 = _iota((2 * CH, 2 * CH), 1)
    same = (i2 >= CH) == (j2 >= CH)
    lf = (same & (i2 >= j2)).astype(F32)
    lb = (same & (i2 <= j2)).astype(F32)
    lsel = _iota((1, 4 * H), 1) < 3 * H
    for m in range(n // (2 * CH)):
        y = g_ref[m * 2 * CH:(m + 1) * 2 * CH, :]
        gcc_s[m * 2 * CH:(m + 1) * 2 * CH, :] = jnp.where(lsel, _mm(lf, y, prec=HI), _mm(lb, y, prec=HI))

    for hh in range(H):
        for d in range(2):
            s_s[hh * 2 + d] = s0_ref[0, d, hh]

    eye = (ii == jj).astype(F32)
    blk_same = [(ii >> s) == (jj >> s) for s in range(1, 7)]

    chains = [(hh, d) for hh in range(H) for d in range(2)]

    def chunk_step(c, carry):
        st = []
        for hh, d in chains:
            cidx = c if d == 0 else nc - 1 - c
            r0 = pl.multiple_of(cidx * CH, CH)
            c0, c1 = hh * DH, (hh + 1) * DH
            gcol = 2 * H + H * d + hh
            bcol = H * d + hh
            incl = (ii >= jj) if d == 0 else (ii <= jj)
            strict = (ii > jj) if d == 0 else (ii < jj)
            qc = qs[pl.ds(r0, CH), c0:c1]
            kc = ks[pl.ds(r0, CH), c0:c1]
            vc = vs[pl.ds(r0, CH), c0:c1]
            beta = g_ref[pl.ds(r0, CH), bcol:bcol + 1]
            gc = gcc_s[pl.ds(r0, CH), gcol:gcol + 1]
            gr = gct_s[cidx, gcol:gcol + 1, :]
            tot = gr[:, CH - 1:CH] if d == 0 else gr[:, 0:1]
            decay = jnp.where(incl, jnp.exp(jnp.where(incl, gc - gr, 0.0)), 0.0)
            kb = kc * beta
            egc = jnp.exp(gc)
            st.append(dict(r0=r0, c0=c0, c1=c1, strict=strict, kc=kc, kb=kb, decay=decay,
                           rhs=jnp.concatenate([vc * beta, kb * egc], axis=1),
                           qd=qc * egc, qc=qc, kd=kc * jnp.exp(tot - gc), gl=jnp.exp(tot)))
        for x in st:
            x["lmat"] = jnp.where(x["strict"], _mmb(x["kb"], x["kc"], NT) * x["decay"], 0.0)
            x["qk"] = _mmb(x["qc"], x["kc"], NT) * x["decay"]
            x["tinv"] = eye - jnp.where(blk_same[0], x["lmat"], 0.0)
        for lvl in range(1, len(blk_same)):
            for x in st:
                off = jnp.where(blk_same[lvl] & ~blk_same[lvl - 1], x["lmat"], 0.0)
                x["ot"] = _mm_inv(off, x["tinv"])
            for x in st:
                x["tinv"] = x["tinv"] - _mm_inv(x["tinv"], x["ot"])
        for x in st:
            x["sol"] = _mm_inv(x["tinv"], x["rhs"])
        for i, x in enumerate(st):
            x["s"] = s_s[i]
            x["sb"] = x["s"].astype(BF16)
            x["v_new"] = x["sol"][:, 0:DH] - _mmb(x["sol"][:, DH:2 * DH], x["sb"])
        for i, (x, (hh, d)) in enumerate(zip(st, chains)):
            o = _mmb(x["qd"], x["sb"]) + _mmb(x["qk"], x["v_new"])
            s_s[i] = x["s"] * x["gl"] + _mmb(x["kd"], x["v_new"], TN)
            oacc[d, pl.ds(x["r0"], CH), x["c0"]:x["c1"]] = o
        return carry

    lax.fori_loop(0, nc, chunk_step, 0)

    for hh in range(H):
        c0, c1 = hh * DH, (hh + 1) * DH
        o = oacc[0, :, c0:c1] + oacc[1, :, c0:c1]
        o = o * lax.rsqrt(jnp.mean(o * o, axis=-1, keepdims=True) + EPS)
        o_ref[:, c0:c1] = o * nw_ref[...] * _silu(z_ref[:, c0:c1])
        for d in range(2):
            sfin_ref[0, d, hh] = s_s[hh * 2 + d]


def _deltanet(p, g, gt, conv_w, norm_w, s0, n, row_blk0):
    bsz = s0.shape[0]
    nc = n // CH
    width = H * DH
    return pl.pallas_call(
        functools.partial(_dn_kernel, n),
        grid=(bsz,),
        in_specs=[pl.BlockSpec((n, width), lambda b: (row_blk0 + b, 0)),
                  pl.BlockSpec((n, width), lambda b: (row_blk0 + b, 1)),
                  pl.BlockSpec((n, width), lambda b: (row_blk0 + b, 2)),
                  pl.BlockSpec((n, width), lambda b: (row_blk0 + b, 3)),
                  pl.BlockSpec((n, 4 * H), lambda b: (row_blk0 + b, 0)),
                  pl.BlockSpec((nc, 4 * H, CH), lambda b: (row_blk0 + b, 0, 0)),
                  pl.BlockSpec((3, 3 * width), lambda b: (0, 0)),
                  pl.BlockSpec((1, DH), lambda b: (0, 0)),
                  pl.BlockSpec((1, 2, H, DH, DH), lambda b: (b, 0, 0, 0, 0))],
        out_specs=[pl.BlockSpec((n, width), lambda b: (b, 0)),
                   pl.BlockSpec((1, 2, H, DH, DH), lambda b: (b, 0, 0, 0, 0))],
        out_shape=[jax.ShapeDtypeStruct((bsz * n, width), F32),
                   jax.ShapeDtypeStruct((bsz, 2, H, DH, DH), F32)],
        scratch_shapes=[pltpu.VMEM((n, width), F32), pltpu.VMEM((n, width), F32),
                        pltpu.VMEM((n, width), F32), pltpu.VMEM((n, 4 * H), F32),
                        pltpu.VMEM((nc, 4 * H, CH), F32), pltpu.VMEM((2 * H, DH, DH), F32),
                        pltpu.VMEM((2, n, width), F32)],
        compiler_params=_params(("arbitrary",)),
        name="deltanet_n%d" % n,
    )(p, p, p, p, g, gt, conv_w, norm_w, s0)


def _mix2_kernel(sb_ref, sc_ref, su_ref, fu_ref, cw_ref, bdc_ref, bds_ref, cs_ref, o_ref):
    w = sb_ref.shape[1]
    cu = sc_ref[...] * su_ref[...]
    o_ref[:, 0:w] = sb_ref[...] * _shift_conv(cu, cw_ref, 0, w)
    fu = fu_ref[...]
    a = _mm(fu, bdc_ref[...], prec=HI)
    b = _mm(fu, bds_ref[...], prec=HI)
    o_ref[:, w:2 * w] = _mm(cs_ref[...], jnp.concatenate([a, b], axis=0), prec=HI)


def _mix2(p, conv_w, bdc, bds, cs, bsz, n, row_blk0):
    w = 256
    return pl.pallas_call(
        _mix2_kernel,
        grid=(bsz,),
        in_specs=[pl.BlockSpec((n, w), lambda b: (row_blk0 + b, 8)),
                  pl.BlockSpec((n, w), lambda b: (row_blk0 + b, 9)),
                  pl.BlockSpec((n, w), lambda b: (row_blk0 + b, 10)),
                  pl.BlockSpec((n, w), lambda b: (row_blk0 + b, 11)),
                  pl.BlockSpec((3, w), lambda b: (0, 0)),
                  pl.BlockSpec((w, w), lambda b: (0, 0)),
                  pl.BlockSpec((w, w), lambda b: (0, 0)),
                  pl.BlockSpec((n, 2 * n), lambda b: (0, 0))],
        out_specs=pl.BlockSpec((n, 2 * w), lambda b: (b, 0)),
        out_shape=jax.ShapeDtypeStruct((bsz * n, 2 * w), F32),
        compiler_params=_params(("arbitrary",)),
        name="conv_fourier_n%d" % n,
    )(p, p, p, p, conv_w, bdc, bds, cs)


def _dft_tables(n):
    gw = 64
    k = jnp.arange(n, dtype=I32)
    ph = (k[:, None] * k[None, :]) % n
    ang = ph.astype(F32) * (2.0 * math.pi / n)
    scale = 1.0 / math.sqrt(n * gw)
    cs = jnp.concatenate([jnp.cos(ang), -jnp.sin(ang)], axis=1) * scale
    c = jnp.arange(256, dtype=I32)
    phc = ((c[:, None] % gw) * (c[None, :] % gw)) % gw
    angc = phc.astype(F32) * (2.0 * math.pi / gw)
    same = (c[:, None] // gw) == (c[None, :] // gw)
    bdc = jnp.where(same, jnp.cos(angc), 0.0)
    bds = jnp.where(same, jnp.sin(angc), 0.0)
    return cs, bdc, bds


def _out_kernel(t1, n2, tm, dn1_ref, dn2_ref, mx1_ref, mx2_ref, x_ref, mod_ref, nw_ref, wo_ref, wr_ref,
                x1_ref, h2_ref, aff_ref):
    t0 = pl.program_id(0) * tm
    r = _mod_row(t0, t1, n2)
    half = D // 2
    dn = jnp.where(t0 < t1, dn1_ref[...], dn2_ref[...])
    mx = jnp.where(t0 < t1, mx1_ref[...], mx2_ref[...])
    mix = _mmb(dn, wo_ref[0:half, :]) + _mmb(mx, wo_ref[half:D, :])
    x1 = x_ref[...] + mod_ref[pl.ds(r, 1), 2 * D:3 * D] * mix
    x1_ref[...] = x1
    y = x1 * lax.rsqrt(jnp.mean(x1 * x1, axis=-1, keepdims=True) + EPS) * nw_ref[...]
    h2 = y * (1.0 + mod_ref[pl.ds(r, 1), 4 * D:5 * D]) + mod_ref[pl.ds(r, 1), 3 * D:4 * D]
    h2_ref[...] = h2
    logits = _mm(h2, wr_ref[...], prec=HI)
    lt = logits.T[0:E, :]
    ex = jnp.exp(lt - jnp.max(lt, axis=0, keepdims=True))
    aff_ref[...] = ex / jnp.sum(ex, axis=0, keepdims=True)


def _out_proj(dn1, dn2, mx1, mx2, x, mod, nw, wo, wr, t1, n2, tm):
    t_all = x.shape[0]
    nt1 = t1 // tm
    spec1 = pl.BlockSpec((tm, D // 2), lambda i: (jnp.minimum(i, nt1 - 1), 0))
    spec2 = pl.BlockSpec((tm, D // 2), lambda i: (jnp.maximum(i - nt1, 0), 0))
    return pl.pallas_call(
        functools.partial(_out_kernel, t1, n2, tm),
        grid=(t_all // tm,),
        in_specs=[spec1, spec2, spec1, spec2,
                  pl.BlockSpec((tm, D), lambda i: (i, 0)),
                  pl.BlockSpec((8, NMOD * D), lambda i: (0, 0)),
                  pl.BlockSpec((1, D), lambda i: (0, 0)),
                  pl.BlockSpec((D, D), lambda i: (0, 0)),
                  pl.BlockSpec((D, 128), lambda i: (0, 0))],
        out_specs=[pl.BlockSpec((tm, D), lambda i: (i, 0)),
                   pl.BlockSpec((tm, D), lambda i: (i, 0)),
                   pl.BlockSpec((E, tm), lambda i: (0, i))],
        out_shape=[jax.ShapeDtypeStruct((t_all, D), F32),
                   jax.ShapeDtypeStruct((t_all, D), F32),
                   jax.ShapeDtypeStruct((E, t_all), F32)],
        compiler_params=_params(("arbitrary",)),
        name="out_proj",
    )(dn1, dn2, mx1, mx2, x, mod, nw, wo, wr)


SEL_TB = 256
N_SLOT_ROWS = 16


def _sel_kernel(t, cap, aff_ref, slots_ref, lo_ref, hi_ref, a3, c3):
    ntb = t // SEL_TB
    aff = aff_ref[...]

    def search(i, thr):
        cand = thr | jnp.left_shift(jnp.int32(1), 30 - i)
        cnt = jnp.sum((aff >= pltpu.bitcast(cand, F32)).astype(F32), axis=1, keepdims=True)
        return jnp.where(cnt >= cap, cand, thr)

    thr = lax.fori_loop(0, 31, search, jnp.zeros((E, 1), I32))
    gt = (aff >= pltpu.bitcast(thr + 1, F32)).astype(F32)
    eq = (aff >= pltpu.bitcast(thr, F32)).astype(F32) - gt
    need = cap - jnp.sum(gt, axis=1, keepdims=True)

    ui = _iota((SEL_TB, SEL_TB), 0)
    uj = _iota((SEL_TB, SEL_TB), 1)
    upper = (ui < uj).astype(BF16)

    def excl_cumsum(rows, dst, r0, r1):
        carry = jnp.zeros((rows.shape[0], 1), F32)
        for j in range(ntb):
            blk = rows[:, j * SEL_TB:(j + 1) * SEL_TB]
            dst[j, r0:r1, :] = _mm(blk.astype(BF16), upper) + carry
            carry = carry + jnp.sum(blk, axis=1, keepdims=True)

    excl_cumsum(eq, c3, 0, E)
    rank_eq = jnp.concatenate([c3[j, 0:E, :] for j in range(ntb)], axis=1)
    sel = jnp.maximum(gt, jnp.where(rank_eq < need, eq, 0.0))
    n_tok = jnp.sum(sel, axis=0, keepdims=True)
    excl_cumsum(jnp.concatenate([sel, jnp.broadcast_to(n_tok, (8, t))], axis=0), c3, 0, E + 8)
    ei = _iota((E, E), 0)
    ej = _iota((E, E), 1)
    below = _mm((ej < ei).astype(BF16), sel.astype(BF16))
    for j in range(ntb):
        sl = slice(j * SEL_TB, (j + 1) * SEL_TB)
        off = c3[j, E:E + 1, :]
        lo_ref[:, sl] = off
        hi_ref[:, sl] = off + n_tok[:, sl]
        rank = off + below[:, sl]
        a3[j, 0] = sel[:, sl]
        a3[j, 1] = aff[:, sl]
        a3[j, 2] = rank

    slot_iota = _iota((cap, SEL_TB), 0).astype(F32)
    tok_iota = _iota((1, SEL_TB), 1)

    def per_expert(e, carry):
        def per_block(j, acc):
            pos = c3[j, pl.ds(e, 1), :]
            chosen = a3[j, 0, pl.ds(e, 1), :]
            w = a3[j, 1, pl.ds(e, 1), :]
            rank = a3[j, 2, pl.ds(e, 1), :].astype(I32)
            tok = tok_iota + j * SEL_TB
            w_hi = w.astype(BF16).astype(F32)
            w_mid = (w - w_hi).astype(BF16).astype(F32)
            w_lo = w - w_hi - w_mid
            vals = jnp.concatenate(
                [(tok >> 7).astype(F32), (tok & 127).astype(F32), w_hi, w_mid, w_lo,
                 (rank >> 7).astype(F32), (rank & 127).astype(F32),
                 jnp.zeros((N_SLOT_ROWS - 7, SEL_TB), F32)], axis=0)
            onehot = jnp.where((slot_iota == pos) & (chosen > 0.0), 1.0, 0.0)
            return acc + _mmb(vals, onehot, NT)
        slots_ref[e] = lax.fori_loop(0, ntb, per_block, jnp.zeros((N_SLOT_ROWS, cap), F32))
        return carry

    lax.fori_loop(0, E, per_expert, 0)


def _select(aff_t, t, cap, col_blk):
    ntb = t // SEL_TB
    return pl.pallas_call(
        functools.partial(_sel_kernel, t, cap),
        grid=(1,),
        in_specs=[pl.BlockSpec((E, t), lambda i: (0, col_blk))],
        out_specs=[pl.BlockSpec((E, N_SLOT_ROWS, cap), lambda i: (0, 0, 0)),
                   pl.BlockSpec((1, t), lambda i: (0, 0)),
                   pl.BlockSpec((1, t), lambda i: (0, 0))],
        out_shape=[jax.ShapeDtypeStruct((E, N_SLOT_ROWS, cap), F32),
                   jax.ShapeDtypeStruct((1, t), F32),
                   jax.ShapeDtypeStruct((1, t), F32)],
        scratch_shapes=[pltpu.VMEM((ntb, 3, E, SEL_TB), F32),
                        pltpu.VMEM((ntb, E + 8, SEL_TB), F32)],
        compiler_params=_params(("arbitrary",)),
        name="select_t%d" % t,
    )(aff_t)


def _ffn_kernel(rc, nch, idx_ref, dst_ref, h2_hbm, wcol_ref, wg_ref, wu_ref, wd_ref, z_hbm,
                xbuf, ybuf, wgb, wub, wdb, gsem, ssem):
    e = pl.program_id(0)
    c = pl.program_id(1)
    step = e * nch + c
    base = step * rc

    def gather(j, carry):
        tok = idx_ref[base + j]
        pltpu.make_async_copy(h2_hbm.at[pl.ds(tok, 1)], xbuf.at[pl.ds(j, 1)], gsem).start()
        return carry

    lax.fori_loop(0, rc, gather, 0)

    @pl.when(c == 0)
    def _():
        wgb[...] = wg_ref[0, 0].astype(BF16)
        wub[...] = wu_ref[0, 0].astype(BF16)
        wdb[...] = wd_ref[0, 0].astype(BF16)

    pltpu.make_async_copy(h2_hbm.at[pl.ds(0, rc)], xbuf, gsem).wait()
    xb = xbuf[...].astype(BF16)
    hid = _silu(_mm(xb, wgb[...])) * _mm(xb, wub[...])
    y = _mm(hid.astype(BF16), wdb[...]) * wcol_ref[...]

    @pl.when(step > 0)
    def _():
        pltpu.make_async_copy(ybuf, z_hbm.at[pl.ds(0, rc)], ssem).wait()

    ybuf[...] = y

    def scatter(j, carry):
        row = dst_ref[base + j]
        pltpu.make_async_copy(ybuf.at[pl.ds(j, 1)], z_hbm.at[pl.ds(row, 1)], ssem).start()
        return carry

    lax.fori_loop(0, rc, scatter, 0)

    @pl.when(step == E * nch - 1)
    def _():
        pltpu.make_async_copy(ybuf, z_hbm.at[pl.ds(0, rc)], ssem).wait()


def _expert_ffn(idx, dst, h2, wcol, w_gate, w_up, w_down, layer, rc, nch):
    zrows = E * nch * rc
    wspec = pl.BlockSpec((1, 1, D, D), lambda e, c, *_: (layer, e, 0, 0))
    return pl.pallas_call(
        functools.partial(_ffn_kernel, rc, nch),
        grid_spec=pltpu.PrefetchScalarGridSpec(
            num_scalar_prefetch=2,
            grid=(E, nch),
            in_specs=[pl.BlockSpec(memory_space=pl.ANY),
                      pl.BlockSpec((rc, 1), lambda e, c, *_: (e * nch + c, 0)),
                      wspec, wspec, wspec],
            out_specs=pl.BlockSpec(memory_space=pl.ANY),
            scratch_shapes=[pltpu.VMEM((rc, D), F32), pltpu.VMEM((rc, D), F32),
                            pltpu.VMEM((D, D), BF16), pltpu.VMEM((D, D), BF16),
                            pltpu.VMEM((D, D), BF16),
                            pltpu.SemaphoreType.DMA(()), pltpu.SemaphoreType.DMA(())]),
        out_shape=jax.ShapeDtypeStruct((zrows, D), F32),
        compiler_params=_params(("arbitrary", "arbitrary")),
        name="expert_ffn",
    )(idx, dst, h2, wcol, w_gate, w_up, w_down)


CMB_TB = 256
CMB_RC = 512


def _comb_kernel(t1, n2, zrows, final, bs_ref, be_ref, x1_ref, lo_ref, hi_ref, mod_ref, fw_ref, z_hbm,
                 o_ref, zbuf, sem):
    i = pl.program_id(0)
    r = _mod_row(i * CMB_TB, t1, n2)
    start = bs_ref[i]
    end = be_ref[i]
    base0 = (start // 8) * 8
    nchunk = (end - base0 + CMB_RC - 1) // CMB_RC
    eye = _iota((CMB_TB, CMB_TB), 0) == _iota((CMB_TB, CMB_TB), 1)
    lo = jnp.sum(jnp.where(eye, lo_ref[...], 0.0), axis=1, keepdims=True)
    hi = jnp.sum(jnp.where(eye, hi_ref[...], 0.0), axis=1, keepdims=True)
    col = _iota((1, CMB_RC), 1)

    def chunk(ci, acc):
        nominal = base0 + ci * CMB_RC
        b = pl.multiple_of(jnp.minimum(nominal, zrows - CMB_RC), 8)
        cp = pltpu.make_async_copy(z_hbm.at[pl.ds(b, CMB_RC)], zbuf, sem)
        cp.start()
        cp.wait()
        rows = (col + b).astype(F32)
        pick = (rows >= lo) & (rows < hi) & (rows >= nominal.astype(F32))
        s = jnp.where(pick, 1.0, 0.0).astype(BF16)
        z = zbuf[...]
        z_hi = z.astype(BF16)
        z_lo = (z - z_hi.astype(F32)).astype(BF16)
        return acc + _mm(s, z_hi) + _mm(s, z_lo)

    moe = lax.fori_loop(0, nchunk, chunk, jnp.zeros((CMB_TB, D), F32))
    x2 = x1_ref[...] + mod_ref[pl.ds(r, 1), 5 * D:6 * D] * moe
    if final:
        x2 = x2 * lax.rsqrt(jnp.mean(x2 * x2, axis=-1, keepdims=True) + EPS) * fw_ref[...]
    o_ref[...] = x2


def _combine(bstart, bend, x1, lo, hi, mod, fw, z, t1, n2, final):
    t_all = x1.shape[0]
    zrows = z.shape[0]
    return pl.pallas_call(
        functools.partial(_comb_kernel, t1, n2, zrows, final),
        grid_spec=pltpu.PrefetchScalarGridSpec(
            num_scalar_prefetch=2,
            grid=(t_all // CMB_TB,),
            in_specs=[pl.BlockSpec((CMB_TB, D), lambda i, *_: (i, 0)),
                      pl.BlockSpec((1, CMB_TB), lambda i, *_: (0, i)),
                      pl.BlockSpec((1, CMB_TB), lambda i, *_: (0, i)),
                      pl.BlockSpec((8, NMOD * D), lambda i, *_: (0, 0)),
                      pl.BlockSpec((1, D), lambda i, *_: (0, 0)),
                      pl.BlockSpec(memory_space=pl.ANY)],
            out_specs=pl.BlockSpec((CMB_TB, D), lambda i, *_: (i, 0)),
            scratch_shapes=[pltpu.VMEM((CMB_RC, D), F32), pltpu.SemaphoreType.DMA(())]),
        out_shape=jax.ShapeDtypeStruct((t_all, D), F32),
        compiler_params=_params(("arbitrary",)),
        name="combine",
    )(bstart, bend, x1, lo, hi, mod, fw, z)


def _grid_pos_embed(n, d):
    rows = n // GRID_W
    r = jnp.repeat(jnp.arange(rows, dtype=F32), GRID_W)
    col = jnp.tile(jnp.arange(GRID_W, dtype=F32), rows)
    quarter = d // 4
    omega = jnp.power(POS_BASE, -jnp.arange(quarter, dtype=F32) / quarter)
    ra = r[:, None] * omega
    ca = col[:, None] * omega
    return jnp.concatenate([jnp.sin(ra), jnp.cos(ra), jnp.sin(ca), jnp.cos(ca)], axis=-1)


def _decode_slots(slots):
    idx = (slots[:, 0] * 128.0 + slots[:, 1]).astype(I32)
    w = slots[:, 2] + slots[:, 3] + slots[:, 4]
    rank = (slots[:, 5] * 128.0 + slots[:, 6]).astype(I32)
    return idx, w, rank


def kernel(x_prompt, x_sample, state_delta, c, c_ctx, w_ada, b_ada, norm1_w, norm2_w, w_in, dn_conv_w,
           dn_a_log, dn_dt_bias, dn_norm_w, sc_conv_w, w_out, w_router, w_gate, w_up, w_down,
           final_norm_w):
    b1, n1, _ = x_prompt.shape
    b2, n2, _ = x_sample.shape
    depth = w_ada.shape[0]
    t1, t2 = b1 * n1, b2 * n2
    t_all = t1 + t2
    cap1 = max(1, 2 * t1 // E)
    cap2 = max(1, 2 * t2 // E)
    tm = 512
    rc = 512 if (cap1 % 512 == 0 and cap2 % 512 == 0) else 128
    assert t1 % n2 == 0 and t1 % tm == 0 and n2 % tm == 0 and n1 % (2 * CH) == 0
    assert t1 % SEL_TB == 0 and t2 % SEL_TB == 0 and t1 % t2 == 0
    assert cap1 % rc == 0 and cap2 % rc == 0 and 2 * t_all >= CMB_RC
    nch = (cap1 + cap2) // rc
    width = H * DH

    cond8 = jnp.zeros((8, D), F32).at[0].set(c_ctx).at[1:1 + b2].set(c)
    mod = _ada(cond8, w_ada, b_ada)

    x = jnp.concatenate([x_prompt.reshape(t1, D),
                         (x_sample + _grid_pos_embed(n2, D)[None]).reshape(t2, D)], axis=0)
    s_zero = jnp.zeros((b1, 2, H, DH, DH), F32)
    tabs1 = _dft_tables(n1)
    tabs2 = _dft_tables(n2)
    ctx_states = []

    for l in range(depth):
        wl = w_in[l]
        wm = jnp.concatenate([wl[:, 0:4 * width], wl[:, 4 * width + 4 * H:]], axis=1).astype(BF16)
        wg = jnp.pad(wl[:, 4 * width:4 * width + 4 * H], ((0, 0), (0, 128 - 4 * H))).astype(BF16)
        al = jnp.pad(dn_a_log[l].reshape(1, 2 * H), ((0, 0), (2 * H, 128 - 4 * H)))
        dt = jnp.pad(dn_dt_bias[l].reshape(1, 2 * H), ((0, 0), (2 * H, 128 - 4 * H)))
        p, g, gt = _in_proj(x, mod[l], norm1_w[l].reshape(1, D), wm, wg, al, dt, t1, n2, tm)

        nwd = dn_norm_w[l].reshape(1, DH)
        dn1, s_ctx = _deltanet(p, g, gt, dn_conv_w[l], nwd, s_zero, n1, 0)
        dn2, _ = _deltanet(p, g, gt, dn_conv_w[l], nwd, state_delta[:, l], n2, t1 // n2)
        ctx_states.append(s_ctx)
        mx1 = _mix2(p, sc_conv_w[l], tabs1[1], tabs1[2], tabs1[0], b1, n1, 0)
        mx2 = _mix2(p, sc_conv_w[l], tabs2[1], tabs2[2], tabs2[0], b2, n2, t1 // n2)

        wr = jnp.pad(w_router[l], ((0, 0), (0, 128 - E)))
        x1, h2, aff_t = _out_proj(dn1, dn2, mx1, mx2, x, mod[l], norm2_w[l].reshape(1, D),
                                  w_out[l].astype(BF16), wr, t1, n2, tm)

        slots1, lo1, hi1 = _select(aff_t, t1, cap1, 0)
        slots2, lo2, hi2 = _select(aff_t, t2, cap2, t1 // t2)
        idx1, wsel1, rank1 = _decode_slots(slots1)
        idx2, wsel2, rank2 = _decode_slots(slots2)
        idx = jnp.concatenate([idx1, idx2 + t1], axis=1).reshape(-1)
        dst = jnp.concatenate([rank1, rank2 + 2 * t1], axis=1).reshape(-1)
        wcol = jnp.concatenate([wsel1, wsel2], axis=1).reshape(-1, 1)
        z = _expert_ffn(idx, dst, h2, wcol, w_gate, w_up, w_down, l, rc, nch)

        lo = jnp.concatenate([lo1, lo2 + 2.0 * t1], axis=1)
        hi = jnp.concatenate([hi1, hi2 + 2.0 * t1], axis=1)
        bstart = lo[0, ::CMB_TB].astype(I32)
        bend = hi[0, CMB_TB - 1::CMB_TB].astype(I32)
        x = _combine(bstart, bend, x1, lo, hi, mod[l], final_norm_w.reshape(1, D), z, t1, n2,
                     l == depth - 1)

    y_prompt = x[:t1].reshape(b1, n1, D)
    y_sample = x[t1:].reshape(b2, n2, D)
    return y_prompt, y_sample, jnp.stack(ctx_states, axis=1)
```

```python
import functools
import math

import jax
import jax.numpy as jnp
from jax import lax
from jax.experimental import pallas as pl
from jax.experimental.pallas import tpu as pltpu

F32 = jnp.float32
BF16 = jnp.bfloat16
I32 = jnp.int32
HI = lax.Precision.HIGHEST

D = 1024
H = 4
DH = 128
CH = 64
E = 16
NMOD = 6
EPS = 1e-6
GRID_W = 64
POS_BASE = 10000.0
P_COLS = 3072
VMEM_LIMIT = 56 * 1024 * 1024

NN = (((1,), (0,)), ((), ()))
NT = (((1,), (1,)), ((), ()))
TN = (((0,), (0,)), ((), ()))


def _mm(a, b, dims=NN, prec=None):
    return lax.dot_general(a, b, dims, precision=prec, preferred_element_type=F32)


def _mmb(a, b, dims=NN):
    return lax.dot_general(a.astype(BF16), b.astype(BF16), dims, preferred_element_type=F32)


def _mm_inv(a, b):
    return _mmb(a, b)


def _silu(x):
    return x / (1.0 + jnp.exp(-x))


def _sigmoid(x):
    return 1.0 / (1.0 + jnp.exp(-x))


def _softplus(x):
    return jnp.maximum(x, 0.0) + jnp.log1p(jnp.exp(-jnp.abs(x)))


def _iota(shape, dim):
    return lax.broadcasted_iota(I32, shape, dim)


def _mod_row(t0, t1, n2):
    return jnp.where(t0 < t1, 0, 1 + jnp.maximum(t0 - t1, 0) // n2)


def _params(sem):
    return pltpu.CompilerParams(dimension_semantics=sem, vmem_limit_bytes=VMEM_LIMIT)


def _ada_kernel(c_ref, w_ref, b_ref, o_ref):
    s = _silu(c_ref[...])
    o_ref[0] = _mm(s, w_ref[0], prec=HI) + b_ref[0]


def _ada(cond8, w_ada, b_ada):
    depth = w_ada.shape[0]
    tn = 1536
    return pl.pallas_call(
        _ada_kernel,
        grid=(depth, NMOD * D // tn),
        in_specs=[pl.BlockSpec((8, D), lambda l, j: (0, 0)),
                  pl.BlockSpec((1, D, tn), lambda l, j: (l, 0, j)),
                  pl.BlockSpec((1, 1, tn), lambda l, j: (l, 0, j))],
        out_specs=pl.BlockSpec((1, 8, tn), lambda l, j: (l, 0, j)),
        out_shape=jax.ShapeDtypeStruct((depth, 8, NMOD * D), F32),
        compiler_params=_params(("arbitrary", "arbitrary")),
        name="ada",
    )(cond8, w_ada, b_ada.reshape(depth, 1, NMOD * D))


def _in_kernel(t1, n2, tm, x_ref, mod_ref, nw_ref, wm_ref, wg_ref, al_ref, dt_ref,
               p_ref, g_ref, gt_ref):
    r = _mod_row(pl.program_id(0) * tm, t1, n2)
    x = x_ref[...]
    y = x * lax.rsqrt(jnp.mean(x * x, axis=-1, keepdims=True) + EPS) * nw_ref[...]
    sh = mod_ref[pl.ds(r, 1), 0:D]
    sc = mod_ref[pl.ds(r, 1), D:2 * D]
    hb = (y * (1.0 + sc) + sh).astype(BF16)
    nsplit = 4
    wc = P_COLS // nsplit
    for j in range(nsplit):
        p_ref[:, j * wc:(j + 1) * wc] = _mm(hb, wm_ref[:, j * wc:(j + 1) * wc])
    raw = _mm(hb, wg_ref[...])
    lane = _iota((1, 128), 1)
    g = -jnp.exp(al_ref[...]) * _softplus(raw + dt_ref[...])
    act = jnp.where(lane < 2 * H, _sigmoid(raw), g)
    g_ref[...] = act[:, 0:4 * H]
    act_t = act.T
    for j in range(tm // CH):
        gt_ref[j] = act_t[0:4 * H, j * CH:(j + 1) * CH]


def _in_proj(x, mod, nw, wm, wg, al, dt, t1, n2, tm):
    t_all = x.shape[0]
    return pl.pallas_call(
        functools.partial(_in_kernel, t1, n2, tm),
        grid=(t_all // tm,),
        in_specs=[pl.BlockSpec((tm, D), lambda i: (i, 0)),
                  pl.BlockSpec((8, NMOD * D), lambda i: (0, 0)),
                  pl.BlockSpec((1, D), lambda i: (0, 0)),
                  pl.BlockSpec((D, P_COLS), lambda i: (0, 0)),
                  pl.BlockSpec((D, 128), lambda i: (0, 0)),
                  pl.BlockSpec((1, 128), lambda i: (0, 0)),
                  pl.BlockSpec((1, 128), lambda i: (0, 0))],
        out_specs=[pl.BlockSpec((tm, P_COLS), lambda i: (i, 0)),
                   pl.BlockSpec((tm, 4 * H), lambda i: (i, 0)),
                   pl.BlockSpec((tm // CH, 4 * H, CH), lambda i: (i, 0, 0))],
        out_shape=[jax.ShapeDtypeStruct((t_all, P_COLS), F32),
                   jax.ShapeDtypeStruct((t_all, 4 * H), F32),
                   jax.ShapeDtypeStruct((t_all // CH, 4 * H, CH), F32)],
        compiler_params=_params(("arbitrary",)),
        name="in_proj",
    )(x, mod, nw, wm, wg, al, dt)


def _shift_conv(x, w_ref, c0, c1):
    n = x.shape[0]
    row = _iota((n, 1), 0)
    prev = jnp.where(row == 0, 0.0, pltpu.roll(x, 1, 0))
    nxt = jnp.where(row == n - 1, 0.0, pltpu.roll(x, n - 1, 0))
    return prev * w_ref[0:1, c0:c1] + x * w_ref[1:2, c0:c1] + nxt * w_ref[2:3, c0:c1]


def _dn_kernel(n, q_ref, k_ref, v_ref, z_ref, g_ref, gt_ref, cw_ref, nw_ref, s0_ref,
               o_ref, sfin_ref, qs, ks, vs, gcc_s, gct_s, s_s, oacc):
    nc = n // CH
    width = H * DH

    for hh in range(H):
        c0, c1 = hh * DH, (hh + 1) * DH
        q = _silu(_shift_conv(q_ref[:, c0:c1], cw_ref, c0, c1))
        q = q * lax.rsqrt(jnp.sum(q * q, axis=-1, keepdims=True) + 1e-6)
        qs[:, c0:c1] = q * (DH ** -0.5)
        k = _silu(_shift_conv(k_ref[:, c0:c1], cw_ref, width + c0, width + c1))
        ks[:, c0:c1] = k * lax.rsqrt(jnp.sum(k * k, axis=-1, keepdims=True) + 1e-6)
        vs[:, c0:c1] = _silu(_shift_conv(v_ref[:, c0:c1], cw_ref, 2 * width + c0, 2 * width + c1))

    ii = _iota((CH, CH), 0)
    jj = _iota((CH, CH), 1)
    x = gt_ref[...].reshape(nc * 4 * H, CH)
    cf = _mm(x, (ii <= jj).astype(F32), prec=HI)
    cb = _mm(x, (ii >= jj).astype(F32), prec=HI)
    rsel = (_iota((nc * 4 * H, 1), 0) & (4 * H - 1)) < 3 * H
    gct_s[...] = jnp.where(rsel, cf, cb).reshape(nc, 4 * H, CH)
    i2 = _iota((2 * CH, 2 * CH), 0)
    j2 = _iota((2 * CH, 2 * CH), 1)
    same = (i2 >= CH) == (j2 >= CH)
    lf = (same & (i2 >= j2)).astype(F32)
    lb = (same & (i2 <= j2)).astype(F32)
    lsel = _iota((1, 4 * H), 1) < 3 * H
    for m in range(n // (2 * CH)):
        y = g_ref[m * 2 * CH:(m + 1) * 2 * CH, :]
        gcc_s[m * 2 * CH:(m + 1) * 2 * CH, :] = jnp.where(lsel, _mm(lf, y, prec=HI), _mm(lb, y, prec=HI))

    for hh in range(H):
        for d in range(2):
            s_s[hh * 2 + d] = s0_ref[0, d, hh]

    eye = (ii == jj).astype(F32)
    blk_same = [(ii >> s) == (jj >> s) for s in range(1, 7)]

    chains = [(hh, d) for hh in range(H) for d in range(2)]

    def chunk_step(c, carry):
        st = []
        for hh, d in chains:
            cidx = c if d == 0 else nc - 1 - c
            r0 = pl.multiple_of(cidx * CH, CH)
            c0, c1 = hh * DH, (hh + 1) * DH
            gcol = 2 * H + H * d + hh
            bcol = H * d + hh
            incl = (ii >= jj) if d == 0 else (ii <= jj)
            strict = (ii > jj) if d == 0 else (ii < jj)
            qc = qs[pl.ds(r0, CH), c0:c1]
            kc = ks[pl.ds(r0, CH), c0:c1]
            vc = vs[pl.ds(r0, CH), c0:c1]
            beta = g_ref[pl.ds(r0, CH), bcol:bcol + 1]
            gc = gcc_s[pl.ds(r0, CH), gcol:gcol + 1]
            gr = gct_s[cidx, gcol:gcol + 1, :]
            tot = gr[:, CH - 1:CH] if d == 0 else gr[:, 0:1]
            decay = jnp.where(incl, jnp.exp(jnp.where(incl, gc - gr, 0.0)), 0.0)
            kb = kc * beta
            egc = jnp.exp(gc)
            st.append(dict(r0=r0, c0=c0, c1=c1, strict=strict, kc=kc, kb=kb, decay=decay,
                           rhs=jnp.concatenate([vc * beta, kb * egc], axis=1),
                           qd=qc * egc, qc=qc, kd=kc * jnp.exp(tot - gc), gl=jnp.exp(tot)))
        for x in st:
            x["lmat"] = jnp.where(x["strict"], _mmb(x["kb"], x["kc"], NT) * x["decay"], 0.0)
            x["qk"] = _mmb(x["qc"], x["kc"], NT) * x["decay"]
            x["tinv"] = eye - jnp.where(blk_same[0], x["lmat"], 0.0)
        for lvl in range(1, len(blk_same)):
            for x in st:
                off = jnp.where(blk_same[lvl] & ~blk_same[lvl - 1], x["lmat"], 0.0)
                x["ot"] = _mm_inv(off, x["tinv"])
            for x in st:
                x["tinv"] = x["tinv"] - _mm_inv(x["tinv"], x["ot"])
        for x in st:
            x["sol"] = _mm_inv(x["tinv"], x["rhs"])
        for i, x in enumerate(st):
            x["s"] = s_s[i]
            x["sb"] = x["s"].astype(BF16)
            x["v_new"] = x["sol"][:, 0:DH] - _mmb(x["sol"][:, DH:2 * DH], x["sb"])
        for i, (x, (hh, d)) in enumerate(zip(st, chains)):
            o = _mmb(x["qd"], x["sb"]) + _mmb(x["qk"], x["v_new"])
            s_s[i] = x["s"] * x["gl"] + _mmb(x["kd"], x["v_new"], TN)
            oacc[d, pl.ds(x["r0"], CH), x["c0"]:x["c1"]] = o
        return carry

    lax.fori_loop(0, nc, chunk_step, 0)

    for hh in range(H):
        c0, c1 = hh * DH, (hh + 1) * DH
        o = oacc[0, :, c0:c1] + oacc[1, :, c0:c1]
        o = o * lax.rsqrt(jnp.mean(o * o, axis=-1, keepdims=True) + EPS)
        o_ref[:, c0:c1] = o * nw_ref[...] * _silu(z_ref[:, c0:c1])
        for d in range(2):
            sfin_ref[0, d, hh] = s_s[hh * 2 + d]


def _deltanet(p, g, gt, conv_w, norm_w, s0, n, row_blk0):
    bsz = s0.shape[0]
    nc = n // CH
    width = H * DH
    return pl.pallas_call(
        functools.partial(_dn_kernel, n),
        grid=(bsz,),
        in_specs=[pl.BlockSpec((n, width), lambda b: (row_blk0 + b, 0)),
                  pl.BlockSpec((n, width), lambda b: (row_blk0 + b, 1)),
                  pl.BlockSpec((n, width), lambda b: (row_blk0 + b, 2)),
                  pl.BlockSpec((n, width), lambda b: (row_blk0 + b, 3)),
                  pl.BlockSpec((n, 4 * H), lambda b: (row_blk0 + b, 0)),
                  pl.BlockSpec((nc, 4 * H, CH), lambda b: (row_blk0 + b, 0, 0)),
                  pl.BlockSpec((3, 3 * width), lambda b: (0, 0)),
                  pl.BlockSpec((1, DH), lambda b: (0, 0)),
                  pl.BlockSpec((1, 2, H, DH, DH), lambda b: (b, 0, 0, 0, 0))],
        out_specs=[pl.BlockSpec((n, width), lambda b: (b, 0)),
                   pl.BlockSpec((1, 2, H, DH, DH), lambda b: (b, 0, 0, 0, 0))],
        out_shape=[jax.ShapeDtypeStruct((bsz * n, width), F32),
                   jax.ShapeDtypeStruct((bsz, 2, H, DH, DH), F32)],
        scratch_shapes=[pltpu.VMEM((n, width), F32), pltpu.VMEM((n, width), F32),
                        pltpu.VMEM((n, width), F32), pltpu.VMEM((n, 4 * H), F32),
                        pltpu.VMEM((nc, 4 * H, CH), F32), pltpu.VMEM((2 * H, DH, DH), F32),
                        pltpu.VMEM((2, n, width), F32)],
        compiler_params=_params(("arbitrary",)),
        name="deltanet_n%d" % n,
    )(p, p, p, p, g, gt, conv_w, norm_w, s0)


def _mix2_kernel(sb_ref, sc_ref, su_ref, fu_ref, cw_ref, bdc_ref, bds_ref, cs_ref, o_ref):
    w = sb_ref.shape[1]
    cu = sc_ref[...] * su_ref[...]
    o_ref[:, 0:w] = sb_ref[...] * _shift_conv(cu, cw_ref, 0, w)
    fu = fu_ref[...]
    a = _mm(fu, bdc_ref[...], prec=HI)
    b = _mm(fu, bds_ref[...], prec=HI)
    o_ref[:, w:2 * w] = _mm(cs_ref[...], jnp.concatenate([a, b], axis=0), prec=HI)


def _mix2(p, conv_w, bdc, bds, cs, bsz, n, row_blk0):
    w = 256
    return pl.pallas_call(
        _mix2_kernel,
        grid=(bsz,),
        in_specs=[pl.BlockSpec((n, w), lambda b: (row_blk0 + b, 8)),
                  pl.BlockSpec((n, w), lambda b: (row_blk0 + b, 9)),
                  pl.BlockSpec((n, w), lambda b: (row_blk0 + b, 10)),
                  pl.BlockSpec((n, w), lambda b: (row_blk0 + b, 11)),
                  pl.BlockSpec((3, w), lambda b: (0, 0)),
                  pl.BlockSpec((w, w), lambda b: (0, 0)),
                  pl.BlockSpec((w, w), lambda b: (0, 0)),
                  pl.BlockSpec((n, 2 * n), lambda b: (0, 0))],
        out_specs=pl.BlockSpec((n, 2 * w), lambda b: (b, 0)),
        out_shape=jax.ShapeDtypeStruct((bsz * n, 2 * w), F32),
        compiler_params=_params(("arbitrary",)),
        name="conv_fourier_n%d" % n,
    )(p, p, p, p, conv_w, bdc, bds, cs)


def _dft_tables(n):
    gw = 64
    k = jnp.arange(n, dtype=I32)
    ph = (k[:, None] * k[None, :]) % n
    ang = ph.astype(F32) * (2.0 * math.pi / n)
    scale = 1.0 / math.sqrt(n * gw)
    cs = jnp.concatenate([jnp.cos(ang), -jnp.sin(ang)], axis=1) * scale
    c = jnp.arange(256, dtype=I32)
    phc = ((c[:, None] % gw) * (c[None, :] % gw)) % gw
    angc = phc.astype(F32) * (2.0 * math.pi / gw)
    same = (c[:, None] // gw) == (c[None, :] // gw)
    bdc = jnp.where(same, jnp.cos(angc), 0.0)
    bds = jnp.where(same, jnp.sin(angc), 0.0)
    return cs, bdc, bds


def _out_kernel(t1, n2, tm, dn1_ref, dn2_ref, mx1_ref, mx2_ref, x_ref, mod_ref, nw_ref, wo_ref, wr_ref,
                x1_ref, h2_ref, aff_ref):
    t0 = pl.program_id(0) * tm
    r = _mod_row(t0, t1, n2)
    half = D // 2
    dn = jnp.where(t0 < t1, dn1_ref[...], dn2_ref[...])
    mx = jnp.where(t0 < t1, mx1_ref[...], mx2_ref[...])
    mix = _mmb(dn, wo_ref[0:half, :]) + _mmb(mx, wo_ref[half:D, :])
    x1 = x_ref[...] + mod_ref[pl.ds(r, 1), 2 * D:3 * D] * mix
    x1_ref[...] = x1
    y = x1 * lax.rsqrt(jnp.mean(x1 * x1, axis=-1, keepdims=True) + EPS) * nw_ref[...]
    h2 = y * (1.0 + mod_ref[pl.ds(r, 1), 4 * D:5 * D]) + mod_ref[pl.ds(r, 1), 3 * D:4 * D]
    h2_ref[...] = h2
    logits = _mm(h2, wr_ref[...], prec=HI)
    lt = logits.T[0:E, :]
    ex = jnp.exp(lt - jnp.max(lt, axis=0, keepdims=True))
    aff_ref[...] = ex / jnp.sum(ex, axis=0, keepdims=True)


def _out_proj(dn1, dn2, mx1, mx2, x, mod, nw, wo, wr, t1, n2, tm):
    t_all = x.shape[0]
    nt1 = t1 // tm
    spec1 = pl.BlockSpec((tm, D // 2), lambda i: (jnp.minimum(i, nt1 - 1), 0))
    spec2 = pl.BlockSpec((tm, D // 2), lambda i: (jnp.maximum(i - nt1, 0), 0))
    return pl.pallas_call(
        functools.partial(_out_kernel, t1, n2, tm),
        grid=(t_all // tm,),
        in_specs=[spec1, spec2, spec1, spec2,
                  pl.BlockSpec((tm, D), lambda i: (i, 0)),
                  pl.BlockSpec((8, NMOD * D), lambda i: (0, 0)),
                  pl.BlockSpec((1, D), lambda i: (0, 0)),
                  pl.BlockSpec((D, D), lambda i: (0, 0)),
                  pl.BlockSpec((D, 128), lambda i: (0, 0))],
        out_specs=[pl.BlockSpec((tm, D), lambda i: (i, 0)),
                   pl.BlockSpec((tm, D), lambda i: (i, 0)),
                   pl.BlockSpec((E, tm), lambda i: (0, i))],
        out_shape=[jax.ShapeDtypeStruct((t_all, D), F32),
                   jax.ShapeDtypeStruct((t_all, D), F32),
                   jax.ShapeDtypeStruct((E, t_all), F32)],
        compiler_params=_params(("arbitrary",)),
        name="out_proj",
    )(dn1, dn2, mx1, mx2, x, mod, nw, wo, wr)


SEL_TB = 256
N_SLOT_ROWS = 16


def _sel_kernel(t, cap, aff_ref, slots_ref, lo_ref, hi_ref, a3, c3):
    ntb = t // SEL_TB
    aff = aff_ref[...]

    def search(i, thr):
        cand = thr | jnp.left_shift(jnp.int32(1), 30 - i)
        cnt = jnp.sum((aff >= pltpu.bitcast(cand, F32)).astype(F32), axis=1, keepdims=True)
        return jnp.where(cnt >= cap, cand, thr)

    thr = lax.fori_loop(0, 31, search, jnp.zeros((E, 1), I32))
    gt = (aff >= pltpu.bitcast(thr + 1, F32)).astype(F32)
    eq = (aff >= pltpu.bitcast(thr, F32)).astype(F32) - gt
    need = cap - jnp.sum(gt, axis=1, keepdims=True)

    ui = _iota((SEL_TB, SEL_TB), 0)
    uj = _iota((SEL_TB, SEL_TB), 1)
    upper = (ui < uj).astype(BF16)

    def excl_cumsum(rows, dst, r0, r1):
        carry = jnp.zeros((rows.shape[0], 1), F32)
        for j in range(ntb):
            blk = rows[:, j * SEL_TB:(j + 1) * SEL_TB]
            dst[j, r0:r1, :] = _mm(blk.astype(BF16), upper) + carry
            carry = carry + jnp.sum(blk, axis=1, keepdims=True)

    excl_cumsum(eq, c3, 0, E)
    rank_eq = jnp.concatenate([c3[j, 0:E, :] for j in range(ntb)], axis=1)
    sel = jnp.maximum(gt, jnp.where(rank_eq < need, eq, 0.0))
    n_tok = jnp.sum(sel, axis=0, keepdims=True)
    excl_cumsum(jnp.concatenate([sel, jnp.broadcast_to(n_tok, (8, t))], axis=0), c3, 0, E + 8)
    ei = _iota((E, E), 0)
    ej = _iota((E, E), 1)
    below = _mm((ej < ei).astype(BF16), sel.astype(BF16))
    for j in range(ntb):
        sl = slice(j * SEL_TB, (j + 1) * SEL_TB)
        off = c3[j, E:E + 1, :]
        lo_ref[:, sl] = off
        hi_ref[:, sl] = off + n_tok[:, sl]
        rank = off + below[:, sl]
        a3[j, 0] = sel[:, sl]
        a3[j, 1] = aff[:, sl]
        a3[j, 2] = rank

    slot_iota = _iota((cap, SEL_TB), 0).astype(F32)
    tok_iota = _iota((1, SEL_TB), 1)

    def per_expert(e, carry):
        def per_block(j, acc):
            pos = c3[j, pl.ds(e, 1), :]
            chosen = a3[j, 0, pl.ds(e, 1), :]
            w = a3[j, 1, pl.ds(e, 1), :]
            rank = a3[j, 2, pl.ds(e, 1), :].astype(I32)
            tok = tok_iota + j * SEL_TB
            w_hi = w.astype(BF16).astype(F32)
            w_mid = (w - w_hi).astype(BF16).astype(F32)
            w_lo = w - w_hi - w_mid
            vals = jnp.concatenate(
                [(tok >> 7).astype(F32), (tok & 127).astype(F32), w_hi, w_mid, w_lo,
                 (rank >> 7).astype(F32), (rank & 127).astype(F32),
                 jnp.zeros((N_SLOT_ROWS - 7, SEL_TB), F32)], axis=0)
            onehot = jnp.where((slot_iota == pos) & (chosen > 0.0), 1.0, 0.0)
            return acc + _mmb(vals, onehot, NT)
        slots_ref[e] = lax.fori_loop(0, ntb, per_block, jnp.zeros((N_SLOT_ROWS, cap), F32))
        return carry

    lax.fori_loop(0, E, per_expert, 0)


def _select(aff_t, t, cap, col_blk):
    ntb = t // SEL_TB
    return pl.pallas_call(
        functools.partial(_sel_kernel, t, cap),
        grid=(1,),
        in_specs=[pl.BlockSpec((E, t), lambda i: (0, col_blk))],
        out_specs=[pl.BlockSpec((E, N_SLOT_ROWS, cap), lambda i: (0, 0, 0)),
                   pl.BlockSpec((1, t), lambda i: (0, 0)),
                   pl.BlockSpec((1, t), lambda i: (0, 0))],
        out_shape=[jax.ShapeDtypeStruct((E, N_SLOT_ROWS, cap), F32),
                   jax.ShapeDtypeStruct((1, t), F32),
                   jax.ShapeDtypeStruct((1, t), F32)],
        scratch_shapes=[pltpu.VMEM((ntb, 3, E, SEL_TB), F32),
                        pltpu.VMEM((ntb, E + 8, SEL_TB), F32)],
        compiler_params=_params(("arbitrary",)),
        name="select_t%d" % t,
    )(aff_t)


DMA_UNROLL = 8


def _ffn_kernel(rc, nch, idx_ref, dst_ref, h2_hbm, wcol_ref, wg_ref, wu_ref, wd_ref, z_hbm,
                xbuf, ybuf, wgb, wub, wdb, gsem, ssem):
    c = pl.program_id(1)
    step = pl.program_id(0) * nch + c
    nsteps = E * nch
    slot = step % 2

    def start_gather(s):
        buf = xbuf.at[s % 2]
        sem = gsem.at[s % 2]

        def issue(jb, carry):
            for u in range(DMA_UNROLL):
                j = jb * DMA_UNROLL + u
                tok = idx_ref[s * rc + j]
                pltpu.make_async_copy(h2_hbm.at[pl.ds(tok, 1)], buf.at[pl.ds(j, 1)], sem).start()
            return carry

        lax.fori_loop(0, rc // DMA_UNROLL, issue, 0)

    @pl.when(step == 0)
    def _():
        start_gather(step)

    @pl.when(step + 1 < nsteps)
    def _():
        start_gather(step + 1)

    @pl.when(c == 0)
    def _():
        wgb[...] = wg_ref[0, 0].astype(BF16)
        wub[...] = wu_ref[0, 0].astype(BF16)
        wdb[...] = wd_ref[0, 0].astype(BF16)

    pltpu.make_async_copy(h2_hbm.at[pl.ds(0, rc)], xbuf.at[slot], gsem.at[slot]).wait()
    xb = xbuf[slot].astype(BF16)
    hid = _silu(_mm(xb, wgb[...])) * _mm(xb, wub[...])
    y = _mm(hid.astype(BF16), wdb[...]) * wcol_ref[...]

    @pl.when(step > 0)
    def _():
        pltpu.make_async_copy(ybuf, z_hbm.at[pl.ds(0, rc)], ssem).wait()

    ybuf[...] = y

    def scatter(jb, carry):
        for u in range(DMA_UNROLL):
            j = jb * DMA_UNROLL + u
            row = dst_ref[step * rc + j]
            pltpu.make_async_copy(ybuf.at[pl.ds(j, 1)], z_hbm.at[pl.ds(row, 1)], ssem).start()
        return carry

    lax.fori_loop(0, rc // DMA_UNROLL, scatter, 0)

    @pl.when(step == nsteps - 1)
    def _():
        pltpu.make_async_copy(ybuf, z_hbm.at[pl.ds(0, rc)], ssem).wait()


def _expert_ffn(idx, dst, h2, wcol, w_gate, w_up, w_down, layer, rc, nch):
    zrows = E * nch * rc
    wspec = pl.BlockSpec((1, 1, D, D), lambda e, c, *_: (layer, e, 0, 0))
    return pl.pallas_call(
        functools.partial(_ffn_kernel, rc, nch),
        grid_spec=pltpu.PrefetchScalarGridSpec(
            num_scalar_prefetch=2,
            grid=(E, nch),
            in_specs=[pl.BlockSpec(memory_space=pl.ANY),
                      pl.BlockSpec((rc, 1), lambda e, c, *_: (e * nch + c, 0)),
                      wspec, wspec, wspec],
            out_specs=pl.BlockSpec(memory_space=pl.ANY),
            scratch_shapes=[pltpu.VMEM((2, rc, D), F32), pltpu.VMEM((rc, D), F32),
                            pltpu.VMEM((D, D), BF16), pltpu.VMEM((D, D), BF16),
                            pltpu.VMEM((D, D), BF16),
                            pltpu.SemaphoreType.DMA((2,)), pltpu.SemaphoreType.DMA(())]),
        out_shape=jax.ShapeDtypeStruct((zrows, D), F32),
        compiler_params=_params(("arbitrary", "arbitrary")),
        name="expert_ffn",
    )(idx, dst, h2, wcol, w_gate, w_up, w_down)


CMB_TB = 256
CMB_RC = 512


def _comb_kernel(t1, n2, zrows, final, bs_ref, be_ref, x1_ref, lo_ref, hi_ref, mod_ref, fw_ref, z_hbm,
                 o_ref, zbuf, sem):
    i = pl.program_id(0)
    nsteps = pl.num_programs(0)
    r = _mod_row(i * CMB_TB, t1, n2)

    def first_row(step):
        return (bs_ref[step] // 8) * 8

    def chunk_copy(nominal, slot):
        b = pl.multiple_of(jnp.minimum(nominal, zrows - CMB_RC), 8)
        return pltpu.make_async_copy(z_hbm.at[pl.ds(b, CMB_RC)], zbuf.at[slot], sem.at[slot])

    base0 = first_row(i)
    nchunk = jnp.maximum((be_ref[i] - base0 + CMB_RC - 1) // CMB_RC, 1)

    @pl.when(i == 0)
    def _():
        chunk_copy(base0, 0).start()

    eye = _iota((CMB_TB, CMB_TB), 0) == _iota((CMB_TB, CMB_TB), 1)
    lo = jnp.sum(jnp.where(eye, lo_ref[...], 0.0), axis=1, keepdims=True)
    hi = jnp.sum(jnp.where(eye, hi_ref[...], 0.0), axis=1, keepdims=True)
    col = _iota((1, CMB_RC), 1)

    def chunk(ci, acc):
        nominal = base0 + ci * CMB_RC
        slot = ci % 2

        @pl.when(ci + 1 < nchunk)
        def _():
            chunk_copy(nominal + CMB_RC, 1 - slot).start()

        chunk_copy(nominal, slot).wait()
        b = jnp.minimum(nominal, zrows - CMB_RC)
        rows = (col + b).astype(F32)
        pick = (rows >= lo) & (rows < hi) & (rows >= nominal.astype(F32))
        s = jnp.where(pick, 1.0, 0.0).astype(BF16)
        z = zbuf[slot]
        z_hi = z.astype(BF16)
        z_lo = (z - z_hi.astype(F32)).astype(BF16)
        return acc + _mm(s, z_hi) + _mm(s, z_lo)

    moe = lax.fori_loop(0, nchunk, chunk, jnp.zeros((CMB_TB, D), F32))

    @pl.when(i + 1 < nsteps)
    def _():
        chunk_copy(first_row(jnp.minimum(i + 1, nsteps - 1)), 0).start()

    x2 = x1_ref[...] + mod_ref[pl.ds(r, 1), 5 * D:6 * D] * moe
    if final:
        x2 = x2 * lax.rsqrt(jnp.mean(x2 * x2, axis=-1, keepdims=True) + EPS) * fw_ref[...]
    o_ref[...] = x2


def _combine(bstart, bend, x1, lo, hi, mod, fw, z, t1, n2, final):
    t_all = x1.shape[0]
    zrows = z.shape[0]
    return pl.pallas_call(
        functools.partial(_comb_kernel, t1, n2, zrows, final),
        grid_spec=pltpu.PrefetchScalarGridSpec(
            num_scalar_prefetch=2,
            grid=(t_all // CMB_TB,),
            in_specs=[pl.BlockSpec((CMB_TB, D), lambda i, *_: (i, 0)),
                      pl.BlockSpec((1, CMB_TB), lambda i, *_: (0, i)),
                      pl.BlockSpec((1, CMB_TB), lambda i, *_: (0, i)),
                      pl.BlockSpec((8, NMOD * D), lambda i, *_: (0, 0)),
                      pl.BlockSpec((1, D), lambda i, *_: (0, 0)),
                      pl.BlockSpec(memory_space=pl.ANY)],
            out_specs=pl.BlockSpec((CMB_TB, D), lambda i, *_: (i, 0)),
            scratch_shapes=[pltpu.VMEM((2, CMB_RC, D), F32), pltpu.SemaphoreType.DMA((2,))]),
        out_shape=jax.ShapeDtypeStruct((t_all, D), F32),
        compiler_params=_params(("arbitrary",)),
        name="combine",
    )(bstart, bend, x1, lo, hi, mod, fw, z)


def _grid_pos_embed(n, d):
    rows = n // GRID_W
    r = jnp.repeat(jnp.arange(rows, dtype=F32), GRID_W)
    col = jnp.tile(jnp.arange(GRID_W, dtype=F32), rows)
    quarter = d // 4
    omega = jnp.power(POS_BASE, -jnp.arange(quarter, dtype=F32) / quarter)
    ra = r[:, None] * omega
    ca = col[:, None] * omega
    return jnp.concatenate([jnp.sin(ra), jnp.cos(ra), jnp.sin(ca), jnp.cos(ca)], axis=-1)


def _decode_slots(slots):
    idx = (slots[:, 0] * 128.0 + slots[:, 1]).astype(I32)
    w = slots[:, 2] + slots[:, 3] + slots[:, 4]
    rank = (slots[:, 5] * 128.0 + slots[:, 6]).astype(I32)
    return idx, w, rank


def kernel(x_prompt, x_sample, state_delta, c, c_ctx, w_ada, b_ada, norm1_w, norm2_w, w_in, dn_conv_w,
           dn_a_log, dn_dt_bias, dn_norm_w, sc_conv_w, w_out, w_router, w_gate, w_up, w_down,
           final_norm_w):
    b1, n1, _ = x_prompt.shape
    b2, n2, _ = x_sample.shape
    depth = w_ada.shape[0]
    t1, t2 = b1 * n1, b2 * n2
    t_all = t1 + t2
    cap1 = max(1, 2 * t1 // E)
    cap2 = max(1, 2 * t2 // E)
    tm = 512
    rc = 512 if (cap1 % 512 == 0 and cap2 % 512 == 0) else 128
    assert t1 % n2 == 0 and t1 % tm == 0 and n2 % tm == 0 and n1 % (2 * CH) == 0
    assert t1 % SEL_TB == 0 and t2 % SEL_TB == 0 and t1 % t2 == 0
    assert cap1 % rc == 0 and cap2 % rc == 0 and 2 * t_all >= CMB_RC
    nch = (cap1 + cap2) // rc
    width = H * DH

    cond8 = jnp.zeros((8, D), F32).at[0].set(c_ctx).at[1:1 + b2].set(c)
    mod = _ada(cond8, w_ada, b_ada)

    x = jnp.concatenate([x_prompt.reshape(t1, D),
                         (x_sample + _grid_pos_embed(n2, D)[None]).reshape(t2, D)], axis=0)
    s_zero = jnp.zeros((b1, 2, H, DH, DH), F32)
    tabs1 = _dft_tables(n1)
    tabs2 = _dft_tables(n2)
    ctx_states = []

    for l in range(depth):
        wl = w_in[l]
        wm = jnp.concatenate([wl[:, 0:4 * width], wl[:, 4 * width + 4 * H:]], axis=1).astype(BF16)
        wg = jnp.pad(wl[:, 4 * width:4 * width + 4 * H], ((0, 0), (0, 128 - 4 * H))).astype(BF16)
        al = jnp.pad(dn_a_log[l].reshape(1, 2 * H), ((0, 0), (2 * H, 128 - 4 * H)))
        dt = jnp.pad(dn_dt_bias[l].reshape(1, 2 * H), ((0, 0), (2 * H, 128 - 4 * H)))
        p, g, gt = _in_proj(x, mod[l], norm1_w[l].reshape(1, D), wm, wg, al, dt, t1, n2, tm)

        nwd = dn_norm_w[l].reshape(1, DH)
        dn1, s_ctx = _deltanet(p, g, gt, dn_conv_w[l], nwd, s_zero, n1, 0)
        dn2, _ = _deltanet(p, g, gt, dn_conv_w[l], nwd, state_delta[:, l], n2, t1 // n2)
        ctx_states.append(s_ctx)
        mx1 = _mix2(p, sc_conv_w[l], tabs1[1], tabs1[2], tabs1[0], b1, n1, 0)
        mx2 = _mix2(p, sc_conv_w[l], tabs2[1], tabs2[2], tabs2[0], b2, n2, t1 // n2)

        wr = jnp.pad(w_router[l], ((0, 0), (0, 128 - E)))
        x1, h2, aff_t = _out_proj(dn1, dn2, mx1, mx2, x, mod[l], norm2_w[l].reshape(1, D),
                                  w_out[l].astype(BF16), wr, t1, n2, tm)

        slots1, lo1, hi1 = _select(aff_t, t1, cap1, 0)
        slots2, lo2, hi2 = _select(aff_t, t2, cap2, t1 // t2)
        idx1, wsel1, rank1 = _decode_slots(slots1)
        idx2, wsel2, rank2 = _decode_slots(slots2)
        idx = jnp.concatenate([idx1, idx2 + t1], axis=1).reshape(-1)
        dst = jnp.concatenate([rank1, rank2 + 2 * t1], axis=1).reshape(-1)
        wcol = jnp.concatenate([wsel1, wsel2], axis=1).reshape(-1, 1)
        z = _expert_ffn(idx, dst, h2, wcol, w_gate, w_up, w_down, l, rc, nch)

        lo = jnp.concatenate([lo1, lo2 + 2.0 * t1], axis=1)
        hi = jnp.concatenate([hi1, hi2 + 2.0 * t1], axis=1)
        bstart = lo[0, ::CMB_TB].astype(I32)
        bend = hi[0, CMB_TB - 1::CMB_TB].astype(I32)
        x = _combine(bstart, bend, x1, lo, hi, mod[l], final_norm_w.reshape(1, D), z, t1, n2,
                     l == depth - 1)

    y_prompt = x[:t1].reshape(b1, n1, D)
    y_sample = x[t1:].reshape(b2, n2, D)
    return y_prompt, y_sample, jnp.stack(ctx_states, axis=1)
```

```python
import functools
import math

import jax
import jax.numpy as jnp
from jax import lax
from jax.experimental import pallas as pl
from jax.experimental.pallas import tpu as pltpu

F32 = jnp.float32
BF16 = jnp.bfloat16
I32 = jnp.int32
HI = lax.Precision.HIGHEST

D = 1024
H = 4
DH = 128
CH = 64
E = 16
NMOD = 6
EPS = 1e-6
GRID_W = 64
POS_BASE = 10000.0
P_COLS = 3072
VMEM_LIMIT = 56 * 1024 * 1024

NN = (((1,), (0,)), ((), ()))
NT = (((1,), (1,)), ((), ()))
TN = (((0,), (0,)), ((), ()))


def _mm(a, b, dims=NN, prec=None):
    return lax.dot_general(a, b, dims, precision=prec, preferred_element_type=F32)


def _mmb(a, b, dims=NN):
    return lax.dot_general(a.astype(BF16), b.astype(BF16), dims, preferred_element_type=F32)


def _mm_inv(a, b):
    return _mmb(a, b)


def _silu(x):
    return x / (1.0 + jnp.exp(-x))


def _sigmoid(x):
    return 1.0 / (1.0 + jnp.exp(-x))


def _softplus(x):
    return jnp.maximum(x, 0.0) + jnp.log1p(jnp.exp(-jnp.abs(x)))


def _iota(shape, dim):
    return lax.broadcasted_iota(I32, shape, dim)


def _mod_row(t0, t1, n2):
    return jnp.where(t0 < t1, 0, 1 + jnp.maximum(t0 - t1, 0) // n2)


def _params(sem):
    return pltpu.CompilerParams(dimension_semantics=sem, vmem_limit_bytes=VMEM_LIMIT)


def _ada_kernel(c_ref, w_ref, b_ref, o_ref):
    s = _silu(c_ref[...])
    o_ref[0] = _mm(s, w_ref[0], prec=HI) + b_ref[0]


def _ada(cond8, w_ada, b_ada):
    depth = w_ada.shape[0]
    tn = 1536
    return pl.pallas_call(
        _ada_kernel,
        grid=(depth, NMOD * D // tn),
        in_specs=[pl.BlockSpec((8, D), lambda l, j: (0, 0)),
                  pl.BlockSpec((1, D, tn), lambda l, j: (l, 0, j)),
                  pl.BlockSpec((1, 1, tn), lambda l, j: (l, 0, j))],
        out_specs=pl.BlockSpec((1, 8, tn), lambda l, j: (l, 0, j)),
        out_shape=jax.ShapeDtypeStruct((depth, 8, NMOD * D), F32),
        compiler_params=_params(("arbitrary", "arbitrary")),
        name="ada",
    )(cond8, w_ada, b_ada.reshape(depth, 1, NMOD * D))


def _in_kernel(t1, n2, tm, x_ref, mod_ref, nw_ref, wm_ref, wg_ref, al_ref, dt_ref,
               p_ref, g_ref, gt_ref):
    r = _mod_row(pl.program_id(0) * tm, t1, n2)
    x = x_ref[...]
    y = x * lax.rsqrt(jnp.mean(x * x, axis=-1, keepdims=True) + EPS) * nw_ref[...]
    sh = mod_ref[pl.ds(r, 1), 0:D]
    sc = mod_ref[pl.ds(r, 1), D:2 * D]
    hb = (y * (1.0 + sc) + sh).astype(BF16)
    nsplit = 4
    wc = P_COLS // nsplit
    for j in range(nsplit):
        p_ref[:, j * wc:(j + 1) * wc] = _mm(hb, wm_ref[:, j * wc:(j + 1) * wc])
    raw = _mm(hb, wg_ref[...])
    lane = _iota((1, 128), 1)
    g = -jnp.exp(al_ref[...]) * _softplus(raw + dt_ref[...])
    act = jnp.where(lane < 2 * H, _sigmoid(raw), g)
    g_ref[...] = act[:, 0:4 * H]
    act_t = act.T
    for j in range(tm // CH):
        gt_ref[j] = act_t[0:4 * H, j * CH:(j + 1) * CH]


def _in_proj(x, mod, nw, wm, wg, al, dt, t1, n2, tm):
    t_all = x.shape[0]
    return pl.pallas_call(
        functools.partial(_in_kernel, t1, n2, tm),
        grid=(t_all // tm,),
        in_specs=[pl.BlockSpec((tm, D), lambda i: (i, 0)),
                  pl.BlockSpec((8, NMOD * D), lambda i: (0, 0)),
                  pl.BlockSpec((1, D), lambda i: (0, 0)),
                  pl.BlockSpec((D, P_COLS), lambda i: (0, 0)),
                  pl.BlockSpec((D, 128), lambda i: (0, 0)),
                  pl.BlockSpec((1, 128), lambda i: (0, 0)),
                  pl.BlockSpec((1, 128), lambda i: (0, 0))],
        out_specs=[pl.BlockSpec((tm, P_COLS), lambda i: (i, 0)),
                   pl.BlockSpec((tm, 4 * H), lambda i: (i, 0)),
                   pl.BlockSpec((tm // CH, 4 * H, CH), lambda i: (i, 0, 0))],
        out_shape=[jax.ShapeDtypeStruct((t_all, P_COLS), F32),
                   jax.ShapeDtypeStruct((t_all, 4 * H), F32),
                   jax.ShapeDtypeStruct((t_all // CH, 4 * H, CH), F32)],
        compiler_params=_params(("arbitrary",)),
        name="in_proj",
    )(x, mod, nw, wm, wg, al, dt)


def _shift_conv(x, w_ref, c0, c1):
    n = x.shape[0]
    row = _iota((n, 1), 0)
    prev = jnp.where(row == 0, 0.0, pltpu.roll(x, 1, 0))
    nxt = jnp.where(row == n - 1, 0.0, pltpu.roll(x, n - 1, 0))
    return prev * w_ref[0:1, c0:c1] + x * w_ref[1:2, c0:c1] + nxt * w_ref[2:3, c0:c1]


DN_POS = 4


def _dn_kernel(n, q_ref, k_ref, v_ref, z_ref, g_ref, gt_ref, cw_ref, nw_ref, s0_ref,
               o_ref, sfin_ref, qs, ks, vs, gcc_s, gct_s, s_s, oacc):
    nc = n // CH
    width = H * DH

    for hh in range(H):
        c0, c1 = hh * DH, (hh + 1) * DH
        q = _silu(_shift_conv(q_ref[:, c0:c1], cw_ref, c0, c1))
        q = q * lax.rsqrt(jnp.sum(q * q, axis=-1, keepdims=True) + 1e-6)
        qs[:, c0:c1] = q * (DH ** -0.5)
        k = _silu(_shift_conv(k_ref[:, c0:c1], cw_ref, width + c0, width + c1))
        ks[:, c0:c1] = k * lax.rsqrt(jnp.sum(k * k, axis=-1, keepdims=True) + 1e-6)
        vs[:, c0:c1] = _silu(_shift_conv(v_ref[:, c0:c1], cw_ref, 2 * width + c0, 2 * width + c1))

    ii = _iota((CH, CH), 0)
    jj = _iota((CH, CH), 1)
    x = gt_ref[...].reshape(nc * 4 * H, CH)
    cf = _mm(x, (ii <= jj).astype(F32), prec=HI)
    cb = _mm(x, (ii >= jj).astype(F32), prec=HI)
    rsel = (_iota((nc * 4 * H, 1), 0) & (4 * H - 1)) < 3 * H
    gct_s[...] = jnp.where(rsel, cf, cb).reshape(nc, 4 * H, CH)
    i2 = _iota((2 * CH, 2 * CH), 0)
    j2 = _iota((2 * CH, 2 * CH), 1)
    same = (i2 >= CH) == (j2 >= CH)
    lf = (same & (i2 >= j2)).astype(F32)
    lb = (same & (i2 <= j2)).astype(F32)
    lsel = _iota((1, 4 * H), 1) < 3 * H
    for m in range(n // (2 * CH)):
        y = g_ref[m * 2 * CH:(m + 1) * 2 * CH, :]
        gcc_s[m * 2 * CH:(m + 1) * 2 * CH, :] = jnp.where(lsel, _mm(lf, y, prec=HI), _mm(lb, y, prec=HI))

    for hh in range(H):
        for d in range(2):
            s_s[hh * 2 + d] = s0_ref[0, d, hh]

    eye = (ii == jj).astype(F32)
    blk_same = [(ii >> s) == (jj >> s) for s in range(1, 7)]

    chains = [(hh, d) for hh in range(H) for d in range(2)]

    def chunk_step(cc, carry):
        st = []
        for p in range(DN_POS):
            for hh, d in chains:
                c = cc * DN_POS + p
                cidx = c if d == 0 else nc - 1 - c
                r0 = pl.multiple_of(cidx * CH, CH)
                c0, c1 = hh * DH, (hh + 1) * DH
                gcol = 2 * H + H * d + hh
                bcol = H * d + hh
                incl = (ii >= jj) if d == 0 else (ii <= jj)
                strict = (ii > jj) if d == 0 else (ii < jj)
                qc = qs[pl.ds(r0, CH), c0:c1]
                kc = ks[pl.ds(r0, CH), c0:c1]
                vc = vs[pl.ds(r0, CH), c0:c1]
                beta = g_ref[pl.ds(r0, CH), bcol:bcol + 1]
                gc = gcc_s[pl.ds(r0, CH), gcol:gcol + 1]
                gr = gct_s[cidx, gcol:gcol + 1, :]
                tot = gr[:, CH - 1:CH] if d == 0 else gr[:, 0:1]
                decay = jnp.where(incl, jnp.exp(jnp.where(incl, gc - gr, 0.0)), 0.0)
                kb = kc * beta
                egc = jnp.exp(gc)
                st.append(dict(r0=r0, c0=c0, c1=c1, d=d, strict=strict, kc=kc, kb=kb, decay=decay,
                               rhs=jnp.concatenate([vc * beta, kb * egc], axis=1),
                               qd=qc * egc, qc=qc, kd=kc * jnp.exp(tot - gc), gl=jnp.exp(tot)))
        for x in st:
            x["lmat"] = jnp.where(x["strict"], _mmb(x["kb"], x["kc"], NT) * x["decay"], 0.0)
            x["qk"] = _mmb(x["qc"], x["kc"], NT) * x["decay"]
            x["tinv"] = eye - jnp.where(blk_same[0], x["lmat"], 0.0)
        for lvl in range(1, len(blk_same)):
            for x in st:
                off = jnp.where(blk_same[lvl] & ~blk_same[lvl - 1], x["lmat"], 0.0)
                x["ot"] = _mm_inv(off, x["tinv"])
            for x in st:
                x["tinv"] = x["tinv"] - _mm_inv(x["tinv"], x["ot"])
        for x in st:
            x["sol"] = _mm_inv(x["tinv"], x["rhs"])
        state = [s_s[i] for i in range(len(chains))]
        for p in range(DN_POS):
            units = st[p * len(chains):(p + 1) * len(chains)]
            for i, x in enumerate(units):
                x["sb"] = state[i].astype(BF16)
                x["v_new"] = x["sol"][:, 0:DH] - _mmb(x["sol"][:, DH:2 * DH], x["sb"])
            for i, x in enumerate(units):
                o = _mmb(x["qd"], x["sb"]) + _mmb(x["qk"], x["v_new"])
                state[i] = state[i] * x["gl"] + _mmb(x["kd"], x["v_new"], TN)
                oacc[x["d"], pl.ds(x["r0"], CH), x["c0"]:x["c1"]] = o
        for i in range(len(chains)):
            s_s[i] = state[i]
        return carry

    lax.fori_loop(0, nc // DN_POS, chunk_step, 0)

    for hh in range(H):
        c0, c1 = hh * DH, (hh + 1) * DH
        o = oacc[0, :, c0:c1] + oacc[1, :, c0:c1]
        o = o * lax.rsqrt(jnp.mean(o * o, axis=-1, keepdims=True) + EPS)
        o_ref[:, c0:c1] = o * nw_ref[...] * _silu(z_ref[:, c0:c1])
        for d in range(2):
            sfin_ref[0, d, hh] = s_s[hh * 2 + d]


def _deltanet(p, g, gt, conv_w, norm_w, s0, n, row_blk0):
    bsz = s0.shape[0]
    nc = n // CH
    width = H * DH
    return pl.pallas_call(
        functools.partial(_dn_kernel, n),
        grid=(bsz,),
        in_specs=[pl.BlockSpec((n, width), lambda b: (row_blk0 + b, 0)),
                  pl.BlockSpec((n, width), lambda b: (row_blk0 + b, 1)),
                  pl.BlockSpec((n, width), lambda b: (row_blk0 + b, 2)),
                  pl.BlockSpec((n, width), lambda b: (row_blk0 + b, 3)),
                  pl.BlockSpec((n, 4 * H), lambda b: (row_blk0 + b, 0)),
                  pl.BlockSpec((nc, 4 * H, CH), lambda b: (row_blk0 + b, 0, 0)),
                  pl.BlockSpec((3, 3 * width), lambda b: (0, 0)),
                  pl.BlockSpec((1, DH), lambda b: (0, 0)),
                  pl.BlockSpec((1, 2, H, DH, DH), lambda b: (b, 0, 0, 0, 0))],
        out_specs=[pl.BlockSpec((n, width), lambda b: (b, 0)),
                   pl.BlockSpec((1, 2, H, DH, DH), lambda b: (b, 0, 0, 0, 0))],
        out_shape=[jax.ShapeDtypeStruct((bsz * n, width), F32),
                   jax.ShapeDtypeStruct((bsz, 2, H, DH, DH), F32)],
        scratch_shapes=[pltpu.VMEM((n, width), F32), pltpu.VMEM((n, width), F32),
                        pltpu.VMEM((n, width), F32), pltpu.VMEM((n, 4 * H), F32),
                        pltpu.VMEM((nc, 4 * H, CH), F32), pltpu.VMEM((2 * H, DH, DH), F32),
                        pltpu.VMEM((2, n, width), F32)],
        compiler_params=_params(("arbitrary",)),
        name="deltanet_n%d" % n,
    )(p, p, p, p, g, gt, conv_w, norm_w, s0)


def _mix2_kernel(sb_ref, sc_ref, su_ref, fu_ref, cw_ref, bdc_ref, bds_ref, cs_ref, o_ref):
    w = sb_ref.shape[1]
    cu = sc_ref[...] * su_ref[...]
    o_ref[:, 0:w] = sb_ref[...] * _shift_conv(cu, cw_ref, 0, w)
    fu = fu_ref[...]
    a = _mm(fu, bdc_ref[...], prec=HI)
    b = _mm(fu, bds_ref[...], prec=HI)
    o_ref[:, w:2 * w] = _mm(cs_ref[...], jnp.concatenate([a, b], axis=0), prec=HI)


def _mix2(p, conv_w, bdc, bds, cs, bsz, n, row_blk0):
    w = 256
    return pl.pallas_call(
        _mix2_kernel,
        grid=(bsz,),
        in_specs=[pl.BlockSpec((n, w), lambda b: (row_blk0 + b, 8)),
                  pl.BlockSpec((n, w), lambda b: (row_blk0 + b, 9)),
                  pl.BlockSpec((n, w), lambda b: (row_blk0 + b, 10)),
                  pl.BlockSpec((n, w), lambda b: (row_blk0 + b, 11)),
                  pl.BlockSpec((3, w), lambda b: (0, 0)),
                  pl.BlockSpec((w, w), lambda b: (0, 0)),
                  pl.BlockSpec((w, w), lambda b: (0, 0)),
                  pl.BlockSpec((n, 2 * n), lambda b: (0, 0))],
        out_specs=pl.BlockSpec((n, 2 * w), lambda b: (b, 0)),
        out_shape=jax.ShapeDtypeStruct((bsz * n, 2 * w), F32),
        compiler_params=_params(("arbitrary",)),
        name="conv_fourier_n%d" % n,
    )(p, p, p, p, conv_w, bdc, bds, cs)


def _dft_tables(n):
    gw = 64
    k = jnp.arange(n, dtype=I32)
    ph = (k[:, None] * k[None, :]) % n
    ang = ph.astype(F32) * (2.0 * math.pi / n)
    scale = 1.0 / math.sqrt(n * gw)
    cs = jnp.concatenate([jnp.cos(ang), -jnp.sin(ang)], axis=1) * scale
    c = jnp.arange(256, dtype=I32)
    phc = ((c[:, None] % gw) * (c[None, :] % gw)) % gw
    angc = phc.astype(F32) * (2.0 * math.pi / gw)
    same = (c[:, None] // gw) == (c[None, :] // gw)
    bdc = jnp.where(same, jnp.cos(angc), 0.0)
    bds = jnp.where(same, jnp.sin(angc), 0.0)
    return cs, bdc, bds


def _out_kernel(t1, n2, tm, dn1_ref, dn2_ref, mx1_ref, mx2_ref, x_ref, mod_ref, nw_ref, wo_ref, wr_ref,
                x1_ref, h2_ref, aff_ref):
    t0 = pl.program_id(0) * tm
    r = _mod_row(t0, t1, n2)
    half = D // 2
    dn = jnp.where(t0 < t1, dn1_ref[...], dn2_ref[...])
    mx = jnp.where(t0 < t1, mx1_ref[...], mx2_ref[...])
    mix = _mmb(dn, wo_ref[0:half, :]) + _mmb(mx, wo_ref[half:D, :])
    x1 = x_ref[...] + mod_ref[pl.ds(r, 1), 2 * D:3 * D] * mix
    x1_ref[...] = x1
    y = x1 * lax.rsqrt(jnp.mean(x1 * x1, axis=-1, keepdims=True) + EPS) * nw_ref[...]
    h2 = y * (1.0 + mod_ref[pl.ds(r, 1), 4 * D:5 * D]) + mod_ref[pl.ds(r, 1), 3 * D:4 * D]
    h2_ref[...] = h2
    logits = _mm(h2, wr_ref[...], prec=HI)
    lt = logits.T[0:E, :]
    ex = jnp.exp(lt - jnp.max(lt, axis=0, keepdims=True))
    aff_ref[...] = ex / jnp.sum(ex, axis=0, keepdims=True)


def _out_proj(dn1, dn2, mx1, mx2, x, mod, nw, wo, wr, t1, n2, tm):
    t_all = x.shape[0]
    nt1 = t1 // tm
    spec1 = pl.BlockSpec((tm, D // 2), lambda i: (jnp.minimum(i, nt1 - 1), 0))
    spec2 = pl.BlockSpec((tm, D // 2), lambda i: (jnp.maximum(i - nt1, 0), 0))
    return pl.pallas_call(
        functools.partial(_out_kernel, t1, n2, tm),
        grid=(t_all // tm,),
        in_specs=[spec1, spec2, spec1, spec2,
                  pl.BlockSpec((tm, D), lambda i: (i, 0)),
                  pl.BlockSpec((8, NMOD * D), lambda i: (0, 0)),
                  pl.BlockSpec((1, D), lambda i: (0, 0)),
                  pl.BlockSpec((D, D), lambda i: (0, 0)),
                  pl.BlockSpec((D, 128), lambda i: (0, 0))],
        out_specs=[pl.BlockSpec((tm, D), lambda i: (i, 0)),
                   pl.BlockSpec((tm, D), lambda i: (i, 0)),
                   pl.BlockSpec((E, tm), lambda i: (0, i))],
        out_shape=[jax.ShapeDtypeStruct((t_all, D), F32),
                   jax.ShapeDtypeStruct((t_all, D), F32),
                   jax.ShapeDtypeStruct((E, t_all), F32)],
        compiler_params=_params(("arbitrary",)),
        name="out_proj",
    )(dn1, dn2, mx1, mx2, x, mod, nw, wo, wr)


SEL_TB = 128
N_SLOT_ROWS = 16
SPILL_BLOCKS = 2


def _sel_kernel(t, cap, aff_ref, slots_ref, lo_ref, hi_ref, a3, c3, starts_v, starts_s, sem):
    ntb = t // SEL_TB
    aff = aff_ref[...]

    def search(i, thr):
        cand = thr | jnp.left_shift(jnp.int32(1), 30 - i)
        cnt = jnp.sum((aff >= pltpu.bitcast(cand, F32)).astype(F32), axis=1, keepdims=True)
        return jnp.where(cnt >= cap, cand, thr)

    thr = lax.fori_loop(0, 31, search, jnp.zeros((E, 1), I32))
    gt = (aff >= pltpu.bitcast(thr + 1, F32)).astype(F32)
    eq = (aff >= pltpu.bitcast(thr, F32)).astype(F32) - gt
    need = cap - jnp.sum(gt, axis=1, keepdims=True)

    ui = _iota((SEL_TB, SEL_TB), 0)
    uj = _iota((SEL_TB, SEL_TB), 1)
    upper = (ui < uj).astype(BF16)
    blk_lane = _iota((1, 128), 1)

    def excl_cumsum(rows, dst, r0, r1):
        carry = jnp.zeros((rows.shape[0], 1), F32)
        starts = jnp.zeros((rows.shape[0], 128), F32)
        for j in range(ntb):
            blk = rows[:, j * SEL_TB:(j + 1) * SEL_TB]
            dst[j, r0:r1, :] = _mm(blk.astype(BF16), upper) + carry
            starts = jnp.where(blk_lane == j, carry, starts)
            carry = carry + jnp.sum(blk, axis=1, keepdims=True)
        return starts

    excl_cumsum(eq, c3, 0, E)
    rank_eq = jnp.concatenate([c3[j, 0:E, :] for j in range(ntb)], axis=1)
    sel = jnp.maximum(gt, jnp.where(rank_eq < need, eq, 0.0))
    n_tok = jnp.sum(sel, axis=0, keepdims=True)
    starts = excl_cumsum(jnp.concatenate([sel, jnp.broadcast_to(n_tok, (8, t))], axis=0), c3, 0, E + 8)
    starts_v[...] = starts.astype(I32)
    to_smem = pltpu.make_async_copy(starts_v, starts_s, sem)
    to_smem.start()
    ei = _iota((E, E), 0)
    ej = _iota((E, E), 1)
    below = _mm((ej < ei).astype(BF16), sel.astype(BF16))
    for j in range(ntb):
        sl = slice(j * SEL_TB, (j + 1) * SEL_TB)
        off = c3[j, E:E + 1, :]
        lo_ref[:, sl] = off
        hi_ref[:, sl] = off + n_tok[:, sl]
        rank = off + below[:, sl]
        a3[j, 0] = sel[:, sl]
        a3[j, 1] = aff[:, sl]
        a3[j, 2] = rank
    slots_ref[...] = jnp.zeros(slots_ref.shape, F32)
    to_smem.wait()

    win_iota = _iota((2 * SEL_TB, SEL_TB), 0)
    tok_iota = _iota((1, SEL_TB), 1)

    def per_expert(e, carry):
        def per_block(j, carry2):
            wb = starts_s[e, j] >> 7
            pos = c3[j, pl.ds(e, 1), :].astype(I32) - wb * SEL_TB
            chosen = a3[j, 0, pl.ds(e, 1), :]
            w = a3[j, 1, pl.ds(e, 1), :]
            rank = a3[j, 2, pl.ds(e, 1), :].astype(I32)
            tok = tok_iota + j * SEL_TB
            w_hi = w.astype(BF16).astype(F32)
            w_mid = (w - w_hi).astype(BF16).astype(F32)
            w_lo = w - w_hi - w_mid
            vals = jnp.concatenate(
                [(tok >> 7).astype(F32), (tok & 127).astype(F32), w_hi, w_mid, w_lo,
                 (rank >> 7).astype(F32), (rank & 127).astype(F32),
                 jnp.zeros((N_SLOT_ROWS - 7, SEL_TB), F32)], axis=0)
            onehot = jnp.where((win_iota == pos) & (chosen > 0.0), 1.0, 0.0)
            placed = _mmb(vals, onehot, NT)
            slots_ref[e, wb] += placed[:, 0:SEL_TB]
            slots_ref[e, wb + 1] += placed[:, SEL_TB:2 * SEL_TB]
            return carry2
        return lax.fori_loop(0, ntb, per_block, carry, unroll=8)

    lax.fori_loop(0, E, per_expert, 0)


def _select(aff_t, t, cap, col_blk):
    ntb = t // SEL_TB
    nsb = cap // SEL_TB + SPILL_BLOCKS
    return pl.pallas_call(
        functools.partial(_sel_kernel, t, cap),
        grid=(1,),
        in_specs=[pl.BlockSpec((E, t), lambda i: (0, col_blk))],
        out_specs=[pl.BlockSpec((E, nsb, N_SLOT_ROWS, SEL_TB), lambda i: (0, 0, 0, 0)),
                   pl.BlockSpec((1, t), lambda i: (0, 0)),
                   pl.BlockSpec((1, t), lambda i: (0, 0))],
        out_shape=[jax.ShapeDtypeStruct((E, nsb, N_SLOT_ROWS, SEL_TB), F32),
                   jax.ShapeDtypeStruct((1, t), F32),
                   jax.ShapeDtypeStruct((1, t), F32)],
        scratch_shapes=[pltpu.VMEM((ntb, 3, E, SEL_TB), F32),
                        pltpu.VMEM((ntb, E + 8, SEL_TB), F32),
                        pltpu.VMEM((E + 8, 128), I32),
                        pltpu.SMEM((E + 8, 128), I32),
                        pltpu.SemaphoreType.DMA(())],
        compiler_params=_params(("arbitrary",)),
        name="select_t%d" % t,
    )(aff_t)


DMA_UNROLL = 8


def _ffn_kernel(rc, nch, idx_ref, dst_ref, h2_hbm, wcol_ref, wg_ref, wu_ref, wd_ref, z_hbm,
                xbuf, ybuf, wgb, wub, wdb, gsem, ssem):
    c = pl.program_id(1)
    step = pl.program_id(0) * nch + c
    nsteps = E * nch
    slot = step % 2

    def start_gather(s):
        buf = xbuf.at[s % 2]
        sem = gsem.at[s % 2]

        def issue(jb, carry):
            for u in range(DMA_UNROLL):
                j = jb * DMA_UNROLL + u
                tok = idx_ref[s * rc + j]
                pltpu.make_async_copy(h2_hbm.at[pl.ds(tok, 1)], buf.at[pl.ds(j, 1)], sem).start()
            return carry

        lax.fori_loop(0, rc // DMA_UNROLL, issue, 0)

    @pl.when(step == 0)
    def _():
        start_gather(step)

    @pl.when(step + 1 < nsteps)
    def _():
        start_gather(step + 1)

    @pl.when(c == 0)
    def _():
        wgb[...] = wg_ref[0, 0].astype(BF16)
        wub[...] = wu_ref[0, 0].astype(BF16)
        wdb[...] = wd_ref[0, 0].astype(BF16)

    pltpu.make_async_copy(h2_hbm.at[pl.ds(0, rc)], xbuf.at[slot], gsem.at[slot]).wait()
    xb = xbuf[slot].astype(BF16)
    hid = _silu(_mm(xb, wgb[...])) * _mm(xb, wub[...])
    y = _mm(hid.astype(BF16), wdb[...]) * wcol_ref[...]

    @pl.when(step > 0)
    def _():
        pltpu.make_async_copy(ybuf, z_hbm.at[pl.ds(0, rc)], ssem).wait()

    ybuf[...] = y

    def scatter(jb, carry):
        for u in range(DMA_UNROLL):
            j = jb * DMA_UNROLL + u
            row = dst_ref[step * rc + j]
            pltpu.make_async_copy(ybuf.at[pl.ds(j, 1)], z_hbm.at[pl.ds(row, 1)], ssem).start()
        return carry

    lax.fori_loop(0, rc // DMA_UNROLL, scatter, 0)

    @pl.when(step == nsteps - 1)
    def _():
        pltpu.make_async_copy(ybuf, z_hbm.at[pl.ds(0, rc)], ssem).wait()


def _expert_ffn(idx, dst, h2, wcol, w_gate, w_up, w_down, layer, rc, nch):
    zrows = E * nch * rc
    wspec = pl.BlockSpec((1, 1, D, D), lambda e, c, *_: (layer, e, 0, 0))
    return pl.pallas_call(
        functools.partial(_ffn_kernel, rc, nch),
        grid_spec=pltpu.PrefetchScalarGridSpec(
            num_scalar_prefetch=2,
            grid=(E, nch),
            in_specs=[pl.BlockSpec(memory_space=pl.ANY),
                      pl.BlockSpec((rc, 1), lambda e, c, *_: (e * nch + c, 0)),
                      wspec, wspec, wspec],
            out_specs=pl.BlockSpec(memory_space=pl.ANY),
            scratch_shapes=[pltpu.VMEM((2, rc, D), F32), pltpu.VMEM((rc, D), F32),
                            pltpu.VMEM((D, D), BF16), pltpu.VMEM((D, D), BF16),
                            pltpu.VMEM((D, D), BF16),
                            pltpu.SemaphoreType.DMA((2,)), pltpu.SemaphoreType.DMA(())]),
        out_shape=jax.ShapeDtypeStruct((zrows, D), F32),
        compiler_params=_params(("arbitrary", "arbitrary")),
        name="expert_ffn",
    )(idx, dst, h2, wcol, w_gate, w_up, w_down)


CMB_TB = 256
CMB_RC = 512


def _comb_kernel(t1, n2, zrows, final, bs_ref, be_ref, x1_ref, lo_ref, hi_ref, mod_ref, fw_ref, z_hbm,
                 o_ref, zbuf, sem):
    i = pl.program_id(0)
    nsteps = pl.num_programs(0)
    r = _mod_row(i * CMB_TB, t1, n2)

    def first_row(step):
        return (bs_ref[step] // 8) * 8

    def chunk_copy(nominal, slot):
        b = pl.multiple_of(jnp.minimum(nominal, zrows - CMB_RC), 8)
        return pltpu.make_async_copy(z_hbm.at[pl.ds(b, CMB_RC)], zbuf.at[slot], sem.at[slot])

    base0 = first_row(i)
    nchunk = jnp.maximum((be_ref[i] - base0 + CMB_RC - 1) // CMB_RC, 1)

    @pl.when(i == 0)
    def _():
        chunk_copy(base0, 0).start()

    eye = _iota((CMB_TB, CMB_TB), 0) == _iota((CMB_TB, CMB_TB), 1)
    lo = jnp.sum(jnp.where(eye, lo_ref[...], 0.0), axis=1, keepdims=True)
    hi = jnp.sum(jnp.where(eye, hi_ref[...], 0.0), axis=1, keepdims=True)
    col = _iota((1, CMB_RC), 1)

    def chunk(ci, acc):
        nominal = base0 + ci * CMB_RC
        slot = ci % 2

        @pl.when(ci + 1 < nchunk)
        def _():
            chunk_copy(nominal + CMB_RC, 1 - slot).start()

        chunk_copy(nominal, slot).wait()
        b = jnp.minimum(nominal, zrows - CMB_RC)
        rows = (col + b).astype(F32)
        pick = (rows >= lo) & (rows < hi) & (rows >= nominal.astype(F32))
        s = jnp.where(pick, 1.0, 0.0).astype(BF16)
        z = zbuf[slot]
        z_hi = z.astype(BF16)
        z_lo = (z - z_hi.astype(F32)).astype(BF16)
        return acc + _mm(s, z_hi) + _mm(s, z_lo)

    moe = lax.fori_loop(0, nchunk, chunk, jnp.zeros((CMB_TB, D), F32))

    @pl.when(i + 1 < nsteps)
    def _():
        chunk_copy(first_row(jnp.minimum(i + 1, nsteps - 1)), 0).start()

    x2 = x1_ref[...] + mod_ref[pl.ds(r, 1), 5 * D:6 * D] * moe
    if final:
        x2 = x2 * lax.rsqrt(jnp.mean(x2 * x2, axis=-1, keepdims=True) + EPS) * fw_ref[...]
    o_ref[...] = x2


def _combine(bstart, bend, x1, lo, hi, mod, fw, z, t1, n2, final):
    t_all = x1.shape[0]
    zrows = z.shape[0]
    return pl.pallas_call(
        functools.partial(_comb_kernel, t1, n2, zrows, final),
        grid_spec=pltpu.PrefetchScalarGridSpec(
            num_scalar_prefetch=2,
            grid=(t_all // CMB_TB,),
            in_specs=[pl.BlockSpec((CMB_TB, D), lambda i, *_: (i, 0)),
                      pl.BlockSpec((1, CMB_TB), lambda i, *_: (0, i)),
                      pl.BlockSpec((1, CMB_TB), lambda i, *_: (0, i)),
                      pl.BlockSpec((8, NMOD * D), lambda i, *_: (0, 0)),
                      pl.BlockSpec((1, D), lambda i, *_: (0, 0)),
                      pl.BlockSpec(memory_space=pl.ANY)],
            out_specs=pl.BlockSpec((CMB_TB, D), lambda i, *_: (i, 0)),
            scratch_shapes=[pltpu.VMEM((2, CMB_RC, D), F32), pltpu.SemaphoreType.DMA((2,))]),
        out_shape=jax.ShapeDtypeStruct((t_all, D), F32),
        compiler_params=_params(("arbitrary",)),
        name="combine",
    )(bstart, bend, x1, lo, hi, mod, fw, z)


def _grid_pos_embed(n, d):
    rows = n // GRID_W
    r = jnp.repeat(jnp.arange(rows, dtype=F32), GRID_W)
    col = jnp.tile(jnp.arange(GRID_W, dtype=F32), rows)
    quarter = d // 4
    omega = jnp.power(POS_BASE, -jnp.arange(quarter, dtype=F32) / quarter)
    ra = r[:, None] * omega
    ca = col[:, None] * omega
    return jnp.concatenate([jnp.sin(ra), jnp.cos(ra), jnp.sin(ca), jnp.cos(ca)], axis=-1)


def _decode_slots(slots):
    e, nsb, rows, tb = slots.shape
    nsb -= SPILL_BLOCKS
    slots = slots[:, :nsb].transpose(0, 2, 1, 3).reshape(e, rows, nsb * tb)
    idx = (slots[:, 0] * 128.0 + slots[:, 1]).astype(I32)
    w = slots[:, 2] + slots[:, 3] + slots[:, 4]
    rank = (slots[:, 5] * 128.0 + slots[:, 6]).astype(I32)
    return idx, w, rank


def kernel(x_prompt, x_sample, state_delta, c, c_ctx, w_ada, b_ada, norm1_w, norm2_w, w_in, dn_conv_w,
           dn_a_log, dn_dt_bias, dn_norm_w, sc_conv_w, w_out, w_router, w_gate, w_up, w_down,
           final_norm_w):
    b1, n1, _ = x_prompt.shape
    b2, n2, _ = x_sample.shape
    depth = w_ada.shape[0]
    t1, t2 = b1 * n1, b2 * n2
    t_all = t1 + t2
    cap1 = max(1, 2 * t1 // E)
    cap2 = max(1, 2 * t2 // E)
    tm = 512
    rc = 512 if (cap1 % 512 == 0 and cap2 % 512 == 0) else 128
    assert t1 % n2 == 0 and t1 % tm == 0 and n2 % tm == 0
    assert n1 % (DN_POS * CH) == 0 and n2 % (DN_POS * CH) == 0
    assert t1 % SEL_TB == 0 and t2 % SEL_TB == 0 and t1 % t2 == 0
    assert cap1 % rc == 0 and cap2 % rc == 0 and 2 * t_all >= CMB_RC
    nch = (cap1 + cap2) // rc
    width = H * DH

    cond8 = jnp.zeros((8, D), F32).at[0].set(c_ctx).at[1:1 + b2].set(c)
    mod = _ada(cond8, w_ada, b_ada)

    x = jnp.concatenate([x_prompt.reshape(t1, D),
                         (x_sample + _grid_pos_embed(n2, D)[None]).reshape(t2, D)], axis=0)
    s_zero = jnp.zeros((b1, 2, H, DH, DH), F32)
    tabs1 = _dft_tables(n1)
    tabs2 = _dft_tables(n2)
    ctx_states = []

    for l in range(depth):
        wl = w_in[l]
        wm = jnp.concatenate([wl[:, 0:4 * width], wl[:, 4 * width + 4 * H:]], axis=1).astype(BF16)
        wg = jnp.pad(wl[:, 4 * width:4 * width + 4 * H], ((0, 0), (0, 128 - 4 * H))).astype(BF16)
        al = jnp.pad(dn_a_log[l].reshape(1, 2 * H), ((0, 0), (2 * H, 128 - 4 * H)))
        dt = jnp.pad(dn_dt_bias[l].reshape(1, 2 * H), ((0, 0), (2 * H, 128 - 4 * H)))
        p, g, gt = _in_proj(x, mod[l], norm1_w[l].reshape(1, D), wm, wg, al, dt, t1, n2, tm)

        nwd = dn_norm_w[l].reshape(1, DH)
        dn1, s_ctx = _deltanet(p, g, gt, dn_conv_w[l], nwd, s_zero, n1, 0)
        dn2, _ = _deltanet(p, g, gt, dn_conv_w[l], nwd, state_delta[:, l], n2, t1 // n2)
        ctx_states.append(s_ctx)
        mx1 = _mix2(p, sc_conv_w[l], tabs1[1], tabs1[2], tabs1[0], b1, n1, 0)
        mx2 = _mix2(p, sc_conv_w[l], tabs2[1], tabs2[2], tabs2[0], b2, n2, t1 // n2)

        wr = jnp.pad(w_router[l], ((0, 0), (0, 128 - E)))
        x1, h2, aff_t = _out_proj(dn1, dn2, mx1, mx2, x, mod[l], norm2_w[l].reshape(1, D),
                                  w_out[l].astype(BF16), wr, t1, n2, tm)

        slots1, lo1, hi1 = _select(aff_t, t1, cap1, 0)
        slots2, lo2, hi2 = _select(aff_t, t2, cap2, t1 // t2)
        idx1, wsel1, rank1 = _decode_slots(slots1)
        idx2, wsel2, rank2 = _decode_slots(slots2)
        idx = jnp.concatenate([idx1, idx2 + t1], axis=1).reshape(-1)
        dst = jnp.concatenate([rank1, rank2 + 2 * t1], axis=1).reshape(-1)
        wcol = jnp.concatenate([wsel1, wsel2], axis=1).reshape(-1, 1)
        z = _expert_ffn(idx, dst, h2, wcol, w_gate, w_up, w_down, l, rc, nch)

        lo = jnp.concatenate([lo1, lo2 + 2.0 * t1], axis=1)
        hi = jnp.concatenate([hi1, hi2 + 2.0 * t1], axis=1)
        bstart = lo[0, ::CMB_TB].astype(I32)
        bend = hi[0, CMB_TB - 1::CMB_TB].astype(I32)
        x = _combine(bstart, bend, x1, lo, hi, mod[l], final_norm_w.reshape(1, D), z, t1, n2,
                     l == depth - 1)

    y_prompt = x[:t1].reshape(b1, n1, D)
    y_sample = x[t1:].reshape(b2, n2, D)
    return y_prompt, y_sample, jnp.stack(ctx_states, axis=1)
```

```python
import functools
import math

import jax
import jax.numpy as jnp
from jax import lax
from jax.experimental import pallas as pl
from jax.experimental.pallas import tpu as pltpu

F32 = jnp.float32
BF16 = jnp.bfloat16
I32 = jnp.int32
HI = lax.Precision.HIGHEST

D = 1024
H = 4
DH = 128
CH = 64
E = 16
NMOD = 6
EPS = 1e-6
GRID_W = 64
POS_BASE = 10000.0
P_COLS = 3072
VMEM_LIMIT = 56 * 1024 * 1024

NN = (((1,), (0,)), ((), ()))
NT = (((1,), (1,)), ((), ()))
TN = (((0,), (0,)), ((), ()))


def _mm(a, b, dims=NN, prec=None):
    return lax.dot_general(a, b, dims, precision=prec, preferred_element_type=F32)


def _mmb(a, b, dims=NN):
    return lax.dot_general(a.astype(BF16), b.astype(BF16), dims, preferred_element_type=F32)


def _mm_inv(a, b):
    return _mmb(a, b)


def _silu(x):
    return x / (1.0 + jnp.exp(-x))


def _sigmoid(x):
    return 1.0 / (1.0 + jnp.exp(-x))


def _softplus(x):
    return jnp.maximum(x, 0.0) + jnp.log1p(jnp.exp(-jnp.abs(x)))


def _iota(shape, dim):
    return lax.broadcasted_iota(I32, shape, dim)


ROW_TILES = D // 128


def _load_token_tiles(ref, n):
    return jnp.concatenate([ref[pl.ds(s, n, stride=ROW_TILES), :] for s in range(ROW_TILES)], axis=1)


def _store_token_tiles(ref, x):
    n = x.shape[0]
    for s in range(ROW_TILES):
        ref[pl.ds(s, n, stride=ROW_TILES), :] = x[:, s * 128:(s + 1) * 128]


def _mod_row(t0, t1, n2):
    return jnp.where(t0 < t1, 0, 1 + jnp.maximum(t0 - t1, 0) // n2)


def _params(sem):
    return pltpu.CompilerParams(dimension_semantics=sem, vmem_limit_bytes=VMEM_LIMIT)


def _ada_kernel(c_ref, w_ref, b_ref, o_ref):
    s = _silu(c_ref[...])
    o_ref[0] = _mm(s, w_ref[0], prec=HI) + b_ref[0]


def _ada(cond8, w_ada, b_ada):
    depth = w_ada.shape[0]
    tn = 1536
    return pl.pallas_call(
        _ada_kernel,
        grid=(depth, NMOD * D // tn),
        in_specs=[pl.BlockSpec((8, D), lambda l, j: (0, 0)),
                  pl.BlockSpec((1, D, tn), lambda l, j: (l, 0, j)),
                  pl.BlockSpec((1, 1, tn), lambda l, j: (l, 0, j))],
        out_specs=pl.BlockSpec((1, 8, tn), lambda l, j: (l, 0, j)),
        out_shape=jax.ShapeDtypeStruct((depth, 8, NMOD * D), F32),
        compiler_params=_params(("arbitrary", "arbitrary")),
        name="ada",
    )(cond8, w_ada, b_ada.reshape(depth, 1, NMOD * D))


def _in_kernel(t1, n2, tm, x_ref, mod_ref, nw_ref, wm_ref, wg_ref, al_ref, dt_ref,
               p_ref, g_ref, gt_ref):
    r = _mod_row(pl.program_id(0) * tm, t1, n2)
    x = x_ref[...]
    y = x * lax.rsqrt(jnp.mean(x * x, axis=-1, keepdims=True) + EPS) * nw_ref[...]
    sh = mod_ref[pl.ds(r, 1), 0:D]
    sc = mod_ref[pl.ds(r, 1), D:2 * D]
    hb = (y * (1.0 + sc) + sh).astype(BF16)
    nsplit = 4
    wc = P_COLS // nsplit
    for j in range(nsplit):
        p_ref[:, j * wc:(j + 1) * wc] = _mm(hb, wm_ref[:, j * wc:(j + 1) * wc])
    raw = _mm(hb, wg_ref[...])
    lane = _iota((1, 128), 1)
    g = -jnp.exp(al_ref[...]) * _softplus(raw + dt_ref[...])
    act = jnp.where(lane < 2 * H, _sigmoid(raw), g)
    g_ref[...] = act[:, 0:4 * H]
    act_t = act.T
    for j in range(tm // CH):
        gt_ref[j] = act_t[0:4 * H, j * CH:(j + 1) * CH]


def _in_proj(x, mod, nw, wm, wg, al, dt, t1, n2, tm):
    t_all = x.shape[0]
    return pl.pallas_call(
        functools.partial(_in_kernel, t1, n2, tm),
        grid=(t_all // tm,),
        in_specs=[pl.BlockSpec((tm, D), lambda i: (i, 0)),
                  pl.BlockSpec((8, NMOD * D), lambda i: (0, 0)),
                  pl.BlockSpec((1, D), lambda i: (0, 0)),
                  pl.BlockSpec((D, P_COLS), lambda i: (0, 0)),
                  pl.BlockSpec((D, 128), lambda i: (0, 0)),
                  pl.BlockSpec((1, 128), lambda i: (0, 0)),
                  pl.BlockSpec((1, 128), lambda i: (0, 0))],
        out_specs=[pl.BlockSpec((tm, P_COLS), lambda i: (i, 0)),
                   pl.BlockSpec((tm, 4 * H), lambda i: (i, 0)),
                   pl.BlockSpec((tm // CH, 4 * H, CH), lambda i: (i, 0, 0))],
        out_shape=[jax.ShapeDtypeStruct((t_all, P_COLS), F32),
                   jax.ShapeDtypeStruct((t_all, 4 * H), F32),
                   jax.ShapeDtypeStruct((t_all // CH, 4 * H, CH), F32)],
        compiler_params=_params(("arbitrary",)),
        name="in_proj",
    )(x, mod, nw, wm, wg, al, dt)


def _shift_conv(x, w_ref, c0, c1):
    n = x.shape[0]
    row = _iota((n, 1), 0)
    prev = jnp.where(row == 0, 0.0, pltpu.roll(x, 1, 0))
    nxt = jnp.where(row == n - 1, 0.0, pltpu.roll(x, n - 1, 0))
    return prev * w_ref[0:1, c0:c1] + x * w_ref[1:2, c0:c1] + nxt * w_ref[2:3, c0:c1]


DN_POS = 4


def _dn_kernel(n, q_ref, k_ref, v_ref, z_ref, g_ref, gt_ref, cw_ref, nw_ref, s0_ref,
               o_ref, sfin_ref, qs, ks, vs, gcc_s, gct_s, s_s, oacc):
    nc = n // CH
    width = H * DH

    for hh in range(H):
        c0, c1 = hh * DH, (hh + 1) * DH
        q = _silu(_shift_conv(q_ref[:, c0:c1], cw_ref, c0, c1))
        q = q * lax.rsqrt(jnp.sum(q * q, axis=-1, keepdims=True) + 1e-6)
        qs[:, c0:c1] = q * (DH ** -0.5)
        k = _silu(_shift_conv(k_ref[:, c0:c1], cw_ref, width + c0, width + c1))
        ks[:, c0:c1] = k * lax.rsqrt(jnp.sum(k * k, axis=-1, keepdims=True) + 1e-6)
        vs[:, c0:c1] = _silu(_shift_conv(v_ref[:, c0:c1], cw_ref, 2 * width + c0, 2 * width + c1))

    ii = _iota((CH, CH), 0)
    jj = _iota((CH, CH), 1)
    x = gt_ref[...].reshape(nc * 4 * H, CH)
    cf = _mm(x, (ii <= jj).astype(F32), prec=HI)
    cb = _mm(x, (ii >= jj).astype(F32), prec=HI)
    rsel = (_iota((nc * 4 * H, 1), 0) & (4 * H - 1)) < 3 * H
    gct_s[...] = jnp.where(rsel, cf, cb).reshape(nc, 4 * H, CH)
    i2 = _iota((2 * CH, 2 * CH), 0)
    j2 = _iota((2 * CH, 2 * CH), 1)
    same = (i2 >= CH) == (j2 >= CH)
    lf = (same & (i2 >= j2)).astype(F32)
    lb = (same & (i2 <= j2)).astype(F32)
    lsel = _iota((1, 4 * H), 1) < 3 * H
    for m in range(n // (2 * CH)):
        y = g_ref[m * 2 * CH:(m + 1) * 2 * CH, :]
        gcc_s[m * 2 * CH:(m + 1) * 2 * CH, :] = jnp.where(lsel, _mm(lf, y, prec=HI), _mm(lb, y, prec=HI))

    for hh in range(H):
        for d in range(2):
            s_s[hh * 2 + d] = s0_ref[0, d, hh]

    eye = (ii == jj).astype(F32)
    blk_same = [(ii >> s) == (jj >> s) for s in range(1, 7)]

    chains = [(hh, d) for hh in range(H) for d in range(2)]

    def chunk_step(cc, carry):
        st = []
        for p in range(DN_POS):
            for hh, d in chains:
                c = cc * DN_POS + p
                cidx = c if d == 0 else nc - 1 - c
                r0 = pl.multiple_of(cidx * CH, CH)
                c0, c1 = hh * DH, (hh + 1) * DH
                gcol = 2 * H + H * d + hh
                bcol = H * d + hh
                incl = (ii >= jj) if d == 0 else (ii <= jj)
                strict = (ii > jj) if d == 0 else (ii < jj)
                qc = qs[pl.ds(r0, CH), c0:c1]
                kc = ks[pl.ds(r0, CH), c0:c1]
                vc = vs[pl.ds(r0, CH), c0:c1]
                beta = g_ref[pl.ds(r0, CH), bcol:bcol + 1]
                gc = gcc_s[pl.ds(r0, CH), gcol:gcol + 1]
                gr = gct_s[cidx, gcol:gcol + 1, :]
                tot = gr[:, CH - 1:CH] if d == 0 else gr[:, 0:1]
                decay = jnp.where(incl, jnp.exp(jnp.where(incl, gc - gr, 0.0)), 0.0)
                kb = kc * beta
                egc = jnp.exp(gc)
                st.append(dict(r0=r0, c0=c0, c1=c1, d=d, strict=strict, kc=kc, kb=kb, decay=decay,
                               rhs=jnp.concatenate([vc * beta, kb * egc], axis=1),
                               qd=qc * egc, qc=qc, kd=kc * jnp.exp(tot - gc), gl=jnp.exp(tot)))
        for x in st:
            x["lmat"] = jnp.where(x["strict"], _mmb(x["kb"], x["kc"], NT) * x["decay"], 0.0)
            x["qk"] = _mmb(x["qc"], x["kc"], NT) * x["decay"]
            x["tinv"] = eye - jnp.where(blk_same[0], x["lmat"], 0.0)
        for lvl in range(1, len(blk_same)):
            for x in st:
                off = jnp.where(blk_same[lvl] & ~blk_same[lvl - 1], x["lmat"], 0.0)
                x["ot"] = _mm_inv(off, x["tinv"])
            for x in st:
                x["tinv"] = x["tinv"] - _mm_inv(x["tinv"], x["ot"])
        for x in st:
            x["sol"] = _mm_inv(x["tinv"], x["rhs"])
        state = [s_s[i] for i in range(len(chains))]
        for p in range(DN_POS):
            units = st[p * len(chains):(p + 1) * len(chains)]
            for i, x in enumerate(units):
                x["sb"] = state[i].astype(BF16)
                x["v_new"] = x["sol"][:, 0:DH] - _mmb(x["sol"][:, DH:2 * DH], x["sb"])
            for i, x in enumerate(units):
                o = _mmb(x["qd"], x["sb"]) + _mmb(x["qk"], x["v_new"])
                state[i] = state[i] * x["gl"] + _mmb(x["kd"], x["v_new"], TN)
                oacc[x["d"], pl.ds(x["r0"], CH), x["c0"]:x["c1"]] = o
        for i in range(len(chains)):
            s_s[i] = state[i]
        return carry

    lax.fori_loop(0, nc // DN_POS, chunk_step, 0)

    for hh in range(H):
        c0, c1 = hh * DH, (hh + 1) * DH
        o = oacc[0, :, c0:c1] + oacc[1, :, c0:c1]
        o = o * lax.rsqrt(jnp.mean(o * o, axis=-1, keepdims=True) + EPS)
        o_ref[:, c0:c1] = o * nw_ref[...] * _silu(z_ref[:, c0:c1])
        for d in range(2):
            sfin_ref[0, d, hh] = s_s[hh * 2 + d]


def _deltanet(p, g, gt, conv_w, norm_w, s0, n, row_blk0):
    bsz = s0.shape[0]
    nc = n // CH
    width = H * DH
    return pl.pallas_call(
        functools.partial(_dn_kernel, n),
        grid=(bsz,),
        in_specs=[pl.BlockSpec((n, width), lambda b: (row_blk0 + b, 0)),
                  pl.BlockSpec((n, width), lambda b: (row_blk0 + b, 1)),
                  pl.BlockSpec((n, width), lambda b: (row_blk0 + b, 2)),
                  pl.BlockSpec((n, width), lambda b: (row_blk0 + b, 3)),
                  pl.BlockSpec((n, 4 * H), lambda b: (row_blk0 + b, 0)),
                  pl.BlockSpec((nc, 4 * H, CH), lambda b: (row_blk0 + b, 0, 0)),
                  pl.BlockSpec((3, 3 * width), lambda b: (0, 0)),
                  pl.BlockSpec((1, DH), lambda b: (0, 0)),
                  pl.BlockSpec((1, 2, H, DH, DH), lambda b: (b, 0, 0, 0, 0))],
        out_specs=[pl.BlockSpec((n, width), lambda b: (b, 0)),
                   pl.BlockSpec((1, 2, H, DH, DH), lambda b: (b, 0, 0, 0, 0))],
        out_shape=[jax.ShapeDtypeStruct((bsz * n, width), F32),
                   jax.ShapeDtypeStruct((bsz, 2, H, DH, DH), F32)],
        scratch_shapes=[pltpu.VMEM((n, width), F32), pltpu.VMEM((n, width), F32),
                        pltpu.VMEM((n, width), F32), pltpu.VMEM((n, 4 * H), F32),
                        pltpu.VMEM((nc, 4 * H, CH), F32), pltpu.VMEM((2 * H, DH, DH), F32),
                        pltpu.VMEM((2, n, width), F32)],
        compiler_params=_params(("arbitrary",)),
        name="deltanet_n%d" % n,
    )(p, p, p, p, g, gt, conv_w, norm_w, s0)


def _mix2_kernel(sb_ref, sc_ref, su_ref, fu_ref, cw_ref, bdc_ref, bds_ref, cs_ref, o_ref):
    w = sb_ref.shape[1]
    cu = sc_ref[...] * su_ref[...]
    o_ref[:, 0:w] = sb_ref[...] * _shift_conv(cu, cw_ref, 0, w)
    fu = fu_ref[...]
    a = _mm(fu, bdc_ref[...], prec=HI)
    b = _mm(fu, bds_ref[...], prec=HI)
    o_ref[:, w:2 * w] = _mm(cs_ref[...], jnp.concatenate([a, b], axis=0), prec=HI)


def _mix2(p, conv_w, bdc, bds, cs, bsz, n, row_blk0):
    w = 256
    return pl.pallas_call(
        _mix2_kernel,
        grid=(bsz,),
        in_specs=[pl.BlockSpec((n, w), lambda b: (row_blk0 + b, 8)),
                  pl.BlockSpec((n, w), lambda b: (row_blk0 + b, 9)),
                  pl.BlockSpec((n, w), lambda b: (row_blk0 + b, 10)),
                  pl.BlockSpec((n, w), lambda b: (row_blk0 + b, 11)),
                  pl.BlockSpec((3, w), lambda b: (0, 0)),
                  pl.BlockSpec((w, w), lambda b: (0, 0)),
                  pl.BlockSpec((w, w), lambda b: (0, 0)),
                  pl.BlockSpec((n, 2 * n), lambda b: (0, 0))],
        out_specs=pl.BlockSpec((n, 2 * w), lambda b: (b, 0)),
        out_shape=jax.ShapeDtypeStruct((bsz * n, 2 * w), F32),
        compiler_params=_params(("arbitrary",)),
        name="conv_fourier_n%d" % n,
    )(p, p, p, p, conv_w, bdc, bds, cs)


def _dft_tables(n):
    gw = 64
    k = jnp.arange(n, dtype=I32)
    ph = (k[:, None] * k[None, :]) % n
    ang = ph.astype(F32) * (2.0 * math.pi / n)
    scale = 1.0 / math.sqrt(n * gw)
    cs = jnp.concatenate([jnp.cos(ang), -jnp.sin(ang)], axis=1) * scale
    c = jnp.arange(256, dtype=I32)
    phc = ((c[:, None] % gw) * (c[None, :] % gw)) % gw
    angc = phc.astype(F32) * (2.0 * math.pi / gw)
    same = (c[:, None] // gw) == (c[None, :] // gw)
    bdc = jnp.where(same, jnp.cos(angc), 0.0)
    bds = jnp.where(same, jnp.sin(angc), 0.0)
    return cs, bdc, bds


def _out_kernel(t1, n2, tm, dn1_ref, dn2_ref, mx1_ref, mx2_ref, x_ref, mod_ref, nw_ref, wo_ref, wr_ref,
                x1_ref, h2_ref, aff_ref):
    t0 = pl.program_id(0) * tm
    r = _mod_row(t0, t1, n2)
    half = D // 2
    dn = jnp.where(t0 < t1, dn1_ref[...], dn2_ref[...])
    mx = jnp.where(t0 < t1, mx1_ref[...], mx2_ref[...])
    mix = _mmb(dn, wo_ref[0:half, :]) + _mmb(mx, wo_ref[half:D, :])
    x1 = x_ref[...] + mod_ref[pl.ds(r, 1), 2 * D:3 * D] * mix
    x1_ref[...] = x1
    y = x1 * lax.rsqrt(jnp.mean(x1 * x1, axis=-1, keepdims=True) + EPS) * nw_ref[...]
    h2 = y * (1.0 + mod_ref[pl.ds(r, 1), 4 * D:5 * D]) + mod_ref[pl.ds(r, 1), 3 * D:4 * D]
    _store_token_tiles(h2_ref, h2)
    logits = _mm(h2, wr_ref[...], prec=HI)
    lt = logits.T[0:E, :]
    ex = jnp.exp(lt - jnp.max(lt, axis=0, keepdims=True))
    aff_ref[...] = ex / jnp.sum(ex, axis=0, keepdims=True)


def _out_proj(dn1, dn2, mx1, mx2, x, mod, nw, wo, wr, t1, n2, tm):
    t_all = x.shape[0]
    nt1 = t1 // tm
    spec1 = pl.BlockSpec((tm, D // 2), lambda i: (jnp.minimum(i, nt1 - 1), 0))
    spec2 = pl.BlockSpec((tm, D // 2), lambda i: (jnp.maximum(i - nt1, 0), 0))
    return pl.pallas_call(
        functools.partial(_out_kernel, t1, n2, tm),
        grid=(t_all // tm,),
        in_specs=[spec1, spec2, spec1, spec2,
                  pl.BlockSpec((tm, D), lambda i: (i, 0)),
                  pl.BlockSpec((8, NMOD * D), lambda i: (0, 0)),
                  pl.BlockSpec((1, D), lambda i: (0, 0)),
                  pl.BlockSpec((D, D), lambda i: (0, 0)),
                  pl.BlockSpec((D, 128), lambda i: (0, 0))],
        out_specs=[pl.BlockSpec((tm, D), lambda i: (i, 0)),
                   pl.BlockSpec((tm * ROW_TILES, 128), lambda i: (i, 0)),
                   pl.BlockSpec((E, tm), lambda i: (0, i))],
        out_shape=[jax.ShapeDtypeStruct((t_all, D), F32),
                   jax.ShapeDtypeStruct((t_all * ROW_TILES, 128), F32),
                   jax.ShapeDtypeStruct((E, t_all), F32)],
        compiler_params=_params(("arbitrary",)),
        name="out_proj",
    )(dn1, dn2, mx1, mx2, x, mod, nw, wo, wr)


SEL_TB = 128
N_SLOT_ROWS = 16
SPILL_BLOCKS = 2


def _sel_kernel(t, cap, aff_ref, slots_ref, lo_ref, hi_ref, a3, c3, starts_v, starts_s, sem):
    ntb = t // SEL_TB
    aff = aff_ref[...]

    def search(i, thr):
        cand = thr | jnp.left_shift(jnp.int32(1), 30 - i)
        cnt = jnp.sum((aff >= pltpu.bitcast(cand, F32)).astype(F32), axis=1, keepdims=True)
        return jnp.where(cnt >= cap, cand, thr)

    thr = lax.fori_loop(0, 31, search, jnp.zeros((E, 1), I32))
    gt = (aff >= pltpu.bitcast(thr + 1, F32)).astype(F32)
    eq = (aff >= pltpu.bitcast(thr, F32)).astype(F32) - gt
    need = cap - jnp.sum(gt, axis=1, keepdims=True)

    ui = _iota((SEL_TB, SEL_TB), 0)
    uj = _iota((SEL_TB, SEL_TB), 1)
    upper = (ui < uj).astype(BF16)
    blk_lane = _iota((1, 128), 1)

    def excl_cumsum(rows, dst, r0, r1):
        carry = jnp.zeros((rows.shape[0], 1), F32)
        starts = jnp.zeros((rows.shape[0], 128), F32)
        for j in range(ntb):
            blk = rows[:, j * SEL_TB:(j + 1) * SEL_TB]
            dst[j, r0:r1, :] = _mm(blk.astype(BF16), upper) + carry
            starts = jnp.where(blk_lane == j, carry, starts)
            carry = carry + jnp.sum(blk, axis=1, keepdims=True)
        return starts

    excl_cumsum(eq, c3, 0, E)
    rank_eq = jnp.concatenate([c3[j, 0:E, :] for j in range(ntb)], axis=1)
    sel = jnp.maximum(gt, jnp.where(rank_eq < need, eq, 0.0))
    n_tok = jnp.sum(sel, axis=0, keepdims=True)
    starts = excl_cumsum(jnp.concatenate([sel, jnp.broadcast_to(n_tok, (8, t))], axis=0), c3, 0, E + 8)
    starts_v[...] = starts.astype(I32)
    to_smem = pltpu.make_async_copy(starts_v, starts_s, sem)
    to_smem.start()
    ei = _iota((E, E), 0)
    ej = _iota((E, E), 1)
    below = _mm((ej < ei).astype(BF16), sel.astype(BF16))
    for j in range(ntb):
        sl = slice(j * SEL_TB, (j + 1) * SEL_TB)
        off = c3[j, E:E + 1, :]
        lo_ref[:, sl] = off
        hi_ref[:, sl] = off + n_tok[:, sl]
        rank = off + below[:, sl]
        a3[j, 0] = sel[:, sl]
        a3[j, 1] = aff[:, sl]
        a3[j, 2] = rank
    slots_ref[...] = jnp.zeros(slots_ref.shape, F32)
    to_smem.wait()

    win_iota = _iota((2 * SEL_TB, SEL_TB), 0)
    tok_iota = _iota((1, SEL_TB), 1)

    def per_expert(e, carry):
        def per_block(j, carry2):
            wb = starts_s[e, j] >> 7
            pos = c3[j, pl.ds(e, 1), :].astype(I32) - wb * SEL_TB
            chosen = a3[j, 0, pl.ds(e, 1), :]
            w = a3[j, 1, pl.ds(e, 1), :]
            rank = a3[j, 2, pl.ds(e, 1), :].astype(I32)
            tok = tok_iota + j * SEL_TB
            w_hi = w.astype(BF16).astype(F32)
            w_mid = (w - w_hi).astype(BF16).astype(F32)
            w_lo = w - w_hi - w_mid
            vals = jnp.concatenate(
                [(tok >> 7).astype(F32), (tok & 127).astype(F32), w_hi, w_mid, w_lo,
                 (rank >> 7).astype(F32), (rank & 127).astype(F32),
                 jnp.zeros((N_SLOT_ROWS - 7, SEL_TB), F32)], axis=0)
            onehot = jnp.where((win_iota == pos) & (chosen > 0.0), 1.0, 0.0)
            placed = _mmb(vals, onehot, NT)
            slots_ref[e, wb] += placed[:, 0:SEL_TB]
            slots_ref[e, wb + 1] += placed[:, SEL_TB:2 * SEL_TB]
            return carry2
        return lax.fori_loop(0, ntb, per_block, carry, unroll=8)

    lax.fori_loop(0, E, per_expert, 0)


def _select(aff_t, t, cap, col_blk):
    ntb = t // SEL_TB
    nsb = cap // SEL_TB + SPILL_BLOCKS
    return pl.pallas_call(
        functools.partial(_sel_kernel, t, cap),
        grid=(1,),
        in_specs=[pl.BlockSpec((E, t), lambda i: (0, col_blk))],
        out_specs=[pl.BlockSpec((E, nsb, N_SLOT_ROWS, SEL_TB), lambda i: (0, 0, 0, 0)),
                   pl.BlockSpec((1, t), lambda i: (0, 0)),
                   pl.BlockSpec((1, t), lambda i: (0, 0))],
        out_shape=[jax.ShapeDtypeStruct((E, nsb, N_SLOT_ROWS, SEL_TB), F32),
                   jax.ShapeDtypeStruct((1, t), F32),
                   jax.ShapeDtypeStruct((1, t), F32)],
        scratch_shapes=[pltpu.VMEM((ntb, 3, E, SEL_TB), F32),
                        pltpu.VMEM((ntb, E + 8, SEL_TB), F32),
                        pltpu.VMEM((E + 8, 128), I32),
                        pltpu.SMEM((E + 8, 128), I32),
                        pltpu.SemaphoreType.DMA(())],
        compiler_params=_params(("arbitrary",)),
        name="select_t%d" % t,
    )(aff_t)


DMA_UNROLL = 8


def _ffn_kernel(rc, nch, idx_ref, dst_ref, h2_hbm, wcol_ref, wg_ref, wu_ref, wd_ref, z_hbm,
                xbuf, ybuf, wgb, wub, wdb, gsem, ssem):
    c = pl.program_id(1)
    step = pl.program_id(0) * nch + c
    nsteps = E * nch
    slot = step % 2

    def start_gather(s):
        buf = xbuf.at[s % 2]
        sem = gsem.at[s % 2]

        def issue(jb, carry):
            for u in range(DMA_UNROLL):
                j = jb * DMA_UNROLL + u
                src = pl.multiple_of(idx_ref[s * rc + j], ROW_TILES)
                pltpu.make_async_copy(h2_hbm.at[pl.ds(src, ROW_TILES)],
                                      buf.at[pl.ds(j * ROW_TILES, ROW_TILES)], sem).start()
            return carry

        lax.fori_loop(0, rc // DMA_UNROLL, issue, 0)

    @pl.when(step == 0)
    def _():
        start_gather(step)

    @pl.when(step + 1 < nsteps)
    def _():
        start_gather(step + 1)

    @pl.when(c == 0)
    def _():
        wgb[...] = wg_ref[0, 0].astype(BF16)
        wub[...] = wu_ref[0, 0].astype(BF16)
        wdb[...] = wd_ref[0, 0].astype(BF16)

    pltpu.make_async_copy(h2_hbm.at[pl.ds(0, rc * ROW_TILES)], xbuf.at[slot], gsem.at[slot]).wait()
    xb = _load_token_tiles(xbuf.at[slot], rc).astype(BF16)
    hid = _silu(_mm(xb, wgb[...])) * _mm(xb, wub[...])
    y = _mm(hid.astype(BF16), wdb[...]) * wcol_ref[...]

    @pl.when(step > 0)
    def _():
        pltpu.make_async_copy(ybuf, z_hbm.at[pl.ds(0, rc * ROW_TILES)], ssem).wait()

    _store_token_tiles(ybuf, y)

    def scatter(jb, carry):
        for u in range(DMA_UNROLL):
            j = jb * DMA_UNROLL + u
            row = pl.multiple_of(dst_ref[step * rc + j], ROW_TILES)
            pltpu.make_async_copy(ybuf.at[pl.ds(j * ROW_TILES, ROW_TILES)],
                                  z_hbm.at[pl.ds(row, ROW_TILES)], ssem).start()
        return carry

    lax.fori_loop(0, rc // DMA_UNROLL, scatter, 0)

    @pl.when(step == nsteps - 1)
    def _():
        pltpu.make_async_copy(ybuf, z_hbm.at[pl.ds(0, rc * ROW_TILES)], ssem).wait()


def _expert_ffn(idx, dst, h2, wcol, w_gate, w_up, w_down, layer, rc, nch):
    zrows = E * nch * rc
    wspec = pl.BlockSpec((1, 1, D, D), lambda e, c, *_: (layer, e, 0, 0))
    return pl.pallas_call(
        functools.partial(_ffn_kernel, rc, nch),
        grid_spec=pltpu.PrefetchScalarGridSpec(
            num_scalar_prefetch=2,
            grid=(E, nch),
            in_specs=[pl.BlockSpec(memory_space=pl.ANY),
                      pl.BlockSpec((rc, 1), lambda e, c, *_: (e * nch + c, 0)),
                      wspec, wspec, wspec],
            out_specs=pl.BlockSpec(memory_space=pl.ANY),
            scratch_shapes=[pltpu.VMEM((2, rc * ROW_TILES, 128), F32),
                            pltpu.VMEM((rc * ROW_TILES, 128), F32),
                            pltpu.VMEM((D, D), BF16), pltpu.VMEM((D, D), BF16),
                            pltpu.VMEM((D, D), BF16),
                            pltpu.SemaphoreType.DMA((2,)), pltpu.SemaphoreType.DMA(())]),
        out_shape=jax.ShapeDtypeStruct((zrows * ROW_TILES, 128), F32),
        compiler_params=_params(("arbitrary", "arbitrary")),
        name="expert_ffn",
    )(idx, dst, h2, wcol, w_gate, w_up, w_down)


CMB_TB = 256
CMB_RC = 512


def _comb_kernel(t1, n2, zrows, final, bs_ref, be_ref, x1_ref, lo_ref, hi_ref, mod_ref, fw_ref, z_hbm,
                 o_ref, zbuf, sem):
    i = pl.program_id(0)
    nsteps = pl.num_programs(0)
    r = _mod_row(i * CMB_TB, t1, n2)

    def first_row(step):
        return (bs_ref[step] // 8) * 8

    def chunk_copy(nominal, slot):
        b = pl.multiple_of(jnp.minimum(nominal, zrows - CMB_RC) * ROW_TILES, 8 * ROW_TILES)
        return pltpu.make_async_copy(z_hbm.at[pl.ds(b, CMB_RC * ROW_TILES)], zbuf.at[slot], sem.at[slot])

    base0 = first_row(i)
    nchunk = jnp.maximum((be_ref[i] - base0 + CMB_RC - 1) // CMB_RC, 1)

    @pl.when(i == 0)
    def _():
        chunk_copy(base0, 0).start()

    eye = _iota((CMB_TB, CMB_TB), 0) == _iota((CMB_TB, CMB_TB), 1)
    lo = jnp.sum(jnp.where(eye, lo_ref[...], 0.0), axis=1, keepdims=True)
    hi = jnp.sum(jnp.where(eye, hi_ref[...], 0.0), axis=1, keepdims=True)
    col = _iota((1, CMB_RC), 1)

    def chunk(ci, acc):
        nominal = base0 + ci * CMB_RC
        slot = ci % 2

        @pl.when(ci + 1 < nchunk)
        def _():
            chunk_copy(nominal + CMB_RC, 1 - slot).start()

        chunk_copy(nominal, slot).wait()
        b = jnp.minimum(nominal, zrows - CMB_RC)
        rows = (col + b).astype(F32)
        pick = (rows >= lo) & (rows < hi) & (rows >= nominal.astype(F32))
        s = jnp.where(pick, 1.0, 0.0).astype(BF16)
        z = _load_token_tiles(zbuf.at[slot], CMB_RC)
        z_hi = z.astype(BF16)
        z_lo = (z - z_hi.astype(F32)).astype(BF16)
        return acc + _mm(s, z_hi) + _mm(s, z_lo)

    moe = lax.fori_loop(0, nchunk, chunk, jnp.zeros((CMB_TB, D), F32))

    @pl.when(i + 1 < nsteps)
    def _():
        chunk_copy(first_row(jnp.minimum(i + 1, nsteps - 1)), 0).start()

    x2 = x1_ref[...] + mod_ref[pl.ds(r, 1), 5 * D:6 * D] * moe
    if final:
        x2 = x2 * lax.rsqrt(jnp.mean(x2 * x2, axis=-1, keepdims=True) + EPS) * fw_ref[...]
    o_ref[...] = x2


def _combine(bstart, bend, x1, lo, hi, mod, fw, z, t1, n2, final):
    t_all = x1.shape[0]
    zrows = z.shape[0] // ROW_TILES
    return pl.pallas_call(
        functools.partial(_comb_kernel, t1, n2, zrows, final),
        grid_spec=pltpu.PrefetchScalarGridSpec(
            num_scalar_prefetch=2,
            grid=(t_all // CMB_TB,),
            in_specs=[pl.BlockSpec((CMB_TB, D), lambda i, *_: (i, 0)),
                      pl.BlockSpec((1, CMB_TB), lambda i, *_: (0, i)),
                      pl.BlockSpec((1, CMB_TB), lambda i, *_: (0, i)),
                      pl.BlockSpec((8, NMOD * D), lambda i, *_: (0, 0)),
                      pl.BlockSpec((1, D), lambda i, *_: (0, 0)),
                      pl.BlockSpec(memory_space=pl.ANY)],
            out_specs=pl.BlockSpec((CMB_TB, D), lambda i, *_: (i, 0)),
            scratch_shapes=[pltpu.VMEM((2, CMB_RC * ROW_TILES, 128), F32),
                            pltpu.SemaphoreType.DMA((2,))]),
        out_shape=jax.ShapeDtypeStruct((t_all, D), F32),
        compiler_params=_params(("arbitrary",)),
        name="combine",
    )(bstart, bend, x1, lo, hi, mod, fw, z)


def _grid_pos_embed(n, d):
    rows = n // GRID_W
    r = jnp.repeat(jnp.arange(rows, dtype=F32), GRID_W)
    col = jnp.tile(jnp.arange(GRID_W, dtype=F32), rows)
    quarter = d // 4
    omega = jnp.power(POS_BASE, -jnp.arange(quarter, dtype=F32) / quarter)
    ra = r[:, None] * omega
    ca = col[:, None] * omega
    return jnp.concatenate([jnp.sin(ra), jnp.cos(ra), jnp.sin(ca), jnp.cos(ca)], axis=-1)


def _decode_slots(slots):
    e, nsb, rows, tb = slots.shape
    nsb -= SPILL_BLOCKS
    slots = slots[:, :nsb].transpose(0, 2, 1, 3).reshape(e, rows, nsb * tb)
    idx = (slots[:, 0] * 128.0 + slots[:, 1]).astype(I32)
    w = slots[:, 2] + slots[:, 3] + slots[:, 4]
    rank = (slots[:, 5] * 128.0 + slots[:, 6]).astype(I32)
    return idx, w, rank


def kernel(x_prompt, x_sample, state_delta, c, c_ctx, w_ada, b_ada, norm1_w, norm2_w, w_in, dn_conv_w,
           dn_a_log, dn_dt_bias, dn_norm_w, sc_conv_w, w_out, w_router, w_gate, w_up, w_down,
           final_norm_w):
    b1, n1, _ = x_prompt.shape
    b2, n2, _ = x_sample.shape
    depth = w_ada.shape[0]
    t1, t2 = b1 * n1, b2 * n2
    t_all = t1 + t2
    cap1 = max(1, 2 * t1 // E)
    cap2 = max(1, 2 * t2 // E)
    tm = 512
    rc = 512 if (cap1 % 512 == 0 and cap2 % 512 == 0) else 128
    assert t1 % n2 == 0 and t1 % tm == 0 and n2 % tm == 0
    assert n1 % (DN_POS * CH) == 0 and n2 % (DN_POS * CH) == 0
    assert t1 % SEL_TB == 0 and t2 % SEL_TB == 0 and t1 % t2 == 0
    assert cap1 % rc == 0 and cap2 % rc == 0 and 2 * t_all >= CMB_RC
    nch = (cap1 + cap2) // rc
    width = H * DH

    cond8 = jnp.zeros((8, D), F32).at[0].set(c_ctx).at[1:1 + b2].set(c)
    mod = _ada(cond8, w_ada, b_ada)

    x = jnp.concatenate([x_prompt.reshape(t1, D),
                         (x_sample + _grid_pos_embed(n2, D)[None]).reshape(t2, D)], axis=0)
    s_zero = jnp.zeros((b1, 2, H, DH, DH), F32)
    tabs1 = _dft_tables(n1)
    tabs2 = _dft_tables(n2)
    ctx_states = []

    for l in range(depth):
        wl = w_in[l]
        wm = jnp.concatenate([wl[:, 0:4 * width], wl[:, 4 * width + 4 * H:]], axis=1).astype(BF16)
        wg = jnp.pad(wl[:, 4 * width:4 * width + 4 * H], ((0, 0), (0, 128 - 4 * H))).astype(BF16)
        al = jnp.pad(dn_a_log[l].reshape(1, 2 * H), ((0, 0), (2 * H, 128 - 4 * H)))
        dt = jnp.pad(dn_dt_bias[l].reshape(1, 2 * H), ((0, 0), (2 * H, 128 - 4 * H)))
        p, g, gt = _in_proj(x, mod[l], norm1_w[l].reshape(1, D), wm, wg, al, dt, t1, n2, tm)

        nwd = dn_norm_w[l].reshape(1, DH)
        dn1, s_ctx = _deltanet(p, g, gt, dn_conv_w[l], nwd, s_zero, n1, 0)
        dn2, _ = _deltanet(p, g, gt, dn_conv_w[l], nwd, state_delta[:, l], n2, t1 // n2)
        ctx_states.append(s_ctx)
        mx1 = _mix2(p, sc_conv_w[l], tabs1[1], tabs1[2], tabs1[0], b1, n1, 0)
        mx2 = _mix2(p, sc_conv_w[l], tabs2[1], tabs2[2], tabs2[0], b2, n2, t1 // n2)

        wr = jnp.pad(w_router[l], ((0, 0), (0, 128 - E)))
        x1, h2, aff_t = _out_proj(dn1, dn2, mx1, mx2, x, mod[l], norm2_w[l].reshape(1, D),
                                  w_out[l].astype(BF16), wr, t1, n2, tm)

        slots1, lo1, hi1 = _select(aff_t, t1, cap1, 0)
        slots2, lo2, hi2 = _select(aff_t, t2, cap2, t1 // t2)
        idx1, wsel1, rank1 = _decode_slots(slots1)
        idx2, wsel2, rank2 = _decode_slots(slots2)
        idx = jnp.concatenate([idx1, idx2 + t1], axis=1).reshape(-1) * ROW_TILES
        dst = jnp.concatenate([rank1, rank2 + 2 * t1], axis=1).reshape(-1) * ROW_TILES
        wcol = jnp.concatenate([wsel1, wsel2], axis=1).reshape(-1, 1)
        z = _expert_ffn(idx, dst, h2, wcol, w_gate, w_up, w_down, l, rc, nch)

        lo = jnp.concatenate([lo1, lo2 + 2.0 * t1], axis=1)
        hi = jnp.concatenate([hi1, hi2 + 2.0 * t1], axis=1)
        bstart = lo[0, ::CMB_TB].astype(I32)
        bend = hi[0, CMB_TB - 1::CMB_TB].astype(I32)
        x = _combine(bstart, bend, x1, lo, hi, mod[l], final_norm_w.reshape(1, D), z, t1, n2,
                     l == depth - 1)

    y_prompt = x[:t1].reshape(b1, n1, D)
    y_sample = x[t1:].reshape(b2, n2, D)
    return y_prompt, y_sample, jnp.stack(ctx_states, axis=1)
```

```python
import functools
import math

import jax
import jax.numpy as jnp
from jax import lax
from jax.experimental import pallas as pl
from jax.experimental.pallas import tpu as pltpu

F32 = jnp.float32
BF16 = jnp.bfloat16
I32 = jnp.int32
HI = lax.Precision.HIGHEST

D = 1024
H = 4
DH = 128
CH = 64
E = 16
NMOD = 6
EPS = 1e-6
GRID_W = 64
POS_BASE = 10000.0
P_COLS = 3072
VMEM_LIMIT = 56 * 1024 * 1024

NN = (((1,), (0,)), ((), ()))
NT = (((1,), (1,)), ((), ()))
TN = (((0,), (0,)), ((), ()))


def _mm(a, b, dims=NN, prec=None):
    return lax.dot_general(a, b, dims, precision=prec, preferred_element_type=F32)


def _mmb(a, b, dims=NN):
    return lax.dot_general(a.astype(BF16), b.astype(BF16), dims, preferred_element_type=F32)


def _mm3(a, b, dims=NN):
    a_hi = a.astype(BF16)
    a_lo = (a - a_hi.astype(F32)).astype(BF16)
    b_hi = b.astype(BF16)
    b_lo = (b - b_hi.astype(F32)).astype(BF16)
    return _mm(a_hi, b_hi, dims) + _mm(a_hi, b_lo, dims) + _mm(a_lo, b_hi, dims)


def _mm_inv(a, b):
    return _mmb(a, b)


def _silu(x):
    return x / (1.0 + jnp.exp(-x))


def _sigmoid(x):
    return 1.0 / (1.0 + jnp.exp(-x))


def _softplus(x):
    return jnp.maximum(x, 0.0) + jnp.log1p(jnp.exp(-jnp.abs(x)))


def _iota(shape, dim):
    return lax.broadcasted_iota(I32, shape, dim)


ROW_TILES = D // 128


def _load_token_tiles(ref, n):
    return jnp.concatenate([ref[pl.ds(s, n, stride=ROW_TILES), :] for s in range(ROW_TILES)], axis=1)


def _store_token_tiles(ref, x):
    n = x.shape[0]
    for s in range(ROW_TILES):
        ref[pl.ds(s, n, stride=ROW_TILES), :] = x[:, s * 128:(s + 1) * 128]


def _mod_row(t0, t1, n2):
    return jnp.where(t0 < t1, 0, 1 + jnp.maximum(t0 - t1, 0) // n2)


def _params(sem):
    return pltpu.CompilerParams(dimension_semantics=sem, vmem_limit_bytes=VMEM_LIMIT)


def _ada_kernel(c_ref, w_ref, b_ref, o_ref):
    s = _silu(c_ref[...])
    o_ref[0] = _mm(s, w_ref[0], prec=HI) + b_ref[0]


def _ada(cond8, w_ada, b_ada):
    depth = w_ada.shape[0]
    tn = 1536
    return pl.pallas_call(
        _ada_kernel,
        grid=(depth, NMOD * D // tn),
        in_specs=[pl.BlockSpec((8, D), lambda l, j: (0, 0)),
                  pl.BlockSpec((1, D, tn), lambda l, j: (l, 0, j)),
                  pl.BlockSpec((1, 1, tn), lambda l, j: (l, 0, j))],
        out_specs=pl.BlockSpec((1, 8, tn), lambda l, j: (l, 0, j)),
        out_shape=jax.ShapeDtypeStruct((depth, 8, NMOD * D), F32),
        compiler_params=_params(("arbitrary", "arbitrary")),
        name="ada",
    )(cond8, w_ada, b_ada.reshape(depth, 1, NMOD * D))


def _in_core(tm, r, x, mod_ref, nw_ref, wm_ref, wg_ref, al_ref, dt_ref, p_ref, g_ref, gt_ref):
    y = x * lax.rsqrt(jnp.mean(x * x, axis=-1, keepdims=True) + EPS) * nw_ref[...]
    sh = mod_ref[pl.ds(r, 1), 0:D]
    sc = mod_ref[pl.ds(r, 1), D:2 * D]
    hb = (y * (1.0 + sc) + sh).astype(BF16)
    nsplit = 4
    wc = P_COLS // nsplit
    for j in range(nsplit):
        p_ref[:, j * wc:(j + 1) * wc] = _mm(hb, wm_ref[:, j * wc:(j + 1) * wc])
    raw = _mm(hb, wg_ref[...])
    lane = _iota((1, 128), 1)
    g = -jnp.exp(al_ref[...]) * _softplus(raw + dt_ref[...])
    act = jnp.where(lane < 2 * H, _sigmoid(raw), g)
    g_ref[...] = act[:, 0:4 * H]
    act_t = act.T
    for j in range(tm // CH):
        gt_ref[j] = act_t[0:4 * H, j * CH:(j + 1) * CH]


def _in_kernel(t1, n2, tm, x_ref, *refs):
    r = _mod_row(pl.program_id(0) * tm, t1, n2)
    _in_core(tm, r, x_ref[...], *refs)


def _in_first_kernel(t1, n2, tm, xp_ref, xs_ref, pos_ref, mod_ref, nw_ref, wm_ref, wg_ref, al_ref,
                     dt_ref, x_ref, p_ref, g_ref, gt_ref):
    t0 = pl.program_id(0) * tm
    x = jnp.where(t0 < t1, xp_ref[...], xs_ref[...] + pos_ref[...])
    x_ref[...] = x
    _in_core(tm, _mod_row(t0, t1, n2), x, mod_ref, nw_ref, wm_ref, wg_ref, al_ref, dt_ref,
             p_ref, g_ref, gt_ref)


def _in_proj(xs, mod, nw, wm, wg, al, dt, t1, n2, tm):
    first = len(xs) == 3
    t_all = t1 + xs[1].shape[0] if first else xs[0].shape[0]
    nt1 = t1 // tm
    wspecs = [pl.BlockSpec((8, NMOD * D), lambda i: (0, 0)),
              pl.BlockSpec((1, D), lambda i: (0, 0)),
              pl.BlockSpec((D, P_COLS), lambda i: (0, 0)),
              pl.BlockSpec((D, 128), lambda i: (0, 0)),
              pl.BlockSpec((1, 128), lambda i: (0, 0)),
              pl.BlockSpec((1, 128), lambda i: (0, 0))]
    out_specs = [pl.BlockSpec((tm, P_COLS), lambda i: (i, 0)),
                 pl.BlockSpec((tm, 4 * H), lambda i: (i, 0)),
                 pl.BlockSpec((tm // CH, 4 * H, CH), lambda i: (i, 0, 0))]
    out_shape = [jax.ShapeDtypeStruct((t_all, P_COLS), F32),
                 jax.ShapeDtypeStruct((t_all, 4 * H), F32),
                 jax.ShapeDtypeStruct((t_all // CH, 4 * H, CH), F32)]
    if first:
        kern = _in_first_kernel
        xspecs = [pl.BlockSpec((tm, D), lambda i: (jnp.minimum(i, nt1 - 1), 0)),
                  pl.BlockSpec((tm, D), lambda i: (jnp.maximum(i - nt1, 0), 0)),
                  pl.BlockSpec((tm, D), lambda i: (jnp.maximum(i - nt1, 0) % (n2 // tm), 0))]
        out_specs = [pl.BlockSpec((tm, D), lambda i: (i, 0))] + out_specs
        out_shape = [jax.ShapeDtypeStruct((t_all, D), F32)] + out_shape
    else:
        kern = _in_kernel
        xspecs = [pl.BlockSpec((tm, D), lambda i: (i, 0))]
    return pl.pallas_call(
        functools.partial(kern, t1, n2, tm),
        grid=(t_all // tm,),
        in_specs=xspecs + wspecs,
        out_specs=out_specs,
        out_shape=out_shape,
        compiler_params=_params(("arbitrary",)),
        name="in_proj_first" if first else "in_proj",
    )(*xs, mod, nw, wm, wg, al, dt)


def _shift_conv(x, w_ref, c0, c1):
    n = x.shape[0]
    row = _iota((n, 1), 0)
    prev = jnp.where(row == 0, 0.0, pltpu.roll(x, 1, 0))
    nxt = jnp.where(row == n - 1, 0.0, pltpu.roll(x, n - 1, 0))
    return prev * w_ref[0:1, c0:c1] + x * w_ref[1:2, c0:c1] + nxt * w_ref[2:3, c0:c1]


DN_POS = 4


def _dn_kernel(n, nprev, q_ref, k_ref, v_ref, z_ref, g_ref, gt_ref, cw_ref, nw_ref, s0_ref, *refs):
    prev_ref = refs[0] if nprev else None
    o_ref, sfin_ref, qs, ks, vs, gcc_s, gct_s, s_s, oacc = refs[1 if nprev else 0:]
    nc = n // CH
    width = H * DH

    for hh in range(H):
        c0, c1 = hh * DH, (hh + 1) * DH
        q = _silu(_shift_conv(q_ref[:, c0:c1], cw_ref, c0, c1))
        q = q * lax.rsqrt(jnp.sum(q * q, axis=-1, keepdims=True) + 1e-6)
        qs[:, c0:c1] = q * (DH ** -0.5)
        k = _silu(_shift_conv(k_ref[:, c0:c1], cw_ref, width + c0, width + c1))
        ks[:, c0:c1] = k * lax.rsqrt(jnp.sum(k * k, axis=-1, keepdims=True) + 1e-6)
        vs[:, c0:c1] = _silu(_shift_conv(v_ref[:, c0:c1], cw_ref, 2 * width + c0, 2 * width + c1))

    ii = _iota((CH, CH), 0)
    jj = _iota((CH, CH), 1)
    x = gt_ref[...].reshape(nc * 4 * H, CH)
    cf = _mm(x, (ii <= jj).astype(F32), prec=HI)
    cb = _mm(x, (ii >= jj).astype(F32), prec=HI)
    rsel = (_iota((nc * 4 * H, 1), 0) & (4 * H - 1)) < 3 * H
    gct_s[...] = jnp.where(rsel, cf, cb).reshape(nc, 4 * H, CH)
    i2 = _iota((2 * CH, 2 * CH), 0)
    j2 = _iota((2 * CH, 2 * CH), 1)
    same = (i2 >= CH) == (j2 >= CH)
    lf = (same & (i2 >= j2)).astype(F32)
    lb = (same & (i2 <= j2)).astype(F32)
    lsel = _iota((1, 4 * H), 1) < 3 * H
    for m in range(n // (2 * CH)):
        y = g_ref[m * 2 * CH:(m + 1) * 2 * CH, :]
        gcc_s[m * 2 * CH:(m + 1) * 2 * CH, :] = jnp.where(lsel, _mm(lf, y, prec=HI), _mm(lb, y, prec=HI))

    for hh in range(H):
        for d in range(2):
            s_s[hh * 2 + d] = s0_ref[0, d, hh]

    eye = (ii == jj).astype(F32)
    blk_same = [(ii >> s) == (jj >> s) for s in range(1, 7)]

    chains = [(hh, d) for hh in range(H) for d in range(2)]

    def chunk_step(cc, carry):
        st = []
        for p in range(DN_POS):
            for hh, d in chains:
                c = cc * DN_POS + p
                cidx = c if d == 0 else nc - 1 - c
                r0 = pl.multiple_of(cidx * CH, CH)
                c0, c1 = hh * DH, (hh + 1) * DH
                gcol = 2 * H + H * d + hh
                bcol = H * d + hh
                incl = (ii >= jj) if d == 0 else (ii <= jj)
                strict = (ii > jj) if d == 0 else (ii < jj)
                qc = qs[pl.ds(r0, CH), c0:c1]
                kc = ks[pl.ds(r0, CH), c0:c1]
                vc = vs[pl.ds(r0, CH), c0:c1]
                beta = g_ref[pl.ds(r0, CH), bcol:bcol + 1]
                gc = gcc_s[pl.ds(r0, CH), gcol:gcol + 1]
                gr = gct_s[cidx, gcol:gcol + 1, :]
                tot = gr[:, CH - 1:CH] if d == 0 else gr[:, 0:1]
                decay = jnp.where(incl, jnp.exp(jnp.where(incl, gc - gr, 0.0)), 0.0)
                kb = kc * beta
                egc = jnp.exp(gc)
                st.append(dict(r0=r0, c0=c0, c1=c1, d=d, strict=strict, kc=kc, kb=kb, decay=decay,
                               rhs=jnp.concatenate([vc * beta, kb * egc], axis=1),
                               qd=qc * egc, qc=qc, kd=kc * jnp.exp(tot - gc), gl=jnp.exp(tot)))
        for x in st:
            x["lmat"] = jnp.where(x["strict"], _mmb(x["kb"], x["kc"], NT) * x["decay"], 0.0)
            x["qk"] = _mmb(x["qc"], x["kc"], NT) * x["decay"]
            x["tinv"] = eye - jnp.where(blk_same[0], x["lmat"], 0.0)
        for lvl in range(1, len(blk_same)):
            for x in st:
                off = jnp.where(blk_same[lvl] & ~blk_same[lvl - 1], x["lmat"], 0.0)
                x["ot"] = _mm_inv(off, x["tinv"])
            for x in st:
                x["tinv"] = x["tinv"] - _mm_inv(x["tinv"], x["ot"])
        for x in st:
            x["sol"] = _mm_inv(x["tinv"], x["rhs"])
        state = [s_s[i] for i in range(len(chains))]
        for p in range(DN_POS):
            units = st[p * len(chains):(p + 1) * len(chains)]
            for i, x in enumerate(units):
                x["sb"] = state[i].astype(BF16)
                x["v_new"] = x["sol"][:, 0:DH] - _mmb(x["sol"][:, DH:2 * DH], x["sb"])
            for i, x in enumerate(units):
                o = _mmb(x["qd"], x["sb"]) + _mmb(x["qk"], x["v_new"])
                state[i] = state[i] * x["gl"] + _mmb(x["kd"], x["v_new"], TN)
                oacc[x["d"], pl.ds(x["r0"], CH), x["c0"]:x["c1"]] = o
        for i in range(len(chains)):
            s_s[i] = state[i]
        return carry

    lax.fori_loop(0, nc // DN_POS, chunk_step, 0)

    for hh in range(H):
        c0, c1 = hh * DH, (hh + 1) * DH
        o = oacc[0, :, c0:c1] + oacc[1, :, c0:c1]
        o = o * lax.rsqrt(jnp.mean(o * o, axis=-1, keepdims=True) + EPS)
        o_ref[:, c0:c1] = o * nw_ref[...] * _silu(z_ref[:, c0:c1])
        for d in range(2):
            sfin_ref[0, nprev, d, hh] = s_s[hh * 2 + d]
    for layer in range(nprev):
        sfin_ref[0, layer] = prev_ref[0, layer]


def _deltanet(p, g, gt, conv_w, norm_w, s0, prev, n, row_blk0):
    bsz = s0.shape[0]
    nc = n // CH
    width = H * DH
    nprev = 0 if prev is None else prev.shape[1]
    state_spec = lambda k: pl.BlockSpec((1, k, 2, H, DH, DH), lambda b: (b, 0, 0, 0, 0, 0))
    prev_args = [] if prev is None else [prev]
    prev_specs = [] if prev is None else [state_spec(nprev)]
    return pl.pallas_call(
        functools.partial(_dn_kernel, n, nprev),
        grid=(bsz,),
        in_specs=[pl.BlockSpec((n, width), lambda b: (row_blk0 + b, 0)),
                  pl.BlockSpec((n, width), lambda b: (row_blk0 + b, 1)),
                  pl.BlockSpec((n, width), lambda b: (row_blk0 + b, 2)),
                  pl.BlockSpec((n, width), lambda b: (row_blk0 + b, 3)),
                  pl.BlockSpec((n, 4 * H), lambda b: (row_blk0 + b, 0)),
                  pl.BlockSpec((nc, 4 * H, CH), lambda b: (row_blk0 + b, 0, 0)),
                  pl.BlockSpec((3, 3 * width), lambda b: (0, 0)),
                  pl.BlockSpec((1, DH), lambda b: (0, 0)),
                  pl.BlockSpec((1, 2, H, DH, DH), lambda b: (b, 0, 0, 0, 0))] + prev_specs,
        out_specs=[pl.BlockSpec((n, width), lambda b: (b, 0)), state_spec(nprev + 1)],
        out_shape=[jax.ShapeDtypeStruct((bsz * n, width), F32),
                   jax.ShapeDtypeStruct((bsz, nprev + 1, 2, H, DH, DH), F32)],
        scratch_shapes=[pltpu.VMEM((n, width), F32), pltpu.VMEM((n, width), F32),
                        pltpu.VMEM((n, width), F32), pltpu.VMEM((n, 4 * H), F32),
                        pltpu.VMEM((nc, 4 * H, CH), F32), pltpu.VMEM((2 * H, DH, DH), F32),
                        pltpu.VMEM((2, n, width), F32)],
        compiler_params=_params(("arbitrary",)),
        name="deltanet_n%d" % n,
    )(p, p, p, p, g, gt, conv_w, norm_w, s0, *prev_args)


def _mix2_kernel(sb_ref, sc_ref, su_ref, fu_ref, cw_ref, bdc_ref, bds_ref, cs_ref, o_ref):
    w = sb_ref.shape[1]
    cu = sc_ref[...] * su_ref[...]
    o_ref[:, 0:w] = sb_ref[...] * _shift_conv(cu, cw_ref, 0, w)
    fu = fu_ref[...]
    a = _mm3(fu, bdc_ref[...])
    b = _mm3(fu, bds_ref[...])
    o_ref[:, w:2 * w] = _mmb(cs_ref[...], jnp.concatenate([a, b], axis=0))


def _mix2(p, conv_w, bdc, bds, cs, bsz, n, row_blk0):
    w = 256
    return pl.pallas_call(
        _mix2_kernel,
        grid=(bsz,),
        in_specs=[pl.BlockSpec((n, w), lambda b: (row_blk0 + b, 8)),
                  pl.BlockSpec((n, w), lambda b: (row_blk0 + b, 9)),
                  pl.BlockSpec((n, w), lambda b: (row_blk0 + b, 10)),
                  pl.BlockSpec((n, w), lambda b: (row_blk0 + b, 11)),
                  pl.BlockSpec((3, w), lambda b: (0, 0)),
                  pl.BlockSpec((w, w), lambda b: (0, 0)),
                  pl.BlockSpec((w, w), lambda b: (0, 0)),
                  pl.BlockSpec((n, 2 * n), lambda b: (0, 0))],
        out_specs=pl.BlockSpec((n, 2 * w), lambda b: (b, 0)),
        out_shape=jax.ShapeDtypeStruct((bsz * n, 2 * w), F32),
        compiler_params=_params(("arbitrary",)),
        name="conv_fourier_n%d" % n,
    )(p, p, p, p, conv_w, bdc, bds, cs)


def _dft_tables(n):
    gw = 64
    k = jnp.arange(n, dtype=I32)
    ph = (k[:, None] * k[None, :]) % n
    ang = ph.astype(F32) * (2.0 * math.pi / n)
    scale = 1.0 / math.sqrt(n * gw)
    cs = (jnp.concatenate([jnp.cos(ang), -jnp.sin(ang)], axis=1) * scale).astype(BF16)
    c = jnp.arange(256, dtype=I32)
    phc = ((c[:, None] % gw) * (c[None, :] % gw)) % gw
    angc = phc.astype(F32) * (2.0 * math.pi / gw)
    same = (c[:, None] // gw) == (c[None, :] // gw)
    bdc = jnp.where(same, jnp.cos(angc), 0.0)
    bds = jnp.where(same, jnp.sin(angc), 0.0)
    return cs, bdc, bds


def _out_kernel(t1, n2, tm, dn1_ref, dn2_ref, mx1_ref, mx2_ref, x_ref, mod_ref, nw_ref, wo_ref, wr_ref,
                x1_ref, h2_ref, aff_ref):
    t0 = pl.program_id(0) * tm
    r = _mod_row(t0, t1, n2)
    half = D // 2
    dn = jnp.where(t0 < t1, dn1_ref[...], dn2_ref[...])
    mx = jnp.where(t0 < t1, mx1_ref[...], mx2_ref[...])
    mix = _mmb(dn, wo_ref[0:half, :]) + _mmb(mx, wo_ref[half:D, :])
    x1 = x_ref[...] + mod_ref[pl.ds(r, 1), 2 * D:3 * D] * mix
    x1_ref[...] = x1
    y = x1 * lax.rsqrt(jnp.mean(x1 * x1, axis=-1, keepdims=True) + EPS) * nw_ref[...]
    h2 = y * (1.0 + mod_ref[pl.ds(r, 1), 4 * D:5 * D]) + mod_ref[pl.ds(r, 1), 3 * D:4 * D]
    _store_token_tiles(h2_ref, h2)
    logits = _mm3(h2, wr_ref[...])
    lt = logits.T[0:E, :]
    ex = jnp.exp(lt - jnp.max(lt, axis=0, keepdims=True))
    aff_ref[...] = ex / jnp.sum(ex, axis=0, keepdims=True)


def _out_proj(dn1, dn2, mx1, mx2, x, mod, nw, wo, wr, t1, n2, tm):
    t_all = x.shape[0]
    nt1 = t1 // tm
    spec1 = pl.BlockSpec((tm, D // 2), lambda i: (jnp.minimum(i, nt1 - 1), 0))
    spec2 = pl.BlockSpec((tm, D // 2), lambda i: (jnp.maximum(i - nt1, 0), 0))
    return pl.pallas_call(
        functools.partial(_out_kernel, t1, n2, tm),
        grid=(t_all // tm,),
        in_specs=[spec1, spec2, spec1, spec2,
                  pl.BlockSpec((tm, D), lambda i: (i, 0)),
                  pl.BlockSpec((8, NMOD * D), lambda i: (0, 0)),
                  pl.BlockSpec((1, D), lambda i: (0, 0)),
                  pl.BlockSpec((D, D), lambda i: (0, 0)),
                  pl.BlockSpec((D, 128), lambda i: (0, 0))],
        out_specs=[pl.BlockSpec((tm, D), lambda i: (i, 0)),
                   pl.BlockSpec((tm * ROW_TILES, 128), lambda i: (i, 0)),
                   pl.BlockSpec((E, tm), lambda i: (0, i))],
        out_shape=[jax.ShapeDtypeStruct((t_all, D), F32),
                   jax.ShapeDtypeStruct((t_all * ROW_TILES, 128), F32),
                   jax.ShapeDtypeStruct((E, t_all), F32)],
        compiler_params=_params(("arbitrary",)),
        name="out_proj",
    )(dn1, dn2, mx1, mx2, x, mod, nw, wo, wr)


SEL_TB = 128
N_SLOT_ROWS = 16
SPILL_BLOCKS = 2


def _sel_kernel(t, cap, aff_ref, slots_ref, lo_ref, hi_ref, a3, c3, starts_v, starts_s, sem):
    ntb = t // SEL_TB
    aff = aff_ref[...]

    def search(i, thr):
        cand = thr | jnp.left_shift(jnp.int32(1), 30 - i)
        cnt = jnp.sum((aff >= pltpu.bitcast(cand, F32)).astype(F32), axis=1, keepdims=True)
        return jnp.where(cnt >= cap, cand, thr)

    thr = lax.fori_loop(0, 31, search, jnp.zeros((E, 1), I32))
    gt = (aff >= pltpu.bitcast(thr + 1, F32)).astype(F32)
    eq = (aff >= pltpu.bitcast(thr, F32)).astype(F32) - gt
    need = cap - jnp.sum(gt, axis=1, keepdims=True)

    ui = _iota((SEL_TB, SEL_TB), 0)
    uj = _iota((SEL_TB, SEL_TB), 1)
    upper = (ui < uj).astype(BF16)
    blk_lane = _iota((1, 128), 1)

    def excl_cumsum(rows, dst, r0, r1):
        carry = jnp.zeros((rows.shape[0], 1), F32)
        starts = jnp.zeros((rows.shape[0], 128), F32)
        for j in range(ntb):
            blk = rows[:, j * SEL_TB:(j + 1) * SEL_TB]
            dst[j, r0:r1, :] = _mm(blk.astype(BF16), upper) + carry
            starts = jnp.where(blk_lane == j, carry, starts)
            carry = carry + jnp.sum(blk, axis=1, keepdims=True)
        return starts

    excl_cumsum(eq, c3, 0, E)
    rank_eq = jnp.concatenate([c3[j, 0:E, :] for j in range(ntb)], axis=1)
    sel = jnp.maximum(gt, jnp.where(rank_eq < need, eq, 0.0))
    n_tok = jnp.sum(sel, axis=0, keepdims=True)
    starts = excl_cumsum(jnp.concatenate([sel, jnp.broadcast_to(n_tok, (8, t))], axis=0), c3, 0, E + 8)
    starts_v[...] = starts.astype(I32)
    to_smem = pltpu.make_async_copy(starts_v, starts_s, sem)
    to_smem.start()
    ei = _iota((E, E), 0)
    ej = _iota((E, E), 1)
    below = _mm((ej < ei).astype(BF16), sel.astype(BF16))
    for j in range(ntb):
        sl = slice(j * SEL_TB, (j + 1) * SEL_TB)
        off = c3[j, E:E + 1, :]
        lo_ref[:, sl] = off
        hi_ref[:, sl] = off + n_tok[:, sl]
        rank = off + below[:, sl]
        a3[j, 0] = sel[:, sl]
        a3[j, 1] = aff[:, sl]
        a3[j, 2] = rank
    slots_ref[...] = jnp.zeros(slots_ref.shape, F32)
    to_smem.wait()

    win_iota = _iota((2 * SEL_TB, SEL_TB), 0)
    tok_iota = _iota((1, SEL_TB), 1)

    def per_expert(e, carry):
        def per_block(j, carry2):
            wb = starts_s[e, j] >> 7
            pos = c3[j, pl.ds(e, 1), :].astype(I32) - wb * SEL_TB
            chosen = a3[j, 0, pl.ds(e, 1), :]
            w = a3[j, 1, pl.ds(e, 1), :]
            rank = a3[j, 2, pl.ds(e, 1), :].astype(I32)
            tok = tok_iota + j * SEL_TB
            w_hi = w.astype(BF16).astype(F32)
            w_mid = (w - w_hi).astype(BF16).astype(F32)
            w_lo = w - w_hi - w_mid
            vals = jnp.concatenate(
                [(tok >> 7).astype(F32), (tok & 127).astype(F32), w_hi, w_mid, w_lo,
                 (rank >> 7).astype(F32), (rank & 127).astype(F32),
                 jnp.zeros((N_SLOT_ROWS - 7, SEL_TB), F32)], axis=0)
            onehot = jnp.where((win_iota == pos) & (chosen > 0.0), 1.0, 0.0)
            placed = _mmb(vals, onehot, NT)
            slots_ref[e, wb] += placed[:, 0:SEL_TB]
            slots_ref[e, wb + 1] += placed[:, SEL_TB:2 * SEL_TB]
            return carry2
        return lax.fori_loop(0, ntb, per_block, carry, unroll=8)

    lax.fori_loop(0, E, per_expert, 0)


def _select(aff_t, t, cap, col_blk):
    ntb = t // SEL_TB
    nsb = cap // SEL_TB + SPILL_BLOCKS
    return pl.pallas_call(
        functools.partial(_sel_kernel, t, cap),
        grid=(1,),
        in_specs=[pl.BlockSpec((E, t), lambda i: (0, col_blk))],
        out_specs=[pl.BlockSpec((E, nsb, N_SLOT_ROWS, SEL_TB), lambda i: (0, 0, 0, 0)),
                   pl.BlockSpec((1, t), lambda i: (0, 0)),
                   pl.BlockSpec((1, t), lambda i: (0, 0))],
        out_shape=[jax.ShapeDtypeStruct((E, nsb, N_SLOT_ROWS, SEL_TB), F32),
                   jax.ShapeDtypeStruct((1, t), F32),
                   jax.ShapeDtypeStruct((1, t), F32)],
        scratch_shapes=[pltpu.VMEM((ntb, 3, E, SEL_TB), F32),
                        pltpu.VMEM((ntb, E + 8, SEL_TB), F32),
                        pltpu.VMEM((E + 8, 128), I32),
                        pltpu.SMEM((E + 8, 128), I32),
                        pltpu.SemaphoreType.DMA(())],
        compiler_params=_params(("arbitrary",)),
        name="select_t%d" % t,
    )(aff_t)


DMA_UNROLL = 8


def _ffn_kernel(rc, nch, idx_ref, dst_ref, h2_hbm, wcol_ref, wg_ref, wu_ref, wd_ref, z_hbm,
                xbuf, ybuf, wgb, wub, wdb, gsem, ssem):
    c = pl.program_id(1)
    step = pl.program_id(0) * nch + c
    nsteps = E * nch
    slot = step % 2

    def start_gather(s):
        buf = xbuf.at[s % 2]
        sem = gsem.at[s % 2]

        def issue(jb, carry):
            for u in range(DMA_UNROLL):
                j = jb * DMA_UNROLL + u
                src = pl.multiple_of(idx_ref[s * rc + j], ROW_TILES)
                pltpu.make_async_copy(h2_hbm.at[pl.ds(src, ROW_TILES)],
                                      buf.at[pl.ds(j * ROW_TILES, ROW_TILES)], sem).start()
            return carry

        lax.fori_loop(0, rc // DMA_UNROLL, issue, 0)

    @pl.when(step == 0)
    def _():
        start_gather(step)

    @pl.when(step + 1 < nsteps)
    def _():
        start_gather(step + 1)

    @pl.when(c == 0)
    def _():
        wgb[...] = wg_ref[0, 0].astype(BF16)
        wub[...] = wu_ref[0, 0].astype(BF16)
        wdb[...] = wd_ref[0, 0].astype(BF16)

    pltpu.make_async_copy(h2_hbm.at[pl.ds(0, rc * ROW_TILES)], xbuf.at[slot], gsem.at[slot]).wait()
    xb = _load_token_tiles(xbuf.at[slot], rc).astype(BF16)
    hid = _silu(_mm(xb, wgb[...])) * _mm(xb, wub[...])
    y = _mm(hid.astype(BF16), wdb[...]) * wcol_ref[...]

    @pl.when(step > 0)
    def _():
        pltpu.make_async_copy(ybuf, z_hbm.at[pl.ds(0, rc * ROW_TILES)], ssem).wait()

    _store_token_tiles(ybuf, y)

    def scatter(jb, carry):
        for u in range(DMA_UNROLL):
            j = jb * DMA_UNROLL + u
            row = pl.multiple_of(dst_ref[step * rc + j], ROW_TILES)
            pltpu.make_async_copy(ybuf.at[pl.ds(j * ROW_TILES, ROW_TILES)],
                                  z_hbm.at[pl.ds(row, ROW_TILES)], ssem).start()
        return carry

    lax.fori_loop(0, rc // DMA_UNROLL, scatter, 0)

    @pl.when(step == nsteps - 1)
    def _():
        pltpu.make_async_copy(ybuf, z_hbm.at[pl.ds(0, rc * ROW_TILES)], ssem).wait()


def _expert_ffn(idx, dst, h2, wcol, w_gate, w_up, w_down, layer, rc, nch):
    zrows = E * nch * rc
    wspec = pl.BlockSpec((1, 1, D, D), lambda e, c, *_: (layer, e, 0, 0))
    return pl.pallas_call(
        functools.partial(_ffn_kernel, rc, nch),
        grid_spec=pltpu.PrefetchScalarGridSpec(
            num_scalar_prefetch=2,
            grid=(E, nch),
            in_specs=[pl.BlockSpec(memory_space=pl.ANY),
                      pl.BlockSpec((rc, 1), lambda e, c, *_: (e * nch + c, 0)),
                      wspec, wspec, wspec],
            out_specs=pl.BlockSpec(memory_space=pl.ANY),
            scratch_shapes=[pltpu.VMEM((2, rc * ROW_TILES, 128), F32),
                            pltpu.VMEM((rc * ROW_TILES, 128), F32),
                            pltpu.VMEM((D, D), BF16), pltpu.VMEM((D, D), BF16),
                            pltpu.VMEM((D, D), BF16),
                            pltpu.SemaphoreType.DMA((2,)), pltpu.SemaphoreType.DMA(())]),
        out_shape=jax.ShapeDtypeStruct((zrows * ROW_TILES, 128), F32),
        compiler_params=_params(("arbitrary", "arbitrary")),
        name="expert_ffn",
    )(idx, dst, h2, wcol, w_gate, w_up, w_down)


CMB_TB = 256
CMB_RC = 512


def _comb_kernel(t1, n2, zrows, final, bs_ref, be_ref, x1_ref, lo_ref, hi_ref, mod_ref, fw_ref, z_hbm,
                 *refs):
    outs, (zbuf, sem) = refs[:-2], refs[-2:]
    i = pl.program_id(0)
    nsteps = pl.num_programs(0)
    r = _mod_row(i * CMB_TB, t1, n2)

    def first_row(step):
        return (bs_ref[step] // 8) * 8

    def chunk_copy(nominal, slot):
        b = pl.multiple_of(jnp.minimum(nominal, zrows - CMB_RC) * ROW_TILES, 8 * ROW_TILES)
        return pltpu.make_async_copy(z_hbm.at[pl.ds(b, CMB_RC * ROW_TILES)], zbuf.at[slot], sem.at[slot])

    base0 = first_row(i)
    nchunk = jnp.maximum((be_ref[i] - base0 + CMB_RC - 1) // CMB_RC, 1)

    @pl.when(i == 0)
    def _():
        chunk_copy(base0, 0).start()

    eye = _iota((CMB_TB, CMB_TB), 0) == _iota((CMB_TB, CMB_TB), 1)
    lo = jnp.sum(jnp.where(eye, lo_ref[...], 0.0), axis=1, keepdims=True)
    hi = jnp.sum(jnp.where(eye, hi_ref[...], 0.0), axis=1, keepdims=True)
    col = _iota((1, CMB_RC), 1)

    def chunk(ci, acc):
        nominal = base0 + ci * CMB_RC
        slot = ci % 2

        @pl.when(ci + 1 < nchunk)
        def _():
            chunk_copy(nominal + CMB_RC, 1 - slot).start()

        chunk_copy(nominal, slot).wait()
        b = jnp.minimum(nominal, zrows - CMB_RC)
        rows = (col + b).astype(F32)
        pick = (rows >= lo) & (rows < hi) & (rows >= nominal.astype(F32))
        s = jnp.where(pick, 1.0, 0.0).astype(BF16)
        z = _load_token_tiles(zbuf.at[slot], CMB_RC)
        z_hi = z.astype(BF16)
        z_lo = (z - z_hi.astype(F32)).astype(BF16)
        return acc + _mm(s, z_hi) + _mm(s, z_lo)

    moe = lax.fori_loop(0, nchunk, chunk, jnp.zeros((CMB_TB, D), F32))

    @pl.when(i + 1 < nsteps)
    def _():
        chunk_copy(first_row(jnp.minimum(i + 1, nsteps - 1)), 0).start()

    x2 = x1_ref[...] + mod_ref[pl.ds(r, 1), 5 * D:6 * D] * moe
    if final:
        y = x2 * lax.rsqrt(jnp.mean(x2 * x2, axis=-1, keepdims=True) + EPS) * fw_ref[...]

        @pl.when(i * CMB_TB < t1)
        def _():
            outs[0][...] = y

        @pl.when(i * CMB_TB >= t1)
        def _():
            outs[1][...] = y
    else:
        outs[0][...] = x2


def _combine(bstart, bend, x1, lo, hi, mod, fw, z, t1, n2, final):
    t_all = x1.shape[0]
    zrows = z.shape[0] // ROW_TILES
    nb1 = t1 // CMB_TB
    if final:
        out_specs = [pl.BlockSpec((CMB_TB, D), lambda i, *_: (jnp.minimum(i, nb1 - 1), 0)),
                     pl.BlockSpec((CMB_TB, D), lambda i, *_: (jnp.maximum(i - nb1, 0), 0))]
        out_shape = [jax.ShapeDtypeStruct((t1, D), F32), jax.ShapeDtypeStruct((t_all - t1, D), F32)]
    else:
        out_specs = pl.BlockSpec((CMB_TB, D), lambda i, *_: (i, 0))
        out_shape = jax.ShapeDtypeStruct((t_all, D), F32)
    return pl.pallas_call(
        functools.partial(_comb_kernel, t1, n2, zrows, final),
        grid_spec=pltpu.PrefetchScalarGridSpec(
            num_scalar_prefetch=2,
            grid=(t_all // CMB_TB,),
            in_specs=[pl.BlockSpec((CMB_TB, D), lambda i, *_: (i, 0)),
                      pl.BlockSpec((1, CMB_TB), lambda i, *_: (0, i)),
                      pl.BlockSpec((1, CMB_TB), lambda i, *_: (0, i)),
                      pl.BlockSpec((8, NMOD * D), lambda i, *_: (0, 0)),
                      pl.BlockSpec((1, D), lambda i, *_: (0, 0)),
                      pl.BlockSpec(memory_space=pl.ANY)],
            out_specs=out_specs,
            scratch_shapes=[pltpu.VMEM((2, CMB_RC * ROW_TILES, 128), F32),
                            pltpu.SemaphoreType.DMA((2,))]),
        out_shape=out_shape,
        compiler_params=_params(("arbitrary",)),
        name="combine",
    )(bstart, bend, x1, lo, hi, mod, fw, z)


def _grid_pos_embed(n, d):
    rows = n // GRID_W
    r = jnp.repeat(jnp.arange(rows, dtype=F32), GRID_W)
    col = jnp.tile(jnp.arange(GRID_W, dtype=F32), rows)
    quarter = d // 4
    omega = jnp.power(POS_BASE, -jnp.arange(quarter, dtype=F32) / quarter)
    ra = r[:, None] * omega
    ca = col[:, None] * omega
    return jnp.concatenate([jnp.sin(ra), jnp.cos(ra), jnp.sin(ca), jnp.cos(ca)], axis=-1)


def _decode_slots(slots):
    e, nsb, rows, tb = slots.shape
    nsb -= SPILL_BLOCKS
    slots = slots[:, :nsb].transpose(0, 2, 1, 3).reshape(e, rows, nsb * tb)
    idx = (slots[:, 0] * 128.0 + slots[:, 1]).astype(I32)
    w = slots[:, 2] + slots[:, 3] + slots[:, 4]
    rank = (slots[:, 5] * 128.0 + slots[:, 6]).astype(I32)
    return idx, w, rank


def kernel(x_prompt, x_sample, state_delta, c, c_ctx, w_ada, b_ada, norm1_w, norm2_w, w_in, dn_conv_w,
           dn_a_log, dn_dt_bias, dn_norm_w, sc_conv_w, w_out, w_router, w_gate, w_up, w_down,
           final_norm_w):
    b1, n1, _ = x_prompt.shape
    b2, n2, _ = x_sample.shape
    depth = w_ada.shape[0]
    t1, t2 = b1 * n1, b2 * n2
    t_all = t1 + t2
    cap1 = max(1, 2 * t1 // E)
    cap2 = max(1, 2 * t2 // E)
    tm = 512
    rc = 512 if (cap1 % 512 == 0 and cap2 % 512 == 0) else 128
    assert t1 % n2 == 0 and t1 % tm == 0 and n2 % tm == 0
    assert n1 % (DN_POS * CH) == 0 and n2 % (DN_POS * CH) == 0
    assert t1 % SEL_TB == 0 and t2 % SEL_TB == 0 and t1 % t2 == 0
    assert cap1 % rc == 0 and cap2 % rc == 0 and 2 * t_all >= CMB_RC
    nch = (cap1 + cap2) // rc
    width = H * DH

    cond8 = jnp.zeros((8, D), F32).at[0].set(c_ctx).at[1:1 + b2].set(c)
    mod = _ada(cond8, w_ada, b_ada)

    x = None
    s_zero = jnp.zeros((b1, 2, H, DH, DH), F32)
    tabs1 = _dft_tables(n1)
    tabs2 = _dft_tables(n2)
    ctx_states = None

    for l in range(depth):
        wl = w_in[l]
        wm = jnp.concatenate([wl[:, 0:4 * width], wl[:, 4 * width + 4 * H:]], axis=1).astype(BF16)
        wg = jnp.pad(wl[:, 4 * width:4 * width + 4 * H], ((0, 0), (0, 128 - 4 * H))).astype(BF16)
        al = jnp.pad(dn_a_log[l].reshape(1, 2 * H), ((0, 0), (2 * H, 128 - 4 * H)))
        dt = jnp.pad(dn_dt_bias[l].reshape(1, 2 * H), ((0, 0), (2 * H, 128 - 4 * H)))
        xs = (x,) if l else (x_prompt.reshape(t1, D), x_sample.reshape(t2, D), _grid_pos_embed(n2, D))
        res = _in_proj(xs, mod[l], norm1_w[l].reshape(1, D), wm, wg, al, dt, t1, n2, tm)
        if not l:
            x, res = res[0], res[1:]
        p, g, gt = res

        nwd = dn_norm_w[l].reshape(1, DH)
        dn1, ctx_states = _deltanet(p, g, gt, dn_conv_w[l], nwd, s_zero, ctx_states, n1, 0)
        dn2, _ = _deltanet(p, g, gt, dn_conv_w[l], nwd, state_delta[:, l], None, n2, t1 // n2)
        mx1 = _mix2(p, sc_conv_w[l], tabs1[1], tabs1[2], tabs1[0], b1, n1, 0)
        mx2 = _mix2(p, sc_conv_w[l], tabs2[1], tabs2[2], tabs2[0], b2, n2, t1 // n2)

        wr = jnp.pad(w_router[l], ((0, 0), (0, 128 - E)))
        x1, h2, aff_t = _out_proj(dn1, dn2, mx1, mx2, x, mod[l], norm2_w[l].reshape(1, D),
                                  w_out[l].astype(BF16), wr, t1, n2, tm)

        slots1, lo1, hi1 = _select(aff_t, t1, cap1, 0)
        slots2, lo2, hi2 = _select(aff_t, t2, cap2, t1 // t2)
        idx1, wsel1, rank1 = _decode_slots(slots1)
        idx2, wsel2, rank2 = _decode_slots(slots2)
        idx = jnp.concatenate([idx1, idx2 + t1], axis=1).reshape(-1) * ROW_TILES
        dst = jnp.concatenate([rank1, rank2 + 2 * t1], axis=1).reshape(-1) * ROW_TILES
        wcol = jnp.concatenate([wsel1, wsel2], axis=1).reshape(-1, 1)
        z = _expert_ffn(idx, dst, h2, wcol, w_gate, w_up, w_down, l, rc, nch)

        lo = jnp.concatenate([lo1, lo2 + 2.0 * t1], axis=1)
        hi = jnp.concatenate([hi1, hi2 + 2.0 * t1], axis=1)
        bstart = lo[0, ::CMB_TB].astype(I32)
        bend = hi[0, CMB_TB - 1::CMB_TB].astype(I32)
        x = _combine(bstart, bend, x1, lo, hi, mod[l], final_norm_w.reshape(1, D), z, t1, n2,
                     l == depth - 1)

    y_prompt, y_sample = x
    return y_prompt.reshape(b1, n1, D), y_sample.reshape(b2, n2, D), ctx_states
```

```python
import functools
import math

import jax
import jax.numpy as jnp
import numpy as np
from jax import lax
from jax.experimental import pallas as pl
from jax.experimental.pallas import tpu as pltpu

F32 = jnp.float32
BF16 = jnp.bfloat16
I32 = jnp.int32
HI = lax.Precision.HIGHEST

D = 1024
H = 4
DH = 128
CH = 64
E = 16
NMOD = 6
EPS = 1e-6
GRID_W = 64
POS_BASE = 10000.0
P_COLS = 3072
VMEM_LIMIT = 56 * 1024 * 1024

NN = (((1,), (0,)), ((), ()))
NT = (((1,), (1,)), ((), ()))
TN = (((0,), (0,)), ((), ()))


def _mm(a, b, dims=NN, prec=None):
    return lax.dot_general(a, b, dims, precision=prec, preferred_element_type=F32)


def _mmb(a, b, dims=NN):
    return lax.dot_general(a.astype(BF16), b.astype(BF16), dims, preferred_element_type=F32)


def _mm3(a, b, dims=NN):
    a_hi = a.astype(BF16)
    a_lo = (a - a_hi.astype(F32)).astype(BF16)
    b_hi = b.astype(BF16)
    b_lo = (b - b_hi.astype(F32)).astype(BF16)
    return _mm(a_hi, b_hi, dims) + _mm(a_hi, b_lo, dims) + _mm(a_lo, b_hi, dims)


def _mm_inv(a, b):
    return _mmb(a, b)


def _silu(x):
    return x / (1.0 + jnp.exp(-x))


def _sigmoid(x):
    return 1.0 / (1.0 + jnp.exp(-x))


def _softplus(x):
    return jnp.maximum(x, 0.0) + jnp.log1p(jnp.exp(-jnp.abs(x)))


def _iota(shape, dim):
    return lax.broadcasted_iota(I32, shape, dim)


ROW_TILES = D // 128


def _load_token_tiles(ref, n):
    return jnp.concatenate([ref[pl.ds(s, n, stride=ROW_TILES), :] for s in range(ROW_TILES)], axis=1)


def _store_token_tiles(ref, x):
    n = x.shape[0]
    for s in range(ROW_TILES):
        ref[pl.ds(s, n, stride=ROW_TILES), :] = x[:, s * 128:(s + 1) * 128]


def _mod_row(t0, t1, n2):
    return jnp.where(t0 < t1, 0, 1 + jnp.maximum(t0 - t1, 0) // n2)


def _params(sem):
    return pltpu.CompilerParams(dimension_semantics=sem, vmem_limit_bytes=VMEM_LIMIT)


def _ada_kernel(c_ref, w_ref, b_ref, o_ref):
    s = _silu(c_ref[...])
    o_ref[0] = _mm3(s, w_ref[0]) + b_ref[0]


def _ada(cond8, w_ada, b_ada):
    depth = w_ada.shape[0]
    tn = 1536
    return pl.pallas_call(
        _ada_kernel,
        grid=(depth, NMOD * D // tn),
        in_specs=[pl.BlockSpec((8, D), lambda l, j: (0, 0)),
                  pl.BlockSpec((1, D, tn), lambda l, j: (l, 0, j)),
                  pl.BlockSpec((1, 1, tn), lambda l, j: (l, 0, j))],
        out_specs=pl.BlockSpec((1, 8, tn), lambda l, j: (l, 0, j)),
        out_shape=jax.ShapeDtypeStruct((depth, 8, NMOD * D), F32),
        compiler_params=_params(("arbitrary", "arbitrary")),
        name="ada",
    )(cond8, w_ada, b_ada.reshape(depth, 1, NMOD * D))


IN_SUB = 2


def _in_core(tm, r, x, mod_ref, nw_ref, wm_ref, wg_ref, al_ref, dt_ref, p_ref, g_ref, gt_ref):
    sh = mod_ref[pl.ds(r, 1), 0:D]
    sc = mod_ref[pl.ds(r, 1), D:2 * D]
    sub = tm // IN_SUB
    nsplit = 4
    wc = P_COLS // nsplit
    lane = _iota((1, 128), 1)
    hbs = []
    for k in range(IN_SUB):
        xk = x[k * sub:(k + 1) * sub]
        y = xk * lax.rsqrt(jnp.mean(xk * xk, axis=-1, keepdims=True) + EPS) * nw_ref[...]
        hbs.append((y * (1.0 + sc) + sh).astype(BF16))
    for k in range(IN_SUB):
        rs = slice(k * sub, (k + 1) * sub)
        for j in range(nsplit):
            p_ref[rs, j * wc:(j + 1) * wc] = _mm(hbs[k], wm_ref[:, j * wc:(j + 1) * wc])
    for k in range(IN_SUB):
        raw = _mm(hbs[k], wg_ref[...])
        g = -jnp.exp(al_ref[...]) * _softplus(raw + dt_ref[...])
        act = jnp.where(lane < 2 * H, _sigmoid(raw), g)
        g_ref[k * sub:(k + 1) * sub, :] = act[:, 0:4 * H]
        act_t = act.T
        for j in range(sub // CH):
            gt_ref[k * (sub // CH) + j] = act_t[0:4 * H, j * CH:(j + 1) * CH]


def _in_kernel(t1, n2, tm, x_ref, *refs):
    r = _mod_row(pl.program_id(0) * tm, t1, n2)
    _in_core(tm, r, x_ref[...], *refs)


def _in_first_kernel(t1, n2, tm, xp_ref, xs_ref, pos_ref, mod_ref, nw_ref, wm_ref, wg_ref, al_ref,
                     dt_ref, x_ref, p_ref, g_ref, gt_ref):
    t0 = pl.program_id(0) * tm
    x = jnp.where(t0 < t1, xp_ref[...], xs_ref[...] + pos_ref[...])
    x_ref[...] = x
    _in_core(tm, _mod_row(t0, t1, n2), x, mod_ref, nw_ref, wm_ref, wg_ref, al_ref, dt_ref,
             p_ref, g_ref, gt_ref)


def _in_proj(xs, mod, nw, wm, wg, al, dt, t1, n2, tm):
    first = len(xs) == 3
    t_all = t1 + xs[1].shape[0] if first else xs[0].shape[0]
    nt1 = t1 // tm
    wspecs = [pl.BlockSpec((8, NMOD * D), lambda i: (0, 0)),
              pl.BlockSpec((1, D), lambda i: (0, 0)),
              pl.BlockSpec((D, P_COLS), lambda i: (0, 0)),
              pl.BlockSpec((D, 128), lambda i: (0, 0)),
              pl.BlockSpec((1, 128), lambda i: (0, 0)),
              pl.BlockSpec((1, 128), lambda i: (0, 0))]
    out_specs = [pl.BlockSpec((tm, P_COLS), lambda i: (i, 0)),
                 pl.BlockSpec((tm, 4 * H), lambda i: (i, 0)),
                 pl.BlockSpec((tm // CH, 4 * H, CH), lambda i: (i, 0, 0))]
    out_shape = [jax.ShapeDtypeStruct((t_all, P_COLS), F32),
                 jax.ShapeDtypeStruct((t_all, 4 * H), F32),
                 jax.ShapeDtypeStruct((t_all // CH, 4 * H, CH), F32)]
    if first:
        kern = _in_first_kernel
        xspecs = [pl.BlockSpec((tm, D), lambda i: (jnp.minimum(i, nt1 - 1), 0)),
                  pl.BlockSpec((tm, D), lambda i: (jnp.maximum(i - nt1, 0), 0)),
                  pl.BlockSpec((tm, D), lambda i: (jnp.maximum(i - nt1, 0) % (n2 // tm), 0))]
        out_specs = [pl.BlockSpec((tm, D), lambda i: (i, 0))] + out_specs
        out_shape = [jax.ShapeDtypeStruct((t_all, D), F32)] + out_shape
    else:
        kern = _in_kernel
        xspecs = [pl.BlockSpec((tm, D), lambda i: (i, 0))]
    return pl.pallas_call(
        functools.partial(kern, t1, n2, tm),
        grid=(t_all // tm,),
        in_specs=xspecs + wspecs,
        out_specs=out_specs,
        out_shape=out_shape,
        compiler_params=_params(("arbitrary",)),
        name="in_proj_first" if first else "in_proj",
    )(*xs, mod, nw, wm, wg, al, dt)


def _shift_conv(x, w_ref, c0, c1):
    n = x.shape[0]
    row = _iota((n, 1), 0)
    prev = jnp.where(row == 0, 0.0, pltpu.roll(x, 1, 0))
    nxt = jnp.where(row == n - 1, 0.0, pltpu.roll(x, n - 1, 0))
    return prev * w_ref[0:1, c0:c1] + x * w_ref[1:2, c0:c1] + nxt * w_ref[2:3, c0:c1]


DN_POS = 4


def _dn_kernel(n, nprev, q_ref, k_ref, v_ref, z_ref, g_ref, gt_ref, cw_ref, nw_ref, s0_ref, *refs):
    prev_ref = refs[0] if nprev else None
    o_ref, sfin_ref, qs, ks, vs, gcc_s, gct_s, s_s, oacc = refs[1 if nprev else 0:]
    nc = n // CH
    width = H * DH

    for hh in range(H):
        c0, c1 = hh * DH, (hh + 1) * DH
        q = _silu(_shift_conv(q_ref[:, c0:c1], cw_ref, c0, c1))
        q = q * lax.rsqrt(jnp.sum(q * q, axis=-1, keepdims=True) + 1e-6)
        qs[:, c0:c1] = q * (DH ** -0.5)
        k = _silu(_shift_conv(k_ref[:, c0:c1], cw_ref, width + c0, width + c1))
        ks[:, c0:c1] = k * lax.rsqrt(jnp.sum(k * k, axis=-1, keepdims=True) + 1e-6)
        vs[:, c0:c1] = _silu(_shift_conv(v_ref[:, c0:c1], cw_ref, 2 * width + c0, 2 * width + c1))

    ii = _iota((CH, CH), 0)
    jj = _iota((CH, CH), 1)
    x = gt_ref[...].reshape(nc * 4 * H, CH)
    cf = _mm(x, (ii <= jj).astype(F32), prec=HI)
    cb = _mm(x, (ii >= jj).astype(F32), prec=HI)
    rsel = (_iota((nc * 4 * H, 1), 0) & (4 * H - 1)) < 3 * H
    gct_s[...] = jnp.where(rsel, cf, cb).reshape(nc, 4 * H, CH)
    i2 = _iota((2 * CH, 2 * CH), 0)
    j2 = _iota((2 * CH, 2 * CH), 1)
    same = (i2 >= CH) == (j2 >= CH)
    lf = (same & (i2 >= j2)).astype(F32)
    lb = (same & (i2 <= j2)).astype(F32)
    lsel = _iota((1, 4 * H), 1) < 3 * H
    for m in range(n // (2 * CH)):
        y = g_ref[m * 2 * CH:(m + 1) * 2 * CH, :]
        gcc_s[m * 2 * CH:(m + 1) * 2 * CH, :] = jnp.where(lsel, _mm(lf, y, prec=HI), _mm(lb, y, prec=HI))

    for hh in range(H):
        for d in range(2):
            s_s[hh * 2 + d] = s0_ref[0, d, hh]

    eye = (ii == jj).astype(F32)
    blk_same = [(ii >> s) == (jj >> s) for s in range(1, 7)]

    chains = [(hh, d) for hh in range(H) for d in range(2)]

    def chunk_step(cc, carry):
        st = []
        for p in range(DN_POS):
            for hh, d in chains:
                c = cc * DN_POS + p
                cidx = c if d == 0 else nc - 1 - c
                r0 = pl.multiple_of(cidx * CH, CH)
                c0, c1 = hh * DH, (hh + 1) * DH
                gcol = 2 * H + H * d + hh
                bcol = H * d + hh
                incl = (ii >= jj) if d == 0 else (ii <= jj)
                strict = (ii > jj) if d == 0 else (ii < jj)
                qc = qs[pl.ds(r0, CH), c0:c1]
                kc = ks[pl.ds(r0, CH), c0:c1]
                vc = vs[pl.ds(r0, CH), c0:c1]
                beta = g_ref[pl.ds(r0, CH), bcol:bcol + 1]
                gc = gcc_s[pl.ds(r0, CH), gcol:gcol + 1]
                gr = gct_s[cidx, gcol:gcol + 1, :]
                tot = gr[:, CH - 1:CH] if d == 0 else gr[:, 0:1]
                decay = jnp.where(incl, jnp.exp(jnp.where(incl, gc - gr, 0.0)), 0.0)
                kb = kc * beta
                egc = jnp.exp(gc)
                st.append(dict(r0=r0, c0=c0, c1=c1, d=d, strict=strict, kc=kc, kb=kb, decay=decay,
                               rhs=jnp.concatenate([vc * beta, kb * egc], axis=1),
                               qd=qc * egc, qc=qc, kd=kc * jnp.exp(tot - gc), gl=jnp.exp(tot)))
        for x in st:
            x["lmat"] = jnp.where(x["strict"], _mmb(x["kb"], x["kc"], NT) * x["decay"], 0.0)
            x["qk"] = _mmb(x["qc"], x["kc"], NT) * x["decay"]
            x["tinv"] = eye - jnp.where(blk_same[0], x["lmat"], 0.0)
        for lvl in range(1, len(blk_same)):
            for x in st:
                off = jnp.where(blk_same[lvl] & ~blk_same[lvl - 1], x["lmat"], 0.0)
                x["ot"] = _mm_inv(off, x["tinv"])
            for x in st:
                x["tinv"] = x["tinv"] - _mm_inv(x["tinv"], x["ot"])
        for x in st:
            x["sol"] = _mm_inv(x["tinv"], x["rhs"])
        state = [s_s[i] for i in range(len(chains))]
        for p in range(DN_POS):
            units = st[p * len(chains):(p + 1) * len(chains)]
            for i, x in enumerate(units):
                x["sb"] = state[i].astype(BF16)
                x["v_new"] = x["sol"][:, 0:DH] - _mmb(x["sol"][:, DH:2 * DH], x["sb"])
            for i, x in enumerate(units):
                o = _mmb(x["qd"], x["sb"]) + _mmb(x["qk"], x["v_new"])
                state[i] = state[i] * x["gl"] + _mmb(x["kd"], x["v_new"], TN)
                oacc[x["d"], pl.ds(x["r0"], CH), x["c0"]:x["c1"]] = o
        for i in range(len(chains)):
            s_s[i] = state[i]
        return carry

    lax.fori_loop(0, nc // DN_POS, chunk_step, 0)

    for hh in range(H):
        c0, c1 = hh * DH, (hh + 1) * DH
        o = oacc[0, :, c0:c1] + oacc[1, :, c0:c1]
        o = o * lax.rsqrt(jnp.mean(o * o, axis=-1, keepdims=True) + EPS)
        o_ref[:, c0:c1] = o * nw_ref[...] * _silu(z_ref[:, c0:c1])
        for d in range(2):
            sfin_ref[0, nprev, d, hh] = s_s[hh * 2 + d]
    for layer in range(nprev):
        sfin_ref[0, layer] = prev_ref[0, layer]


def _deltanet(p, g, gt, conv_w, norm_w, s0, prev, n, row_blk0):
    bsz = s0.shape[0]
    nc = n // CH
    width = H * DH
    nprev = 0 if prev is None else prev.shape[1]
    state_spec = lambda k: pl.BlockSpec((1, k, 2, H, DH, DH), lambda b: (b, 0, 0, 0, 0, 0))
    prev_args = [] if prev is None else [prev]
    prev_specs = [] if prev is None else [state_spec(nprev)]
    return pl.pallas_call(
        functools.partial(_dn_kernel, n, nprev),
        grid=(bsz,),
        in_specs=[pl.BlockSpec((n, width), lambda b: (row_blk0 + b, 0)),
                  pl.BlockSpec((n, width), lambda b: (row_blk0 + b, 1)),
                  pl.BlockSpec((n, width), lambda b: (row_blk0 + b, 2)),
                  pl.BlockSpec((n, width), lambda b: (row_blk0 + b, 3)),
                  pl.BlockSpec((n, 4 * H), lambda b: (row_blk0 + b, 0)),
                  pl.BlockSpec((nc, 4 * H, CH), lambda b: (row_blk0 + b, 0, 0)),
                  pl.BlockSpec((3, 3 * width), lambda b: (0, 0)),
                  pl.BlockSpec((1, DH), lambda b: (0, 0)),
                  pl.BlockSpec((1, 2, H, DH, DH), lambda b: (b, 0, 0, 0, 0))] + prev_specs,
        out_specs=[pl.BlockSpec((n, width), lambda b: (b, 0)), state_spec(nprev + 1)],
        out_shape=[jax.ShapeDtypeStruct((bsz * n, width), F32),
                   jax.ShapeDtypeStruct((bsz, nprev + 1, 2, H, DH, DH), F32)],
        scratch_shapes=[pltpu.VMEM((n, width), F32), pltpu.VMEM((n, width), F32),
                        pltpu.VMEM((n, width), F32), pltpu.VMEM((n, 4 * H), F32),
                        pltpu.VMEM((nc, 4 * H, CH), F32), pltpu.VMEM((2 * H, DH, DH), F32),
                        pltpu.VMEM((2, n, width), F32)],
        compiler_params=_params(("arbitrary",)),
        name="deltanet_n%d" % n,
    )(p, p, p, p, g, gt, conv_w, norm_w, s0, *prev_args)


def _mix2_kernel(sb_ref, sc_ref, su_ref, fu_ref, cw_ref, bdc_ref, bds_ref, cs_ref, o_ref):
    w = sb_ref.shape[1]
    cu = sc_ref[...] * su_ref[...]
    o_ref[:, 0:w] = sb_ref[...] * _shift_conv(cu, cw_ref, 0, w)
    fu = fu_ref[...]
    a = _mm3(fu, bdc_ref[...])
    b = _mm3(fu, bds_ref[...])
    o_ref[:, w:2 * w] = _mmb(cs_ref[...], jnp.concatenate([a, b], axis=0))


def _mix2(p, conv_w, bdc, bds, cs, bsz, n, row_blk0):
    w = 256
    return pl.pallas_call(
        _mix2_kernel,
        grid=(bsz,),
        in_specs=[pl.BlockSpec((n, w), lambda b: (row_blk0 + b, 8)),
                  pl.BlockSpec((n, w), lambda b: (row_blk0 + b, 9)),
                  pl.BlockSpec((n, w), lambda b: (row_blk0 + b, 10)),
                  pl.BlockSpec((n, w), lambda b: (row_blk0 + b, 11)),
                  pl.BlockSpec((3, w), lambda b: (0, 0)),
                  pl.BlockSpec((w, w), lambda b: (0, 0)),
                  pl.BlockSpec((w, w), lambda b: (0, 0)),
                  pl.BlockSpec((n, 2 * n), lambda b: (0, 0))],
        out_specs=pl.BlockSpec((n, 2 * w), lambda b: (b, 0)),
        out_shape=jax.ShapeDtypeStruct((bsz * n, 2 * w), F32),
        compiler_params=_params(("arbitrary",)),
        name="conv_fourier_n%d" % n,
    )(p, p, p, p, conv_w, bdc, bds, cs)


def _dft_tables(n):
    gw = 64
    k = np.arange(n, dtype=np.int64)
    ang = ((k[:, None] * k[None, :]) % n) * (2.0 * math.pi / n)
    scale = 1.0 / math.sqrt(n * gw)
    cs = np.concatenate([np.cos(ang), -np.sin(ang)], axis=1) * scale
    c = np.arange(256, dtype=np.int64)
    angc = (((c[:, None] % gw) * (c[None, :] % gw)) % gw) * (2.0 * math.pi / gw)
    same = (c[:, None] // gw) == (c[None, :] // gw)
    bdc = np.where(same, np.cos(angc), 0.0)
    bds = np.where(same, np.sin(angc), 0.0)
    return jnp.asarray(cs, F32).astype(BF16), jnp.asarray(bdc, F32), jnp.asarray(bds, F32)


OUT_SUB = 2


def _out_kernel(t1, n2, tm, dn1_ref, dn2_ref, mx1_ref, mx2_ref, x_ref, mod_ref, nw_ref, wo_ref, wr_ref,
                x1_ref, h2_ref, aff_ref):
    t0 = pl.program_id(0) * tm
    r = _mod_row(t0, t1, n2)
    half = D // 2
    sub = tm // OUT_SUB
    mixes = []
    for k in range(OUT_SUB):
        rs = slice(k * sub, (k + 1) * sub)
        dn = jnp.where(t0 < t1, dn1_ref[rs, :], dn2_ref[rs, :])
        mx = jnp.where(t0 < t1, mx1_ref[rs, :], mx2_ref[rs, :])
        mixes.append(_mmb(dn, wo_ref[0:half, :]) + _mmb(mx, wo_ref[half:D, :]))
    h2s = []
    for k in range(OUT_SUB):
        rs = slice(k * sub, (k + 1) * sub)
        x1 = x_ref[rs, :] + mod_ref[pl.ds(r, 1), 2 * D:3 * D] * mixes[k]
        x1_ref[rs, :] = x1
        y = x1 * lax.rsqrt(jnp.mean(x1 * x1, axis=-1, keepdims=True) + EPS) * nw_ref[...]
        h2 = y * (1.0 + mod_ref[pl.ds(r, 1), 4 * D:5 * D]) + mod_ref[pl.ds(r, 1), 3 * D:4 * D]
        _store_token_tiles(h2_ref.at[pl.ds(k * sub * ROW_TILES, sub * ROW_TILES)], h2)
        h2s.append(h2)
    logits = [_mm3(h2, wr_ref[...]) for h2 in h2s]
    for k in range(OUT_SUB):
        lt = logits[k].T[0:E, :]
        ex = jnp.exp(lt - jnp.max(lt, axis=0, keepdims=True))
        aff_ref[:, k * sub:(k + 1) * sub] = ex / jnp.sum(ex, axis=0, keepdims=True)


def _out_proj(dn1, dn2, mx1, mx2, x, mod, nw, wo, wr, t1, n2, tm):
    t_all = x.shape[0]
    nt1 = t1 // tm
    spec1 = pl.BlockSpec((tm, D // 2), lambda i: (jnp.minimum(i, nt1 - 1), 0))
    spec2 = pl.BlockSpec((tm, D // 2), lambda i: (jnp.maximum(i - nt1, 0), 0))
    return pl.pallas_call(
        functools.partial(_out_kernel, t1, n2, tm),
        grid=(t_all // tm,),
        in_specs=[spec1, spec2, spec1, spec2,
                  pl.BlockSpec((tm, D), lambda i: (i, 0)),
                  pl.BlockSpec((8, NMOD * D), lambda i: (0, 0)),
                  pl.BlockSpec((1, D), lambda i: (0, 0)),
                  pl.BlockSpec((D, D), lambda i: (0, 0)),
                  pl.BlockSpec((D, 128), lambda i: (0, 0))],
        out_specs=[pl.BlockSpec((tm, D), lambda i: (i, 0)),
                   pl.BlockSpec((tm * ROW_TILES, 128), lambda i: (i, 0)),
                   pl.BlockSpec((E, tm), lambda i: (0, i))],
        out_shape=[jax.ShapeDtypeStruct((t_all, D), F32),
                   jax.ShapeDtypeStruct((t_all * ROW_TILES, 128), F32),
                   jax.ShapeDtypeStruct((E, t_all), F32)],
        compiler_params=_params(("arbitrary",)),
        name="out_proj",
    )(dn1, dn2, mx1, mx2, x, mod, nw, wo, wr)


SEL_TB = 128
N_SLOT_ROWS = 16
SPILL_BLOCKS = 2


def _sel_kernel(t, cap, aff_ref, slots_ref, lo_ref, hi_ref, a3, c3, starts_v, starts_s, sem):
    ntb = t // SEL_TB
    aff = aff_ref[...]

    def search(i, thr):
        cand = thr | jnp.left_shift(jnp.int32(1), 30 - i)
        cnt = jnp.sum((aff >= pltpu.bitcast(cand, F32)).astype(F32), axis=1, keepdims=True)
        return jnp.where(cnt >= cap, cand, thr)

    thr = lax.fori_loop(0, 31, search, jnp.zeros((E, 1), I32))
    gt = (aff >= pltpu.bitcast(thr + 1, F32)).astype(F32)
    eq = (aff >= pltpu.bitcast(thr, F32)).astype(F32) - gt
    need = cap - jnp.sum(gt, axis=1, keepdims=True)

    ui = _iota((SEL_TB, SEL_TB), 0)
    uj = _iota((SEL_TB, SEL_TB), 1)
    upper = (ui < uj).astype(BF16)
    blk_lane = _iota((1, 128), 1)

    def excl_cumsum(rows, dst, r0, r1):
        carry = jnp.zeros((rows.shape[0], 1), F32)
        starts = jnp.zeros((rows.shape[0], 128), F32)
        for j in range(ntb):
            blk = rows[:, j * SEL_TB:(j + 1) * SEL_TB]
            dst[j, r0:r1, :] = _mm(blk.astype(BF16), upper) + carry
            starts = jnp.where(blk_lane == j, carry, starts)
            carry = carry + jnp.sum(blk, axis=1, keepdims=True)
        return starts

    excl_cumsum(eq, c3, 0, E)
    rank_eq = jnp.concatenate([c3[j, 0:E, :] for j in range(ntb)], axis=1)
    sel = jnp.maximum(gt, jnp.where(rank_eq < need, eq, 0.0))
    n_tok = jnp.sum(sel, axis=0, keepdims=True)
    starts = excl_cumsum(jnp.concatenate([sel, jnp.broadcast_to(n_tok, (8, t))], axis=0), c3, 0, E + 8)
    starts_v[...] = starts.astype(I32)
    to_smem = pltpu.make_async_copy(starts_v, starts_s, sem)
    to_smem.start()
    ei = _iota((E, E), 0)
    ej = _iota((E, E), 1)
    below = _mm((ej < ei).astype(BF16), sel.astype(BF16))
    for j in range(ntb):
        sl = slice(j * SEL_TB, (j + 1) * SEL_TB)
        off = c3[j, E:E + 1, :]
        lo_ref[:, sl] = off
        hi_ref[:, sl] = off + n_tok[:, sl]
        rank = off + below[:, sl]
        a3[j, 0] = sel[:, sl]
        a3[j, 1] = aff[:, sl]
        a3[j, 2] = rank
    slots_ref[...] = jnp.zeros(slots_ref.shape, F32)
    to_smem.wait()

    win_iota = _iota((2 * SEL_TB, SEL_TB), 0)
    tok_iota = _iota((1, SEL_TB), 1)

    def per_expert(e, carry):
        def per_block(j, carry2):
            wb = starts_s[e, j] >> 7
            pos = c3[j, pl.ds(e, 1), :].astype(I32) - wb * SEL_TB
            chosen = a3[j, 0, pl.ds(e, 1), :]
            w = a3[j, 1, pl.ds(e, 1), :]
            rank = a3[j, 2, pl.ds(e, 1), :].astype(I32)
            tok = tok_iota + j * SEL_TB
            w_hi = w.astype(BF16).astype(F32)
            w_mid = (w - w_hi).astype(BF16).astype(F32)
            w_lo = w - w_hi - w_mid
            vals = jnp.concatenate(
                [(tok >> 7).astype(F32), (tok & 127).astype(F32), w_hi, w_mid, w_lo,
                 (rank >> 7).astype(F32), (rank & 127).astype(F32),
                 jnp.zeros((N_SLOT_ROWS - 7, SEL_TB), F32)], axis=0)
            onehot = jnp.where((win_iota == pos) & (chosen > 0.0), 1.0, 0.0)
            placed = _mmb(vals, onehot, NT)
            slots_ref[e, wb] += placed[:, 0:SEL_TB]
            slots_ref[e, wb + 1] += placed[:, SEL_TB:2 * SEL_TB]
            return carry2
        return lax.fori_loop(0, ntb, per_block, carry, unroll=8)

    lax.fori_loop(0, E, per_expert, 0)


def _select(aff_t, t, cap, col_blk):
    ntb = t // SEL_TB
    nsb = cap // SEL_TB + SPILL_BLOCKS
    return pl.pallas_call(
        functools.partial(_sel_kernel, t, cap),
        grid=(1,),
        in_specs=[pl.BlockSpec((E, t), lambda i: (0, col_blk))],
        out_specs=[pl.BlockSpec((E, nsb, N_SLOT_ROWS, SEL_TB), lambda i: (0, 0, 0, 0)),
                   pl.BlockSpec((1, t), lambda i: (0, 0)),
                   pl.BlockSpec((1, t), lambda i: (0, 0))],
        out_shape=[jax.ShapeDtypeStruct((E, nsb, N_SLOT_ROWS, SEL_TB), F32),
                   jax.ShapeDtypeStruct((1, t), F32),
                   jax.ShapeDtypeStruct((1, t), F32)],
        scratch_shapes=[pltpu.VMEM((ntb, 3, E, SEL_TB), F32),
                        pltpu.VMEM((ntb, E + 8, SEL_TB), F32),
                        pltpu.VMEM((E + 8, 128), I32),
                        pltpu.SMEM((E + 8, 128), I32),
                        pltpu.SemaphoreType.DMA(())],
        compiler_params=_params(("arbitrary",)),
        name="select_t%d" % t,
    )(aff_t)


DMA_UNROLL = 8


def _ffn_kernel(rc, nch, idx_ref, dst_ref, h2_hbm, wcol_ref, wg_ref, wu_ref, wd_ref, z_hbm,
                xbuf, ybuf, wgb, wub, wdb, gsem, ssem):
    c = pl.program_id(1)
    step = pl.program_id(0) * nch + c
    nsteps = E * nch
    slot = step % 2

    def start_gather(s):
        buf = xbuf.at[s % 2]
        sem = gsem.at[s % 2]

        def issue(jb, carry):
            for u in range(DMA_UNROLL):
                j = jb * DMA_UNROLL + u
                src = pl.multiple_of(idx_ref[s * rc + j], ROW_TILES)
                pltpu.make_async_copy(h2_hbm.at[pl.ds(src, ROW_TILES)],
                                      buf.at[pl.ds(j * ROW_TILES, ROW_TILES)], sem).start(priority=u % 2)
            return carry

        lax.fori_loop(0, rc // DMA_UNROLL, issue, 0)

    @pl.when(step == 0)
    def _():
        start_gather(step)

    @pl.when(step + 1 < nsteps)
    def _():
        start_gather(step + 1)

    @pl.when(c == 0)
    def _():
        wgb[...] = wg_ref[0, 0].astype(BF16)
        wub[...] = wu_ref[0, 0].astype(BF16)
        wdb[...] = wd_ref[0, 0].astype(BF16)

    pltpu.make_async_copy(h2_hbm.at[pl.ds(0, rc * ROW_TILES)], xbuf.at[slot], gsem.at[slot]).wait()
    xb = _load_token_tiles(xbuf.at[slot], rc).astype(BF16)
    hid = _silu(_mm(xb, wgb[...])) * _mm(xb, wub[...])
    y = _mm(hid.astype(BF16), wdb[...]) * wcol_ref[...]

    @pl.when(step > 0)
    def _():
        pltpu.make_async_copy(ybuf, z_hbm.at[pl.ds(0, rc * ROW_TILES)], ssem).wait()

    _store_token_tiles(ybuf, y)

    def scatter(jb, carry):
        for u in range(DMA_UNROLL):
            j = jb * DMA_UNROLL + u
            row = pl.multiple_of(dst_ref[step * rc + j], ROW_TILES)
            pltpu.make_async_copy(ybuf.at[pl.ds(j * ROW_TILES, ROW_TILES)],
                                  z_hbm.at[pl.ds(row, ROW_TILES)], ssem).start(priority=u % 2)
        return carry

    lax.fori_loop(0, rc // DMA_UNROLL, scatter, 0)

    @pl.when(step == nsteps - 1)
    def _():
        pltpu.make_async_copy(ybuf, z_hbm.at[pl.ds(0, rc * ROW_TILES)], ssem).wait()


def _expert_ffn(idx, dst, h2, wcol, w_gate, w_up, w_down, layer, rc, nch):
    zrows = E * nch * rc
    wspec = pl.BlockSpec((1, 1, D, D), lambda e, c, *_: (layer, e, 0, 0))
    return pl.pallas_call(
        functools.partial(_ffn_kernel, rc, nch),
        grid_spec=pltpu.PrefetchScalarGridSpec(
            num_scalar_prefetch=2,
            grid=(E, nch),
            in_specs=[pl.BlockSpec(memory_space=pl.ANY),
                      pl.BlockSpec((rc, 1), lambda e, c, *_: (e * nch + c, 0)),
                      wspec, wspec, wspec],
            out_specs=pl.BlockSpec(memory_space=pl.ANY),
            scratch_shapes=[pltpu.VMEM((2, rc * ROW_TILES, 128), F32),
                            pltpu.VMEM((rc * ROW_TILES, 128), F32),
                            pltpu.VMEM((D, D), BF16), pltpu.VMEM((D, D), BF16),
                            pltpu.VMEM((D, D), BF16),
                            pltpu.SemaphoreType.DMA((2,)), pltpu.SemaphoreType.DMA(())]),
        out_shape=jax.ShapeDtypeStruct((zrows * ROW_TILES, 128), F32),
        compiler_params=_params(("arbitrary", "arbitrary")),
        name="expert_ffn",
    )(idx, dst, h2, wcol, w_gate, w_up, w_down)


CMB_TB = 256
CMB_RC = 512


def _comb_kernel(t1, n2, zrows, final, bs_ref, be_ref, x1_ref, lo_ref, hi_ref, mod_ref, fw_ref, z_hbm,
                 *refs):
    outs, (zbuf, sem) = refs[:-2], refs[-2:]
    i = pl.program_id(0)
    nsteps = pl.num_programs(0)
    r = _mod_row(i * CMB_TB, t1, n2)

    def first_row(step):
        return (bs_ref[step] // 8) * 8

    def chunk_copy(nominal, slot):
        b = pl.multiple_of(jnp.minimum(nominal, zrows - CMB_RC) * ROW_TILES, 8 * ROW_TILES)
        return pltpu.make_async_copy(z_hbm.at[pl.ds(b, CMB_RC * ROW_TILES)], zbuf.at[slot], sem.at[slot])

    base0 = first_row(i)
    nchunk = jnp.maximum((be_ref[i] - base0 + CMB_RC - 1) // CMB_RC, 1)

    @pl.when(i == 0)
    def _():
        chunk_copy(base0, 0).start()

    eye = _iota((CMB_TB, CMB_TB), 0) == _iota((CMB_TB, CMB_TB), 1)
    lo = jnp.sum(jnp.where(eye, lo_ref[...], 0.0), axis=1, keepdims=True)
    hi = jnp.sum(jnp.where(eye, hi_ref[...], 0.0), axis=1, keepdims=True)
    col = _iota((1, CMB_RC), 1)

    def chunk(ci, acc):
        nominal = base0 + ci * CMB_RC
        slot = ci % 2

        @pl.when(ci + 1 < nchunk)
        def _():
            chunk_copy(nominal + CMB_RC, 1 - slot).start()

        chunk_copy(nominal, slot).wait()
        b = jnp.minimum(nominal, zrows - CMB_RC)
        rows = (col + b).astype(F32)
        pick = (rows >= lo) & (rows < hi) & (rows >= nominal.astype(F32))
        s = jnp.where(pick, 1.0, 0.0).astype(BF16)
        return acc + _mmb(s, _load_token_tiles(zbuf.at[slot], CMB_RC))

    moe = lax.fori_loop(0, nchunk, chunk, jnp.zeros((CMB_TB, D), F32))

    @pl.when(i + 1 < nsteps)
    def _():
        chunk_copy(first_row(jnp.minimum(i + 1, nsteps - 1)), 0).start()

    x2 = x1_ref[...] + mod_ref[pl.ds(r, 1), 5 * D:6 * D] * moe
    if final:
        y = x2 * lax.rsqrt(jnp.mean(x2 * x2, axis=-1, keepdims=True) + EPS) * fw_ref[...]

        @pl.when(i * CMB_TB < t1)
        def _():
            outs[0][...] = y

        @pl.when(i * CMB_TB >= t1)
        def _():
            outs[1][...] = y
    else:
        outs[0][...] = x2


def _combine(bstart, bend, x1, lo, hi, mod, fw, z, t1, n2, final):
    t_all = x1.shape[0]
    zrows = z.shape[0] // ROW_TILES
    nb1 = t1 // CMB_TB
    if final:
        out_specs = [pl.BlockSpec((CMB_TB, D), lambda i, *_: (jnp.minimum(i, nb1 - 1), 0)),
                     pl.BlockSpec((CMB_TB, D), lambda i, *_: (jnp.maximum(i - nb1, 0), 0))]
        out_shape = [jax.ShapeDtypeStruct((t1, D), F32), jax.ShapeDtypeStruct((t_all - t1, D), F32)]
    else:
        out_specs = pl.BlockSpec((CMB_TB, D), lambda i, *_: (i, 0))
        out_shape = jax.ShapeDtypeStruct((t_all, D), F32)
    return pl.pallas_call(
        functools.partial(_comb_kernel, t1, n2, zrows, final),
        grid_spec=pltpu.PrefetchScalarGridSpec(
            num_scalar_prefetch=2,
            grid=(t_all // CMB_TB,),
            in_specs=[pl.BlockSpec((CMB_TB, D), lambda i, *_: (i, 0)),
                      pl.BlockSpec((1, CMB_TB), lambda i, *_: (0, i)),
                      pl.BlockSpec((1, CMB_TB), lambda i, *_: (0, i)),
                      pl.BlockSpec((8, NMOD * D), lambda i, *_: (0, 0)),
                      pl.BlockSpec((1, D), lambda i, *_: (0, 0)),
                      pl.BlockSpec(memory_space=pl.ANY)],
            out_specs=out_specs,
            scratch_shapes=[pltpu.VMEM((2, CMB_RC * ROW_TILES, 128), F32),
                            pltpu.SemaphoreType.DMA((2,))]),
        out_shape=out_shape,
        compiler_params=_params(("arbitrary",)),
        name="combine",
    )(bstart, bend, x1, lo, hi, mod, fw, z)


def _grid_pos_embed(n, d):
    rows = n // GRID_W
    r = np.repeat(np.arange(rows, dtype=np.float64), GRID_W)
    col = np.tile(np.arange(GRID_W, dtype=np.float64), rows)
    quarter = d // 4
    omega = np.power(POS_BASE, -np.arange(quarter, dtype=np.float64) / quarter)
    ra = r[:, None] * omega
    ca = col[:, None] * omega
    return jnp.asarray(np.concatenate([np.sin(ra), np.cos(ra), np.sin(ca), np.cos(ca)], axis=-1), F32)


def _decode_slots(slots):
    e, nsb, rows, tb = slots.shape
    nsb -= SPILL_BLOCKS
    slots = slots[:, :nsb].transpose(0, 2, 1, 3).reshape(e, rows, nsb * tb)
    idx = (slots[:, 0] * 128.0 + slots[:, 1]).astype(I32)
    w = slots[:, 2] + slots[:, 3] + slots[:, 4]
    rank = (slots[:, 5] * 128.0 + slots[:, 6]).astype(I32)
    return idx, w, rank


def kernel(x_prompt, x_sample, state_delta, c, c_ctx, w_ada, b_ada, norm1_w, norm2_w, w_in, dn_conv_w,
           dn_a_log, dn_dt_bias, dn_norm_w, sc_conv_w, w_out, w_router, w_gate, w_up, w_down,
           final_norm_w):
    b1, n1, _ = x_prompt.shape
    b2, n2, _ = x_sample.shape
    depth = w_ada.shape[0]
    t1, t2 = b1 * n1, b2 * n2
    t_all = t1 + t2
    cap1 = max(1, 2 * t1 // E)
    cap2 = max(1, 2 * t2 // E)
    tm = 512
    rc = 512 if (cap1 % 512 == 0 and cap2 % 512 == 0) else 128
    assert t1 % n2 == 0 and t1 % tm == 0 and n2 % tm == 0
    assert n1 % (DN_POS * CH) == 0 and n2 % (DN_POS * CH) == 0
    assert t1 % SEL_TB == 0 and t2 % SEL_TB == 0 and t1 % t2 == 0
    assert cap1 % rc == 0 and cap2 % rc == 0 and 2 * t_all >= CMB_RC
    nch = (cap1 + cap2) // rc
    width = H * DH

    cond8 = jnp.zeros((8, D), F32).at[0].set(c_ctx).at[1:1 + b2].set(c)
    mod = _ada(cond8, w_ada, b_ada)

    x = None
    s_zero = jnp.zeros((b1, 2, H, DH, DH), F32)
    tabs1 = _dft_tables(n1)
    tabs2 = _dft_tables(n2)
    ctx_states = None

    for l in range(depth):
        wl = w_in[l]
        wm = jnp.concatenate([wl[:, 0:4 * width], wl[:, 4 * width + 4 * H:]], axis=1).astype(BF16)
        wg = jnp.pad(wl[:, 4 * width:4 * width + 4 * H], ((0, 0), (0, 128 - 4 * H))).astype(BF16)
        al = jnp.pad(dn_a_log[l].reshape(1, 2 * H), ((0, 0), (2 * H, 128 - 4 * H)))
        dt = jnp.pad(dn_dt_bias[l].reshape(1, 2 * H), ((0, 0), (2 * H, 128 - 4 * H)))
        xs = (x,) if l else (x_prompt.reshape(t1, D), x_sample.reshape(t2, D), _grid_pos_embed(n2, D))
        res = _in_proj(xs, mod[l], norm1_w[l].reshape(1, D), wm, wg, al, dt, t1, n2, tm)
        if not l:
            x, res = res[0], res[1:]
        p, g, gt = res

        nwd = dn_norm_w[l].reshape(1, DH)
        dn1, ctx_states = _deltanet(p, g, gt, dn_conv_w[l], nwd, s_zero, ctx_states, n1, 0)
        dn2, _ = _deltanet(p, g, gt, dn_conv_w[l], nwd, state_delta[:, l], None, n2, t1 // n2)
        mx1 = _mix2(p, sc_conv_w[l], tabs1[1], tabs1[2], tabs1[0], b1, n1, 0)
        mx2 = _mix2(p, sc_conv_w[l], tabs2[1], tabs2[2], tabs2[0], b2, n2, t1 // n2)

        wr = jnp.pad(w_router[l], ((0, 0), (0, 128 - E)))
        x1, h2, aff_t = _out_proj(dn1, dn2, mx1, mx2, x, mod[l], norm2_w[l].reshape(1, D),
                                  w_out[l].astype(BF16), wr, t1, n2, tm)

        slots1, lo1, hi1 = _select(aff_t, t1, cap1, 0)
        slots2, lo2, hi2 = _select(aff_t, t2, cap2, t1 // t2)
        idx1, wsel1, rank1 = _decode_slots(slots1)
        idx2, wsel2, rank2 = _decode_slots(slots2)
        idx = jnp.concatenate([idx1, idx2 + t1], axis=1).reshape(-1) * ROW_TILES
        dst = jnp.concatenate([rank1, rank2 + 2 * t1], axis=1).reshape(-1) * ROW_TILES
        wcol = jnp.concatenate([wsel1, wsel2], axis=1).reshape(-1, 1)
        z = _expert_ffn(idx, dst, h2, wcol, w_gate, w_up, w_down, l, rc, nch)

        lo = jnp.concatenate([lo1, lo2 + 2.0 * t1], axis=1)
        hi = jnp.concatenate([hi1, hi2 + 2.0 * t1], axis=1)
        bstart = lo[0, ::CMB_TB].astype(I32)
        bend = hi[0, CMB_TB - 1::CMB_TB].astype(I32)
        x = _combine(bstart, bend, x1, lo, hi, mod[l], final_norm_w.reshape(1, D), z, t1, n2,
                     l == depth - 1)

    y_prompt, y_sample = x
    return y_prompt.reshape(b1, n1, D), y_sample.reshape(b2, n2, D), ctx_states
```

```python
import functools
import math

import jax
import jax.numpy as jnp
import numpy as np
from jax import lax
from jax.experimental import pallas as pl
from jax.experimental.pallas import tpu as pltpu

F32 = jnp.float32
BF16 = jnp.bfloat16
I32 = jnp.int32
HI = lax.Precision.HIGHEST

D = 1024
H = 4
DH = 128
CH = 64
E = 16
NMOD = 6
EPS = 1e-6
GRID_W = 64
POS_BASE = 10000.0
P_COLS = 3072
VMEM_LIMIT = 56 * 1024 * 1024

NN = (((1,), (0,)), ((), ()))
NT = (((1,), (1,)), ((), ()))
TN = (((0,), (0,)), ((), ()))


def _mm(a, b, dims=NN, prec=None):
    return lax.dot_general(a, b, dims, precision=prec, preferred_element_type=F32)


def _mmb(a, b, dims=NN):
    return lax.dot_general(a.astype(BF16), b.astype(BF16), dims, preferred_element_type=F32)


def _mm3(a, b, dims=NN):
    a_hi = a.astype(BF16)
    a_lo = (a - a_hi.astype(F32)).astype(BF16)
    b_hi = b.astype(BF16)
    b_lo = (b - b_hi.astype(F32)).astype(BF16)
    return _mm(a_hi, b_hi, dims) + _mm(a_hi, b_lo, dims) + _mm(a_lo, b_hi, dims)


def _mm_inv(a, b):
    return _mmb(a, b)


def _silu(x):
    return x / (1.0 + jnp.exp(-x))


def _sigmoid(x):
    return 1.0 / (1.0 + jnp.exp(-x))


def _softplus(x):
    return jnp.maximum(x, 0.0) + jnp.log1p(jnp.exp(-jnp.abs(x)))


def _iota(shape, dim):
    return lax.broadcasted_iota(I32, shape, dim)


ROW_TILES = D // 128


def _load_token_tiles(ref, n):
    return jnp.concatenate([ref[pl.ds(s, n, stride=ROW_TILES), :] for s in range(ROW_TILES)], axis=1)


def _store_token_tiles(ref, x):
    n = x.shape[0]
    for s in range(ROW_TILES):
        ref[pl.ds(s, n, stride=ROW_TILES), :] = x[:, s * 128:(s + 1) * 128]


def _mod_row(t0, t1, n2):
    return jnp.where(t0 < t1, 0, 1 + jnp.maximum(t0 - t1, 0) // n2)


def _params(sem):
    return pltpu.CompilerParams(dimension_semantics=sem, vmem_limit_bytes=VMEM_LIMIT)


def _ada_kernel(c_ref, w_ref, b_ref, o_ref):
    s = _silu(c_ref[...])
    o_ref[0] = _mm3(s, w_ref[0]) + b_ref[0]


def _ada(cond8, w_ada, b_ada):
    depth = w_ada.shape[0]
    tn = 1536
    return pl.pallas_call(
        _ada_kernel,
        grid=(depth, NMOD * D // tn),
        in_specs=[pl.BlockSpec((8, D), lambda l, j: (0, 0)),
                  pl.BlockSpec((1, D, tn), lambda l, j: (l, 0, j)),
                  pl.BlockSpec((1, 1, tn), lambda l, j: (l, 0, j))],
        out_specs=pl.BlockSpec((1, 8, tn), lambda l, j: (l, 0, j)),
        out_shape=jax.ShapeDtypeStruct((depth, 8, NMOD * D), F32),
        compiler_params=_params(("arbitrary", "arbitrary")),
        name="ada",
    )(cond8, w_ada, b_ada.reshape(depth, 1, NMOD * D))


IN_SUB = 2


def _in_core(tm, r, x, mod_ref, nw_ref, wm_ref, wg_ref, al_ref, dt_ref, p_ref, g_ref, gt_ref):
    sh = mod_ref[pl.ds(r, 1), 0:D]
    sc = mod_ref[pl.ds(r, 1), D:2 * D]
    sub = tm // IN_SUB
    nsplit = 4
    wc = P_COLS // nsplit
    lane = _iota((1, 128), 1)
    hbs = []
    for k in range(IN_SUB):
        xk = x[k * sub:(k + 1) * sub]
        y = xk * lax.rsqrt(jnp.mean(xk * xk, axis=-1, keepdims=True) + EPS) * nw_ref[...]
        hbs.append((y * (1.0 + sc) + sh).astype(BF16))
    for k in range(IN_SUB):
        rs = slice(k * sub, (k + 1) * sub)
        for j in range(nsplit):
            p_ref[rs, j * wc:(j + 1) * wc] = _mm(hbs[k], wm_ref[:, j * wc:(j + 1) * wc])
    for k in range(IN_SUB):
        raw = _mm(hbs[k], wg_ref[...])
        g = -jnp.exp(al_ref[...]) * _softplus(raw + dt_ref[...])
        act = jnp.where(lane < 2 * H, _sigmoid(raw), g)
        g_ref[k * sub:(k + 1) * sub, :] = act[:, 0:4 * H]
        act_t = act.T
        for j in range(sub // CH):
            gt_ref[k * (sub // CH) + j] = act_t[0:4 * H, j * CH:(j + 1) * CH]


def _in_kernel(t1, n2, tm, x_ref, *refs):
    r = _mod_row(pl.program_id(0) * tm, t1, n2)
    _in_core(tm, r, x_ref[...], *refs)


def _in_first_kernel(t1, n2, tm, xp_ref, xs_ref, pos_ref, mod_ref, nw_ref, wm_ref, wg_ref, al_ref,
                     dt_ref, x_ref, p_ref, g_ref, gt_ref):
    t0 = pl.program_id(0) * tm
    x = jnp.where(t0 < t1, xp_ref[...], xs_ref[...] + pos_ref[...])
    x_ref[...] = x
    _in_core(tm, _mod_row(t0, t1, n2), x, mod_ref, nw_ref, wm_ref, wg_ref, al_ref, dt_ref,
             p_ref, g_ref, gt_ref)


def _in_proj(xs, mod, nw, wm, wg, al, dt, t1, n2, tm):
    first = len(xs) == 3
    t_all = t1 + xs[1].shape[0] if first else xs[0].shape[0]
    nt1 = t1 // tm
    wspecs = [pl.BlockSpec((8, NMOD * D), lambda i: (0, 0)),
              pl.BlockSpec((1, D), lambda i: (0, 0)),
              pl.BlockSpec((D, P_COLS), lambda i: (0, 0)),
              pl.BlockSpec((D, 128), lambda i: (0, 0)),
              pl.BlockSpec((1, 128), lambda i: (0, 0)),
              pl.BlockSpec((1, 128), lambda i: (0, 0))]
    out_specs = [pl.BlockSpec((tm, P_COLS), lambda i: (i, 0)),
                 pl.BlockSpec((tm, 4 * H), lambda i: (i, 0)),
                 pl.BlockSpec((tm // CH, 4 * H, CH), lambda i: (i, 0, 0))]
    out_shape = [jax.ShapeDtypeStruct((t_all, P_COLS), F32),
                 jax.ShapeDtypeStruct((t_all, 4 * H), F32),
                 jax.ShapeDtypeStruct((t_all // CH, 4 * H, CH), F32)]
    if first:
        kern = _in_first_kernel
        xspecs = [pl.BlockSpec((tm, D), lambda i: (jnp.minimum(i, nt1 - 1), 0)),
                  pl.BlockSpec((tm, D), lambda i: (jnp.maximum(i - nt1, 0), 0)),
                  pl.BlockSpec((tm, D), lambda i: (jnp.maximum(i - nt1, 0) % (n2 // tm), 0))]
        out_specs = [pl.BlockSpec((tm, D), lambda i: (i, 0))] + out_specs
        out_shape = [jax.ShapeDtypeStruct((t_all, D), F32)] + out_shape
    else:
        kern = _in_kernel
        xspecs = [pl.BlockSpec((tm, D), lambda i: (i, 0))]
    return pl.pallas_call(
        functools.partial(kern, t1, n2, tm),
        grid=(t_all // tm,),
        in_specs=xspecs + wspecs,
        out_specs=out_specs,
        out_shape=out_shape,
        compiler_params=_params(("arbitrary",)),
        name="in_proj_first" if first else "in_proj",
    )(*xs, mod, nw, wm, wg, al, dt)


def _shift_conv(x, w_ref, c0, c1):
    n = x.shape[0]
    row = _iota((n, 1), 0)
    prev = jnp.where(row == 0, 0.0, pltpu.roll(x, 1, 0))
    nxt = jnp.where(row == n - 1, 0.0, pltpu.roll(x, n - 1, 0))
    return prev * w_ref[0:1, c0:c1] + x * w_ref[1:2, c0:c1] + nxt * w_ref[2:3, c0:c1]


DN_POS = 4


def _dn_kernel(n, nprev, q_ref, k_ref, v_ref, z_ref, g_ref, gt_ref, cw_ref, nw_ref, s0_ref, *refs):
    prev_ref = refs[0] if nprev else None
    o_ref, sfin_ref, qs, ks, vs, gcc_s, gct_s, s_s, oacc = refs[1 if nprev else 0:]
    nc = n // CH
    width = H * DH

    for hh in range(H):
        c0, c1 = hh * DH, (hh + 1) * DH
        q = _silu(_shift_conv(q_ref[:, c0:c1], cw_ref, c0, c1))
        q = q * lax.rsqrt(jnp.sum(q * q, axis=-1, keepdims=True) + 1e-6)
        qs[:, c0:c1] = q * (DH ** -0.5)
        k = _silu(_shift_conv(k_ref[:, c0:c1], cw_ref, width + c0, width + c1))
        ks[:, c0:c1] = k * lax.rsqrt(jnp.sum(k * k, axis=-1, keepdims=True) + 1e-6)
        vs[:, c0:c1] = _silu(_shift_conv(v_ref[:, c0:c1], cw_ref, 2 * width + c0, 2 * width + c1))

    ii = _iota((CH, CH), 0)
    jj = _iota((CH, CH), 1)
    x = gt_ref[...].reshape(nc * 4 * H, CH)
    cf = _mm(x, (ii <= jj).astype(F32), prec=HI)
    cb = _mm(x, (ii >= jj).astype(F32), prec=HI)
    rsel = (_iota((nc * 4 * H, 1), 0) & (4 * H - 1)) < 3 * H
    gct_s[...] = jnp.where(rsel, cf, cb).reshape(nc, 4 * H, CH)
    i2 = _iota((2 * CH, 2 * CH), 0)
    j2 = _iota((2 * CH, 2 * CH), 1)
    same = (i2 >= CH) == (j2 >= CH)
    lf = (same & (i2 >= j2)).astype(F32)
    lb = (same & (i2 <= j2)).astype(F32)
    lsel = _iota((1, 4 * H), 1) < 3 * H
    for m in range(n // (2 * CH)):
        y = g_ref[m * 2 * CH:(m + 1) * 2 * CH, :]
        gcc_s[m * 2 * CH:(m + 1) * 2 * CH, :] = jnp.where(lsel, _mm(lf, y, prec=HI), _mm(lb, y, prec=HI))

    for hh in range(H):
        for d in range(2):
            s_s[hh * 2 + d] = s0_ref[0, d, hh]

    eye = (ii == jj).astype(F32)
    blk_same = [(ii >> s) == (jj >> s) for s in range(1, 7)]

    chains = [(hh, d) for hh in range(H) for d in range(2)]

    def chunk_step(cc, carry):
        st = []
        for p in range(DN_POS):
            for hh, d in chains:
                c = cc * DN_POS + p
                cidx = c if d == 0 else nc - 1 - c
                r0 = pl.multiple_of(cidx * CH, CH)
                c0, c1 = hh * DH, (hh + 1) * DH
                gcol = 2 * H + H * d + hh
                bcol = H * d + hh
                incl = (ii >= jj) if d == 0 else (ii <= jj)
                strict = (ii > jj) if d == 0 else (ii < jj)
                qc = qs[pl.ds(r0, CH), c0:c1]
                kc = ks[pl.ds(r0, CH), c0:c1]
                vc = vs[pl.ds(r0, CH), c0:c1]
                beta = g_ref[pl.ds(r0, CH), bcol:bcol + 1]
                gc = gcc_s[pl.ds(r0, CH), gcol:gcol + 1]
                gr = gct_s[cidx, gcol:gcol + 1, :]
                tot = gr[:, CH - 1:CH] if d == 0 else gr[:, 0:1]
                decay = jnp.where(incl, jnp.exp(jnp.where(incl, gc - gr, 0.0)), 0.0)
                kb = kc * beta
                egc = jnp.exp(gc)
                st.append(dict(r0=r0, c0=c0, c1=c1, d=d, strict=strict, kc=kc, kb=kb, decay=decay,
                               rhs=jnp.concatenate([vc * beta, kb * egc], axis=1),
                               qd=qc * egc, qc=qc, kd=kc * jnp.exp(tot - gc), gl=jnp.exp(tot)))
        for x in st:
            x["lmat"] = jnp.where(x["strict"], _mmb(x["kb"], x["kc"], NT) * x["decay"], 0.0)
            x["qk"] = _mmb(x["qc"], x["kc"], NT) * x["decay"]
            x["tinv"] = eye - jnp.where(blk_same[0], x["lmat"], 0.0)
        for lvl in range(1, len(blk_same)):
            for x in st:
                off = jnp.where(blk_same[lvl] & ~blk_same[lvl - 1], x["lmat"], 0.0)
                x["ot"] = _mm_inv(off, x["tinv"])
            for x in st:
                x["tinv"] = x["tinv"] - _mm_inv(x["tinv"], x["ot"])
        for x in st:
            x["sol"] = _mm_inv(x["tinv"], x["rhs"])
        state = [s_s[i] for i in range(len(chains))]
        for p in range(DN_POS):
            units = st[p * len(chains):(p + 1) * len(chains)]
            for i, x in enumerate(units):
                x["sb"] = state[i].astype(BF16)
                x["v_new"] = x["sol"][:, 0:DH] - _mmb(x["sol"][:, DH:2 * DH], x["sb"])
            for i, x in enumerate(units):
                o = _mmb(x["qd"], x["sb"]) + _mmb(x["qk"], x["v_new"])
                state[i] = state[i] * x["gl"] + _mmb(x["kd"], x["v_new"], TN)
                oacc[x["d"], pl.ds(x["r0"], CH), x["c0"]:x["c1"]] = o
        for i in range(len(chains)):
            s_s[i] = state[i]
        return carry

    lax.fori_loop(0, nc // DN_POS, chunk_step, 0)

    for hh in range(H):
        c0, c1 = hh * DH, (hh + 1) * DH
        o = oacc[0, :, c0:c1] + oacc[1, :, c0:c1]
        o = o * lax.rsqrt(jnp.mean(o * o, axis=-1, keepdims=True) + EPS)
        o_ref[:, c0:c1] = o * nw_ref[...] * _silu(z_ref[:, c0:c1])
        for d in range(2):
            sfin_ref[0, nprev, d, hh] = s_s[hh * 2 + d]
    for layer in range(nprev):
        sfin_ref[0, layer] = prev_ref[0, layer]


def _deltanet(p, g, gt, conv_w, norm_w, s0, prev, n, row_blk0):
    bsz = s0.shape[0]
    nc = n // CH
    width = H * DH
    nprev = 0 if prev is None else prev.shape[1]
    state_spec = lambda k: pl.BlockSpec((1, k, 2, H, DH, DH), lambda b: (b, 0, 0, 0, 0, 0))
    prev_args = [] if prev is None else [prev]
    prev_specs = [] if prev is None else [state_spec(nprev)]
    return pl.pallas_call(
        functools.partial(_dn_kernel, n, nprev),
        grid=(bsz,),
        in_specs=[pl.BlockSpec((n, width), lambda b: (row_blk0 + b, 0)),
                  pl.BlockSpec((n, width), lambda b: (row_blk0 + b, 1)),
                  pl.BlockSpec((n, width), lambda b: (row_blk0 + b, 2)),
                  pl.BlockSpec((n, width), lambda b: (row_blk0 + b, 3)),
                  pl.BlockSpec((n, 4 * H), lambda b: (row_blk0 + b, 0)),
                  pl.BlockSpec((nc, 4 * H, CH), lambda b: (row_blk0 + b, 0, 0)),
                  pl.BlockSpec((3, 3 * width), lambda b: (0, 0)),
                  pl.BlockSpec((1, DH), lambda b: (0, 0)),
                  pl.BlockSpec((1, 2, H, DH, DH), lambda b: (b, 0, 0, 0, 0))] + prev_specs,
        out_specs=[pl.BlockSpec((n, width), lambda b: (b, 0)), state_spec(nprev + 1)],
        out_shape=[jax.ShapeDtypeStruct((bsz * n, width), F32),
                   jax.ShapeDtypeStruct((bsz, nprev + 1, 2, H, DH, DH), F32)],
        scratch_shapes=[pltpu.VMEM((n, width), F32), pltpu.VMEM((n, width), F32),
                        pltpu.VMEM((n, width), F32), pltpu.VMEM((n, 4 * H), F32),
                        pltpu.VMEM((nc, 4 * H, CH), F32), pltpu.VMEM((2 * H, DH, DH), F32),
                        pltpu.VMEM((2, n, width), F32)],
        compiler_params=_params(("arbitrary",)),
        name="deltanet_n%d" % n,
    )(p, p, p, p, g, gt, conv_w, norm_w, s0, *prev_args)


def _mix2_kernel(sb_ref, sc_ref, su_ref, fu_ref, cw_ref, bdc_ref, bds_ref, cs_ref, o_ref):
    w = sb_ref.shape[1]
    cu = sc_ref[...] * su_ref[...]
    o_ref[:, 0:w] = sb_ref[...] * _shift_conv(cu, cw_ref, 0, w)
    fu = fu_ref[...]
    a = _mm3(fu, bdc_ref[...])
    b = _mm3(fu, bds_ref[...])
    o_ref[:, w:2 * w] = _mmb(cs_ref[...], jnp.concatenate([a, b], axis=0))


def _mix2(p, conv_w, bdc, bds, cs, bsz, n, row_blk0):
    w = 256
    return pl.pallas_call(
        _mix2_kernel,
        grid=(bsz,),
        in_specs=[pl.BlockSpec((n, w), lambda b: (row_blk0 + b, 8)),
                  pl.BlockSpec((n, w), lambda b: (row_blk0 + b, 9)),
                  pl.BlockSpec((n, w), lambda b: (row_blk0 + b, 10)),
                  pl.BlockSpec((n, w), lambda b: (row_blk0 + b, 11)),
                  pl.BlockSpec((3, w), lambda b: (0, 0)),
                  pl.BlockSpec((w, w), lambda b: (0, 0)),
                  pl.BlockSpec((w, w), lambda b: (0, 0)),
                  pl.BlockSpec((n, 2 * n), lambda b: (0, 0))],
        out_specs=pl.BlockSpec((n, 2 * w), lambda b: (b, 0)),
        out_shape=jax.ShapeDtypeStruct((bsz * n, 2 * w), F32),
        compiler_params=_params(("arbitrary",)),
        name="conv_fourier_n%d" % n,
    )(p, p, p, p, conv_w, bdc, bds, cs)


def _dft_tables(n):
    gw = 64
    k = np.arange(n, dtype=np.int64)
    ang = ((k[:, None] * k[None, :]) % n) * (2.0 * math.pi / n)
    scale = 1.0 / math.sqrt(n * gw)
    cs = np.concatenate([np.cos(ang), -np.sin(ang)], axis=1) * scale
    c = np.arange(256, dtype=np.int64)
    angc = (((c[:, None] % gw) * (c[None, :] % gw)) % gw) * (2.0 * math.pi / gw)
    same = (c[:, None] // gw) == (c[None, :] // gw)
    bdc = np.where(same, np.cos(angc), 0.0)
    bds = np.where(same, np.sin(angc), 0.0)
    return jnp.asarray(cs, F32).astype(BF16), jnp.asarray(bdc, F32), jnp.asarray(bds, F32)


OUT_SUB = 2


def _out_kernel(t1, n2, tm, dn1_ref, dn2_ref, mx1_ref, mx2_ref, x_ref, mod_ref, nw_ref, wo_ref, wr_ref,
                x1_ref, h2_ref, aff_ref):
    t0 = pl.program_id(0) * tm
    r = _mod_row(t0, t1, n2)
    half = D // 2
    sub = tm // OUT_SUB
    mixes = []
    for k in range(OUT_SUB):
        rs = slice(k * sub, (k + 1) * sub)
        dn = jnp.where(t0 < t1, dn1_ref[rs, :], dn2_ref[rs, :])
        mx = jnp.where(t0 < t1, mx1_ref[rs, :], mx2_ref[rs, :])
        mixes.append(_mmb(dn, wo_ref[0:half, :]) + _mmb(mx, wo_ref[half:D, :]))
    h2s = []
    for k in range(OUT_SUB):
        rs = slice(k * sub, (k + 1) * sub)
        x1 = x_ref[rs, :] + mod_ref[pl.ds(r, 1), 2 * D:3 * D] * mixes[k]
        x1_ref[rs, :] = x1
        y = x1 * lax.rsqrt(jnp.mean(x1 * x1, axis=-1, keepdims=True) + EPS) * nw_ref[...]
        h2 = y * (1.0 + mod_ref[pl.ds(r, 1), 4 * D:5 * D]) + mod_ref[pl.ds(r, 1), 3 * D:4 * D]
        _store_token_tiles(h2_ref.at[pl.ds(k * sub * ROW_TILES, sub * ROW_TILES)], h2)
        h2s.append(h2)
    logits = [_mm3(h2, wr_ref[...]) for h2 in h2s]
    for k in range(OUT_SUB):
        lt = logits[k].T[0:E, :]
        ex = jnp.exp(lt - jnp.max(lt, axis=0, keepdims=True))
        aff_ref[:, k * sub:(k + 1) * sub] = ex / jnp.sum(ex, axis=0, keepdims=True)


def _out_proj(dn1, dn2, mx1, mx2, x, mod, nw, wo, wr, t1, n2, tm):
    t_all = x.shape[0]
    nt1 = t1 // tm
    spec1 = pl.BlockSpec((tm, D // 2), lambda i: (jnp.minimum(i, nt1 - 1), 0))
    spec2 = pl.BlockSpec((tm, D // 2), lambda i: (jnp.maximum(i - nt1, 0), 0))
    return pl.pallas_call(
        functools.partial(_out_kernel, t1, n2, tm),
        grid=(t_all // tm,),
        in_specs=[spec1, spec2, spec1, spec2,
                  pl.BlockSpec((tm, D), lambda i: (i, 0)),
                  pl.BlockSpec((8, NMOD * D), lambda i: (0, 0)),
                  pl.BlockSpec((1, D), lambda i: (0, 0)),
                  pl.BlockSpec((D, D), lambda i: (0, 0)),
                  pl.BlockSpec((D, 128), lambda i: (0, 0))],
        out_specs=[pl.BlockSpec((tm, D), lambda i: (i, 0)),
                   pl.BlockSpec((tm * ROW_TILES, 128), lambda i: (i, 0)),
                   pl.BlockSpec((E, tm), lambda i: (0, i))],
        out_shape=[jax.ShapeDtypeStruct((t_all, D), F32),
                   jax.ShapeDtypeStruct((t_all * ROW_TILES, 128), F32),
                   jax.ShapeDtypeStruct((E, t_all), F32)],
        compiler_params=_params(("arbitrary",)),
        name="out_proj",
    )(dn1, dn2, mx1, mx2, x, mod, nw, wo, wr)


SEL_TB = 128
N_SLOT_ROWS = 16
SPILL_BLOCKS = 2


def _sel_kernel(t, cap, aff_ref, slots_ref, lo_ref, hi_ref, a3, c3, starts_v, starts_s, sem):
    ntb = t // SEL_TB
    aff = aff_ref[...]

    def search(i, thr):
        cand = thr | jnp.left_shift(jnp.int32(1), 30 - i)
        cnt = jnp.sum((aff >= pltpu.bitcast(cand, F32)).astype(F32), axis=1, keepdims=True)
        return jnp.where(cnt >= cap, cand, thr)

    thr = lax.fori_loop(0, 31, search, jnp.zeros((E, 1), I32))
    gt = (aff >= pltpu.bitcast(thr + 1, F32)).astype(F32)
    eq = (aff >= pltpu.bitcast(thr, F32)).astype(F32) - gt
    need = cap - jnp.sum(gt, axis=1, keepdims=True)

    ui = _iota((SEL_TB, SEL_TB), 0)
    uj = _iota((SEL_TB, SEL_TB), 1)
    upper = (ui < uj).astype(BF16)
    blk_lane = _iota((1, 128), 1)

    def excl_cumsum(rows, dst, r0, r1):
        carry = jnp.zeros((rows.shape[0], 1), F32)
        starts = jnp.zeros((rows.shape[0], 128), F32)
        for j in range(ntb):
            blk = rows[:, j * SEL_TB:(j + 1) * SEL_TB]
            dst[j, r0:r1, :] = _mm(blk.astype(BF16), upper) + carry
            starts = jnp.where(blk_lane == j, carry, starts)
            carry = carry + jnp.sum(blk, axis=1, keepdims=True)
        return starts

    excl_cumsum(eq, c3, 0, E)
    rank_eq = jnp.concatenate([c3[j, 0:E, :] for j in range(ntb)], axis=1)
    sel = jnp.maximum(gt, jnp.where(rank_eq < need, eq, 0.0))
    n_tok = jnp.sum(sel, axis=0, keepdims=True)
    starts = excl_cumsum(jnp.concatenate([sel, jnp.broadcast_to(n_tok, (8, t))], axis=0), c3, 0, E + 8)
    starts_v[...] = starts.astype(I32)
    to_smem = pltpu.make_async_copy(starts_v, starts_s, sem)
    to_smem.start()
    ei = _iota((E, E), 0)
    ej = _iota((E, E), 1)
    below = _mm((ej < ei).astype(BF16), sel.astype(BF16))
    for j in range(ntb):
        sl = slice(j * SEL_TB, (j + 1) * SEL_TB)
        off = c3[j, E:E + 1, :]
        lo_ref[:, sl] = off
        hi_ref[:, sl] = off + n_tok[:, sl]
        rank = off + below[:, sl]
        a3[j, 0] = sel[:, sl]
        a3[j, 1] = aff[:, sl]
        a3[j, 2] = rank
    slots_ref[...] = jnp.zeros(slots_ref.shape, F32)
    to_smem.wait()

    win_iota = _iota((2 * SEL_TB, SEL_TB), 0)
    tok_iota = _iota((1, SEL_TB), 1)

    def per_expert(e, carry):
        def per_block(j, carry2):
            wb = starts_s[e, j] >> 7
            pos = c3[j, pl.ds(e, 1), :].astype(I32) - wb * SEL_TB
            chosen = a3[j, 0, pl.ds(e, 1), :]
            w = a3[j, 1, pl.ds(e, 1), :]
            rank = a3[j, 2, pl.ds(e, 1), :].astype(I32)
            tok = tok_iota + j * SEL_TB
            w_hi = w.astype(BF16).astype(F32)
            w_mid = (w - w_hi).astype(BF16).astype(F32)
            w_lo = w - w_hi - w_mid
            vals = jnp.concatenate(
                [(tok >> 7).astype(F32), (tok & 127).astype(F32), w_hi, w_mid, w_lo,
                 (rank >> 7).astype(F32), (rank & 127).astype(F32),
                 jnp.zeros((N_SLOT_ROWS - 7, SEL_TB), F32)], axis=0)
            onehot = jnp.where((win_iota == pos) & (chosen > 0.0), 1.0, 0.0)
            placed = _mmb(vals, onehot, NT)
            slots_ref[e, wb] += placed[:, 0:SEL_TB]
            slots_ref[e, wb + 1] += placed[:, SEL_TB:2 * SEL_TB]
            return carry2
        return lax.fori_loop(0, ntb, per_block, carry, unroll=8)

    lax.fori_loop(0, E, per_expert, 0)


def _select(aff_t, t, cap, col_blk):
    ntb = t // SEL_TB
    nsb = cap // SEL_TB + SPILL_BLOCKS
    return pl.pallas_call(
        functools.partial(_sel_kernel, t, cap),
        grid=(1,),
        in_specs=[pl.BlockSpec((E, t), lambda i: (0, col_blk))],
        out_specs=[pl.BlockSpec((E, nsb, N_SLOT_ROWS, SEL_TB), lambda i: (0, 0, 0, 0)),
                   pl.BlockSpec((1, t), lambda i: (0, 0)),
                   pl.BlockSpec((1, t), lambda i: (0, 0))],
        out_shape=[jax.ShapeDtypeStruct((E, nsb, N_SLOT_ROWS, SEL_TB), F32),
                   jax.ShapeDtypeStruct((1, t), F32),
                   jax.ShapeDtypeStruct((1, t), F32)],
        scratch_shapes=[pltpu.VMEM((ntb, 3, E, SEL_TB), F32),
                        pltpu.VMEM((ntb, E + 8, SEL_TB), F32),
                        pltpu.VMEM((E + 8, 128), I32),
                        pltpu.SMEM((E + 8, 128), I32),
                        pltpu.SemaphoreType.DMA(())],
        compiler_params=_params(("arbitrary",)),
        name="select_t%d" % t,
    )(aff_t)


DMA_UNROLL = 16


def _ffn_kernel(rc, nch, idx_ref, dst_ref, h2_hbm, wcol_ref, wg_ref, wu_ref, wd_ref, z_hbm,
                xbuf, ybuf, wgb, wub, wdb, gsem, ssem):
    c = pl.program_id(1)
    step = pl.program_id(0) * nch + c
    nsteps = E * nch
    slot = step % 2

    def start_gather(s):
        buf = xbuf.at[s % 2]
        sem = gsem.at[s % 2]

        def issue(jb, carry):
            for u in range(DMA_UNROLL):
                j = jb * DMA_UNROLL + u
                src = pl.multiple_of(idx_ref[s * rc + j], ROW_TILES)
                pltpu.make_async_copy(h2_hbm.at[pl.ds(src, ROW_TILES)],
                                      buf.at[pl.ds(j * ROW_TILES, ROW_TILES)], sem).start(priority=u % 2)
            return carry

        lax.fori_loop(0, rc // DMA_UNROLL, issue, 0)

    @pl.when(step == 0)
    def _():
        start_gather(step)

    @pl.when(step + 1 < nsteps)
    def _():
        start_gather(step + 1)

    @pl.when(c == 0)
    def _():
        wgb[...] = wg_ref[0, 0].astype(BF16)
        wub[...] = wu_ref[0, 0].astype(BF16)
        wdb[...] = wd_ref[0, 0].astype(BF16)

    pltpu.make_async_copy(h2_hbm.at[pl.ds(0, rc * ROW_TILES)], xbuf.at[slot], gsem.at[slot]).wait()
    xb = _load_token_tiles(xbuf.at[slot], rc).astype(BF16)
    hid = _silu(_mm(xb, wgb[...])) * _mm(xb, wub[...])
    y = _mm(hid.astype(BF16), wdb[...]) * wcol_ref[...]

    @pl.when(step > 0)
    def _():
        pltpu.make_async_copy(ybuf, z_hbm.at[pl.ds(0, rc * ROW_TILES)], ssem).wait()

    _store_token_tiles(ybuf, y)

    def scatter(jb, carry):
        for u in range(DMA_UNROLL):
            j = jb * DMA_UNROLL + u
            row = pl.multiple_of(dst_ref[step * rc + j], ROW_TILES)
            pltpu.make_async_copy(ybuf.at[pl.ds(j * ROW_TILES, ROW_TILES)],
                                  z_hbm.at[pl.ds(row, ROW_TILES)], ssem).start(priority=u % 2)
        return carry

    lax.fori_loop(0, rc // DMA_UNROLL, scatter, 0)

    @pl.when(step == nsteps - 1)
    def _():
        pltpu.make_async_copy(ybuf, z_hbm.at[pl.ds(0, rc * ROW_TILES)], ssem).wait()


def _expert_ffn(idx, dst, h2, wcol, w_gate, w_up, w_down, layer, rc, nch):
    zrows = E * nch * rc
    wspec = pl.BlockSpec((1, 1, D, D), lambda e, c, *_: (layer, e, 0, 0))
    return pl.pallas_call(
        functools.partial(_ffn_kernel, rc, nch),
        grid_spec=pltpu.PrefetchScalarGridSpec(
            num_scalar_prefetch=2,
            grid=(E, nch),
            in_specs=[pl.BlockSpec(memory_space=pl.ANY),
                      pl.BlockSpec((rc, 1), lambda e, c, *_: (e * nch + c, 0)),
                      wspec, wspec, wspec],
            out_specs=pl.BlockSpec(memory_space=pl.ANY),
            scratch_shapes=[pltpu.VMEM((2, rc * ROW_TILES, 128), F32),
                            pltpu.VMEM((rc * ROW_TILES, 128), F32),
                            pltpu.VMEM((D, D), BF16), pltpu.VMEM((D, D), BF16),
                            pltpu.VMEM((D, D), BF16),
                            pltpu.SemaphoreType.DMA((2,)), pltpu.SemaphoreType.DMA(())]),
        out_shape=jax.ShapeDtypeStruct((zrows * ROW_TILES, 128), F32),
        compiler_params=_params(("arbitrary", "arbitrary")),
        name="expert_ffn",
    )(idx, dst, h2, wcol, w_gate, w_up, w_down)


CMB_TB = 256
CMB_RC = 512
CMB_BUFS = 3


def _comb_kernel(t1, n2, zrows, final, first_ref, nom_ref, x1_ref, lo_ref, hi_ref, mod_ref, fw_ref, z_hbm,
                 *refs):
    outs, (zbuf, sem) = refs[:-2], refs[-2:]
    i = pl.program_id(0)
    nsteps = pl.num_programs(0)
    r = _mod_row(i * CMB_TB, t1, n2)
    total = first_ref[nsteps]

    def chunk_copy(g):
        b = pl.multiple_of(jnp.minimum(nom_ref[g], zrows - CMB_RC) * ROW_TILES, 8 * ROW_TILES)
        slot = g % CMB_BUFS
        return pltpu.make_async_copy(z_hbm.at[pl.ds(b, CMB_RC * ROW_TILES)], zbuf.at[slot], sem.at[slot])

    @pl.when(i == 0)
    def _():
        for g in range(CMB_BUFS - 1):
            @pl.when(g < total)
            def _():
                chunk_copy(g).start()

    eye = _iota((CMB_TB, CMB_TB), 0) == _iota((CMB_TB, CMB_TB), 1)
    lo = jnp.sum(jnp.where(eye, lo_ref[...], 0.0), axis=1, keepdims=True)
    hi = jnp.sum(jnp.where(eye, hi_ref[...], 0.0), axis=1, keepdims=True)
    col = _iota((1, CMB_RC), 1)

    def chunk(g, acc):
        @pl.when(g + CMB_BUFS - 1 < total)
        def _():
            chunk_copy(g + CMB_BUFS - 1).start()

        chunk_copy(g).wait()
        nominal = nom_ref[g]
        rows = (col + jnp.minimum(nominal, zrows - CMB_RC)).astype(F32)
        pick = (rows >= lo) & (rows < hi) & (rows >= nominal.astype(F32))
        s = jnp.where(pick, 1.0, 0.0).astype(BF16)
        return acc + _mmb(s, _load_token_tiles(zbuf.at[g % CMB_BUFS], CMB_RC))

    moe = lax.fori_loop(first_ref[i], first_ref[i + 1], chunk, jnp.zeros((CMB_TB, D), F32))

    x2 = x1_ref[...] + mod_ref[pl.ds(r, 1), 5 * D:6 * D] * moe
    if final:
        y = x2 * lax.rsqrt(jnp.mean(x2 * x2, axis=-1, keepdims=True) + EPS) * fw_ref[...]

        @pl.when(i * CMB_TB < t1)
        def _():
            outs[0][...] = y

        @pl.when(i * CMB_TB >= t1)
        def _():
            outs[1][...] = y
    else:
        outs[0][...] = x2


def _combine(lo, hi, x1, mod, fw, z, t1, n2, final):
    t_all = x1.shape[0]
    zrows = z.shape[0] // ROW_TILES
    nsteps = t_all // CMB_TB
    base = (lo[0, ::CMB_TB].astype(I32) // 8) * 8
    end = hi[0, CMB_TB - 1::CMB_TB].astype(I32)
    nchunk = jnp.maximum((end - base + CMB_RC - 1) // CMB_RC, 1)
    first = jnp.concatenate([jnp.zeros((1,), I32), jnp.cumsum(nchunk)])
    max_chunks = zrows // CMB_RC + 2 * nsteps
    g = jnp.arange(max_chunks, dtype=I32)
    step_of = jnp.minimum(jnp.sum(first[None, 1:] <= g[:, None], axis=1), nsteps - 1)
    nominal = base[step_of] + (g - first[step_of]) * CMB_RC
    nb1 = t1 // CMB_TB
    if final:
        out_specs = [pl.BlockSpec((CMB_TB, D), lambda i, *_: (jnp.minimum(i, nb1 - 1), 0)),
                     pl.BlockSpec((CMB_TB, D), lambda i, *_: (jnp.maximum(i - nb1, 0), 0))]
        out_shape = [jax.ShapeDtypeStruct((t1, D), F32), jax.ShapeDtypeStruct((t_all - t1, D), F32)]
    else:
        out_specs = pl.BlockSpec((CMB_TB, D), lambda i, *_: (i, 0))
        out_shape = jax.ShapeDtypeStruct((t_all, D), F32)
    return pl.pallas_call(
        functools.partial(_comb_kernel, t1, n2, zrows, final),
        grid_spec=pltpu.PrefetchScalarGridSpec(
            num_scalar_prefetch=2,
            grid=(t_all // CMB_TB,),
            in_specs=[pl.BlockSpec((CMB_TB, D), lambda i, *_: (i, 0)),
                      pl.BlockSpec((1, CMB_TB), lambda i, *_: (0, i)),
                      pl.BlockSpec((1, CMB_TB), lambda i, *_: (0, i)),
                      pl.BlockSpec((8, NMOD * D), lambda i, *_: (0, 0)),
                      pl.BlockSpec((1, D), lambda i, *_: (0, 0)),
                      pl.BlockSpec(memory_space=pl.ANY)],
            out_specs=out_specs,
            scratch_shapes=[pltpu.VMEM((CMB_BUFS, CMB_RC * ROW_TILES, 128), F32),
                            pltpu.SemaphoreType.DMA((CMB_BUFS,))]),
        out_shape=out_shape,
        compiler_params=_params(("arbitrary",)),
        name="combine",
    )(first, nominal, x1, lo, hi, mod, fw, z)


def _grid_pos_embed(n, d):
    rows = n // GRID_W
    r = np.repeat(np.arange(rows, dtype=np.float64), GRID_W)
    col = np.tile(np.arange(GRID_W, dtype=np.float64), rows)
    quarter = d // 4
    omega = np.power(POS_BASE, -np.arange(quarter, dtype=np.float64) / quarter)
    ra = r[:, None] * omega
    ca = col[:, None] * omega
    return jnp.asarray(np.concatenate([np.sin(ra), np.cos(ra), np.sin(ca), np.cos(ca)], axis=-1), F32)


def _decode_slots(slots):
    e, nsb, rows, tb = slots.shape
    nsb -= SPILL_BLOCKS
    slots = slots[:, :nsb].transpose(0, 2, 1, 3).reshape(e, rows, nsb * tb)
    idx = (slots[:, 0] * 128.0 + slots[:, 1]).astype(I32)
    w = slots[:, 2] + slots[:, 3] + slots[:, 4]
    rank = (slots[:, 5] * 128.0 + slots[:, 6]).astype(I32)
    return idx, w, rank


def kernel(x_prompt, x_sample, state_delta, c, c_ctx, w_ada, b_ada, norm1_w, norm2_w, w_in, dn_conv_w,
           dn_a_log, dn_dt_bias, dn_norm_w, sc_conv_w, w_out, w_router, w_gate, w_up, w_down,
           final_norm_w):
    b1, n1, _ = x_prompt.shape
    b2, n2, _ = x_sample.shape
    depth = w_ada.shape[0]
    t1, t2 = b1 * n1, b2 * n2
    t_all = t1 + t2
    cap1 = max(1, 2 * t1 // E)
    cap2 = max(1, 2 * t2 // E)
    tm = 512
    rc = 512 if (cap1 % 512 == 0 and cap2 % 512 == 0) else 128
    assert t1 % n2 == 0 and t1 % tm == 0 and n2 % tm == 0
    assert n1 % (DN_POS * CH) == 0 and n2 % (DN_POS * CH) == 0
    assert t1 % SEL_TB == 0 and t2 % SEL_TB == 0 and t1 % t2 == 0
    assert cap1 % rc == 0 and cap2 % rc == 0 and 2 * t_all >= CMB_RC
    nch = (cap1 + cap2) // rc
    width = H * DH

    cond8 = jnp.zeros((8, D), F32).at[0].set(c_ctx).at[1:1 + b2].set(c)
    mod = _ada(cond8, w_ada, b_ada)

    x = None
    s_zero = jnp.zeros((b1, 2, H, DH, DH), F32)
    tabs1 = _dft_tables(n1)
    tabs2 = _dft_tables(n2)
    ctx_states = None

    for l in range(depth):
        wl = w_in[l]
        wm = jnp.concatenate([wl[:, 0:4 * width], wl[:, 4 * width + 4 * H:]], axis=1).astype(BF16)
        wg = jnp.pad(wl[:, 4 * width:4 * width + 4 * H], ((0, 0), (0, 128 - 4 * H))).astype(BF16)
        al = jnp.pad(dn_a_log[l].reshape(1, 2 * H), ((0, 0), (2 * H, 128 - 4 * H)))
        dt = jnp.pad(dn_dt_bias[l].reshape(1, 2 * H), ((0, 0), (2 * H, 128 - 4 * H)))
        xs = (x,) if l else (x_prompt.reshape(t1, D), x_sample.reshape(t2, D), _grid_pos_embed(n2, D))
        res = _in_proj(xs, mod[l], norm1_w[l].reshape(1, D), wm, wg, al, dt, t1, n2, tm)
        if not l:
            x, res = res[0], res[1:]
        p, g, gt = res

        nwd = dn_norm_w[l].reshape(1, DH)
        dn1, ctx_states = _deltanet(p, g, gt, dn_conv_w[l], nwd, s_zero, ctx_states, n1, 0)
        dn2, _ = _deltanet(p, g, gt, dn_conv_w[l], nwd, state_delta[:, l], None, n2, t1 // n2)
        mx1 = _mix2(p, sc_conv_w[l], tabs1[1], tabs1[2], tabs1[0], b1, n1, 0)
        mx2 = _mix2(p, sc_conv_w[l], tabs2[1], tabs2[2], tabs2[0], b2, n2, t1 // n2)

        wr = jnp.pad(w_router[l], ((0, 0), (0, 128 - E)))
        x1, h2, aff_t = _out_proj(dn1, dn2, mx1, mx2, x, mod[l], norm2_w[l].reshape(1, D),
                                  w_out[l].astype(BF16), wr, t1, n2, tm)

        slots1, lo1, hi1 = _select(aff_t, t1, cap1, 0)
        slots2, lo2, hi2 = _select(aff_t, t2, cap2, t1 // t2)
        idx1, wsel1, rank1 = _decode_slots(slots1)
        idx2, wsel2, rank2 = _decode_slots(slots2)
        idx = jnp.concatenate([idx1, idx2 + t1], axis=1).reshape(-1) * ROW_TILES
        dst = jnp.concatenate([rank1, rank2 + 2 * t1], axis=1).reshape(-1) * ROW_TILES
        wcol = jnp.concatenate([wsel1, wsel2], axis=1).reshape(-1, 1)
        z = _expert_ffn(idx, dst, h2, wcol, w_gate, w_up, w_down, l, rc, nch)

        lo = jnp.concatenate([lo1, lo2 + 2.0 * t1], axis=1)
        hi = jnp.concatenate([hi1, hi2 + 2.0 * t1], axis=1)
        x = _combine(lo, hi, x1, mod[l], final_norm_w.reshape(1, D), z, t1, n2, l == depth - 1)

    y_prompt, y_sample = x
    return y_prompt.reshape(b1, n1, D), y_sample.reshape(b2, n2, D), ctx_states
```

```python
import functools
import math

import jax
import jax.numpy as jnp
import numpy as np
from jax import lax
from jax.experimental import pallas as pl
from jax.experimental.pallas import tpu as pltpu

F32 = jnp.float32
BF16 = jnp.bfloat16
I32 = jnp.int32
HI = lax.Precision.HIGHEST

D = 1024
H = 4
DH = 128
CH = 64
E = 16
NMOD = 6
EPS = 1e-6
GRID_W = 64
POS_BASE = 10000.0
P_COLS = 3072
VMEM_LIMIT = 56 * 1024 * 1024

NN = (((1,), (0,)), ((), ()))
NT = (((1,), (1,)), ((), ()))
TN = (((0,), (0,)), ((), ()))


def _mm(a, b, dims=NN, prec=None):
    return lax.dot_general(a, b, dims, precision=prec, preferred_element_type=F32)


def _mmb(a, b, dims=NN):
    return lax.dot_general(a.astype(BF16), b.astype(BF16), dims, preferred_element_type=F32)


def _mm3(a, b, dims=NN):
    a_hi = a.astype(BF16)
    a_lo = (a - a_hi.astype(F32)).astype(BF16)
    b_hi = b.astype(BF16)
    b_lo = (b - b_hi.astype(F32)).astype(BF16)
    return _mm(a_hi, b_hi, dims) + _mm(a_hi, b_lo, dims) + _mm(a_lo, b_hi, dims)


def _mm_inv(a, b):
    return _mmb(a, b)


def _silu(x):
    return x / (1.0 + jnp.exp(-x))


def _sigmoid(x):
    return 1.0 / (1.0 + jnp.exp(-x))


def _softplus(x):
    return jnp.maximum(x, 0.0) + jnp.log1p(jnp.exp(-jnp.abs(x)))


def _iota(shape, dim):
    return lax.broadcasted_iota(I32, shape, dim)


ROW_TILES = D // 128


def _load_token_tiles(ref, n):
    return jnp.concatenate([ref[pl.ds(s, n, stride=ROW_TILES), :] for s in range(ROW_TILES)], axis=1)


def _store_token_tiles(ref, x):
    n = x.shape[0]
    for s in range(ROW_TILES):
        ref[pl.ds(s, n, stride=ROW_TILES), :] = x[:, s * 128:(s + 1) * 128]


def _mod_row(t0, t1, n2):
    return jnp.where(t0 < t1, 0, 1 + jnp.maximum(t0 - t1, 0) // n2)


def _params(sem):
    return pltpu.CompilerParams(dimension_semantics=sem, vmem_limit_bytes=VMEM_LIMIT)


def _ada_kernel(c_ref, w_ref, b_ref, o_ref):
    s = _silu(c_ref[...])
    o_ref[0] = _mm3(s, w_ref[0]) + b_ref[0]


def _ada(cond8, w_ada, b_ada):
    depth = w_ada.shape[0]
    tn = 1536
    return pl.pallas_call(
        _ada_kernel,
        grid=(depth, NMOD * D // tn),
        in_specs=[pl.BlockSpec((8, D), lambda l, j: (0, 0)),
                  pl.BlockSpec((1, D, tn), lambda l, j: (l, 0, j)),
                  pl.BlockSpec((1, 1, tn), lambda l, j: (l, 0, j))],
        out_specs=pl.BlockSpec((1, 8, tn), lambda l, j: (l, 0, j)),
        out_shape=jax.ShapeDtypeStruct((depth, 8, NMOD * D), F32),
        compiler_params=_params(("arbitrary", "arbitrary")),
        name="ada",
    )(cond8, w_ada, b_ada.reshape(depth, 1, NMOD * D))


IN_SUB = 2


def _in_core(tm, r, x, mod_ref, nw_ref, wm_ref, wg_ref, al_ref, dt_ref, p_ref, g_ref, gt_ref):
    sh = mod_ref[pl.ds(r, 1), 0:D]
    sc = mod_ref[pl.ds(r, 1), D:2 * D]
    sub = tm // IN_SUB
    nsplit = 4
    wc = P_COLS // nsplit
    lane = _iota((1, 128), 1)
    hbs = []
    for k in range(IN_SUB):
        xk = x[k * sub:(k + 1) * sub]
        y = xk * lax.rsqrt(jnp.mean(xk * xk, axis=-1, keepdims=True) + EPS) * nw_ref[...]
        hbs.append((y * (1.0 + sc) + sh).astype(BF16))
    for k in range(IN_SUB):
        rs = slice(k * sub, (k + 1) * sub)
        for j in range(nsplit):
            p_ref[rs, j * wc:(j + 1) * wc] = _mm(hbs[k], wm_ref[:, j * wc:(j + 1) * wc])
    for k in range(IN_SUB):
        raw = _mm(hbs[k], wg_ref[...])
        g = -jnp.exp(al_ref[...]) * _softplus(raw + dt_ref[...])
        act = jnp.where(lane < 2 * H, _sigmoid(raw), g)
        g_ref[k * sub:(k + 1) * sub, :] = act[:, 0:4 * H]
        act_t = act.T
        for j in range(sub // CH):
            gt_ref[k * (sub // CH) + j] = act_t[0:4 * H, j * CH:(j + 1) * CH]


def _in_kernel(t1, n2, tm, x_ref, *refs):
    r = _mod_row(pl.program_id(0) * tm, t1, n2)
    _in_core(tm, r, x_ref[...], *refs)


def _in_first_kernel(t1, n2, tm, xp_ref, xs_ref, pos_ref, mod_ref, nw_ref, wm_ref, wg_ref, al_ref,
                     dt_ref, x_ref, p_ref, g_ref, gt_ref):
    t0 = pl.program_id(0) * tm
    x = jnp.where(t0 < t1, xp_ref[...], xs_ref[...] + pos_ref[...])
    x_ref[...] = x
    _in_core(tm, _mod_row(t0, t1, n2), x, mod_ref, nw_ref, wm_ref, wg_ref, al_ref, dt_ref,
             p_ref, g_ref, gt_ref)


def _in_proj(xs, mod, nw, wm, wg, al, dt, t1, n2, tm):
    first = len(xs) == 3
    t_all = t1 + xs[1].shape[0] if first else xs[0].shape[0]
    nt1 = t1 // tm
    wspecs = [pl.BlockSpec((8, NMOD * D), lambda i: (0, 0)),
              pl.BlockSpec((1, D), lambda i: (0, 0)),
              pl.BlockSpec((D, P_COLS), lambda i: (0, 0)),
              pl.BlockSpec((D, 128), lambda i: (0, 0)),
              pl.BlockSpec((1, 128), lambda i: (0, 0)),
              pl.BlockSpec((1, 128), lambda i: (0, 0))]
    out_specs = [pl.BlockSpec((tm, P_COLS), lambda i: (i, 0)),
                 pl.BlockSpec((tm, 4 * H), lambda i: (i, 0)),
                 pl.BlockSpec((tm // CH, 4 * H, CH), lambda i: (i, 0, 0))]
    out_shape = [jax.ShapeDtypeStruct((t_all, P_COLS), F32),
                 jax.ShapeDtypeStruct((t_all, 4 * H), F32),
                 jax.ShapeDtypeStruct((t_all // CH, 4 * H, CH), F32)]
    if first:
        kern = _in_first_kernel
        xspecs = [pl.BlockSpec((tm, D), lambda i: (jnp.minimum(i, nt1 - 1), 0)),
                  pl.BlockSpec((tm, D), lambda i: (jnp.maximum(i - nt1, 0), 0)),
                  pl.BlockSpec((tm, D), lambda i: (jnp.maximum(i - nt1, 0) % (n2 // tm), 0))]
        out_specs = [pl.BlockSpec((tm, D), lambda i: (i, 0))] + out_specs
        out_shape = [jax.ShapeDtypeStruct((t_all, D), F32)] + out_shape
    else:
        kern = _in_kernel
        xspecs = [pl.BlockSpec((tm, D), lambda i: (i, 0))]
    return pl.pallas_call(
        functools.partial(kern, t1, n2, tm),
        grid=(t_all // tm,),
        in_specs=xspecs + wspecs,
        out_specs=out_specs,
        out_shape=out_shape,
        compiler_params=_params(("arbitrary",)),
        name="in_proj_first" if first else "in_proj",
    )(*xs, mod, nw, wm, wg, al, dt)


def _shift_conv(x, w_ref, c0, c1, period=None):
    n = x.shape[0]
    row = _iota((n, 1), 0) & ((period or n) - 1)
    prev = jnp.where(row == 0, 0.0, pltpu.roll(x, 1, 0))
    nxt = jnp.where(row == (period or n) - 1, 0.0, pltpu.roll(x, n - 1, 0))
    return prev * w_ref[0:1, c0:c1] + x * w_ref[1:2, c0:c1] + nxt * w_ref[2:3, c0:c1]


DN_POS = 4


def _dn_kernel(n, nprev, q_ref, k_ref, v_ref, z_ref, g_ref, gt_ref, cw_ref, nw_ref, s0_ref, *refs):
    prev_ref = refs[0] if nprev else None
    o_ref, sfin_ref, qs, ks, vs, gcc_s, gct_s, s_s, oacc = refs[1 if nprev else 0:]
    nc = n // CH
    width = H * DH

    for hh in range(H):
        c0, c1 = hh * DH, (hh + 1) * DH
        q = _silu(_shift_conv(q_ref[:, c0:c1], cw_ref, c0, c1))
        q = q * lax.rsqrt(jnp.sum(q * q, axis=-1, keepdims=True) + 1e-6)
        qs[:, c0:c1] = q * (DH ** -0.5)
        k = _silu(_shift_conv(k_ref[:, c0:c1], cw_ref, width + c0, width + c1))
        ks[:, c0:c1] = k * lax.rsqrt(jnp.sum(k * k, axis=-1, keepdims=True) + 1e-6)
        vs[:, c0:c1] = _silu(_shift_conv(v_ref[:, c0:c1], cw_ref, 2 * width + c0, 2 * width + c1))

    ii = _iota((CH, CH), 0)
    jj = _iota((CH, CH), 1)
    x = gt_ref[...].reshape(nc * 4 * H, CH)
    cf = _mm(x, (ii <= jj).astype(F32), prec=HI)
    cb = _mm(x, (ii >= jj).astype(F32), prec=HI)
    rsel = (_iota((nc * 4 * H, 1), 0) & (4 * H - 1)) < 3 * H
    gct_s[...] = jnp.where(rsel, cf, cb).reshape(nc, 4 * H, CH)
    i2 = _iota((2 * CH, 2 * CH), 0)
    j2 = _iota((2 * CH, 2 * CH), 1)
    same = (i2 >= CH) == (j2 >= CH)
    lf = (same & (i2 >= j2)).astype(F32)
    lb = (same & (i2 <= j2)).astype(F32)
    lsel = _iota((1, 4 * H), 1) < 3 * H
    for m in range(n // (2 * CH)):
        y = g_ref[m * 2 * CH:(m + 1) * 2 * CH, :]
        gcc_s[m * 2 * CH:(m + 1) * 2 * CH, :] = jnp.where(lsel, _mm(lf, y, prec=HI), _mm(lb, y, prec=HI))

    for hh in range(H):
        for d in range(2):
            s_s[hh * 2 + d] = s0_ref[0, d, hh]

    eye = (ii == jj).astype(F32)
    blk_same = [(ii >> s) == (jj >> s) for s in range(1, 7)]
    merge_mask = [blk_same[lvl] & ~blk_same[lvl - 1] for lvl in range(1, len(blk_same))]

    chains = [(hh, d) for hh in range(H) for d in range(2)]

    def chunk_step(cc, carry):
        st = []
        for p in range(DN_POS):
            for hh, d in chains:
                c = cc * DN_POS + p
                cidx = c if d == 0 else nc - 1 - c
                r0 = pl.multiple_of(cidx * CH, CH)
                c0, c1 = hh * DH, (hh + 1) * DH
                gcol = 2 * H + H * d + hh
                bcol = H * d + hh
                incl = (ii >= jj) if d == 0 else (ii <= jj)
                strict = (ii > jj) if d == 0 else (ii < jj)
                qc = qs[pl.ds(r0, CH), c0:c1]
                kc = ks[pl.ds(r0, CH), c0:c1]
                vc = vs[pl.ds(r0, CH), c0:c1]
                beta = g_ref[pl.ds(r0, CH), bcol:bcol + 1]
                gc = gcc_s[pl.ds(r0, CH), gcol:gcol + 1]
                gr = gct_s[cidx, gcol:gcol + 1, :]
                tot = gr[:, CH - 1:CH] if d == 0 else gr[:, 0:1]
                decay = jnp.where(incl, jnp.exp(jnp.where(incl, gc - gr, 0.0)), 0.0)
                kb = kc * beta
                egc = jnp.exp(gc)
                st.append(dict(r0=r0, c0=c0, c1=c1, d=d, strict=strict, kc=kc, kb=kb, decay=decay,
                               rhs=jnp.concatenate([vc * beta, kb * egc], axis=1),
                               qd=qc * egc, qc=qc, kd=kc * jnp.exp(tot - gc), gl=jnp.exp(tot)))
        for x in st:
            x["lmat"] = jnp.where(x["strict"], _mmb(x["kb"], x["kc"], NT) * x["decay"], 0.0)
            x["qk"] = _mmb(x["qc"], x["kc"], NT) * x["decay"]
            x["tinv"] = eye - jnp.where(blk_same[0], x["lmat"], 0.0)
        for mask in merge_mask:
            for x in st:
                x["ot"] = _mm_inv(jnp.where(mask, x["lmat"], 0.0), x["tinv"])
            for x in st:
                x["tinv"] = x["tinv"] - _mm_inv(x["tinv"], x["ot"])
        for x in st:
            x["sol"] = _mm_inv(x["tinv"], x["rhs"])
        state = [s_s[i] for i in range(len(chains))]
        for p in range(DN_POS):
            units = st[p * len(chains):(p + 1) * len(chains)]
            for i, x in enumerate(units):
                x["sb"] = state[i].astype(BF16)
                x["v_new"] = x["sol"][:, 0:DH] - _mmb(x["sol"][:, DH:2 * DH], x["sb"])
            for i, x in enumerate(units):
                o = _mmb(x["qd"], x["sb"]) + _mmb(x["qk"], x["v_new"])
                state[i] = state[i] * x["gl"] + _mmb(x["kd"], x["v_new"], TN)
                oacc[x["d"], pl.ds(x["r0"], CH), x["c0"]:x["c1"]] = o
        for i in range(len(chains)):
            s_s[i] = state[i]
        return carry

    lax.fori_loop(0, nc // DN_POS, chunk_step, 0)

    for hh in range(H):
        c0, c1 = hh * DH, (hh + 1) * DH
        o = oacc[0, :, c0:c1] + oacc[1, :, c0:c1]
        o = o * lax.rsqrt(jnp.mean(o * o, axis=-1, keepdims=True) + EPS)
        o_ref[:, c0:c1] = o * nw_ref[...] * _silu(z_ref[:, c0:c1])
        for d in range(2):
            sfin_ref[0, nprev, d, hh] = s_s[hh * 2 + d]
    for layer in range(nprev):
        sfin_ref[0, layer] = prev_ref[0, layer]


def _deltanet(p, g, gt, conv_w, norm_w, s0, prev, n, row_blk0):
    bsz = s0.shape[0]
    nc = n // CH
    width = H * DH
    nprev = 0 if prev is None else prev.shape[1]
    state_spec = lambda k: pl.BlockSpec((1, k, 2, H, DH, DH), lambda b: (b, 0, 0, 0, 0, 0))
    prev_args = [] if prev is None else [prev]
    prev_specs = [] if prev is None else [state_spec(nprev)]
    return pl.pallas_call(
        functools.partial(_dn_kernel, n, nprev),
        grid=(bsz,),
        in_specs=[pl.BlockSpec((n, width), lambda b: (row_blk0 + b, 0)),
                  pl.BlockSpec((n, width), lambda b: (row_blk0 + b, 1)),
                  pl.BlockSpec((n, width), lambda b: (row_blk0 + b, 2)),
                  pl.BlockSpec((n, width), lambda b: (row_blk0 + b, 3)),
                  pl.BlockSpec((n, 4 * H), lambda b: (row_blk0 + b, 0)),
                  pl.BlockSpec((nc, 4 * H, CH), lambda b: (row_blk0 + b, 0, 0)),
                  pl.BlockSpec((3, 3 * width), lambda b: (0, 0)),
                  pl.BlockSpec((1, DH), lambda b: (0, 0)),
                  pl.BlockSpec((1, 2, H, DH, DH), lambda b: (b, 0, 0, 0, 0))] + prev_specs,
        out_specs=[pl.BlockSpec((n, width), lambda b: (b, 0)), state_spec(nprev + 1)],
        out_shape=[jax.ShapeDtypeStruct((bsz * n, width), F32),
                   jax.ShapeDtypeStruct((bsz, nprev + 1, 2, H, DH, DH), F32)],
        scratch_shapes=[pltpu.VMEM((n, width), F32), pltpu.VMEM((n, width), F32),
                        pltpu.VMEM((n, width), F32), pltpu.VMEM((n, 4 * H), F32),
                        pltpu.VMEM((nc, 4 * H, CH), F32), pltpu.VMEM((2 * H, DH, DH), F32),
                        pltpu.VMEM((2, n, width), F32)],
        compiler_params=_params(("arbitrary",)),
        name="deltanet_n%d" % n,
    )(p, p, p, p, g, gt, conv_w, norm_w, s0, *prev_args)


MIX_GROUP = 4


def _mix2_kernel(n, sb_ref, sc_ref, su_ref, fu_ref, cw_ref, bdc_ref, bds_ref, cs_ref, o_ref):
    rows, w = sb_ref.shape
    cu = sc_ref[...] * su_ref[...]
    o_ref[:, 0:w] = sb_ref[...] * _shift_conv(cu, cw_ref, 0, w, n)
    fu = fu_ref[...]
    a = _mm3(fu, bdc_ref[...])
    b = _mm3(fu, bds_ref[...])
    for q in range(rows // n):
        rs = slice(q * n, (q + 1) * n)
        o_ref[rs, w:2 * w] = _mmb(cs_ref[...], jnp.concatenate([a[rs], b[rs]], axis=0))


def _mix2(p, conv_w, bdc, bds, cs, bsz, n, row_blk0, group):
    w = 256
    rows = group * n
    blk0 = row_blk0 // group
    assert bsz % group == 0 and row_blk0 % group == 0
    return pl.pallas_call(
        functools.partial(_mix2_kernel, n),
        grid=(bsz // group,),
        in_specs=[pl.BlockSpec((rows, w), lambda b: (blk0 + b, 8)),
                  pl.BlockSpec((rows, w), lambda b: (blk0 + b, 9)),
                  pl.BlockSpec((rows, w), lambda b: (blk0 + b, 10)),
                  pl.BlockSpec((rows, w), lambda b: (blk0 + b, 11)),
                  pl.BlockSpec((3, w), lambda b: (0, 0)),
                  pl.BlockSpec((w, w), lambda b: (0, 0)),
                  pl.BlockSpec((w, w), lambda b: (0, 0)),
                  pl.BlockSpec((n, 2 * n), lambda b: (0, 0))],
        out_specs=pl.BlockSpec((rows, 2 * w), lambda b: (b, 0)),
        out_shape=jax.ShapeDtypeStruct((bsz * n, 2 * w), F32),
        compiler_params=_params(("arbitrary",)),
        name="conv_fourier_n%d" % n,
    )(p, p, p, p, conv_w, bdc, bds, cs)


def _dft_tables(n):
    gw = 64
    k = np.arange(n, dtype=np.int64)
    ang = ((k[:, None] * k[None, :]) % n) * (2.0 * math.pi / n)
    scale = 1.0 / math.sqrt(n * gw)
    cs = np.concatenate([np.cos(ang), -np.sin(ang)], axis=1) * scale
    c = np.arange(256, dtype=np.int64)
    angc = (((c[:, None] % gw) * (c[None, :] % gw)) % gw) * (2.0 * math.pi / gw)
    same = (c[:, None] // gw) == (c[None, :] // gw)
    bdc = np.where(same, np.cos(angc), 0.0)
    bds = np.where(same, np.sin(angc), 0.0)
    return jnp.asarray(cs, F32).astype(BF16), jnp.asarray(bdc, F32), jnp.asarray(bds, F32)


OUT_SUB = 2


def _out_kernel(t1, n2, tm, dn1_ref, dn2_ref, mx1_ref, mx2_ref, x_ref, mod_ref, nw_ref, wo_ref, wr_ref,
                x1_ref, h2_ref, aff_ref):
    t0 = pl.program_id(0) * tm
    r = _mod_row(t0, t1, n2)
    half = D // 2
    sub = tm // OUT_SUB
    mixes = []
    for k in range(OUT_SUB):
        rs = slice(k * sub, (k + 1) * sub)
        dn = jnp.where(t0 < t1, dn1_ref[rs, :], dn2_ref[rs, :])
        mx = jnp.where(t0 < t1, mx1_ref[rs, :], mx2_ref[rs, :])
        mixes.append(_mmb(dn, wo_ref[0:half, :]) + _mmb(mx, wo_ref[half:D, :]))
    h2s = []
    for k in range(OUT_SUB):
        rs = slice(k * sub, (k + 1) * sub)
        x1 = x_ref[rs, :] + mod_ref[pl.ds(r, 1), 2 * D:3 * D] * mixes[k]
        x1_ref[rs, :] = x1
        y = x1 * lax.rsqrt(jnp.mean(x1 * x1, axis=-1, keepdims=True) + EPS) * nw_ref[...]
        h2 = y * (1.0 + mod_ref[pl.ds(r, 1), 4 * D:5 * D]) + mod_ref[pl.ds(r, 1), 3 * D:4 * D]
        _store_token_tiles(h2_ref.at[pl.ds(k * sub * ROW_TILES, sub * ROW_TILES)], h2)
        h2s.append(h2)
    logits = [_mm3(h2, wr_ref[...]) for h2 in h2s]
    for k in range(OUT_SUB):
        lt = logits[k].T[0:E, :]
        ex = jnp.exp(lt - jnp.max(lt, axis=0, keepdims=True))
        aff_ref[:, k * sub:(k + 1) * sub] = ex / jnp.sum(ex, axis=0, keepdims=True)


def _out_proj(dn1, dn2, mx1, mx2, x, mod, nw, wo, wr, t1, n2, tm):
    t_all = x.shape[0]
    nt1 = t1 // tm
    spec1 = pl.BlockSpec((tm, D // 2), lambda i: (jnp.minimum(i, nt1 - 1), 0))
    spec2 = pl.BlockSpec((tm, D // 2), lambda i: (jnp.maximum(i - nt1, 0), 0))
    return pl.pallas_call(
        functools.partial(_out_kernel, t1, n2, tm),
        grid=(t_all // tm,),
        in_specs=[spec1, spec2, spec1, spec2,
                  pl.BlockSpec((tm, D), lambda i: (i, 0)),
                  pl.BlockSpec((8, NMOD * D), lambda i: (0, 0)),
                  pl.BlockSpec((1, D), lambda i: (0, 0)),
                  pl.BlockSpec((D, D), lambda i: (0, 0)),
                  pl.BlockSpec((D, 128), lambda i: (0, 0))],
        out_specs=[pl.BlockSpec((tm, D), lambda i: (i, 0)),
                   pl.BlockSpec((tm * ROW_TILES, 128), lambda i: (i, 0)),
                   pl.BlockSpec((E, tm), lambda i: (0, i))],
        out_shape=[jax.ShapeDtypeStruct((t_all, D), F32),
                   jax.ShapeDtypeStruct((t_all * ROW_TILES, 128), F32),
                   jax.ShapeDtypeStruct((E, t_all), F32)],
        compiler_params=_params(("arbitrary",)),
        name="out_proj",
    )(dn1, dn2, mx1, mx2, x, mod, nw, wo, wr)


SEL_TB = 128
N_SLOT_ROWS = 16
SPILL_BLOCKS = 2


def _sel_kernel(t, cap, aff_ref, slots_ref, lo_ref, hi_ref, a3, c3, starts_v, starts_s, sem):
    ntb = t // SEL_TB
    aff = aff_ref[...]

    def search(i, thr):
        cand = thr | jnp.left_shift(jnp.int32(1), 30 - i)
        cnt = jnp.sum((aff >= pltpu.bitcast(cand, F32)).astype(F32), axis=1, keepdims=True)
        return jnp.where(cnt >= cap, cand, thr)

    thr = lax.fori_loop(0, 31, search, jnp.zeros((E, 1), I32))
    gt = (aff >= pltpu.bitcast(thr + 1, F32)).astype(F32)
    eq = (aff >= pltpu.bitcast(thr, F32)).astype(F32) - gt
    need = cap - jnp.sum(gt, axis=1, keepdims=True)

    ui = _iota((SEL_TB, SEL_TB), 0)
    uj = _iota((SEL_TB, SEL_TB), 1)
    upper = (ui < uj).astype(BF16)
    blk_lane = _iota((1, 128), 1)

    def excl_cumsum(rows, dst, r0, r1):
        carry = jnp.zeros((rows.shape[0], 1), F32)
        starts = jnp.zeros((rows.shape[0], 128), F32)
        for j in range(ntb):
            blk = rows[:, j * SEL_TB:(j + 1) * SEL_TB]
            dst[j, r0:r1, :] = _mm(blk.astype(BF16), upper) + carry
            starts = jnp.where(blk_lane == j, carry, starts)
            carry = carry + jnp.sum(blk, axis=1, keepdims=True)
        return starts

    excl_cumsum(eq, c3, 0, E)
    rank_eq = jnp.concatenate([c3[j, 0:E, :] for j in range(ntb)], axis=1)
    sel = jnp.maximum(gt, jnp.where(rank_eq < need, eq, 0.0))
    n_tok = jnp.sum(sel, axis=0, keepdims=True)
    starts = excl_cumsum(jnp.concatenate([sel, jnp.broadcast_to(n_tok, (8, t))], axis=0), c3, 0, E + 8)
    starts_v[...] = starts.astype(I32)
    to_smem = pltpu.make_async_copy(starts_v, starts_s, sem)
    to_smem.start()
    ei = _iota((E, E), 0)
    ej = _iota((E, E), 1)
    below = _mm((ej < ei).astype(BF16), sel.astype(BF16))
    for j in range(ntb):
        sl = slice(j * SEL_TB, (j + 1) * SEL_TB)
        off = c3[j, E:E + 1, :]
        lo_ref[:, sl] = off
        hi_ref[:, sl] = off + n_tok[:, sl]
        rank = off + below[:, sl]
        a3[j, 0] = sel[:, sl]
        a3[j, 1] = aff[:, sl]
        a3[j, 2] = rank
    slots_ref[...] = jnp.zeros(slots_ref.shape, F32)
    to_smem.wait()

    win_iota = _iota((2 * SEL_TB, SEL_TB), 0)
    tok_iota = _iota((1, SEL_TB), 1)

    def per_expert(e, carry):
        def per_block(j, carry2):
            wb = starts_s[e, j] >> 7
            chosen = a3[j, 0, pl.ds(e, 1), :]
            pos = c3[j, pl.ds(e, 1), :].astype(I32) - wb * SEL_TB
            pos = jnp.where(chosen > 0.0, pos, -1)
            w = a3[j, 1, pl.ds(e, 1), :]
            rank = a3[j, 2, pl.ds(e, 1), :].astype(I32)
            tok = tok_iota + j * SEL_TB
            w_hi = w.astype(BF16).astype(F32)
            w_mid = (w - w_hi).astype(BF16).astype(F32)
            w_lo = w - w_hi - w_mid
            vals = jnp.concatenate(
                [(tok >> 7).astype(F32), (tok & 127).astype(F32), w_hi, w_mid, w_lo,
                 (rank >> 7).astype(F32), (rank & 127).astype(F32),
                 jnp.zeros((N_SLOT_ROWS - 7, SEL_TB), F32)], axis=0)
            onehot = jnp.where(win_iota == pos, 1.0, 0.0)
            placed = _mmb(vals, onehot, NT)
            slots_ref[e, wb] += placed[:, 0:SEL_TB]
            slots_ref[e, wb + 1] += placed[:, SEL_TB:2 * SEL_TB]
            return carry2
        return lax.fori_loop(0, ntb, per_block, carry, unroll=8)

    lax.fori_loop(0, E, per_expert, 0)


def _select(aff_t, t, cap, col_blk):
    ntb = t // SEL_TB
    nsb = cap // SEL_TB + SPILL_BLOCKS
    return pl.pallas_call(
        functools.partial(_sel_kernel, t, cap),
        grid=(1,),
        in_specs=[pl.BlockSpec((E, t), lambda i: (0, col_blk))],
        out_specs=[pl.BlockSpec((E, nsb, N_SLOT_ROWS, SEL_TB), lambda i: (0, 0, 0, 0)),
                   pl.BlockSpec((1, t), lambda i: (0, 0)),
                   pl.BlockSpec((1, t), lambda i: (0, 0))],
        out_shape=[jax.ShapeDtypeStruct((E, nsb, N_SLOT_ROWS, SEL_TB), F32),
                   jax.ShapeDtypeStruct((1, t), F32),
                   jax.ShapeDtypeStruct((1, t), F32)],
        scratch_shapes=[pltpu.VMEM((ntb, 3, E, SEL_TB), F32),
                        pltpu.VMEM((ntb, E + 8, SEL_TB), F32),
                        pltpu.VMEM((E + 8, 128), I32),
                        pltpu.SMEM((E + 8, 128), I32),
                        pltpu.SemaphoreType.DMA(())],
        compiler_params=_params(("arbitrary",)),
        name="select_t%d" % t,
    )(aff_t)


DMA_UNROLL = 16


def _ffn_kernel(rc, nch, idx_ref, dst_ref, h2_hbm, wcol_ref, wg_ref, wu_ref, wd_ref, z_hbm,
                xbuf, ybuf, wgb, wub, wdb, gsem, ssem):
    c = pl.program_id(1)
    step = pl.program_id(0) * nch + c
    nsteps = E * nch
    slot = step % 2

    def start_gather(s):
        buf = xbuf.at[s % 2]
        sem = gsem.at[s % 2]

        def issue(jb, carry):
            for u in range(DMA_UNROLL):
                j = jb * DMA_UNROLL + u
                src = pl.multiple_of(idx_ref[s * rc + j], ROW_TILES)
                pltpu.make_async_copy(h2_hbm.at[pl.ds(src, ROW_TILES)],
                                      buf.at[pl.ds(j * ROW_TILES, ROW_TILES)], sem).start(priority=u % 2)
            return carry

        lax.fori_loop(0, rc // DMA_UNROLL, issue, 0)

    @pl.when(step == 0)
    def _():
        start_gather(step)

    @pl.when(step + 1 < nsteps)
    def _():
        start_gather(step + 1)

    @pl.when(c == 0)
    def _():
        wgb[...] = wg_ref[0, 0].astype(BF16)
        wub[...] = wu_ref[0, 0].astype(BF16)
        wdb[...] = wd_ref[0, 0].astype(BF16)

    pltpu.make_async_copy(h2_hbm.at[pl.ds(0, rc * ROW_TILES)], xbuf.at[slot], gsem.at[slot]).wait()
    xb = _load_token_tiles(xbuf.at[slot], rc).astype(BF16)
    hid = _silu(_mm(xb, wgb[...])) * _mm(xb, wub[...])
    y = _mm(hid.astype(BF16), wdb[...]) * wcol_ref[...]

    @pl.when(step > 0)
    def _():
        pltpu.make_async_copy(ybuf, z_hbm.at[pl.ds(0, rc * ROW_TILES)], ssem).wait()

    _store_token_tiles(ybuf, y)

    def scatter(jb, carry):
        for u in range(DMA_UNROLL):
            j = jb * DMA_UNROLL + u
            row = pl.multiple_of(dst_ref[step * rc + j], ROW_TILES)
            pltpu.make_async_copy(ybuf.at[pl.ds(j * ROW_TILES, ROW_TILES)],
                                  z_hbm.at[pl.ds(row, ROW_TILES)], ssem).start(priority=u % 2)
        return carry

    lax.fori_loop(0, rc // DMA_UNROLL, scatter, 0)

    @pl.when(step == nsteps - 1)
    def _():
        pltpu.make_async_copy(ybuf, z_hbm.at[pl.ds(0, rc * ROW_TILES)], ssem).wait()


def _expert_ffn(idx, dst, h2, wcol, w_gate, w_up, w_down, layer, rc, nch):
    zrows = E * nch * rc
    wspec = pl.BlockSpec((1, 1, D, D), lambda e, c, *_: (layer, e, 0, 0))
    return pl.pallas_call(
        functools.partial(_ffn_kernel, rc, nch),
        grid_spec=pltpu.PrefetchScalarGridSpec(
            num_scalar_prefetch=2,
            grid=(E, nch),
            in_specs=[pl.BlockSpec(memory_space=pl.ANY),
                      pl.BlockSpec((rc, 1), lambda e, c, *_: (e * nch + c, 0)),
                      wspec, wspec, wspec],
            out_specs=pl.BlockSpec(memory_space=pl.ANY),
            scratch_shapes=[pltpu.VMEM((2, rc * ROW_TILES, 128), F32),
                            pltpu.VMEM((rc * ROW_TILES, 128), F32),
                            pltpu.VMEM((D, D), BF16), pltpu.VMEM((D, D), BF16),
                            pltpu.VMEM((D, D), BF16),
                            pltpu.SemaphoreType.DMA((2,)), pltpu.SemaphoreType.DMA(())]),
        out_shape=jax.ShapeDtypeStruct((zrows * ROW_TILES, 128), F32),
        compiler_params=_params(("arbitrary", "arbitrary")),
        name="expert_ffn",
    )(idx, dst, h2, wcol, w_gate, w_up, w_down)


CMB_TB = 256
CMB_RC = 512
CMB_BUFS = 3


def _comb_kernel(t1, n2, zrows, final, first_ref, nom_ref, x1_ref, lo_ref, hi_ref, mod_ref, fw_ref, z_hbm,
                 *refs):
    outs, (zbuf, sem) = refs[:-2], refs[-2:]
    i = pl.program_id(0)
    nsteps = pl.num_programs(0)
    r = _mod_row(i * CMB_TB, t1, n2)
    total = first_ref[nsteps]

    def chunk_copy(g):
        b = pl.multiple_of(jnp.minimum(nom_ref[g], zrows - CMB_RC) * ROW_TILES, 8 * ROW_TILES)
        slot = g % CMB_BUFS
        return pltpu.make_async_copy(z_hbm.at[pl.ds(b, CMB_RC * ROW_TILES)], zbuf.at[slot], sem.at[slot])

    @pl.when(i == 0)
    def _():
        for g in range(CMB_BUFS - 1):
            @pl.when(g < total)
            def _():
                chunk_copy(g).start()

    eye = _iota((CMB_TB, CMB_TB), 0) == _iota((CMB_TB, CMB_TB), 1)
    lo = jnp.sum(jnp.where(eye, lo_ref[...], 0.0), axis=1, keepdims=True)
    hi = jnp.sum(jnp.where(eye, hi_ref[...], 0.0), axis=1, keepdims=True)
    col = _iota((1, CMB_RC), 1)

    def chunk(g, acc):
        @pl.when(g + CMB_BUFS - 1 < total)
        def _():
            chunk_copy(g + CMB_BUFS - 1).start()

        chunk_copy(g).wait()
        nominal = nom_ref[g]
        rows = (col + jnp.minimum(nominal, zrows - CMB_RC)).astype(F32)
        lo_c = jnp.maximum(lo, nominal.astype(F32))
        s = jnp.where(rows >= lo_c, jnp.where(rows < hi, 1.0, 0.0), 0.0).astype(BF16)
        return acc + _mmb(s, _load_token_tiles(zbuf.at[g % CMB_BUFS], CMB_RC))

    moe = lax.fori_loop(first_ref[i], first_ref[i + 1], chunk, jnp.zeros((CMB_TB, D), F32))

    x2 = x1_ref[...] + mod_ref[pl.ds(r, 1), 5 * D:6 * D] * moe
    if final:
        y = x2 * lax.rsqrt(jnp.mean(x2 * x2, axis=-1, keepdims=True) + EPS) * fw_ref[...]

        @pl.when(i * CMB_TB < t1)
        def _():
            outs[0][...] = y

        @pl.when(i * CMB_TB >= t1)
        def _():
            outs[1][...] = y
    else:
        outs[0][...] = x2


def _combine(lo, hi, x1, mod, fw, z, t1, n2, final):
    t_all = x1.shape[0]
    zrows = z.shape[0] // ROW_TILES
    nsteps = t_all // CMB_TB
    base = (lo[0, ::CMB_TB].astype(I32) // 8) * 8
    end = hi[0, CMB_TB - 1::CMB_TB].astype(I32)
    nchunk = jnp.maximum((end - base + CMB_RC - 1) // CMB_RC, 1)
    first = jnp.concatenate([jnp.zeros((1,), I32), jnp.cumsum(nchunk)])
    max_chunks = zrows // CMB_RC + 2 * nsteps
    g = jnp.arange(max_chunks, dtype=I32)
    step_of = jnp.minimum(jnp.sum(first[None, 1:] <= g[:, None], axis=1), nsteps - 1)
    nominal = base[step_of] + (g - first[step_of]) * CMB_RC
    nb1 = t1 // CMB_TB
    if final:
        out_specs = [pl.BlockSpec((CMB_TB, D), lambda i, *_: (jnp.minimum(i, nb1 - 1), 0)),
                     pl.BlockSpec((CMB_TB, D), lambda i, *_: (jnp.maximum(i - nb1, 0), 0))]
        out_shape = [jax.ShapeDtypeStruct((t1, D), F32), jax.ShapeDtypeStruct((t_all - t1, D), F32)]
    else:
        out_specs = pl.BlockSpec((CMB_TB, D), lambda i, *_: (i, 0))
        out_shape = jax.ShapeDtypeStruct((t_all, D), F32)
    return pl.pallas_call(
        functools.partial(_comb_kernel, t1, n2, zrows, final),
        grid_spec=pltpu.PrefetchScalarGridSpec(
            num_scalar_prefetch=2,
            grid=(t_all // CMB_TB,),
            in_specs=[pl.BlockSpec((CMB_TB, D), lambda i, *_: (i, 0)),
                      pl.BlockSpec((1, CMB_TB), lambda i, *_: (0, i)),
                      pl.BlockSpec((1, CMB_TB), lambda i, *_: (0, i)),
                      pl.BlockSpec((8, NMOD * D), lambda i, *_: (0, 0)),
                      pl.BlockSpec((1, D), lambda i, *_: (0, 0)),
                      pl.BlockSpec(memory_space=pl.ANY)],
            out_specs=out_specs,
            scratch_shapes=[pltpu.VMEM((CMB_BUFS, CMB_RC * ROW_TILES, 128), F32),
                            pltpu.SemaphoreType.DMA((CMB_BUFS,))]),
        out_shape=out_shape,
        compiler_params=_params(("arbitrary",)),
        name="combine",
    )(first, nominal, x1, lo, hi, mod, fw, z)


def _grid_pos_embed(n, d):
    rows = n // GRID_W
    r = np.repeat(np.arange(rows, dtype=np.float64), GRID_W)
    col = np.tile(np.arange(GRID_W, dtype=np.float64), rows)
    quarter = d // 4
    omega = np.power(POS_BASE, -np.arange(quarter, dtype=np.float64) / quarter)
    ra = r[:, None] * omega
    ca = col[:, None] * omega
    return jnp.asarray(np.concatenate([np.sin(ra), np.cos(ra), np.sin(ca), np.cos(ca)], axis=-1), F32)


def _decode_slots(slots):
    e, nsb, rows, tb = slots.shape
    nsb -= SPILL_BLOCKS
    slots = slots[:, :nsb].transpose(0, 2, 1, 3).reshape(e, rows, nsb * tb)
    idx = (slots[:, 0] * 128.0 + slots[:, 1]).astype(I32)
    w = slots[:, 2] + slots[:, 3] + slots[:, 4]
    rank = (slots[:, 5] * 128.0 + slots[:, 6]).astype(I32)
    return idx, w, rank


def kernel(x_prompt, x_sample, state_delta, c, c_ctx, w_ada, b_ada, norm1_w, norm2_w, w_in, dn_conv_w,
           dn_a_log, dn_dt_bias, dn_norm_w, sc_conv_w, w_out, w_router, w_gate, w_up, w_down,
           final_norm_w):
    b1, n1, _ = x_prompt.shape
    b2, n2, _ = x_sample.shape
    depth = w_ada.shape[0]
    t1, t2 = b1 * n1, b2 * n2
    t_all = t1 + t2
    cap1 = max(1, 2 * t1 // E)
    cap2 = max(1, 2 * t2 // E)
    tm = 512
    rc = 512 if (cap1 % 512 == 0 and cap2 % 512 == 0) else 128
    assert t1 % n2 == 0 and t1 % tm == 0 and n2 % tm == 0
    assert n1 % (DN_POS * CH) == 0 and n2 % (DN_POS * CH) == 0
    assert t1 % SEL_TB == 0 and t2 % SEL_TB == 0 and t1 % t2 == 0
    assert cap1 % rc == 0 and cap2 % rc == 0 and 2 * t_all >= CMB_RC
    nch = (cap1 + cap2) // rc
    width = H * DH

    cond8 = jnp.zeros((8, D), F32).at[0].set(c_ctx).at[1:1 + b2].set(c)
    mod = _ada(cond8, w_ada, b_ada)

    x = None
    s_zero = jnp.zeros((b1, 2, H, DH, DH), F32)
    tabs1 = _dft_tables(n1)
    tabs2 = _dft_tables(n2)
    ctx_states = None

    for l in range(depth):
        wl = w_in[l]
        wm = jnp.concatenate([wl[:, 0:4 * width], wl[:, 4 * width + 4 * H:]], axis=1).astype(BF16)
        wg = jnp.pad(wl[:, 4 * width:4 * width + 4 * H], ((0, 0), (0, 128 - 4 * H))).astype(BF16)
        al = jnp.pad(dn_a_log[l].reshape(1, 2 * H), ((0, 0), (2 * H, 128 - 4 * H)))
        dt = jnp.pad(dn_dt_bias[l].reshape(1, 2 * H), ((0, 0), (2 * H, 128 - 4 * H)))
        xs = (x,) if l else (x_prompt.reshape(t1, D), x_sample.reshape(t2, D), _grid_pos_embed(n2, D))
        res = _in_proj(xs, mod[l], norm1_w[l].reshape(1, D), wm, wg, al, dt, t1, n2, tm)
        if not l:
            x, res = res[0], res[1:]
        p, g, gt = res

        nwd = dn_norm_w[l].reshape(1, DH)
        dn1, ctx_states = _deltanet(p, g, gt, dn_conv_w[l], nwd, s_zero, ctx_states, n1, 0)
        dn2, _ = _deltanet(p, g, gt, dn_conv_w[l], nwd, state_delta[:, l], None, n2, t1 // n2)
        mx1 = _mix2(p, sc_conv_w[l], tabs1[1], tabs1[2], tabs1[0], b1, n1, 0, MIX_GROUP)
        mx2 = _mix2(p, sc_conv_w[l], tabs2[1], tabs2[2], tabs2[0], b2, n2, t1 // n2, 1)

        wr = jnp.pad(w_router[l], ((0, 0), (0, 128 - E)))
        x1, h2, aff_t = _out_proj(dn1, dn2, mx1, mx2, x, mod[l], norm2_w[l].reshape(1, D),
                                  w_out[l].astype(BF16), wr, t1, n2, tm)

        slots1, lo1, hi1 = _select(aff_t, t1, cap1, 0)
        slots2, lo2, hi2 = _select(aff_t, t2, cap2, t1 // t2)
        idx1, wsel1, rank1 = _decode_slots(slots1)
        idx2, wsel2, rank2 = _decode_slots(slots2)
        idx = jnp.concatenate([idx1, idx2 + t1], axis=1).reshape(-1) * ROW_TILES
        dst = jnp.concatenate([rank1, rank2 + 2 * t1], axis=1).reshape(-1) * ROW_TILES
        wcol = jnp.concatenate([wsel1, wsel2], axis=1).reshape(-1, 1)
        z = _expert_ffn(idx, dst, h2, wcol, w_gate, w_up, w_down, l, rc, nch)

        lo = jnp.concatenate([lo1, lo2 + 2.0 * t1], axis=1)
        hi = jnp.concatenate([hi1, hi2 + 2.0 * t1], axis=1)
        x = _combine(lo, hi, x1, mod[l], final_norm_w.reshape(1, D), z, t1, n2, l == depth - 1)

    y_prompt, y_sample = x
    return y_prompt.reshape(b1, n1, D), y_sample.reshape(b2, n2, D), ctx_states
```

```python
import functools
import math

import jax
import jax.numpy as jnp
import numpy as np
from jax import lax
from jax.experimental import pallas as pl
from jax.experimental.pallas import tpu as pltpu

F32 = jnp.float32
BF16 = jnp.bfloat16
I32 = jnp.int32
HI = lax.Precision.HIGHEST

D = 1024
H = 4
DH = 128
CH = 64
E = 16
NMOD = 6
EPS = 1e-6
GRID_W = 64
POS_BASE = 10000.0
P_COLS = 3072
VMEM_LIMIT = 56 * 1024 * 1024

NN = (((1,), (0,)), ((), ()))
NT = (((1,), (1,)), ((), ()))
TN = (((0,), (0,)), ((), ()))


def _mm(a, b, dims=NN, prec=None):
    return lax.dot_general(a, b, dims, precision=prec, preferred_element_type=F32)


def _mmb(a, b, dims=NN):
    return lax.dot_general(a.astype(BF16), b.astype(BF16), dims, preferred_element_type=F32)


def _mm3(a, b, dims=NN):
    a_hi = a.astype(BF16)
    a_lo = (a - a_hi.astype(F32)).astype(BF16)
    b_hi = b.astype(BF16)
    b_lo = (b - b_hi.astype(F32)).astype(BF16)
    return _mm(a_hi, b_hi, dims) + _mm(a_hi, b_lo, dims) + _mm(a_lo, b_hi, dims)


def _mm_inv(a, b):
    return _mmb(a, b)


def _silu(x):
    return x / (1.0 + jnp.exp(-x))


def _sigmoid(x):
    return 1.0 / (1.0 + jnp.exp(-x))


def _softplus(x):
    return jnp.maximum(x, 0.0) + jnp.log1p(jnp.exp(-jnp.abs(x)))


def _iota(shape, dim):
    return lax.broadcasted_iota(I32, shape, dim)


ROW_TILES = D // 128


def _load_token_tiles(ref, n):
    return jnp.concatenate([ref[pl.ds(s, n, stride=ROW_TILES), :] for s in range(ROW_TILES)], axis=1)


def _store_token_tiles(ref, x):
    n = x.shape[0]
    for s in range(ROW_TILES):
        ref[pl.ds(s, n, stride=ROW_TILES), :] = x[:, s * 128:(s + 1) * 128]


def _mod_row(t0, t1, n2):
    return jnp.where(t0 < t1, 0, 1 + jnp.maximum(t0 - t1, 0) // n2)


def _params(sem):
    return pltpu.CompilerParams(dimension_semantics=sem, vmem_limit_bytes=VMEM_LIMIT)


def _ada_kernel(c_ref, w_ref, b_ref, o_ref):
    s = _silu(c_ref[...])
    o_ref[0] = _mm3(s, w_ref[0]) + b_ref[0]


def _ada(cond8, w_ada, b_ada):
    depth = w_ada.shape[0]
    tn = 1536
    return pl.pallas_call(
        _ada_kernel,
        grid=(depth, NMOD * D // tn),
        in_specs=[pl.BlockSpec((8, D), lambda l, j: (0, 0)),
                  pl.BlockSpec((1, D, tn), lambda l, j: (l, 0, j)),
                  pl.BlockSpec((1, 1, tn), lambda l, j: (l, 0, j))],
        out_specs=pl.BlockSpec((1, 8, tn), lambda l, j: (l, 0, j)),
        out_shape=jax.ShapeDtypeStruct((depth, 8, NMOD * D), F32),
        compiler_params=_params(("arbitrary", "arbitrary")),
        name="ada",
    )(cond8, w_ada, b_ada.reshape(depth, 1, NMOD * D))


IN_SUB = 2


def _in_core(tm, r, x, mod_ref, nw_ref, wm_ref, wg_ref, al_ref, dt_ref, p_ref, g_ref, gt_ref):
    sh = mod_ref[pl.ds(r, 1), 0:D]
    sc = mod_ref[pl.ds(r, 1), D:2 * D]
    sub = tm // IN_SUB
    nsplit = 4
    wc = P_COLS // nsplit
    lane = _iota((1, 128), 1)
    hbs = []
    for k in range(IN_SUB):
        xk = x[k * sub:(k + 1) * sub]
        y = xk * lax.rsqrt(jnp.mean(xk * xk, axis=-1, keepdims=True) + EPS) * nw_ref[...]
        hbs.append((y * (1.0 + sc) + sh).astype(BF16))
    for k in range(IN_SUB):
        rs = slice(k * sub, (k + 1) * sub)
        for j in range(nsplit):
            p_ref[rs, j * wc:(j + 1) * wc] = _mm(hbs[k], wm_ref[:, j * wc:(j + 1) * wc])
    for k in range(IN_SUB):
        raw = _mm(hbs[k], wg_ref[...])
        g = -jnp.exp(al_ref[...]) * _softplus(raw + dt_ref[...])
        act = jnp.where(lane < 2 * H, _sigmoid(raw), g)
        g_ref[k * sub:(k + 1) * sub, :] = act[:, 0:4 * H]
        act_t = act.T
        for j in range(sub // CH):
            gt_ref[k * (sub // CH) + j] = act_t[0:4 * H, j * CH:(j + 1) * CH]


def _in_kernel(t1, n2, tm, x_ref, *refs):
    r = _mod_row(pl.program_id(0) * tm, t1, n2)
    _in_core(tm, r, x_ref[...], *refs)


def _in_first_kernel(t1, n2, tm, xp_ref, xs_ref, pos_ref, mod_ref, nw_ref, wm_ref, wg_ref, al_ref,
                     dt_ref, x_ref, p_ref, g_ref, gt_ref):
    t0 = pl.program_id(0) * tm
    x = jnp.where(t0 < t1, xp_ref[...], xs_ref[...] + pos_ref[...])
    x_ref[...] = x
    _in_core(tm, _mod_row(t0, t1, n2), x, mod_ref, nw_ref, wm_ref, wg_ref, al_ref, dt_ref,
             p_ref, g_ref, gt_ref)


def _in_proj(xs, mod, nw, wm, wg, al, dt, t1, n2, tm):
    first = len(xs) == 3
    t_all = t1 + xs[1].shape[0] if first else xs[0].shape[0]
    nt1 = t1 // tm
    wspecs = [pl.BlockSpec((8, NMOD * D), lambda i: (0, 0)),
              pl.BlockSpec((1, D), lambda i: (0, 0)),
              pl.BlockSpec((D, P_COLS), lambda i: (0, 0)),
              pl.BlockSpec((D, 128), lambda i: (0, 0)),
              pl.BlockSpec((1, 128), lambda i: (0, 0)),
              pl.BlockSpec((1, 128), lambda i: (0, 0))]
    out_specs = [pl.BlockSpec((tm, P_COLS), lambda i: (i, 0)),
                 pl.BlockSpec((tm, 4 * H), lambda i: (i, 0)),
                 pl.BlockSpec((tm // CH, 4 * H, CH), lambda i: (i, 0, 0))]
    out_shape = [jax.ShapeDtypeStruct((t_all, P_COLS), F32),
                 jax.ShapeDtypeStruct((t_all, 4 * H), F32),
                 jax.ShapeDtypeStruct((t_all // CH, 4 * H, CH), F32)]
    if first:
        kern = _in_first_kernel
        xspecs = [pl.BlockSpec((tm, D), lambda i: (jnp.minimum(i, nt1 - 1), 0)),
                  pl.BlockSpec((tm, D), lambda i: (jnp.maximum(i - nt1, 0), 0)),
                  pl.BlockSpec((tm, D), lambda i: (jnp.maximum(i - nt1, 0) % (n2 // tm), 0))]
        out_specs = [pl.BlockSpec((tm, D), lambda i: (i, 0))] + out_specs
        out_shape = [jax.ShapeDtypeStruct((t_all, D), F32)] + out_shape
    else:
        kern = _in_kernel
        xspecs = [pl.BlockSpec((tm, D), lambda i: (i, 0))]
    return pl.pallas_call(
        functools.partial(kern, t1, n2, tm),
        grid=(t_all // tm,),
        in_specs=xspecs + wspecs,
        out_specs=out_specs,
        out_shape=out_shape,
        compiler_params=_params(("arbitrary",)),
        name="in_proj_first" if first else "in_proj",
    )(*xs, mod, nw, wm, wg, al, dt)


def _shift_conv(x, w_ref, c0, c1, period=None):
    n = x.shape[0]
    row = _iota((n, 1), 0) & ((period or n) - 1)
    prev = jnp.where(row == 0, 0.0, pltpu.roll(x, 1, 0))
    nxt = jnp.where(row == (period or n) - 1, 0.0, pltpu.roll(x, n - 1, 0))
    return prev * w_ref[0:1, c0:c1] + x * w_ref[1:2, c0:c1] + nxt * w_ref[2:3, c0:c1]


DN_POS = 4


def _dn_kernel(n, nprev, has_s0, q_ref, k_ref, v_ref, z_ref, g_ref, gt_ref, cw_ref, nw_ref, *refs):
    refs = list(refs)
    s0_ref = refs.pop(0) if has_s0 else None
    prev_ref = refs.pop(0) if nprev else None
    o_ref, sfin_ref, qs, ks, vs, gcc_s, gct_s, s_s, oacc = refs
    nc = n // CH
    width = H * DH

    for hh in range(H):
        c0, c1 = hh * DH, (hh + 1) * DH
        q = _silu(_shift_conv(q_ref[:, c0:c1], cw_ref, c0, c1))
        q = q * lax.rsqrt(jnp.sum(q * q, axis=-1, keepdims=True) + 1e-6)
        qs[:, c0:c1] = q * (DH ** -0.5)
        k = _silu(_shift_conv(k_ref[:, c0:c1], cw_ref, width + c0, width + c1))
        ks[:, c0:c1] = k * lax.rsqrt(jnp.sum(k * k, axis=-1, keepdims=True) + 1e-6)
        vs[:, c0:c1] = _silu(_shift_conv(v_ref[:, c0:c1], cw_ref, 2 * width + c0, 2 * width + c1))

    ii = _iota((CH, CH), 0)
    jj = _iota((CH, CH), 1)
    x = gt_ref[...].reshape(nc * 4 * H, CH)
    cf = _mm(x, (ii <= jj).astype(F32), prec=HI)
    cb = _mm(x, (ii >= jj).astype(F32), prec=HI)
    rsel = (_iota((nc * 4 * H, 1), 0) & (4 * H - 1)) < 3 * H
    gct_s[...] = jnp.where(rsel, cf, cb).reshape(nc, 4 * H, CH)
    i2 = _iota((2 * CH, 2 * CH), 0)
    j2 = _iota((2 * CH, 2 * CH), 1)
    same = (i2 >= CH) == (j2 >= CH)
    lf = (same & (i2 >= j2)).astype(F32)
    lb = (same & (i2 <= j2)).astype(F32)
    lsel = _iota((1, 4 * H), 1) < 3 * H
    for m in range(n // (2 * CH)):
        y = g_ref[m * 2 * CH:(m + 1) * 2 * CH, :]
        gcc_s[m * 2 * CH:(m + 1) * 2 * CH, :] = jnp.where(lsel, _mm(lf, y, prec=HI), _mm(lb, y, prec=HI))

    for hh in range(H):
        for d in range(2):
            s_s[hh * 2 + d] = s0_ref[0, d, hh] if has_s0 else jnp.zeros((DH, DH), F32)

    eye = (ii == jj).astype(F32)
    blk_same = [(ii >> s) == (jj >> s) for s in range(1, 7)]
    merge_mask = [blk_same[lvl] & ~blk_same[lvl - 1] for lvl in range(1, len(blk_same))]

    chains = [(hh, d) for hh in range(H) for d in range(2)]

    def chunk_step(cc, carry):
        st = []
        for p in range(DN_POS):
            for hh, d in chains:
                c = cc * DN_POS + p
                cidx = c if d == 0 else nc - 1 - c
                r0 = pl.multiple_of(cidx * CH, CH)
                c0, c1 = hh * DH, (hh + 1) * DH
                gcol = 2 * H + H * d + hh
                bcol = H * d + hh
                incl = (ii >= jj) if d == 0 else (ii <= jj)
                strict = (ii > jj) if d == 0 else (ii < jj)
                qc = qs[pl.ds(r0, CH), c0:c1]
                kc = ks[pl.ds(r0, CH), c0:c1]
                vc = vs[pl.ds(r0, CH), c0:c1]
                beta = g_ref[pl.ds(r0, CH), bcol:bcol + 1]
                gc = gcc_s[pl.ds(r0, CH), gcol:gcol + 1]
                gr = gct_s[cidx, gcol:gcol + 1, :]
                tot = gr[:, CH - 1:CH] if d == 0 else gr[:, 0:1]
                decay = jnp.where(incl, jnp.exp(jnp.where(incl, gc - gr, 0.0)), 0.0)
                kb = kc * beta
                egc = jnp.exp(gc)
                st.append(dict(r0=r0, c0=c0, c1=c1, d=d, strict=strict, kc=kc, kb=kb, decay=decay,
                               rhs=jnp.concatenate([vc * beta, kb * egc], axis=1),
                               qd=qc * egc, qc=qc, kd=kc * jnp.exp(tot - gc), gl=jnp.exp(tot)))
        for x in st:
            x["lmat"] = jnp.where(x["strict"], _mmb(x["kb"], x["kc"], NT) * x["decay"], 0.0)
            x["qk"] = _mmb(x["qc"], x["kc"], NT) * x["decay"]
            x["tinv"] = eye - jnp.where(blk_same[0], x["lmat"], 0.0)
        for mask in merge_mask:
            for x in st:
                x["ot"] = _mm_inv(jnp.where(mask, x["lmat"], 0.0), x["tinv"])
            for x in st:
                x["tinv"] = x["tinv"] - _mm_inv(x["tinv"], x["ot"])
        for x in st:
            x["sol"] = _mm_inv(x["tinv"], x["rhs"])
        state = [s_s[i] for i in range(len(chains))]
        for p in range(DN_POS):
            units = st[p * len(chains):(p + 1) * len(chains)]
            for i, x in enumerate(units):
                x["sb"] = state[i].astype(BF16)
                x["v_new"] = x["sol"][:, 0:DH] - _mmb(x["sol"][:, DH:2 * DH], x["sb"])
            for i, x in enumerate(units):
                o = _mmb(x["qd"], x["sb"]) + _mmb(x["qk"], x["v_new"])
                state[i] = state[i] * x["gl"] + _mmb(x["kd"], x["v_new"], TN)
                oacc[x["d"], pl.ds(x["r0"], CH), x["c0"]:x["c1"]] = o
        for i in range(len(chains)):
            s_s[i] = state[i]
        return carry

    lax.fori_loop(0, nc // DN_POS, chunk_step, 0)

    for hh in range(H):
        c0, c1 = hh * DH, (hh + 1) * DH
        o = oacc[0, :, c0:c1] + oacc[1, :, c0:c1]
        o = o * lax.rsqrt(jnp.mean(o * o, axis=-1, keepdims=True) + EPS)
        o_ref[:, c0:c1] = o * nw_ref[...] * _silu(z_ref[:, c0:c1])
        for d in range(2):
            sfin_ref[0, nprev, d, hh] = s_s[hh * 2 + d]
    for layer in range(nprev):
        sfin_ref[0, layer] = prev_ref[0, layer]


def _deltanet(p, g, gt, conv_w, norm_w, s0, prev, bsz, n, row_blk0):
    nc = n // CH
    width = H * DH
    nprev = 0 if prev is None else prev.shape[1]
    state_spec = lambda k: pl.BlockSpec((1, k, 2, H, DH, DH), lambda b: (b, 0, 0, 0, 0, 0))
    opt_args, opt_specs = [], []
    if s0 is not None:
        opt_args.append(s0)
        opt_specs.append(pl.BlockSpec((1, 2, H, DH, DH), lambda b: (b, 0, 0, 0, 0)))
    if prev is not None:
        opt_args.append(prev)
        opt_specs.append(state_spec(nprev))
    return pl.pallas_call(
        functools.partial(_dn_kernel, n, nprev, s0 is not None),
        grid=(bsz,),
        in_specs=[pl.BlockSpec((n, width), lambda b: (row_blk0 + b, 0)),
                  pl.BlockSpec((n, width), lambda b: (row_blk0 + b, 1)),
                  pl.BlockSpec((n, width), lambda b: (row_blk0 + b, 2)),
                  pl.BlockSpec((n, width), lambda b: (row_blk0 + b, 3)),
                  pl.BlockSpec((n, 4 * H), lambda b: (row_blk0 + b, 0)),
                  pl.BlockSpec((nc, 4 * H, CH), lambda b: (row_blk0 + b, 0, 0)),
                  pl.BlockSpec((3, 3 * width), lambda b: (0, 0)),
                  pl.BlockSpec((1, DH), lambda b: (0, 0))] + opt_specs,
        out_specs=[pl.BlockSpec((n, width), lambda b: (b, 0)), state_spec(nprev + 1)],
        out_shape=[jax.ShapeDtypeStruct((bsz * n, width), F32),
                   jax.ShapeDtypeStruct((bsz, nprev + 1, 2, H, DH, DH), F32)],
        scratch_shapes=[pltpu.VMEM((n, width), F32), pltpu.VMEM((n, width), F32),
                        pltpu.VMEM((n, width), F32), pltpu.VMEM((n, 4 * H), F32),
                        pltpu.VMEM((nc, 4 * H, CH), F32), pltpu.VMEM((2 * H, DH, DH), F32),
                        pltpu.VMEM((2, n, width), F32)],
        compiler_params=_params(("arbitrary",)),
        name="deltanet_n%d" % n,
    )(p, p, p, p, g, gt, conv_w, norm_w, *opt_args)


MIX_GROUP = 4


def _mix2_kernel(n, sb_ref, sc_ref, su_ref, fu_ref, cw_ref, bdc_ref, bds_ref, cs_ref, o_ref):
    rows, w = sb_ref.shape
    cu = sc_ref[...] * su_ref[...]
    o_ref[:, 0:w] = sb_ref[...] * _shift_conv(cu, cw_ref, 0, w, n)
    fu = fu_ref[...]
    a = _mm3(fu, bdc_ref[...])
    b = _mm3(fu, bds_ref[...])
    for q in range(rows // n):
        rs = slice(q * n, (q + 1) * n)
        o_ref[rs, w:2 * w] = _mmb(cs_ref[...], jnp.concatenate([a[rs], b[rs]], axis=0))


def _mix2(p, conv_w, bdc, bds, cs, bsz, n, row_blk0, group):
    w = 256
    rows = group * n
    blk0 = row_blk0 // group
    assert bsz % group == 0 and row_blk0 % group == 0
    return pl.pallas_call(
        functools.partial(_mix2_kernel, n),
        grid=(bsz // group,),
        in_specs=[pl.BlockSpec((rows, w), lambda b: (blk0 + b, 8)),
                  pl.BlockSpec((rows, w), lambda b: (blk0 + b, 9)),
                  pl.BlockSpec((rows, w), lambda b: (blk0 + b, 10)),
                  pl.BlockSpec((rows, w), lambda b: (blk0 + b, 11)),
                  pl.BlockSpec((3, w), lambda b: (0, 0)),
                  pl.BlockSpec((w, w), lambda b: (0, 0)),
                  pl.BlockSpec((w, w), lambda b: (0, 0)),
                  pl.BlockSpec((n, 2 * n), lambda b: (0, 0))],
        out_specs=pl.BlockSpec((rows, 2 * w), lambda b: (b, 0)),
        out_shape=jax.ShapeDtypeStruct((bsz * n, 2 * w), F32),
        compiler_params=_params(("arbitrary",)),
        name="conv_fourier_n%d" % n,
    )(p, p, p, p, conv_w, bdc, bds, cs)


def _dft_tables(n):
    gw = 64
    k = np.arange(n, dtype=np.int64)
    ang = ((k[:, None] * k[None, :]) % n) * (2.0 * math.pi / n)
    scale = 1.0 / math.sqrt(n * gw)
    cs = np.concatenate([np.cos(ang), -np.sin(ang)], axis=1) * scale
    c = np.arange(256, dtype=np.int64)
    angc = (((c[:, None] % gw) * (c[None, :] % gw)) % gw) * (2.0 * math.pi / gw)
    same = (c[:, None] // gw) == (c[None, :] // gw)
    bdc = np.where(same, np.cos(angc), 0.0)
    bds = np.where(same, np.sin(angc), 0.0)
    return jnp.asarray(cs, F32).astype(BF16), jnp.asarray(bdc, F32), jnp.asarray(bds, F32)


OUT_SUB = 2


def _out_kernel(t1, n2, tm, dn1_ref, dn2_ref, mx1_ref, mx2_ref, x_ref, mod_ref, nw_ref, wo_ref, wr_ref,
                x1_ref, h2_ref, aff_ref):
    t0 = pl.program_id(0) * tm
    r = _mod_row(t0, t1, n2)
    half = D // 2
    sub = tm // OUT_SUB
    mixes = []
    for k in range(OUT_SUB):
        rs = slice(k * sub, (k + 1) * sub)
        dn = jnp.where(t0 < t1, dn1_ref[rs, :], dn2_ref[rs, :])
        mx = jnp.where(t0 < t1, mx1_ref[rs, :], mx2_ref[rs, :])
        mixes.append(_mmb(dn, wo_ref[0:half, :]) + _mmb(mx, wo_ref[half:D, :]))
    h2s = []
    for k in range(OUT_SUB):
        rs = slice(k * sub, (k + 1) * sub)
        x1 = x_ref[rs, :] + mod_ref[pl.ds(r, 1), 2 * D:3 * D] * mixes[k]
        x1_ref[rs, :] = x1
        y = x1 * lax.rsqrt(jnp.mean(x1 * x1, axis=-1, keepdims=True) + EPS) * nw_ref[...]
        h2 = y * (1.0 + mod_ref[pl.ds(r, 1), 4 * D:5 * D]) + mod_ref[pl.ds(r, 1), 3 * D:4 * D]
        _store_token_tiles(h2_ref.at[pl.ds(k * sub * ROW_TILES, sub * ROW_TILES)], h2)
        h2s.append(h2)
    logits = [_mm3(h2, wr_ref[...]) for h2 in h2s]
    for k in range(OUT_SUB):
        lt = logits[k].T[0:E, :]
        ex = jnp.exp(lt - jnp.max(lt, axis=0, keepdims=True))
        aff_ref[:, k * sub:(k + 1) * sub] = ex / jnp.sum(ex, axis=0, keepdims=True)


def _out_proj(dn1, dn2, mx1, mx2, x, mod, nw, wo, wr, t1, n2, tm):
    t_all = x.shape[0]
    nt1 = t1 // tm
    spec1 = pl.BlockSpec((tm, D // 2), lambda i: (jnp.minimum(i, nt1 - 1), 0))
    spec2 = pl.BlockSpec((tm, D // 2), lambda i: (jnp.maximum(i - nt1, 0), 0))
    return pl.pallas_call(
        functools.partial(_out_kernel, t1, n2, tm),
        grid=(t_all // tm,),
        in_specs=[spec1, spec2, spec1, spec2,
                  pl.BlockSpec((tm, D), lambda i: (i, 0)),
                  pl.BlockSpec((8, NMOD * D), lambda i: (0, 0)),
                  pl.BlockSpec((1, D), lambda i: (0, 0)),
                  pl.BlockSpec((D, D), lambda i: (0, 0)),
                  pl.BlockSpec((D, 128), lambda i: (0, 0))],
        out_specs=[pl.BlockSpec((tm, D), lambda i: (i, 0)),
                   pl.BlockSpec((tm * ROW_TILES, 128), lambda i: (i, 0)),
                   pl.BlockSpec((E, tm), lambda i: (0, i))],
        out_shape=[jax.ShapeDtypeStruct((t_all, D), F32),
                   jax.ShapeDtypeStruct((t_all * ROW_TILES, 128), F32),
                   jax.ShapeDtypeStruct((E, t_all), F32)],
        compiler_params=_params(("arbitrary",)),
        name="out_proj",
    )(dn1, dn2, mx1, mx2, x, mod, nw, wo, wr)


SEL_TB = 128
N_SLOT_ROWS = 16
SPILL_BLOCKS = 2


def _sel_kernel(t, cap, aff_ref, slots_ref, lo_ref, hi_ref, a3, c3, starts_v, starts_s, sem):
    ntb = t // SEL_TB
    aff = aff_ref[...]

    def search(i, thr):
        cand = thr | jnp.left_shift(jnp.int32(1), 30 - i)
        cnt = jnp.sum((aff >= pltpu.bitcast(cand, F32)).astype(F32), axis=1, keepdims=True)
        return jnp.where(cnt >= cap, cand, thr)

    thr = lax.fori_loop(0, 31, search, jnp.zeros((E, 1), I32))
    gt = (aff >= pltpu.bitcast(thr + 1, F32)).astype(F32)
    eq = (aff >= pltpu.bitcast(thr, F32)).astype(F32) - gt
    need = cap - jnp.sum(gt, axis=1, keepdims=True)

    ui = _iota((SEL_TB, SEL_TB), 0)
    uj = _iota((SEL_TB, SEL_TB), 1)
    upper = (ui < uj).astype(BF16)
    blk_lane = _iota((1, 128), 1)

    def excl_cumsum(rows, dst, r0, r1):
        carry = jnp.zeros((rows.shape[0], 1), F32)
        starts = jnp.zeros((rows.shape[0], 128), F32)
        for j in range(ntb):
            blk = rows[:, j * SEL_TB:(j + 1) * SEL_TB]
            dst[j, r0:r1, :] = _mm(blk.astype(BF16), upper) + carry
            starts = jnp.where(blk_lane == j, carry, starts)
            carry = carry + jnp.sum(blk, axis=1, keepdims=True)
        return starts

    excl_cumsum(eq, c3, 0, E)
    rank_eq = jnp.concatenate([c3[j, 0:E, :] for j in range(ntb)], axis=1)
    sel = jnp.maximum(gt, jnp.where(rank_eq < need, eq, 0.0))
    n_tok = jnp.sum(sel, axis=0, keepdims=True)
    starts = excl_cumsum(jnp.concatenate([sel, jnp.broadcast_to(n_tok, (8, t))], axis=0), c3, 0, E + 8)
    starts_v[...] = starts.astype(I32)
    to_smem = pltpu.make_async_copy(starts_v, starts_s, sem)
    to_smem.start()
    ei = _iota((E, E), 0)
    ej = _iota((E, E), 1)
    below = _mm((ej < ei).astype(BF16), sel.astype(BF16))
    for j in range(ntb):
        sl = slice(j * SEL_TB, (j + 1) * SEL_TB)
        off = c3[j, E:E + 1, :]
        lo_ref[:, sl] = off
        hi_ref[:, sl] = off + n_tok[:, sl]
        rank = off + below[:, sl]
        a3[j, 0] = sel[:, sl]
        a3[j, 1] = aff[:, sl]
        a3[j, 2] = rank
    slots_ref[...] = jnp.zeros(slots_ref.shape, F32)
    to_smem.wait()

    win_iota = _iota((2 * SEL_TB, SEL_TB), 0)
    tok_iota = _iota((1, SEL_TB), 1)

    def per_expert(e, carry):
        def per_block(j, carry2):
            wb = starts_s[e, j] >> 7
            chosen = a3[j, 0, pl.ds(e, 1), :]
            pos = c3[j, pl.ds(e, 1), :].astype(I32) - wb * SEL_TB
            pos = jnp.where(chosen > 0.0, pos, -1)
            w = a3[j, 1, pl.ds(e, 1), :]
            rank = a3[j, 2, pl.ds(e, 1), :].astype(I32)
            tok = tok_iota + j * SEL_TB
            w_hi = w.astype(BF16).astype(F32)
            w_mid = (w - w_hi).astype(BF16).astype(F32)
            w_lo = w - w_hi - w_mid
            vals = jnp.concatenate(
                [(tok >> 7).astype(F32), (tok & 127).astype(F32), w_hi, w_mid, w_lo,
                 (rank >> 7).astype(F32), (rank & 127).astype(F32),
                 jnp.zeros((N_SLOT_ROWS - 7, SEL_TB), F32)], axis=0)
            onehot = jnp.where(win_iota == pos, 1.0, 0.0)
            placed = _mmb(vals, onehot, NT)
            slots_ref[e, wb] += placed[:, 0:SEL_TB]
            slots_ref[e, wb + 1] += placed[:, SEL_TB:2 * SEL_TB]
            return carry2
        return lax.fori_loop(0, ntb, per_block, carry, unroll=16)

    lax.fori_loop(0, E, per_expert, 0)


def _select(aff_t, t, cap, col_blk):
    ntb = t // SEL_TB
    nsb = cap // SEL_TB + SPILL_BLOCKS
    return pl.pallas_call(
        functools.partial(_sel_kernel, t, cap),
        grid=(1,),
        in_specs=[pl.BlockSpec((E, t), lambda i: (0, col_blk))],
        out_specs=[pl.BlockSpec((E, nsb, N_SLOT_ROWS, SEL_TB), lambda i: (0, 0, 0, 0)),
                   pl.BlockSpec((1, t), lambda i: (0, 0)),
                   pl.BlockSpec((1, t), lambda i: (0, 0))],
        out_shape=[jax.ShapeDtypeStruct((E, nsb, N_SLOT_ROWS, SEL_TB), F32),
                   jax.ShapeDtypeStruct((1, t), F32),
                   jax.ShapeDtypeStruct((1, t), F32)],
        scratch_shapes=[pltpu.VMEM((ntb, 3, E, SEL_TB), F32),
                        pltpu.VMEM((ntb, E + 8, SEL_TB), F32),
                        pltpu.VMEM((E + 8, 128), I32),
                        pltpu.SMEM((E + 8, 128), I32),
                        pltpu.SemaphoreType.DMA(())],
        compiler_params=_params(("arbitrary",)),
        name="select_t%d" % t,
    )(aff_t)


DMA_UNROLL = 16


def _ffn_kernel(rc, nch, idx_ref, dst_ref, h2_hbm, wcol_ref, wg_ref, wu_ref, wd_ref, z_hbm,
                xbuf, ybuf, wgb, wub, wdb, gsem, ssem):
    c = pl.program_id(1)
    step = pl.program_id(0) * nch + c
    nsteps = E * nch
    slot = step % 2

    def start_gather(s):
        buf = xbuf.at[s % 2]
        sem = gsem.at[s % 2]

        def issue(jb, carry):
            for u in range(DMA_UNROLL):
                j = jb * DMA_UNROLL + u
                src = pl.multiple_of(idx_ref[s * rc + j], ROW_TILES)
                pltpu.make_async_copy(h2_hbm.at[pl.ds(src, ROW_TILES)],
                                      buf.at[pl.ds(j * ROW_TILES, ROW_TILES)], sem).start(priority=u % 2)
            return carry

        lax.fori_loop(0, rc // DMA_UNROLL, issue, 0)

    @pl.when(step == 0)
    def _():
        start_gather(step)

    @pl.when(step + 1 < nsteps)
    def _():
        start_gather(step + 1)

    @pl.when(c == 0)
    def _():
        wgb[...] = wg_ref[0, 0].astype(BF16)
        wub[...] = wu_ref[0, 0].astype(BF16)
        wdb[...] = wd_ref[0, 0].astype(BF16)

    pltpu.make_async_copy(h2_hbm.at[pl.ds(0, rc * ROW_TILES)], xbuf.at[slot], gsem.at[slot]).wait()
    xb = _load_token_tiles(xbuf.at[slot], rc).astype(BF16)
    hid = _silu(_mm(xb, wgb[...])) * _mm(xb, wub[...])
    y = _mm(hid.astype(BF16), wdb[...]) * wcol_ref[...]

    @pl.when(step > 0)
    def _():
        pltpu.make_async_copy(ybuf, z_hbm.at[pl.ds(0, rc * ROW_TILES)], ssem).wait()

    _store_token_tiles(ybuf, y)

    def scatter(jb, carry):
        for u in range(DMA_UNROLL):
            j = jb * DMA_UNROLL + u
            row = pl.multiple_of(dst_ref[step * rc + j], ROW_TILES)
            pltpu.make_async_copy(ybuf.at[pl.ds(j * ROW_TILES, ROW_TILES)],
                                  z_hbm.at[pl.ds(row, ROW_TILES)], ssem).start(priority=u % 2)
        return carry

    lax.fori_loop(0, rc // DMA_UNROLL, scatter, 0)

    @pl.when(step == nsteps - 1)
    def _():
        pltpu.make_async_copy(ybuf, z_hbm.at[pl.ds(0, rc * ROW_TILES)], ssem).wait()


def _expert_ffn(idx, dst, h2, wcol, w_gate, w_up, w_down, layer, rc, nch):
    zrows = E * nch * rc
    wspec = pl.BlockSpec((1, 1, D, D), lambda e, c, *_: (layer, e, 0, 0))
    return pl.pallas_call(
        functools.partial(_ffn_kernel, rc, nch),
        grid_spec=pltpu.PrefetchScalarGridSpec(
            num_scalar_prefetch=2,
            grid=(E, nch),
            in_specs=[pl.BlockSpec(memory_space=pl.ANY),
                      pl.BlockSpec((rc, 1), lambda e, c, *_: (e * nch + c, 0)),
                      wspec, wspec, wspec],
            out_specs=pl.BlockSpec(memory_space=pl.ANY),
            scratch_shapes=[pltpu.VMEM((2, rc * ROW_TILES, 128), F32),
                            pltpu.VMEM((rc * ROW_TILES, 128), F32),
                            pltpu.VMEM((D, D), BF16), pltpu.VMEM((D, D), BF16),
                            pltpu.VMEM((D, D), BF16),
                            pltpu.SemaphoreType.DMA((2,)), pltpu.SemaphoreType.DMA(())]),
        out_shape=jax.ShapeDtypeStruct((zrows * ROW_TILES, 128), F32),
        compiler_params=_params(("arbitrary", "arbitrary")),
        name="expert_ffn",
    )(idx, dst, h2, wcol, w_gate, w_up, w_down)


CMB_TB = 256
CMB_RC = 512
CMB_BUFS = 3


def _comb_kernel(t1, n2, zrows, final, first_ref, nom_ref, x1_ref, lo_ref, hi_ref, mod_ref, fw_ref, z_hbm,
                 *refs):
    outs, (zbuf, sem) = refs[:-2], refs[-2:]
    i = pl.program_id(0)
    nsteps = pl.num_programs(0)
    r = _mod_row(i * CMB_TB, t1, n2)
    total = first_ref[nsteps]

    def chunk_copy(g):
        b = pl.multiple_of(jnp.minimum(nom_ref[g], zrows - CMB_RC) * ROW_TILES, 8 * ROW_TILES)
        slot = g % CMB_BUFS
        return pltpu.make_async_copy(z_hbm.at[pl.ds(b, CMB_RC * ROW_TILES)], zbuf.at[slot], sem.at[slot])

    @pl.when(i == 0)
    def _():
        for g in range(CMB_BUFS - 1):
            @pl.when(g < total)
            def _():
                chunk_copy(g).start()

    eye = _iota((CMB_TB, CMB_TB), 0) == _iota((CMB_TB, CMB_TB), 1)
    lo = jnp.sum(jnp.where(eye, lo_ref[...], 0.0), axis=1, keepdims=True)
    hi = jnp.sum(jnp.where(eye, hi_ref[...], 0.0), axis=1, keepdims=True)
    col = _iota((1, CMB_RC), 1)

    def chunk(g, acc):
        @pl.when(g + CMB_BUFS - 1 < total)
        def _():
            chunk_copy(g + CMB_BUFS - 1).start()

        chunk_copy(g).wait()
        nominal = nom_ref[g]
        rows = (col + jnp.minimum(nominal, zrows - CMB_RC)).astype(F32)
        lo_c = jnp.maximum(lo, nominal.astype(F32))
        s = jnp.where(rows >= lo_c, jnp.where(rows < hi, 1.0, 0.0), 0.0).astype(BF16)
        return acc + _mmb(s, _load_token_tiles(zbuf.at[g % CMB_BUFS], CMB_RC))

    moe = lax.fori_loop(first_ref[i], first_ref[i + 1], chunk, jnp.zeros((CMB_TB, D), F32))

    x2 = x1_ref[...] + mod_ref[pl.ds(r, 1), 5 * D:6 * D] * moe
    if final:
        y = x2 * lax.rsqrt(jnp.mean(x2 * x2, axis=-1, keepdims=True) + EPS) * fw_ref[...]

        @pl.when(i * CMB_TB < t1)
        def _():
            outs[0][...] = y

        @pl.when(i * CMB_TB >= t1)
        def _():
            outs[1][...] = y
    else:
        outs[0][...] = x2


def _combine(lo, hi, x1, mod, fw, z, t1, n2, final):
    t_all = x1.shape[0]
    zrows = z.shape[0] // ROW_TILES
    nsteps = t_all // CMB_TB
    base = (lo[0, ::CMB_TB].astype(I32) // 8) * 8
    end = hi[0, CMB_TB - 1::CMB_TB].astype(I32)
    nchunk = jnp.maximum((end - base + CMB_RC - 1) // CMB_RC, 1)
    first = jnp.concatenate([jnp.zeros((1,), I32), jnp.cumsum(nchunk)])
    max_chunks = zrows // CMB_RC + 2 * nsteps
    g = jnp.arange(max_chunks, dtype=I32)
    step_of = jnp.minimum(jnp.sum(first[None, 1:] <= g[:, None], axis=1), nsteps - 1)
    nominal = base[step_of] + (g - first[step_of]) * CMB_RC
    nb1 = t1 // CMB_TB
    if final:
        out_specs = [pl.BlockSpec((CMB_TB, D), lambda i, *_: (jnp.minimum(i, nb1 - 1), 0)),
                     pl.BlockSpec((CMB_TB, D), lambda i, *_: (jnp.maximum(i - nb1, 0), 0))]
        out_shape = [jax.ShapeDtypeStruct((t1, D), F32), jax.ShapeDtypeStruct((t_all - t1, D), F32)]
    else:
        out_specs = pl.BlockSpec((CMB_TB, D), lambda i, *_: (i, 0))
        out_shape = jax.ShapeDtypeStruct((t_all, D), F32)
    return pl.pallas_call(
        functools.partial(_comb_kernel, t1, n2, zrows, final),
        grid_spec=pltpu.PrefetchScalarGridSpec(
            num_scalar_prefetch=2,
            grid=(t_all // CMB_TB,),
            in_specs=[pl.BlockSpec((CMB_TB, D), lambda i, *_: (i, 0)),
                      pl.BlockSpec((1, CMB_TB), lambda i, *_: (0, i)),
                      pl.BlockSpec((1, CMB_TB), lambda i, *_: (0, i)),
                      pl.BlockSpec((8, NMOD * D), lambda i, *_: (0, 0)),
                      pl.BlockSpec((1, D), lambda i, *_: (0, 0)),
                      pl.BlockSpec(memory_space=pl.ANY)],
            out_specs=out_specs,
            scratch_shapes=[pltpu.VMEM((CMB_BUFS, CMB_RC * ROW_TILES, 128), F32),
                            pltpu.SemaphoreType.DMA((CMB_BUFS,))]),
        out_shape=out_shape,
        compiler_params=_params(("arbitrary",)),
        name="combine",
    )(first, nominal, x1, lo, hi, mod, fw, z)


def _grid_pos_embed(n, d):
    rows = n // GRID_W
    r = np.repeat(np.arange(rows, dtype=np.float64), GRID_W)
    col = np.tile(np.arange(GRID_W, dtype=np.float64), rows)
    quarter = d // 4
    omega = np.power(POS_BASE, -np.arange(quarter, dtype=np.float64) / quarter)
    ra = r[:, None] * omega
    ca = col[:, None] * omega
    return jnp.asarray(np.concatenate([np.sin(ra), np.cos(ra), np.sin(ca), np.cos(ca)], axis=-1), F32)


def _decode_slots(slots):
    e, nsb, rows, tb = slots.shape
    nsb -= SPILL_BLOCKS
    slots = slots[:, :nsb].transpose(0, 2, 1, 3).reshape(e, rows, nsb * tb)
    idx = (slots[:, 0] * 128.0 + slots[:, 1]).astype(I32)
    w = slots[:, 2] + slots[:, 3] + slots[:, 4]
    rank = (slots[:, 5] * 128.0 + slots[:, 6]).astype(I32)
    return idx, w, rank


def kernel(x_prompt, x_sample, state_delta, c, c_ctx, w_ada, b_ada, norm1_w, norm2_w, w_in, dn_conv_w,
           dn_a_log, dn_dt_bias, dn_norm_w, sc_conv_w, w_out, w_router, w_gate, w_up, w_down,
           final_norm_w):
    b1, n1, _ = x_prompt.shape
    b2, n2, _ = x_sample.shape
    depth = w_ada.shape[0]
    t1, t2 = b1 * n1, b2 * n2
    t_all = t1 + t2
    cap1 = max(1, 2 * t1 // E)
    cap2 = max(1, 2 * t2 // E)
    tm = 512
    rc = 512 if (cap1 % 512 == 0 and cap2 % 512 == 0) else 128
    assert t1 % n2 == 0 and t1 % tm == 0 and n2 % tm == 0
    assert n1 % (DN_POS * CH) == 0 and n2 % (DN_POS * CH) == 0
    assert t1 % SEL_TB == 0 and t2 % SEL_TB == 0 and t1 % t2 == 0
    assert cap1 % rc == 0 and cap2 % rc == 0 and 2 * t_all >= CMB_RC
    nch = (cap1 + cap2) // rc
    width = H * DH

    cond8 = jnp.zeros((8, D), F32).at[0].set(c_ctx).at[1:1 + b2].set(c)
    mod = _ada(cond8, w_ada, b_ada)

    x = None
    tabs1 = _dft_tables(n1)
    tabs2 = _dft_tables(n2)
    ctx_states = None

    for l in range(depth):
        wl = w_in[l]
        wm = jnp.concatenate([wl[:, 0:4 * width], wl[:, 4 * width + 4 * H:]], axis=1).astype(BF16)
        wg = jnp.pad(wl[:, 4 * width:4 * width + 4 * H], ((0, 0), (0, 128 - 4 * H))).astype(BF16)
        al = jnp.pad(dn_a_log[l].reshape(1, 2 * H), ((0, 0), (2 * H, 128 - 4 * H)))
        dt = jnp.pad(dn_dt_bias[l].reshape(1, 2 * H), ((0, 0), (2 * H, 128 - 4 * H)))
        xs = (x,) if l else (x_prompt.reshape(t1, D), x_sample.reshape(t2, D), _grid_pos_embed(n2, D))
        res = _in_proj(xs, mod[l], norm1_w[l].reshape(1, D), wm, wg, al, dt, t1, n2, tm)
        if not l:
            x, res = res[0], res[1:]
        p, g, gt = res

        nwd = dn_norm_w[l].reshape(1, DH)
        dn1, ctx_states = _deltanet(p, g, gt, dn_conv_w[l], nwd, None, ctx_states, b1, n1, 0)
        dn2, _ = _deltanet(p, g, gt, dn_conv_w[l], nwd, state_delta[:, l], None, b2, n2, t1 // n2)
        mx1 = _mix2(p, sc_conv_w[l], tabs1[1], tabs1[2], tabs1[0], b1, n1, 0, MIX_GROUP)
        mx2 = _mix2(p, sc_conv_w[l], tabs2[1], tabs2[2], tabs2[0], b2, n2, t1 // n2, 1)

        wr = jnp.pad(w_router[l], ((0, 0), (0, 128 - E)))
        x1, h2, aff_t = _out_proj(dn1, dn2, mx1, mx2, x, mod[l], norm2_w[l].reshape(1, D),
                                  w_out[l].astype(BF16), wr, t1, n2, tm)

        slots1, lo1, hi1 = _select(aff_t, t1, cap1, 0)
        slots2, lo2, hi2 = _select(aff_t, t2, cap2, t1 // t2)
        idx1, wsel1, rank1 = _decode_slots(slots1)
        idx2, wsel2, rank2 = _decode_slots(slots2)
        idx = jnp.concatenate([idx1, idx2 + t1], axis=1).reshape(-1) * ROW_TILES
        dst = jnp.concatenate([rank1, rank2 + 2 * t1], axis=1).reshape(-1) * ROW_TILES
        wcol = jnp.concatenate([wsel1, wsel2], axis=1).reshape(-1, 1)
        z = _expert_ffn(idx, dst, h2, wcol, w_gate, w_up, w_down, l, rc, nch)

        lo = jnp.concatenate([lo1, lo2 + 2.0 * t1], axis=1)
        hi = jnp.concatenate([hi1, hi2 + 2.0 * t1], axis=1)
        x = _combine(lo, hi, x1, mod[l], final_norm_w.reshape(1, D), z, t1, n2, l == depth - 1)

    y_prompt, y_sample = x
    return y_prompt.reshape(b1, n1, D), y_sample.reshape(b2, n2, D), ctx_states
```

```python
import functools
import math

import jax
import jax.numpy as jnp
import numpy as np
from jax import lax
from jax.experimental import pallas as pl
from jax.experimental.pallas import tpu as pltpu

F32 = jnp.float32
BF16 = jnp.bfloat16
I32 = jnp.int32
HI = lax.Precision.HIGHEST

D = 1024
H = 4
DH = 128
CH = 64
E = 16
NMOD = 6
EPS = 1e-6
GRID_W = 64
POS_BASE = 10000.0
P_COLS = 3072
VMEM_LIMIT = 56 * 1024 * 1024

NN = (((1,), (0,)), ((), ()))
NT = (((1,), (1,)), ((), ()))
TN = (((0,), (0,)), ((), ()))


def _mm(a, b, dims=NN, prec=None):
    return lax.dot_general(a, b, dims, precision=prec, preferred_element_type=F32)


def _mmb(a, b, dims=NN):
    return lax.dot_general(a.astype(BF16), b.astype(BF16), dims, preferred_element_type=F32)


def _mm3(a, b, dims=NN):
    a_hi = a.astype(BF16)
    a_lo = (a - a_hi.astype(F32)).astype(BF16)
    b_hi = b.astype(BF16)
    b_lo = (b - b_hi.astype(F32)).astype(BF16)
    return _mm(a_hi, b_hi, dims) + _mm(a_hi, b_lo, dims) + _mm(a_lo, b_hi, dims)


def _mm_inv(a, b):
    return _mmb(a, b)


def _silu(x):
    return x / (1.0 + jnp.exp(-x))


def _sigmoid(x):
    return 1.0 / (1.0 + jnp.exp(-x))


def _softplus(x):
    return jnp.maximum(x, 0.0) + jnp.log1p(jnp.exp(-jnp.abs(x)))


def _iota(shape, dim):
    return lax.broadcasted_iota(I32, shape, dim)


ROW_TILES = D // 128


def _load_token_tiles(ref, n):
    return jnp.concatenate([ref[pl.ds(s, n, stride=ROW_TILES), :] for s in range(ROW_TILES)], axis=1)


def _store_token_tiles(ref, x):
    n = x.shape[0]
    for s in range(ROW_TILES):
        ref[pl.ds(s, n, stride=ROW_TILES), :] = x[:, s * 128:(s + 1) * 128]


def _mod_row(t0, t1, n2):
    return jnp.where(t0 < t1, 0, 1 + jnp.maximum(t0 - t1, 0) // n2)


def _params(sem):
    return pltpu.CompilerParams(dimension_semantics=sem, vmem_limit_bytes=VMEM_LIMIT)


def _ada_kernel(c_ref, w_ref, b_ref, o_ref):
    s = _silu(c_ref[...])
    o_ref[0] = _mm3(s, w_ref[0]) + b_ref[0]


def _ada(cond8, w_ada, b_ada):
    depth = w_ada.shape[0]
    tn = 1536
    return pl.pallas_call(
        _ada_kernel,
        grid=(depth, NMOD * D // tn),
        in_specs=[pl.BlockSpec((8, D), lambda l, j: (0, 0)),
                  pl.BlockSpec((1, D, tn), lambda l, j: (l, 0, j)),
                  pl.BlockSpec((1, 1, tn), lambda l, j: (l, 0, j))],
        out_specs=pl.BlockSpec((1, 8, tn), lambda l, j: (l, 0, j)),
        out_shape=jax.ShapeDtypeStruct((depth, 8, NMOD * D), F32),
        compiler_params=_params(("arbitrary", "arbitrary")),
        name="ada",
    )(cond8, w_ada, b_ada.reshape(depth, 1, NMOD * D))


IN_SUB = 2


def _in_core(tm, r, x, mod_ref, nw_ref, wm_ref, wg_ref, al_ref, dt_ref, p_ref, g_ref, gt_ref):
    sh = mod_ref[pl.ds(r, 1), 0:D]
    sc = mod_ref[pl.ds(r, 1), D:2 * D]
    sub = tm // IN_SUB
    nsplit = 4
    wc = P_COLS // nsplit
    lane = _iota((1, 128), 1)
    hbs = []
    for k in range(IN_SUB):
        xk = x[k * sub:(k + 1) * sub]
        y = xk * lax.rsqrt(jnp.mean(xk * xk, axis=-1, keepdims=True) + EPS) * nw_ref[...]
        hbs.append((y * (1.0 + sc) + sh).astype(BF16))
    for k in range(IN_SUB):
        rs = slice(k * sub, (k + 1) * sub)
        for j in range(nsplit):
            p_ref[rs, j * wc:(j + 1) * wc] = _mm(hbs[k], wm_ref[:, j * wc:(j + 1) * wc])
    for k in range(IN_SUB):
        raw = _mm(hbs[k], wg_ref[...])
        g = -jnp.exp(al_ref[...]) * _softplus(raw + dt_ref[...])
        act = jnp.where(lane < 2 * H, _sigmoid(raw), g)
        g_ref[k * sub:(k + 1) * sub, :] = act[:, 0:4 * H]
        act_t = act.T
        for j in range(sub // CH):
            gt_ref[k * (sub // CH) + j] = act_t[0:4 * H, j * CH:(j + 1) * CH]


def _in_kernel(t1, n2, tm, x_ref, *refs):
    r = _mod_row(pl.program_id(0) * tm, t1, n2)
    _in_core(tm, r, x_ref[...], *refs)


def _in_first_kernel(t1, n2, tm, xp_ref, xs_ref, pos_ref, mod_ref, nw_ref, wm_ref, wg_ref, al_ref,
                     dt_ref, x_ref, p_ref, g_ref, gt_ref):
    t0 = pl.program_id(0) * tm
    x = jnp.where(t0 < t1, xp_ref[...], xs_ref[...] + pos_ref[...])
    x_ref[...] = x
    _in_core(tm, _mod_row(t0, t1, n2), x, mod_ref, nw_ref, wm_ref, wg_ref, al_ref, dt_ref,
             p_ref, g_ref, gt_ref)


def _in_proj(xs, mod, nw, wm, wg, al, dt, t1, n2, tm):
    first = len(xs) == 3
    t_all = t1 + xs[1].shape[0] if first else xs[0].shape[0]
    nt1 = t1 // tm
    wspecs = [pl.BlockSpec((8, NMOD * D), lambda i: (0, 0)),
              pl.BlockSpec((1, D), lambda i: (0, 0)),
              pl.BlockSpec((D, P_COLS), lambda i: (0, 0)),
              pl.BlockSpec((D, 128), lambda i: (0, 0)),
              pl.BlockSpec((1, 128), lambda i: (0, 0)),
              pl.BlockSpec((1, 128), lambda i: (0, 0))]
    out_specs = [pl.BlockSpec((tm, P_COLS), lambda i: (i, 0)),
                 pl.BlockSpec((tm, 4 * H), lambda i: (i, 0)),
                 pl.BlockSpec((tm // CH, 4 * H, CH), lambda i: (i, 0, 0))]
    out_shape = [jax.ShapeDtypeStruct((t_all, P_COLS), F32),
                 jax.ShapeDtypeStruct((t_all, 4 * H), F32),
                 jax.ShapeDtypeStruct((t_all // CH, 4 * H, CH), F32)]
    if first:
        kern = _in_first_kernel
        xspecs = [pl.BlockSpec((tm, D), lambda i: (jnp.minimum(i, nt1 - 1), 0)),
                  pl.BlockSpec((tm, D), lambda i: (jnp.maximum(i - nt1, 0), 0)),
                  pl.BlockSpec((tm, D), lambda i: (jnp.maximum(i - nt1, 0) % (n2 // tm), 0))]
        out_specs = [pl.BlockSpec((tm, D), lambda i: (i, 0))] + out_specs
        out_shape = [jax.ShapeDtypeStruct((t_all, D), F32)] + out_shape
    else:
        kern = _in_kernel
        xspecs = [pl.BlockSpec((tm, D), lambda i: (i, 0))]
    return pl.pallas_call(
        functools.partial(kern, t1, n2, tm),
        grid=(t_all // tm,),
        in_specs=xspecs + wspecs,
        out_specs=out_specs,
        out_shape=out_shape,
        compiler_params=_params(("arbitrary",)),
        name="in_proj_first" if first else "in_proj",
    )(*xs, mod, nw, wm, wg, al, dt)


def _shift_conv(x, w_ref, c0, c1, period=None):
    n = x.shape[0]
    row = _iota((n, 1), 0) & ((period or n) - 1)
    prev = jnp.where(row == 0, 0.0, pltpu.roll(x, 1, 0))
    nxt = jnp.where(row == (period or n) - 1, 0.0, pltpu.roll(x, n - 1, 0))
    return prev * w_ref[0:1, c0:c1] + x * w_ref[1:2, c0:c1] + nxt * w_ref[2:3, c0:c1]


DN_POS = 4


def _dn_kernel(n, nprev, has_s0, q_ref, k_ref, v_ref, z_ref, g_ref, gt_ref, cw_ref, nw_ref, *refs):
    refs = list(refs)
    s0_ref = refs.pop(0) if has_s0 else None
    prev_ref = refs.pop(0) if nprev else None
    o_ref, sfin_ref, qs, ks, vs, gcc_s, gct_s, s_s, oacc = refs
    nc = n // CH
    width = H * DH

    for hh in range(H):
        c0, c1 = hh * DH, (hh + 1) * DH
        q = _silu(_shift_conv(q_ref[:, c0:c1], cw_ref, c0, c1))
        q = q * lax.rsqrt(jnp.sum(q * q, axis=-1, keepdims=True) + 1e-6)
        qs[:, c0:c1] = q * (DH ** -0.5)
        k = _silu(_shift_conv(k_ref[:, c0:c1], cw_ref, width + c0, width + c1))
        ks[:, c0:c1] = k * lax.rsqrt(jnp.sum(k * k, axis=-1, keepdims=True) + 1e-6)
        vs[:, c0:c1] = _silu(_shift_conv(v_ref[:, c0:c1], cw_ref, 2 * width + c0, 2 * width + c1))

    ii = _iota((CH, CH), 0)
    jj = _iota((CH, CH), 1)
    x = gt_ref[...].reshape(nc * 4 * H, CH)
    cf = _mm(x, (ii <= jj).astype(F32), prec=HI)
    cb = _mm(x, (ii >= jj).astype(F32), prec=HI)
    rsel = (_iota((nc * 4 * H, 1), 0) & (4 * H - 1)) < 3 * H
    gct_s[...] = jnp.where(rsel, cf, cb).reshape(nc, 4 * H, CH)
    i2 = _iota((2 * CH, 2 * CH), 0)
    j2 = _iota((2 * CH, 2 * CH), 1)
    same = (i2 >= CH) == (j2 >= CH)
    lf = (same & (i2 >= j2)).astype(F32)
    lb = (same & (i2 <= j2)).astype(F32)
    lsel = _iota((1, 4 * H), 1) < 3 * H
    for m in range(n // (2 * CH)):
        y = g_ref[m * 2 * CH:(m + 1) * 2 * CH, :]
        gcc_s[m * 2 * CH:(m + 1) * 2 * CH, :] = jnp.where(lsel, _mm(lf, y, prec=HI), _mm(lb, y, prec=HI))

    for hh in range(H):
        for d in range(2):
            s_s[hh * 2 + d] = s0_ref[0, d, hh] if has_s0 else jnp.zeros((DH, DH), F32)

    eye = (ii == jj).astype(F32)
    blk_same = [(ii >> s) == (jj >> s) for s in range(1, 7)]
    merge_mask = [blk_same[lvl] & ~blk_same[lvl - 1] for lvl in range(1, len(blk_same))]

    chains = [(hh, d) for hh in range(H) for d in range(2)]

    def chunk_step(cc, carry):
        st = []
        for p in range(DN_POS):
            for hh, d in chains:
                c = cc * DN_POS + p
                cidx = c if d == 0 else nc - 1 - c
                r0 = pl.multiple_of(cidx * CH, CH)
                c0, c1 = hh * DH, (hh + 1) * DH
                gcol = 2 * H + H * d + hh
                bcol = H * d + hh
                incl = (ii >= jj) if d == 0 else (ii <= jj)
                strict = (ii > jj) if d == 0 else (ii < jj)
                qc = qs[pl.ds(r0, CH), c0:c1]
                kc = ks[pl.ds(r0, CH), c0:c1]
                vc = vs[pl.ds(r0, CH), c0:c1]
                beta = g_ref[pl.ds(r0, CH), bcol:bcol + 1]
                gc = gcc_s[pl.ds(r0, CH), gcol:gcol + 1]
                gr = gct_s[cidx, gcol:gcol + 1, :]
                tot = gr[:, CH - 1:CH] if d == 0 else gr[:, 0:1]
                decay = jnp.where(incl, jnp.exp(jnp.where(incl, gc - gr, 0.0)), 0.0)
                kb = kc * beta
                egc = jnp.exp(gc)
                st.append(dict(r0=r0, c0=c0, c1=c1, d=d, strict=strict, kc=kc, kb=kb, decay=decay,
                               rhs=jnp.concatenate([vc * beta, kb * egc], axis=1),
                               qd=qc * egc, qc=qc, kd=kc * jnp.exp(tot - gc), gl=jnp.exp(tot)))
        for x in st:
            x["lmat"] = jnp.where(x["strict"], _mmb(x["kb"], x["kc"], NT) * x["decay"], 0.0)
            x["qk"] = _mmb(x["qc"], x["kc"], NT) * x["decay"]
            x["tinv"] = eye - jnp.where(blk_same[0], x["lmat"], 0.0)
        for mask in merge_mask:
            for x in st:
                x["ot"] = _mm_inv(jnp.where(mask, x["lmat"], 0.0), x["tinv"])
            for x in st:
                x["tinv"] = x["tinv"] - _mm_inv(x["tinv"], x["ot"])
        for x in st:
            x["sol"] = _mm_inv(x["tinv"], x["rhs"])
        state = [s_s[i] for i in range(len(chains))]
        for p in range(DN_POS):
            units = st[p * len(chains):(p + 1) * len(chains)]
            for i, x in enumerate(units):
                x["sb"] = state[i].astype(BF16)
                x["v_new"] = x["sol"][:, 0:DH] - _mmb(x["sol"][:, DH:2 * DH], x["sb"])
            for i, x in enumerate(units):
                o = _mmb(x["qd"], x["sb"]) + _mmb(x["qk"], x["v_new"])
                state[i] = state[i] * x["gl"] + _mmb(x["kd"], x["v_new"], TN)
                oacc[x["d"], pl.ds(x["r0"], CH), x["c0"]:x["c1"]] = o
        for i in range(len(chains)):
            s_s[i] = state[i]
        return carry

    lax.fori_loop(0, nc // DN_POS, chunk_step, 0)

    for hh in range(H):
        c0, c1 = hh * DH, (hh + 1) * DH
        o = oacc[0, :, c0:c1] + oacc[1, :, c0:c1]
        o = o * lax.rsqrt(jnp.mean(o * o, axis=-1, keepdims=True) + EPS)
        o_ref[:, c0:c1] = (o * nw_ref[...] * _silu(z_ref[:, c0:c1])).astype(o_ref.dtype)
        for d in range(2):
            sfin_ref[0, nprev, d, hh] = s_s[hh * 2 + d]
    for layer in range(nprev):
        sfin_ref[0, layer] = prev_ref[0, layer]


def _deltanet(p, g, gt, conv_w, norm_w, s0, prev, bsz, n, row_blk0):
    nc = n // CH
    width = H * DH
    nprev = 0 if prev is None else prev.shape[1]
    state_spec = lambda k: pl.BlockSpec((1, k, 2, H, DH, DH), lambda b: (b, 0, 0, 0, 0, 0))
    opt_args, opt_specs = [], []
    if s0 is not None:
        opt_args.append(s0)
        opt_specs.append(pl.BlockSpec((1, 2, H, DH, DH), lambda b: (b, 0, 0, 0, 0)))
    if prev is not None:
        opt_args.append(prev)
        opt_specs.append(state_spec(nprev))
    return pl.pallas_call(
        functools.partial(_dn_kernel, n, nprev, s0 is not None),
        grid=(bsz,),
        in_specs=[pl.BlockSpec((n, width), lambda b: (row_blk0 + b, 0)),
                  pl.BlockSpec((n, width), lambda b: (row_blk0 + b, 1)),
                  pl.BlockSpec((n, width), lambda b: (row_blk0 + b, 2)),
                  pl.BlockSpec((n, width), lambda b: (row_blk0 + b, 3)),
                  pl.BlockSpec((n, 4 * H), lambda b: (row_blk0 + b, 0)),
                  pl.BlockSpec((nc, 4 * H, CH), lambda b: (row_blk0 + b, 0, 0)),
                  pl.BlockSpec((3, 3 * width), lambda b: (0, 0)),
                  pl.BlockSpec((1, DH), lambda b: (0, 0))] + opt_specs,
        out_specs=[pl.BlockSpec((n, width), lambda b: (b, 0)), state_spec(nprev + 1)],
        out_shape=[jax.ShapeDtypeStruct((bsz * n, width), BF16),
                   jax.ShapeDtypeStruct((bsz, nprev + 1, 2, H, DH, DH), F32)],
        scratch_shapes=[pltpu.VMEM((n, width), F32), pltpu.VMEM((n, width), F32),
                        pltpu.VMEM((n, width), F32), pltpu.VMEM((n, 4 * H), F32),
                        pltpu.VMEM((nc, 4 * H, CH), F32), pltpu.VMEM((2 * H, DH, DH), F32),
                        pltpu.VMEM((2, n, width), F32)],
        compiler_params=_params(("arbitrary",)),
        name="deltanet_n%d" % n,
    )(p, p, p, p, g, gt, conv_w, norm_w, *opt_args)


MIX_GROUP = 4


def _mix2_kernel(n, sb_ref, sc_ref, su_ref, fu_ref, cw_ref, bdc_ref, bds_ref, cs_ref, o_ref):
    rows, w = sb_ref.shape
    cu = sc_ref[...] * su_ref[...]
    o_ref[:, 0:w] = (sb_ref[...] * _shift_conv(cu, cw_ref, 0, w, n)).astype(o_ref.dtype)
    fu = fu_ref[...]
    a = _mm3(fu, bdc_ref[...])
    b = _mm3(fu, bds_ref[...])
    for q in range(rows // n):
        rs = slice(q * n, (q + 1) * n)
        fo = _mmb(cs_ref[...], jnp.concatenate([a[rs], b[rs]], axis=0))
        o_ref[rs, w:2 * w] = fo.astype(o_ref.dtype)


def _mix2(p, conv_w, bdc, bds, cs, bsz, n, row_blk0, group):
    w = 256
    rows = group * n
    blk0 = row_blk0 // group
    assert bsz % group == 0 and row_blk0 % group == 0
    return pl.pallas_call(
        functools.partial(_mix2_kernel, n),
        grid=(bsz // group,),
        in_specs=[pl.BlockSpec((rows, w), lambda b: (blk0 + b, 8)),
                  pl.BlockSpec((rows, w), lambda b: (blk0 + b, 9)),
                  pl.BlockSpec((rows, w), lambda b: (blk0 + b, 10)),
                  pl.BlockSpec((rows, w), lambda b: (blk0 + b, 11)),
                  pl.BlockSpec((3, w), lambda b: (0, 0)),
                  pl.BlockSpec((w, w), lambda b: (0, 0)),
                  pl.BlockSpec((w, w), lambda b: (0, 0)),
                  pl.BlockSpec((n, 2 * n), lambda b: (0, 0))],
        out_specs=pl.BlockSpec((rows, 2 * w), lambda b: (b, 0)),
        out_shape=jax.ShapeDtypeStruct((bsz * n, 2 * w), BF16),
        compiler_params=_params(("arbitrary",)),
        name="conv_fourier_n%d" % n,
    )(p, p, p, p, conv_w, bdc, bds, cs)


def _dft_tables(n):
    gw = 64
    k = np.arange(n, dtype=np.int64)
    ang = ((k[:, None] * k[None, :]) % n) * (2.0 * math.pi / n)
    scale = 1.0 / math.sqrt(n * gw)
    cs = np.concatenate([np.cos(ang), -np.sin(ang)], axis=1) * scale
    c = np.arange(256, dtype=np.int64)
    angc = (((c[:, None] % gw) * (c[None, :] % gw)) % gw) * (2.0 * math.pi / gw)
    same = (c[:, None] // gw) == (c[None, :] // gw)
    bdc = np.where(same, np.cos(angc), 0.0)
    bds = np.where(same, np.sin(angc), 0.0)
    return jnp.asarray(cs, F32).astype(BF16), jnp.asarray(bdc, F32), jnp.asarray(bds, F32)


OUT_SUB = 2


def _out_kernel(t1, n2, tm, dn1_ref, dn2_ref, mx1_ref, mx2_ref, x_ref, mod_ref, nw_ref, wo_ref, wr_ref,
                x1_ref, h2_ref, aff_ref):
    t0 = pl.program_id(0) * tm
    r = _mod_row(t0, t1, n2)
    half = D // 2
    sub = tm // OUT_SUB
    mixes = []
    for k in range(OUT_SUB):
        rs = slice(k * sub, (k + 1) * sub)
        dn = jnp.where(t0 < t1, dn1_ref[rs, :], dn2_ref[rs, :])
        mx = jnp.where(t0 < t1, mx1_ref[rs, :], mx2_ref[rs, :])
        mixes.append(_mmb(dn, wo_ref[0:half, :]) + _mmb(mx, wo_ref[half:D, :]))
    h2s = []
    for k in range(OUT_SUB):
        rs = slice(k * sub, (k + 1) * sub)
        x1 = x_ref[rs, :] + mod_ref[pl.ds(r, 1), 2 * D:3 * D] * mixes[k]
        x1_ref[rs, :] = x1
        y = x1 * lax.rsqrt(jnp.mean(x1 * x1, axis=-1, keepdims=True) + EPS) * nw_ref[...]
        h2 = y * (1.0 + mod_ref[pl.ds(r, 1), 4 * D:5 * D]) + mod_ref[pl.ds(r, 1), 3 * D:4 * D]
        _store_token_tiles(h2_ref.at[pl.ds(k * sub * ROW_TILES, sub * ROW_TILES)], h2)
        h2s.append(h2)
    logits = [_mm3(h2, wr_ref[...]) for h2 in h2s]
    for k in range(OUT_SUB):
        lt = logits[k].T[0:E, :]
        ex = jnp.exp(lt - jnp.max(lt, axis=0, keepdims=True))
        aff_ref[:, k * sub:(k + 1) * sub] = ex / jnp.sum(ex, axis=0, keepdims=True)


def _out_proj(dn1, dn2, mx1, mx2, x, mod, nw, wo, wr, t1, n2, tm):
    t_all = x.shape[0]
    nt1 = t1 // tm
    spec1 = pl.BlockSpec((tm, D // 2), lambda i: (jnp.minimum(i, nt1 - 1), 0))
    spec2 = pl.BlockSpec((tm, D // 2), lambda i: (jnp.maximum(i - nt1, 0), 0))
    return pl.pallas_call(
        functools.partial(_out_kernel, t1, n2, tm),
        grid=(t_all // tm,),
        in_specs=[spec1, spec2, spec1, spec2,
                  pl.BlockSpec((tm, D), lambda i: (i, 0)),
                  pl.BlockSpec((8, NMOD * D), lambda i: (0, 0)),
                  pl.BlockSpec((1, D), lambda i: (0, 0)),
                  pl.BlockSpec((D, D), lambda i: (0, 0)),
                  pl.BlockSpec((D, 128), lambda i: (0, 0))],
        out_specs=[pl.BlockSpec((tm, D), lambda i: (i, 0)),
                   pl.BlockSpec((tm * ROW_TILES, 128), lambda i: (i, 0)),
                   pl.BlockSpec((E, tm), lambda i: (0, i))],
        out_shape=[jax.ShapeDtypeStruct((t_all, D), F32),
                   jax.ShapeDtypeStruct((t_all * ROW_TILES, 128), F32),
                   jax.ShapeDtypeStruct((E, t_all), F32)],
        compiler_params=_params(("arbitrary",)),
        name="out_proj",
    )(dn1, dn2, mx1, mx2, x, mod, nw, wo, wr)


SEL_TB = 128
N_SLOT_ROWS = 16
SPILL_BLOCKS = 2


def _sel_kernel(t, cap, aff_ref, slots_ref, lo_ref, hi_ref, a3, c3, starts_v, starts_s, sem):
    ntb = t // SEL_TB
    aff = aff_ref[...]

    def search(i, thr):
        cand = thr | jnp.left_shift(jnp.int32(1), 30 - i)
        cnt = jnp.sum((aff >= pltpu.bitcast(cand, F32)).astype(F32), axis=1, keepdims=True)
        return jnp.where(cnt >= cap, cand, thr)

    thr = lax.fori_loop(0, 31, search, jnp.zeros((E, 1), I32))
    gt = (aff >= pltpu.bitcast(thr + 1, F32)).astype(F32)
    eq = (aff >= pltpu.bitcast(thr, F32)).astype(F32) - gt
    need = cap - jnp.sum(gt, axis=1, keepdims=True)

    ui = _iota((SEL_TB, SEL_TB), 0)
    uj = _iota((SEL_TB, SEL_TB), 1)
    upper = (ui < uj).astype(BF16)
    blk_lane = _iota((1, 128), 1)

    def excl_cumsum(rows, dst, r0, r1):
        carry = jnp.zeros((rows.shape[0], 1), F32)
        starts = jnp.zeros((rows.shape[0], 128), F32)
        for j in range(ntb):
            blk = rows[:, j * SEL_TB:(j + 1) * SEL_TB]
            dst[j, r0:r1, :] = _mm(blk.astype(BF16), upper) + carry
            starts = jnp.where(blk_lane == j, carry, starts)
            carry = carry + jnp.sum(blk, axis=1, keepdims=True)
        return starts

    excl_cumsum(eq, c3, 0, E)
    rank_eq = jnp.concatenate([c3[j, 0:E, :] for j in range(ntb)], axis=1)
    sel = jnp.maximum(gt, jnp.where(rank_eq < need, eq, 0.0))
    n_tok = jnp.sum(sel, axis=0, keepdims=True)
    starts = excl_cumsum(jnp.concatenate([sel, jnp.broadcast_to(n_tok, (8, t))], axis=0), c3, 0, E + 8)
    starts_v[...] = starts.astype(I32)
    to_smem = pltpu.make_async_copy(starts_v, starts_s, sem)
    to_smem.start()
    ei = _iota((E, E), 0)
    ej = _iota((E, E), 1)
    below = _mm((ej < ei).astype(BF16), sel.astype(BF16))
    for j in range(ntb):
        sl = slice(j * SEL_TB, (j + 1) * SEL_TB)
        off = c3[j, E:E + 1, :]
        lo_ref[:, sl] = off
        hi_ref[:, sl] = off + n_tok[:, sl]
        rank = off + below[:, sl]
        a3[j, 0] = sel[:, sl]
        a3[j, 1] = aff[:, sl]
        a3[j, 2] = rank
    slots_ref[...] = jnp.zeros(slots_ref.shape, F32)
    to_smem.wait()

    win_iota = _iota((2 * SEL_TB, SEL_TB), 0)
    tok_iota = _iota((1, SEL_TB), 1)

    def per_expert(e, carry):
        def per_block(j, carry2):
            wb = starts_s[e, j] >> 7
            chosen = a3[j, 0, pl.ds(e, 1), :]
            pos = c3[j, pl.ds(e, 1), :].astype(I32) - wb * SEL_TB
            pos = jnp.where(chosen > 0.0, pos, -1)
            w = a3[j, 1, pl.ds(e, 1), :]
            rank = a3[j, 2, pl.ds(e, 1), :].astype(I32)
            tok = tok_iota + j * SEL_TB
            w_hi = w.astype(BF16).astype(F32)
            w_mid = (w - w_hi).astype(BF16).astype(F32)
            w_lo = w - w_hi - w_mid
            vals = jnp.concatenate(
                [(tok >> 7).astype(F32), (tok & 127).astype(F32), w_hi, w_mid, w_lo,
                 (rank >> 7).astype(F32), (rank & 127).astype(F32),
                 jnp.zeros((N_SLOT_ROWS - 7, SEL_TB), F32)], axis=0)
            onehot = jnp.where(win_iota == pos, 1.0, 0.0)
            placed = _mmb(vals, onehot, NT)
            slots_ref[e, wb] += placed[:, 0:SEL_TB]
            slots_ref[e, wb + 1] += placed[:, SEL_TB:2 * SEL_TB]
            return carry2
        return lax.fori_loop(0, ntb, per_block, carry, unroll=16)

    lax.fori_loop(0, E, per_expert, 0)


def _select(aff_t, t, cap, col_blk):
    ntb = t // SEL_TB
    nsb = cap // SEL_TB + SPILL_BLOCKS
    return pl.pallas_call(
        functools.partial(_sel_kernel, t, cap),
        grid=(1,),
        in_specs=[pl.BlockSpec((E, t), lambda i: (0, col_blk))],
        out_specs=[pl.BlockSpec((E, nsb, N_SLOT_ROWS, SEL_TB), lambda i: (0, 0, 0, 0)),
                   pl.BlockSpec((1, t), lambda i: (0, 0)),
                   pl.BlockSpec((1, t), lambda i: (0, 0))],
        out_shape=[jax.ShapeDtypeStruct((E, nsb, N_SLOT_ROWS, SEL_TB), F32),
                   jax.ShapeDtypeStruct((1, t), F32),
                   jax.ShapeDtypeStruct((1, t), F32)],
        scratch_shapes=[pltpu.VMEM((ntb, 3, E, SEL_TB), F32),
                        pltpu.VMEM((ntb, E + 8, SEL_TB), F32),
                        pltpu.VMEM((E + 8, 128), I32),
                        pltpu.SMEM((E + 8, 128), I32),
                        pltpu.SemaphoreType.DMA(())],
        compiler_params=_params(("arbitrary",)),
        name="select_t%d" % t,
    )(aff_t)


DMA_UNROLL = 16


def _ffn_kernel(rc, nch, idx_ref, dst_ref, h2_hbm, wcol_ref, wg_ref, wu_ref, wd_ref, z_hbm,
                xbuf, ybuf, wgb, wub, wdb, gsem, ssem):
    c = pl.program_id(1)
    step = pl.program_id(0) * nch + c
    nsteps = E * nch
    slot = step % 2

    def start_gather(s):
        buf = xbuf.at[s % 2]
        sem = gsem.at[s % 2]

        def issue(jb, carry):
            for u in range(DMA_UNROLL):
                j = jb * DMA_UNROLL + u
                src = pl.multiple_of(idx_ref[s * rc + j], ROW_TILES)
                pltpu.make_async_copy(h2_hbm.at[pl.ds(src, ROW_TILES)],
                                      buf.at[pl.ds(j * ROW_TILES, ROW_TILES)], sem).start(priority=u % 2)
            return carry

        lax.fori_loop(0, rc // DMA_UNROLL, issue, 0)

    @pl.when(step == 0)
    def _():
        start_gather(step)

    @pl.when(step + 1 < nsteps)
    def _():
        start_gather(step + 1)

    @pl.when(c == 0)
    def _():
        wgb[...] = wg_ref[0, 0].astype(BF16)
        wub[...] = wu_ref[0, 0].astype(BF16)
        wdb[...] = wd_ref[0, 0].astype(BF16)

    pltpu.make_async_copy(h2_hbm.at[pl.ds(0, rc * ROW_TILES)], xbuf.at[slot], gsem.at[slot]).wait()
    xb = _load_token_tiles(xbuf.at[slot], rc).astype(BF16)
    hid = _silu(_mm(xb, wgb[...])) * _mm(xb, wub[...])
    y = _mm(hid.astype(BF16), wdb[...]) * wcol_ref[...]

    @pl.when(step > 0)
    def _():
        pltpu.make_async_copy(ybuf, z_hbm.at[pl.ds(0, rc * ROW_TILES)], ssem).wait()

    _store_token_tiles(ybuf, y)

    def scatter(jb, carry):
        for u in range(DMA_UNROLL):
            j = jb * DMA_UNROLL + u
            row = pl.multiple_of(dst_ref[step * rc + j], ROW_TILES)
            pltpu.make_async_copy(ybuf.at[pl.ds(j * ROW_TILES, ROW_TILES)],
                                  z_hbm.at[pl.ds(row, ROW_TILES)], ssem).start(priority=u % 2)
        return carry

    lax.fori_loop(0, rc // DMA_UNROLL, scatter, 0)

    @pl.when(step == nsteps - 1)
    def _():
        pltpu.make_async_copy(ybuf, z_hbm.at[pl.ds(0, rc * ROW_TILES)], ssem).wait()


def _expert_ffn(idx, dst, h2, wcol, w_gate, w_up, w_down, layer, rc, nch):
    zrows = E * nch * rc
    wspec = pl.BlockSpec((1, 1, D, D), lambda e, c, *_: (layer, e, 0, 0))
    return pl.pallas_call(
        functools.partial(_ffn_kernel, rc, nch),
        grid_spec=pltpu.PrefetchScalarGridSpec(
            num_scalar_prefetch=2,
            grid=(E, nch),
            in_specs=[pl.BlockSpec(memory_space=pl.ANY),
                      pl.BlockSpec((rc, 1), lambda e, c, *_: (e * nch + c, 0)),
                      wspec, wspec, wspec],
            out_specs=pl.BlockSpec(memory_space=pl.ANY),
            scratch_shapes=[pltpu.VMEM((2, rc * ROW_TILES, 128), F32),
                            pltpu.VMEM((rc * ROW_TILES, 128), F32),
                            pltpu.VMEM((D, D), BF16), pltpu.VMEM((D, D), BF16),
                            pltpu.VMEM((D, D), BF16),
                            pltpu.SemaphoreType.DMA((2,)), pltpu.SemaphoreType.DMA(())]),
        out_shape=jax.ShapeDtypeStruct((zrows * ROW_TILES, 128), F32),
        compiler_params=_params(("arbitrary", "arbitrary")),
        name="expert_ffn",
    )(idx, dst, h2, wcol, w_gate, w_up, w_down)


CMB_TB = 256
CMB_RC = 512
CMB_BUFS = 4


def _comb_kernel(t1, n2, zrows, final, first_ref, nom_ref, x1_ref, lo_ref, hi_ref, mod_ref, fw_ref, z_hbm,
                 *refs):
    outs, (zbuf, sem) = refs[:-2], refs[-2:]
    i = pl.program_id(0)
    nsteps = pl.num_programs(0)
    r = _mod_row(i * CMB_TB, t1, n2)
    total = first_ref[nsteps]

    def chunk_copy(g):
        b = pl.multiple_of(jnp.minimum(nom_ref[g], zrows - CMB_RC) * ROW_TILES, 8 * ROW_TILES)
        slot = g % CMB_BUFS
        return pltpu.make_async_copy(z_hbm.at[pl.ds(b, CMB_RC * ROW_TILES)], zbuf.at[slot], sem.at[slot])

    @pl.when(i == 0)
    def _():
        for g in range(CMB_BUFS - 1):
            @pl.when(g < total)
            def _():
                chunk_copy(g).start()

    eye = _iota((CMB_TB, CMB_TB), 0) == _iota((CMB_TB, CMB_TB), 1)
    lo = jnp.sum(jnp.where(eye, lo_ref[...], 0.0), axis=1, keepdims=True)
    hi = jnp.sum(jnp.where(eye, hi_ref[...], 0.0), axis=1, keepdims=True)
    col = _iota((1, CMB_RC), 1)

    def chunk(g, acc):
        @pl.when(g + CMB_BUFS - 1 < total)
        def _():
            chunk_copy(g + CMB_BUFS - 1).start()

        chunk_copy(g).wait()
        nominal = nom_ref[g]
        rows = (col + jnp.minimum(nominal, zrows - CMB_RC)).astype(F32)
        lo_c = jnp.maximum(lo, nominal.astype(F32))
        s = jnp.where(rows >= lo_c, jnp.where(rows < hi, 1.0, 0.0), 0.0).astype(BF16)
        return acc + _mmb(s, _load_token_tiles(zbuf.at[g % CMB_BUFS], CMB_RC))

    moe = lax.fori_loop(first_ref[i], first_ref[i + 1], chunk, jnp.zeros((CMB_TB, D), F32))

    x2 = x1_ref[...] + mod_ref[pl.ds(r, 1), 5 * D:6 * D] * moe
    if final:
        y = x2 * lax.rsqrt(jnp.mean(x2 * x2, axis=-1, keepdims=True) + EPS) * fw_ref[...]

        @pl.when(i * CMB_TB < t1)
        def _():
            outs[0][...] = y

        @pl.when(i * CMB_TB >= t1)
        def _():
            outs[1][...] = y
    else:
        outs[0][...] = x2


def _combine(lo, hi, x1, mod, fw, z, t1, n2, final):
    t_all = x1.shape[0]
    zrows = z.shape[0] // ROW_TILES
    nsteps = t_all // CMB_TB
    base = (lo[0, ::CMB_TB].astype(I32) // 8) * 8
    end = hi[0, CMB_TB - 1::CMB_TB].astype(I32)
    nchunk = jnp.maximum((end - base + CMB_RC - 1) // CMB_RC, 1)
    first = jnp.concatenate([jnp.zeros((1,), I32), jnp.cumsum(nchunk)])
    max_chunks = zrows // CMB_RC + 2 * nsteps
    g = jnp.arange(max_chunks, dtype=I32)
    step_of = jnp.minimum(jnp.sum(first[None, 1:] <= g[:, None], axis=1), nsteps - 1)
    nominal = base[step_of] + (g - first[step_of]) * CMB_RC
    nb1 = t1 // CMB_TB
    if final:
        out_specs = [pl.BlockSpec((CMB_TB, D), lambda i, *_: (jnp.minimum(i, nb1 - 1), 0)),
                     pl.BlockSpec((CMB_TB, D), lambda i, *_: (jnp.maximum(i - nb1, 0), 0))]
        out_shape = [jax.ShapeDtypeStruct((t1, D), F32), jax.ShapeDtypeStruct((t_all - t1, D), F32)]
    else:
        out_specs = pl.BlockSpec((CMB_TB, D), lambda i, *_: (i, 0))
        out_shape = jax.ShapeDtypeStruct((t_all, D), F32)
    return pl.pallas_call(
        functools.partial(_comb_kernel, t1, n2, zrows, final),
        grid_spec=pltpu.PrefetchScalarGridSpec(
            num_scalar_prefetch=2,
            grid=(t_all // CMB_TB,),
            in_specs=[pl.BlockSpec((CMB_TB, D), lambda i, *_: (i, 0)),
                      pl.BlockSpec((1, CMB_TB), lambda i, *_: (0, i)),
                      pl.BlockSpec((1, CMB_TB), lambda i, *_: (0, i)),
                      pl.BlockSpec((8, NMOD * D), lambda i, *_: (0, 0)),
                      pl.BlockSpec((1, D), lambda i, *_: (0, 0)),
                      pl.BlockSpec(memory_space=pl.ANY)],
            out_specs=out_specs,
            scratch_shapes=[pltpu.VMEM((CMB_BUFS, CMB_RC * ROW_TILES, 128), F32),
                            pltpu.SemaphoreType.DMA((CMB_BUFS,))]),
        out_shape=out_shape,
        compiler_params=_params(("arbitrary",)),
        name="combine",
    )(first, nominal, x1, lo, hi, mod, fw, z)


def _grid_pos_embed(n, d):
    rows = n // GRID_W
    r = np.repeat(np.arange(rows, dtype=np.float64), GRID_W)
    col = np.tile(np.arange(GRID_W, dtype=np.float64), rows)
    quarter = d // 4
    omega = np.power(POS_BASE, -np.arange(quarter, dtype=np.float64) / quarter)
    ra = r[:, None] * omega
    ca = col[:, None] * omega
    return jnp.asarray(np.concatenate([np.sin(ra), np.cos(ra), np.sin(ca), np.cos(ca)], axis=-1), F32)


def _decode_slots(slots):
    e, nsb, rows, tb = slots.shape
    nsb -= SPILL_BLOCKS
    slots = slots[:, :nsb].transpose(0, 2, 1, 3).reshape(e, rows, nsb * tb)
    idx = (slots[:, 0] * 128.0 + slots[:, 1]).astype(I32)
    w = slots[:, 2] + slots[:, 3] + slots[:, 4]
    rank = (slots[:, 5] * 128.0 + slots[:, 6]).astype(I32)
    return idx, w, rank


def kernel(x_prompt, x_sample, state_delta, c, c_ctx, w_ada, b_ada, norm1_w, norm2_w, w_in, dn_conv_w,
           dn_a_log, dn_dt_bias, dn_norm_w, sc_conv_w, w_out, w_router, w_gate, w_up, w_down,
           final_norm_w):
    b1, n1, _ = x_prompt.shape
    b2, n2, _ = x_sample.shape
    depth = w_ada.shape[0]
    t1, t2 = b1 * n1, b2 * n2
    t_all = t1 + t2
    cap1 = max(1, 2 * t1 // E)
    cap2 = max(1, 2 * t2 // E)
    tm = 512
    rc = 512 if (cap1 % 512 == 0 and cap2 % 512 == 0) else 128
    assert t1 % n2 == 0 and t1 % tm == 0 and n2 % tm == 0
    assert n1 % (DN_POS * CH) == 0 and n2 % (DN_POS * CH) == 0
    assert t1 % SEL_TB == 0 and t2 % SEL_TB == 0 and t1 % t2 == 0
    assert cap1 % rc == 0 and cap2 % rc == 0 and 2 * t_all >= CMB_RC
    nch = (cap1 + cap2) // rc
    width = H * DH

    cond8 = jnp.zeros((8, D), F32).at[0].set(c_ctx).at[1:1 + b2].set(c)
    mod = _ada(cond8, w_ada, b_ada)

    x = None
    tabs1 = _dft_tables(n1)
    tabs2 = _dft_tables(n2)
    ctx_states = None

    for l in range(depth):
        wl = w_in[l]
        wm = jnp.concatenate([wl[:, 0:4 * width], wl[:, 4 * width + 4 * H:]], axis=1).astype(BF16)
        wg = jnp.pad(wl[:, 4 * width:4 * width + 4 * H], ((0, 0), (0, 128 - 4 * H))).astype(BF16)
        al = jnp.pad(dn_a_log[l].reshape(1, 2 * H), ((0, 0), (2 * H, 128 - 4 * H)))
        dt = jnp.pad(dn_dt_bias[l].reshape(1, 2 * H), ((0, 0), (2 * H, 128 - 4 * H)))
        xs = (x,) if l else (x_prompt.reshape(t1, D), x_sample.reshape(t2, D), _grid_pos_embed(n2, D))
        res = _in_proj(xs, mod[l], norm1_w[l].reshape(1, D), wm, wg, al, dt, t1, n2, tm)
        if not l:
            x, res = res[0], res[1:]
        p, g, gt = res

        nwd = dn_norm_w[l].reshape(1, DH)
        dn1, ctx_states = _deltanet(p, g, gt, dn_conv_w[l], nwd, None, ctx_states, b1, n1, 0)
        dn2, _ = _deltanet(p, g, gt, dn_conv_w[l], nwd, state_delta[:, l], None, b2, n2, t1 // n2)
        mx1 = _mix2(p, sc_conv_w[l], tabs1[1], tabs1[2], tabs1[0], b1, n1, 0, MIX_GROUP)
        mx2 = _mix2(p, sc_conv_w[l], tabs2[1], tabs2[2], tabs2[0], b2, n2, t1 // n2, 1)

        wr = jnp.pad(w_router[l], ((0, 0), (0, 128 - E)))
        x1, h2, aff_t = _out_proj(dn1, dn2, mx1, mx2, x, mod[l], norm2_w[l].reshape(1, D),
                                  w_out[l].astype(BF16), wr, t1, n2, tm)

        slots1, lo1, hi1 = _select(aff_t, t1, cap1, 0)
        slots2, lo2, hi2 = _select(aff_t, t2, cap2, t1 // t2)
        idx1, wsel1, rank1 = _decode_slots(slots1)
        idx2, wsel2, rank2 = _decode_slots(slots2)
        idx = jnp.concatenate([idx1, idx2 + t1], axis=1).reshape(-1) * ROW_TILES
        dst = jnp.concatenate([rank1, rank2 + 2 * t1], axis=1).reshape(-1) * ROW_TILES
        wcol = jnp.concatenate([wsel1, wsel2], axis=1).reshape(-1, 1)
        z = _expert_ffn(idx, dst, h2, wcol, w_gate, w_up, w_down, l, rc, nch)

        lo = jnp.concatenate([lo1, lo2 + 2.0 * t1], axis=1)
        hi = jnp.concatenate([hi1, hi2 + 2.0 * t1], axis=1)
        x = _combine(lo, hi, x1, mod[l], final_norm_w.reshape(1, D), z, t1, n2, l == depth - 1)

    y_prompt, y_sample = x
    return y_prompt.reshape(b1, n1, D), y_sample.reshape(b2, n2, D), ctx_states
```

```python
import functools
import math

import jax
import jax.numpy as jnp
import numpy as np
from jax import lax
from jax.experimental import pallas as pl
from jax.experimental.pallas import tpu as pltpu

F32 = jnp.float32
BF16 = jnp.bfloat16
I32 = jnp.int32
HI = lax.Precision.HIGHEST

D = 1024
H = 4
DH = 128
CH = 64
E = 16
NMOD = 6
EPS = 1e-6
GRID_W = 64
POS_BASE = 10000.0
P_COLS = 3072
VMEM_LIMIT = 56 * 1024 * 1024

NN = (((1,), (0,)), ((), ()))
NT = (((1,), (1,)), ((), ()))
TN = (((0,), (0,)), ((), ()))


def _mm(a, b, dims=NN, prec=None):
    return lax.dot_general(a, b, dims, precision=prec, preferred_element_type=F32)


def _mmb(a, b, dims=NN):
    return lax.dot_general(a.astype(BF16), b.astype(BF16), dims, preferred_element_type=F32)


def _mm3(a, b, dims=NN):
    a_hi = a.astype(BF16)
    a_lo = (a - a_hi.astype(F32)).astype(BF16)
    b_hi = b.astype(BF16)
    b_lo = (b - b_hi.astype(F32)).astype(BF16)
    return _mm(a_hi, b_hi, dims) + _mm(a_hi, b_lo, dims) + _mm(a_lo, b_hi, dims)


def _mm_inv(a, b):
    return _mmb(a, b)


def _silu(x):
    return x / (1.0 + jnp.exp(-x))


def _sigmoid(x):
    return 1.0 / (1.0 + jnp.exp(-x))


def _softplus(x):
    return jnp.maximum(x, 0.0) + jnp.log1p(jnp.exp(-jnp.abs(x)))


def _iota(shape, dim):
    return lax.broadcasted_iota(I32, shape, dim)


ROW_TILES = D // 128


def _load_token_tiles(ref, n):
    return jnp.concatenate([ref[pl.ds(s, n, stride=ROW_TILES), :] for s in range(ROW_TILES)], axis=1)


def _store_token_tiles(ref, x):
    n = x.shape[0]
    for s in range(ROW_TILES):
        ref[pl.ds(s, n, stride=ROW_TILES), :] = x[:, s * 128:(s + 1) * 128]


def _mod_row(t0, t1, n2):
    return jnp.where(t0 < t1, 0, 1 + jnp.maximum(t0 - t1, 0) // n2)


def _params(sem):
    return pltpu.CompilerParams(dimension_semantics=sem, vmem_limit_bytes=VMEM_LIMIT)


def _ada_kernel(c_ref, w_ref, b_ref, o_ref):
    s = _silu(c_ref[...])
    o_ref[0] = _mm3(s, w_ref[0]) + b_ref[0]


def _ada(cond8, w_ada, b_ada):
    depth = w_ada.shape[0]
    tn = 1536
    return pl.pallas_call(
        _ada_kernel,
        grid=(depth, NMOD * D // tn),
        in_specs=[pl.BlockSpec((8, D), lambda l, j: (0, 0)),
                  pl.BlockSpec((1, D, tn), lambda l, j: (l, 0, j)),
                  pl.BlockSpec((1, 1, tn), lambda l, j: (l, 0, j))],
        out_specs=pl.BlockSpec((1, 8, tn), lambda l, j: (l, 0, j)),
        out_shape=jax.ShapeDtypeStruct((depth, 8, NMOD * D), F32),
        compiler_params=_params(("arbitrary", "arbitrary")),
        name="ada",
    )(cond8, w_ada, b_ada.reshape(depth, 1, NMOD * D))


IN_SUB = 2


def _in_core(tm, r, x, mod_ref, nw_ref, wm_ref, wg_ref, al_ref, dt_ref, p_ref, g_ref, gt_ref):
    sh = mod_ref[pl.ds(r, 1), 0:D]
    sc = mod_ref[pl.ds(r, 1), D:2 * D]
    sub = tm // IN_SUB
    nsplit = 4
    wc = P_COLS // nsplit
    lane = _iota((1, 128), 1)
    hbs = []
    for k in range(IN_SUB):
        xk = x[k * sub:(k + 1) * sub]
        y = xk * lax.rsqrt(jnp.mean(xk * xk, axis=-1, keepdims=True) + EPS) * nw_ref[...]
        hbs.append((y * (1.0 + sc) + sh).astype(BF16))
    for k in range(IN_SUB):
        rs = slice(k * sub, (k + 1) * sub)
        for j in range(nsplit):
            p_ref[rs, j * wc:(j + 1) * wc] = _mm(hbs[k], wm_ref[:, j * wc:(j + 1) * wc])
    for k in range(IN_SUB):
        raw = _mm(hbs[k], wg_ref[...])
        g = -jnp.exp(al_ref[...]) * _softplus(raw + dt_ref[...])
        act = jnp.where(lane < 2 * H, _sigmoid(raw), g)
        g_ref[k * sub:(k + 1) * sub, :] = act[:, 0:4 * H]
        act_t = act.T
        for j in range(sub // CH):
            gt_ref[k * (sub // CH) + j] = act_t[0:4 * H, j * CH:(j + 1) * CH]


def _in_kernel(t1, n2, tm, x_ref, *refs):
    r = _mod_row(pl.program_id(0) * tm, t1, n2)
    _in_core(tm, r, x_ref[...], *refs)


def _in_first_kernel(t1, n2, tm, xp_ref, xs_ref, pos_ref, mod_ref, nw_ref, wm_ref, wg_ref, al_ref,
                     dt_ref, x_ref, p_ref, g_ref, gt_ref):
    t0 = pl.program_id(0) * tm
    x = jnp.where(t0 < t1, xp_ref[...], xs_ref[...] + pos_ref[...])
    x_ref[...] = x
    _in_core(tm, _mod_row(t0, t1, n2), x, mod_ref, nw_ref, wm_ref, wg_ref, al_ref, dt_ref,
             p_ref, g_ref, gt_ref)


def _in_proj(xs, mod, nw, wm, wg, al, dt, t1, n2, tm):
    first = len(xs) == 3
    t_all = t1 + xs[1].shape[0] if first else xs[0].shape[0]
    nt1 = t1 // tm
    wspecs = [pl.BlockSpec((8, NMOD * D), lambda i: (0, 0)),
              pl.BlockSpec((1, D), lambda i: (0, 0)),
              pl.BlockSpec((D, P_COLS), lambda i: (0, 0)),
              pl.BlockSpec((D, 128), lambda i: (0, 0)),
              pl.BlockSpec((1, 128), lambda i: (0, 0)),
              pl.BlockSpec((1, 128), lambda i: (0, 0))]
    out_specs = [pl.BlockSpec((tm, P_COLS), lambda i: (i, 0)),
                 pl.BlockSpec((tm, 4 * H), lambda i: (i, 0)),
                 pl.BlockSpec((tm // CH, 4 * H, CH), lambda i: (i, 0, 0))]
    out_shape = [jax.ShapeDtypeStruct((t_all, P_COLS), F32),
                 jax.ShapeDtypeStruct((t_all, 4 * H), F32),
                 jax.ShapeDtypeStruct((t_all // CH, 4 * H, CH), F32)]
    if first:
        kern = _in_first_kernel
        xspecs = [pl.BlockSpec((tm, D), lambda i: (jnp.minimum(i, nt1 - 1), 0)),
                  pl.BlockSpec((tm, D), lambda i: (jnp.maximum(i - nt1, 0), 0)),
                  pl.BlockSpec((tm, D), lambda i: (jnp.maximum(i - nt1, 0) % (n2 // tm), 0))]
        out_specs = [pl.BlockSpec((tm, D), lambda i: (i, 0))] + out_specs
        out_shape = [jax.ShapeDtypeStruct((t_all, D), F32)] + out_shape
    else:
        kern = _in_kernel
        xspecs = [pl.BlockSpec((tm, D), lambda i: (i, 0))]
    return pl.pallas_call(
        functools.partial(kern, t1, n2, tm),
        grid=(t_all // tm,),
        in_specs=xspecs + wspecs,
        out_specs=out_specs,
        out_shape=out_shape,
        compiler_params=_params(("arbitrary",)),
        name="in_proj_first" if first else "in_proj",
    )(*xs, mod, nw, wm, wg, al, dt)


def _shift_conv(x, w_ref, c0, c1, period=None):
    n = x.shape[0]
    row = _iota((n, 1), 0) & ((period or n) - 1)
    prev = jnp.where(row == 0, 0.0, pltpu.roll(x, 1, 0))
    nxt = jnp.where(row == (period or n) - 1, 0.0, pltpu.roll(x, n - 1, 0))
    return prev * w_ref[0:1, c0:c1] + x * w_ref[1:2, c0:c1] + nxt * w_ref[2:3, c0:c1]


DN_POS = 4


def _dn_kernel(n, nprev, has_s0, q_ref, k_ref, v_ref, z_ref, g_ref, gt_ref, cw_ref, nw_ref, *refs):
    refs = list(refs)
    s0_ref = refs.pop(0) if has_s0 else None
    prev_ref = refs.pop(0) if nprev else None
    o_ref, sfin_ref, qs, ks, vs, gcc_s, gct_s, s_s, oacc = refs
    nc = n // CH
    width = H * DH

    for hh in range(H):
        c0, c1 = hh * DH, (hh + 1) * DH
        q = _silu(_shift_conv(q_ref[:, c0:c1], cw_ref, c0, c1))
        q = q * lax.rsqrt(jnp.sum(q * q, axis=-1, keepdims=True) + 1e-6)
        qs[:, c0:c1] = q * (DH ** -0.5)
        k = _silu(_shift_conv(k_ref[:, c0:c1], cw_ref, width + c0, width + c1))
        ks[:, c0:c1] = k * lax.rsqrt(jnp.sum(k * k, axis=-1, keepdims=True) + 1e-6)
        vs[:, c0:c1] = _silu(_shift_conv(v_ref[:, c0:c1], cw_ref, 2 * width + c0, 2 * width + c1))

    ii = _iota((CH, CH), 0)
    jj = _iota((CH, CH), 1)
    x = gt_ref[...].reshape(nc * 4 * H, CH)
    cf = _mm(x, (ii <= jj).astype(F32), prec=HI)
    cb = _mm(x, (ii >= jj).astype(F32), prec=HI)
    rsel = (_iota((nc * 4 * H, 1), 0) & (4 * H - 1)) < 3 * H
    gct_s[...] = jnp.where(rsel, cf, cb).reshape(nc, 4 * H, CH)
    i2 = _iota((2 * CH, 2 * CH), 0)
    j2 = _iota((2 * CH, 2 * CH), 1)
    same = (i2 >= CH) == (j2 >= CH)
    lf = (same & (i2 >= j2)).astype(F32)
    lb = (same & (i2 <= j2)).astype(F32)
    lsel = _iota((1, 4 * H), 1) < 3 * H
    for m in range(n // (2 * CH)):
        y = g_ref[m * 2 * CH:(m + 1) * 2 * CH, :]
        gcc_s[m * 2 * CH:(m + 1) * 2 * CH, :] = jnp.where(lsel, _mm(lf, y, prec=HI), _mm(lb, y, prec=HI))

    for hh in range(H):
        for d in range(2):
            s_s[hh * 2 + d] = s0_ref[0, d, hh] if has_s0 else jnp.zeros((DH, DH), F32)

    eye = (ii == jj).astype(F32)
    blk_same = [(ii >> s) == (jj >> s) for s in range(1, 7)]
    merge_mask = [blk_same[lvl] & ~blk_same[lvl - 1] for lvl in range(1, len(blk_same))]

    chains = [(hh, d) for hh in range(H) for d in range(2)]

    def chunk_step(cc, carry):
        st = []
        for p in range(DN_POS):
            for hh, d in chains:
                c = cc * DN_POS + p
                cidx = c if d == 0 else nc - 1 - c
                r0 = pl.multiple_of(cidx * CH, CH)
                c0, c1 = hh * DH, (hh + 1) * DH
                gcol = 2 * H + H * d + hh
                bcol = H * d + hh
                incl = (ii >= jj) if d == 0 else (ii <= jj)
                strict = (ii > jj) if d == 0 else (ii < jj)
                qc = qs[pl.ds(r0, CH), c0:c1]
                kc = ks[pl.ds(r0, CH), c0:c1]
                vc = vs[pl.ds(r0, CH), c0:c1]
                beta = g_ref[pl.ds(r0, CH), bcol:bcol + 1]
                gc = gcc_s[pl.ds(r0, CH), gcol:gcol + 1]
                gr = gct_s[cidx, gcol:gcol + 1, :]
                tot = gr[:, CH - 1:CH] if d == 0 else gr[:, 0:1]
                decay = jnp.where(incl, jnp.exp(jnp.where(incl, gc - gr, 0.0)), 0.0)
                kb = kc * beta
                egc = jnp.exp(gc)
                st.append(dict(r0=r0, c0=c0, c1=c1, d=d, strict=strict, kc=kc, kb=kb, decay=decay,
                               rhs=jnp.concatenate([vc * beta, kb * egc], axis=1),
                               qd=qc * egc, qc=qc, kd=kc * jnp.exp(tot - gc), gl=jnp.exp(tot)))
        for x in st:
            x["lmat"] = jnp.where(x["strict"], _mmb(x["kb"], x["kc"], NT) * x["decay"], 0.0)
            x["qk"] = _mmb(x["qc"], x["kc"], NT) * x["decay"]
            x["tinv"] = eye - jnp.where(blk_same[0], x["lmat"], 0.0)
        for mask in merge_mask:
            for x in st:
                x["ot"] = _mm_inv(jnp.where(mask, x["lmat"], 0.0), x["tinv"])
            for x in st:
                x["tinv"] = x["tinv"] - _mm_inv(x["tinv"], x["ot"])
        for x in st:
            x["sol"] = _mm_inv(x["tinv"], x["rhs"])
        state = [s_s[i] for i in range(len(chains))]
        for p in range(DN_POS):
            units = st[p * len(chains):(p + 1) * len(chains)]
            for i, x in enumerate(units):
                x["sb"] = state[i].astype(BF16)
                x["v_new"] = x["sol"][:, 0:DH] - _mmb(x["sol"][:, DH:2 * DH], x["sb"])
            for i, x in enumerate(units):
                o = _mmb(x["qd"], x["sb"]) + _mmb(x["qk"], x["v_new"])
                state[i] = state[i] * x["gl"] + _mmb(x["kd"], x["v_new"], TN)
                oacc[x["d"], pl.ds(x["r0"], CH), x["c0"]:x["c1"]] = o
        for i in range(len(chains)):
            s_s[i] = state[i]
        return carry

    lax.fori_loop(0, nc // DN_POS, chunk_step, 0)

    for hh in range(H):
        c0, c1 = hh * DH, (hh + 1) * DH
        o = oacc[0, :, c0:c1] + oacc[1, :, c0:c1]
        o = o * lax.rsqrt(jnp.mean(o * o, axis=-1, keepdims=True) + EPS)
        o_ref[:, c0:c1] = (o * nw_ref[...] * _silu(z_ref[:, c0:c1])).astype(o_ref.dtype)
        for d in range(2):
            sfin_ref[0, nprev, d, hh] = s_s[hh * 2 + d]
    for layer in range(nprev):
        sfin_ref[0, layer] = prev_ref[0, layer]


def _deltanet(p, g, gt, conv_w, norm_w, s0, prev, bsz, n, row_blk0):
    nc = n // CH
    width = H * DH
    nprev = 0 if prev is None else prev.shape[1]
    state_spec = lambda k: pl.BlockSpec((1, k, 2, H, DH, DH), lambda b: (b, 0, 0, 0, 0, 0))
    opt_args, opt_specs = [], []
    if s0 is not None:
        opt_args.append(s0)
        opt_specs.append(pl.BlockSpec((1, 2, H, DH, DH), lambda b: (b, 0, 0, 0, 0)))
    if prev is not None:
        opt_args.append(prev)
        opt_specs.append(state_spec(nprev))
    return pl.pallas_call(
        functools.partial(_dn_kernel, n, nprev, s0 is not None),
        grid=(bsz,),
        in_specs=[pl.BlockSpec((n, width), lambda b: (row_blk0 + b, 0)),
                  pl.BlockSpec((n, width), lambda b: (row_blk0 + b, 1)),
                  pl.BlockSpec((n, width), lambda b: (row_blk0 + b, 2)),
                  pl.BlockSpec((n, width), lambda b: (row_blk0 + b, 3)),
                  pl.BlockSpec((n, 4 * H), lambda b: (row_blk0 + b, 0)),
                  pl.BlockSpec((nc, 4 * H, CH), lambda b: (row_blk0 + b, 0, 0)),
                  pl.BlockSpec((3, 3 * width), lambda b: (0, 0)),
                  pl.BlockSpec((1, DH), lambda b: (0, 0))] + opt_specs,
        out_specs=[pl.BlockSpec((n, width), lambda b: (b, 0)), state_spec(nprev + 1)],
        out_shape=[jax.ShapeDtypeStruct((bsz * n, width), BF16),
                   jax.ShapeDtypeStruct((bsz, nprev + 1, 2, H, DH, DH), F32)],
        scratch_shapes=[pltpu.VMEM((n, width), F32), pltpu.VMEM((n, width), F32),
                        pltpu.VMEM((n, width), F32), pltpu.VMEM((n, 4 * H), F32),
                        pltpu.VMEM((nc, 4 * H, CH), F32), pltpu.VMEM((2 * H, DH, DH), F32),
                        pltpu.VMEM((2, n, width), F32)],
        compiler_params=_params(("arbitrary",)),
        name="deltanet_n%d" % n,
    )(p, p, p, p, g, gt, conv_w, norm_w, *opt_args)


MIX_GROUP = 4


def _mix2_kernel(n, sb_ref, sc_ref, su_ref, fu_ref, cw_ref, bdc_ref, bds_ref, cs_ref, o_ref):
    rows, w = sb_ref.shape
    cu = sc_ref[...] * su_ref[...]
    o_ref[:, 0:w] = (sb_ref[...] * _shift_conv(cu, cw_ref, 0, w, n)).astype(o_ref.dtype)
    fu = fu_ref[...]
    a = _mm3(fu, bdc_ref[...])
    b = _mm3(fu, bds_ref[...])
    for q in range(rows // n):
        rs = slice(q * n, (q + 1) * n)
        fo = _mmb(cs_ref[...], jnp.concatenate([a[rs], b[rs]], axis=0))
        o_ref[rs, w:2 * w] = fo.astype(o_ref.dtype)


def _mix2(p, conv_w, bdc, bds, cs, bsz, n, row_blk0, group):
    w = 256
    rows = group * n
    blk0 = row_blk0 // group
    assert bsz % group == 0 and row_blk0 % group == 0
    return pl.pallas_call(
        functools.partial(_mix2_kernel, n),
        grid=(bsz // group,),
        in_specs=[pl.BlockSpec((rows, w), lambda b: (blk0 + b, 8)),
                  pl.BlockSpec((rows, w), lambda b: (blk0 + b, 9)),
                  pl.BlockSpec((rows, w), lambda b: (blk0 + b, 10)),
                  pl.BlockSpec((rows, w), lambda b: (blk0 + b, 11)),
                  pl.BlockSpec((3, w), lambda b: (0, 0)),
                  pl.BlockSpec((w, w), lambda b: (0, 0)),
                  pl.BlockSpec((w, w), lambda b: (0, 0)),
                  pl.BlockSpec((n, 2 * n), lambda b: (0, 0))],
        out_specs=pl.BlockSpec((rows, 2 * w), lambda b: (b, 0)),
        out_shape=jax.ShapeDtypeStruct((bsz * n, 2 * w), BF16),
        compiler_params=_params(("arbitrary",)),
        name="conv_fourier_n%d" % n,
    )(p, p, p, p, conv_w, bdc, bds, cs)


def _dft_tables(n):
    gw = 64
    k = np.arange(n, dtype=np.int64)
    ang = ((k[:, None] * k[None, :]) % n) * (2.0 * math.pi / n)
    scale = 1.0 / math.sqrt(n * gw)
    cs = np.concatenate([np.cos(ang), -np.sin(ang)], axis=1) * scale
    c = np.arange(256, dtype=np.int64)
    angc = (((c[:, None] % gw) * (c[None, :] % gw)) % gw) * (2.0 * math.pi / gw)
    same = (c[:, None] // gw) == (c[None, :] // gw)
    bdc = np.where(same, np.cos(angc), 0.0)
    bds = np.where(same, np.sin(angc), 0.0)
    return jnp.asarray(cs, F32).astype(BF16), jnp.asarray(bdc, F32), jnp.asarray(bds, F32)


OUT_SUB = 2


def _out_kernel(t1, n2, tm, dn1_ref, dn2_ref, mx1_ref, mx2_ref, x_ref, mod_ref, nw_ref, wo_ref, wr_ref,
                x1_ref, h2_ref, aff_ref):
    t0 = pl.program_id(0) * tm
    r = _mod_row(t0, t1, n2)
    half = D // 2
    sub = tm // OUT_SUB
    mixes = []
    for k in range(OUT_SUB):
        rs = slice(k * sub, (k + 1) * sub)
        dn = jnp.where(t0 < t1, dn1_ref[rs, :], dn2_ref[rs, :])
        mx = jnp.where(t0 < t1, mx1_ref[rs, :], mx2_ref[rs, :])
        mixes.append(_mmb(dn, wo_ref[0:half, :]) + _mmb(mx, wo_ref[half:D, :]))
    h2s = []
    for k in range(OUT_SUB):
        rs = slice(k * sub, (k + 1) * sub)
        x1 = x_ref[rs, :] + mod_ref[pl.ds(r, 1), 2 * D:3 * D] * mixes[k]
        x1_ref[rs, :] = x1
        y = x1 * lax.rsqrt(jnp.mean(x1 * x1, axis=-1, keepdims=True) + EPS) * nw_ref[...]
        h2 = y * (1.0 + mod_ref[pl.ds(r, 1), 4 * D:5 * D]) + mod_ref[pl.ds(r, 1), 3 * D:4 * D]
        _store_token_tiles(h2_ref.at[pl.ds(k * sub * ROW_TILES, sub * ROW_TILES)], h2)
        h2s.append(h2)
    logits = [_mm3(h2, wr_ref[...]) for h2 in h2s]
    for k in range(OUT_SUB):
        lt = logits[k].T[0:E, :]
        ex = jnp.exp(lt - jnp.max(lt, axis=0, keepdims=True))
        aff_ref[:, k * sub:(k + 1) * sub] = ex / jnp.sum(ex, axis=0, keepdims=True)


def _out_proj(dn1, dn2, mx1, mx2, x, mod, nw, wo, wr, t1, n2, tm):
    t_all = x.shape[0]
    nt1 = t1 // tm
    spec1 = pl.BlockSpec((tm, D // 2), lambda i: (jnp.minimum(i, nt1 - 1), 0))
    spec2 = pl.BlockSpec((tm, D // 2), lambda i: (jnp.maximum(i - nt1, 0), 0))
    return pl.pallas_call(
        functools.partial(_out_kernel, t1, n2, tm),
        grid=(t_all // tm,),
        in_specs=[spec1, spec2, spec1, spec2,
                  pl.BlockSpec((tm, D), lambda i: (i, 0)),
                  pl.BlockSpec((8, NMOD * D), lambda i: (0, 0)),
                  pl.BlockSpec((1, D), lambda i: (0, 0)),
                  pl.BlockSpec((D, D), lambda i: (0, 0)),
                  pl.BlockSpec((D, 128), lambda i: (0, 0))],
        out_specs=[pl.BlockSpec((tm, D), lambda i: (i, 0)),
                   pl.BlockSpec((tm * ROW_TILES, 128), lambda i: (i, 0)),
                   pl.BlockSpec((E, tm), lambda i: (0, i))],
        out_shape=[jax.ShapeDtypeStruct((t_all, D), F32),
                   jax.ShapeDtypeStruct((t_all * ROW_TILES, 128), F32),
                   jax.ShapeDtypeStruct((E, t_all), F32)],
        compiler_params=_params(("arbitrary",)),
        name="out_proj",
    )(dn1, dn2, mx1, mx2, x, mod, nw, wo, wr)


SEL_TB = 128
N_SLOT_ROWS = 16
SPILL_BLOCKS = 2


def _sel_kernel(t, cap, aff_ref, slots_ref, lo_ref, hi_ref, a3, c3, starts_v, starts_s, sem):
    ntb = t // SEL_TB
    aff = aff_ref[...]

    def search(i, thr):
        cand = thr | jnp.left_shift(jnp.int32(1), 30 - i)
        cnt = jnp.sum((aff >= pltpu.bitcast(cand, F32)).astype(F32), axis=1, keepdims=True)
        return jnp.where(cnt >= cap, cand, thr)

    thr = lax.fori_loop(0, 31, search, jnp.zeros((E, 1), I32))
    gt = (aff >= pltpu.bitcast(thr + 1, F32)).astype(F32)
    eq = (aff >= pltpu.bitcast(thr, F32)).astype(F32) - gt
    need = cap - jnp.sum(gt, axis=1, keepdims=True)

    ui = _iota((SEL_TB, SEL_TB), 0)
    uj = _iota((SEL_TB, SEL_TB), 1)
    upper = (ui < uj).astype(BF16)
    blk_lane = _iota((1, 128), 1)

    def excl_cumsum(rows, dst, r0, r1):
        carry = jnp.zeros((rows.shape[0], 1), F32)
        starts = jnp.zeros((rows.shape[0], 128), F32)
        for j in range(ntb):
            blk = rows[:, j * SEL_TB:(j + 1) * SEL_TB]
            dst[j, r0:r1, :] = _mm(blk.astype(BF16), upper) + carry
            starts = jnp.where(blk_lane == j, carry, starts)
            carry = carry + jnp.sum(blk, axis=1, keepdims=True)
        return starts

    excl_cumsum(eq, c3, 0, E)
    rank_eq = jnp.concatenate([c3[j, 0:E, :] for j in range(ntb)], axis=1)
    sel = jnp.maximum(gt, jnp.where(rank_eq < need, eq, 0.0))
    n_tok = jnp.sum(sel, axis=0, keepdims=True)
    starts = excl_cumsum(jnp.concatenate([sel, jnp.broadcast_to(n_tok, (8, t))], axis=0), c3, 0, E + 8)
    starts_v[...] = starts.astype(I32)
    to_smem = pltpu.make_async_copy(starts_v, starts_s, sem)
    to_smem.start()
    ei = _iota((E, E), 0)
    ej = _iota((E, E), 1)
    below = _mm((ej < ei).astype(BF16), sel.astype(BF16))
    for j in range(ntb):
        sl = slice(j * SEL_TB, (j + 1) * SEL_TB)
        off = c3[j, E:E + 1, :]
        lo_ref[:, sl] = off
        hi_ref[:, sl] = off + n_tok[:, sl]
        rank = off + below[:, sl]
        a3[j, 0] = sel[:, sl]
        a3[j, 1] = aff[:, sl]
        a3[j, 2] = rank
    slots_ref[...] = jnp.zeros(slots_ref.shape, F32)
    to_smem.wait()

    win_iota = _iota((2 * SEL_TB, SEL_TB), 0)
    tok_iota = _iota((1, SEL_TB), 1)

    def per_expert(e, carry):
        def per_block(j, carry2):
            wb = starts_s[e, j] >> 7
            chosen = a3[j, 0, pl.ds(e, 1), :]
            pos = c3[j, pl.ds(e, 1), :].astype(I32) - wb * SEL_TB
            pos = jnp.where(chosen > 0.0, pos, -1)
            w = a3[j, 1, pl.ds(e, 1), :]
            rank = a3[j, 2, pl.ds(e, 1), :].astype(I32)
            tok = tok_iota + j * SEL_TB
            w_hi = w.astype(BF16).astype(F32)
            w_mid = (w - w_hi).astype(BF16).astype(F32)
            w_lo = w - w_hi - w_mid
            vals = jnp.concatenate(
                [(tok >> 7).astype(F32), (tok & 127).astype(F32), w_hi, w_mid, w_lo,
                 (rank >> 7).astype(F32), (rank & 127).astype(F32),
                 jnp.zeros((N_SLOT_ROWS - 7, SEL_TB), F32)], axis=0)
            onehot = jnp.where(win_iota == pos, 1.0, 0.0)
            placed = _mmb(vals, onehot, NT)
            slots_ref[e, wb] += placed[:, 0:SEL_TB]
            slots_ref[e, wb + 1] += placed[:, SEL_TB:2 * SEL_TB]
            return carry2
        return lax.fori_loop(0, ntb, per_block, carry, unroll=16)

    lax.fori_loop(0, E, per_expert, 0)


def _select(aff_t, t, cap, col_blk):
    ntb = t // SEL_TB
    nsb = cap // SEL_TB + SPILL_BLOCKS
    return pl.pallas_call(
        functools.partial(_sel_kernel, t, cap),
        grid=(1,),
        in_specs=[pl.BlockSpec((E, t), lambda i: (0, col_blk))],
        out_specs=[pl.BlockSpec((E, nsb, N_SLOT_ROWS, SEL_TB), lambda i: (0, 0, 0, 0)),
                   pl.BlockSpec((1, t), lambda i: (0, 0)),
                   pl.BlockSpec((1, t), lambda i: (0, 0))],
        out_shape=[jax.ShapeDtypeStruct((E, nsb, N_SLOT_ROWS, SEL_TB), F32),
                   jax.ShapeDtypeStruct((1, t), F32),
                   jax.ShapeDtypeStruct((1, t), F32)],
        scratch_shapes=[pltpu.VMEM((ntb, 3, E, SEL_TB), F32),
                        pltpu.VMEM((ntb, E + 8, SEL_TB), F32),
                        pltpu.VMEM((E + 8, 128), I32),
                        pltpu.SMEM((E + 8, 128), I32),
                        pltpu.SemaphoreType.DMA(())],
        compiler_params=_params(("arbitrary",)),
        name="select_t%d" % t,
    )(aff_t)


DMA_UNROLL = 16
FFN_ROWS = (768, 512, 256, 128)


def _ffn_kernel(rc, nch, idx_ref, dst_ref, h2_hbm, wcol_ref, wg_ref, wu_ref, wd_ref, z_hbm,
                xbuf, ybuf, wgb, wub, wdb, gsem, ssem):
    c = pl.program_id(1)
    step = pl.program_id(0) * nch + c
    nsteps = E * nch
    slot = step % 2

    def start_gather(s):
        buf = xbuf.at[s % 2]
        sem = gsem.at[s % 2]

        def issue(jb, carry):
            for u in range(DMA_UNROLL):
                j = jb * DMA_UNROLL + u
                src = pl.multiple_of(idx_ref[s * rc + j], ROW_TILES)
                pltpu.make_async_copy(h2_hbm.at[pl.ds(src, ROW_TILES)],
                                      buf.at[pl.ds(j * ROW_TILES, ROW_TILES)], sem).start(priority=u % 2)
            return carry

        lax.fori_loop(0, rc // DMA_UNROLL, issue, 0)

    @pl.when(step == 0)
    def _():
        start_gather(step)

    @pl.when(step + 1 < nsteps)
    def _():
        start_gather(step + 1)

    @pl.when(c == 0)
    def _():
        wgb[...] = wg_ref[0, 0].astype(BF16)
        wub[...] = wu_ref[0, 0].astype(BF16)
        wdb[...] = wd_ref[0, 0].astype(BF16)

    pltpu.make_async_copy(h2_hbm.at[pl.ds(0, rc * ROW_TILES)], xbuf.at[slot], gsem.at[slot]).wait()
    xb = _load_token_tiles(xbuf.at[slot], rc).astype(BF16)
    hid = _silu(_mm(xb, wgb[...])) * _mm(xb, wub[...])
    y = _mm(hid.astype(BF16), wdb[...]) * wcol_ref[...]

    @pl.when(step > 0)
    def _():
        pltpu.make_async_copy(ybuf, z_hbm.at[pl.ds(0, rc * ROW_TILES)], ssem).wait()

    _store_token_tiles(ybuf, y)

    def scatter(jb, carry):
        for u in range(DMA_UNROLL):
            j = jb * DMA_UNROLL + u
            row = pl.multiple_of(dst_ref[step * rc + j], ROW_TILES)
            pltpu.make_async_copy(ybuf.at[pl.ds(j * ROW_TILES, ROW_TILES)],
                                  z_hbm.at[pl.ds(row, ROW_TILES)], ssem).start(priority=u % 2)
        return carry

    lax.fori_loop(0, rc // DMA_UNROLL, scatter, 0)

    @pl.when(step == nsteps - 1)
    def _():
        pltpu.make_async_copy(ybuf, z_hbm.at[pl.ds(0, rc * ROW_TILES)], ssem).wait()


def _expert_ffn(idx, dst, h2, wcol, w_gate, w_up, w_down, layer, rc, nch):
    zrows = E * nch * rc
    wspec = pl.BlockSpec((1, 1, D, D), lambda e, c, *_: (layer, e, 0, 0))
    return pl.pallas_call(
        functools.partial(_ffn_kernel, rc, nch),
        grid_spec=pltpu.PrefetchScalarGridSpec(
            num_scalar_prefetch=2,
            grid=(E, nch),
            in_specs=[pl.BlockSpec(memory_space=pl.ANY),
                      pl.BlockSpec((rc, 1), lambda e, c, *_: (e * nch + c, 0)),
                      wspec, wspec, wspec],
            out_specs=pl.BlockSpec(memory_space=pl.ANY),
            scratch_shapes=[pltpu.VMEM((2, rc * ROW_TILES, 128), F32),
                            pltpu.VMEM((rc * ROW_TILES, 128), F32),
                            pltpu.VMEM((D, D), BF16), pltpu.VMEM((D, D), BF16),
                            pltpu.VMEM((D, D), BF16),
                            pltpu.SemaphoreType.DMA((2,)), pltpu.SemaphoreType.DMA(())]),
        out_shape=jax.ShapeDtypeStruct((zrows * ROW_TILES, 128), F32),
        compiler_params=_params(("arbitrary", "arbitrary")),
        name="expert_ffn",
    )(idx, dst, h2, wcol, w_gate, w_up, w_down)


CMB_TB = 256
CMB_RC = 512
CMB_BUFS = 4


def _comb_kernel(t1, n2, zrows, final, first_ref, nom_ref, x1_ref, lo_ref, hi_ref, mod_ref, fw_ref, z_hbm,
                 *refs):
    outs, (zbuf, sem) = refs[:-2], refs[-2:]
    i = pl.program_id(0)
    nsteps = pl.num_programs(0)
    r = _mod_row(i * CMB_TB, t1, n2)
    total = first_ref[nsteps]

    def chunk_copy(g):
        b = pl.multiple_of(jnp.minimum(nom_ref[g], zrows - CMB_RC) * ROW_TILES, 8 * ROW_TILES)
        slot = g % CMB_BUFS
        return pltpu.make_async_copy(z_hbm.at[pl.ds(b, CMB_RC * ROW_TILES)], zbuf.at[slot], sem.at[slot])

    @pl.when(i == 0)
    def _():
        for g in range(CMB_BUFS - 1):
            @pl.when(g < total)
            def _():
                chunk_copy(g).start()

    eye = _iota((CMB_TB, CMB_TB), 0) == _iota((CMB_TB, CMB_TB), 1)
    lo = jnp.sum(jnp.where(eye, lo_ref[...], 0.0), axis=1, keepdims=True)
    hi = jnp.sum(jnp.where(eye, hi_ref[...], 0.0), axis=1, keepdims=True)
    col = _iota((1, CMB_RC), 1)

    def chunk(g, acc):
        @pl.when(g + CMB_BUFS - 1 < total)
        def _():
            chunk_copy(g + CMB_BUFS - 1).start()

        chunk_copy(g).wait()
        nominal = nom_ref[g]
        rows = (col + jnp.minimum(nominal, zrows - CMB_RC)).astype(F32)
        lo_c = jnp.maximum(lo, nominal.astype(F32))
        s = jnp.where(rows >= lo_c, jnp.where(rows < hi, 1.0, 0.0), 0.0).astype(BF16)
        return acc + _mmb(s, _load_token_tiles(zbuf.at[g % CMB_BUFS], CMB_RC))

    moe = lax.fori_loop(first_ref[i], first_ref[i + 1], chunk, jnp.zeros((CMB_TB, D), F32))

    x2 = x1_ref[...] + mod_ref[pl.ds(r, 1), 5 * D:6 * D] * moe
    if final:
        y = x2 * lax.rsqrt(jnp.mean(x2 * x2, axis=-1, keepdims=True) + EPS) * fw_ref[...]

        @pl.when(i * CMB_TB < t1)
        def _():
            outs[0][...] = y

        @pl.when(i * CMB_TB >= t1)
        def _():
            outs[1][...] = y
    else:
        outs[0][...] = x2


def _combine(lo, hi, x1, mod, fw, z, t1, n2, final):
    t_all = x1.shape[0]
    zrows = z.shape[0] // ROW_TILES
    nsteps = t_all // CMB_TB
    base = (lo[0, ::CMB_TB].astype(I32) // 8) * 8
    end = hi[0, CMB_TB - 1::CMB_TB].astype(I32)
    nchunk = jnp.maximum((end - base + CMB_RC - 1) // CMB_RC, 1)
    first = jnp.concatenate([jnp.zeros((1,), I32), jnp.cumsum(nchunk)])
    max_chunks = zrows // CMB_RC + 2 * nsteps
    g = jnp.arange(max_chunks, dtype=I32)
    step_of = jnp.minimum(jnp.sum(first[None, 1:] <= g[:, None], axis=1), nsteps - 1)
    nominal = base[step_of] + (g - first[step_of]) * CMB_RC
    nb1 = t1 // CMB_TB
    if final:
        out_specs = [pl.BlockSpec((CMB_TB, D), lambda i, *_: (jnp.minimum(i, nb1 - 1), 0)),
                     pl.BlockSpec((CMB_TB, D), lambda i, *_: (jnp.maximum(i - nb1, 0), 0))]
        out_shape = [jax.ShapeDtypeStruct((t1, D), F32), jax.ShapeDtypeStruct((t_all - t1, D), F32)]
    else:
        out_specs = pl.BlockSpec((CMB_TB, D), lambda i, *_: (i, 0))
        out_shape = jax.ShapeDtypeStruct((t_all, D), F32)
    return pl.pallas_call(
        functools.partial(_comb_kernel, t1, n2, zrows, final),
        grid_spec=pltpu.PrefetchScalarGridSpec(
            num_scalar_prefetch=2,
            grid=(t_all // CMB_TB,),
            in_specs=[pl.BlockSpec((CMB_TB, D), lambda i, *_: (i, 0)),
                      pl.BlockSpec((1, CMB_TB), lambda i, *_: (0, i)),
                      pl.BlockSpec((1, CMB_TB), lambda i, *_: (0, i)),
                      pl.BlockSpec((8, NMOD * D), lambda i, *_: (0, 0)),
                      pl.BlockSpec((1, D), lambda i, *_: (0, 0)),
                      pl.BlockSpec(memory_space=pl.ANY)],
            out_specs=out_specs,
            scratch_shapes=[pltpu.VMEM((CMB_BUFS, CMB_RC * ROW_TILES, 128), F32),
                            pltpu.SemaphoreType.DMA((CMB_BUFS,))]),
        out_shape=out_shape,
        compiler_params=_params(("arbitrary",)),
        name="combine",
    )(first, nominal, x1, lo, hi, mod, fw, z)


def _grid_pos_embed(n, d):
    rows = n // GRID_W
    r = np.repeat(np.arange(rows, dtype=np.float64), GRID_W)
    col = np.tile(np.arange(GRID_W, dtype=np.float64), rows)
    quarter = d // 4
    omega = np.power(POS_BASE, -np.arange(quarter, dtype=np.float64) / quarter)
    ra = r[:, None] * omega
    ca = col[:, None] * omega
    return jnp.asarray(np.concatenate([np.sin(ra), np.cos(ra), np.sin(ca), np.cos(ca)], axis=-1), F32)


def _decode_slots(slots):
    e, nsb, rows, tb = slots.shape
    nsb -= SPILL_BLOCKS
    slots = slots[:, :nsb].transpose(0, 2, 1, 3).reshape(e, rows, nsb * tb)
    idx = (slots[:, 0] * 128.0 + slots[:, 1]).astype(I32)
    w = slots[:, 2] + slots[:, 3] + slots[:, 4]
    rank = (slots[:, 5] * 128.0 + slots[:, 6]).astype(I32)
    return idx, w, rank


def kernel(x_prompt, x_sample, state_delta, c, c_ctx, w_ada, b_ada, norm1_w, norm2_w, w_in, dn_conv_w,
           dn_a_log, dn_dt_bias, dn_norm_w, sc_conv_w, w_out, w_router, w_gate, w_up, w_down,
           final_norm_w):
    b1, n1, _ = x_prompt.shape
    b2, n2, _ = x_sample.shape
    depth = w_ada.shape[0]
    t1, t2 = b1 * n1, b2 * n2
    t_all = t1 + t2
    cap1 = max(1, 2 * t1 // E)
    cap2 = max(1, 2 * t2 // E)
    tm = 512
    rc = next(r for r in FFN_ROWS if (cap1 + cap2) % r == 0)
    assert t1 % n2 == 0 and t1 % tm == 0 and n2 % tm == 0
    assert n1 % (DN_POS * CH) == 0 and n2 % (DN_POS * CH) == 0
    assert t1 % SEL_TB == 0 and t2 % SEL_TB == 0 and t1 % t2 == 0
    assert cap1 % SEL_TB == 0 and cap2 % SEL_TB == 0 and 2 * t_all >= CMB_RC
    nch = (cap1 + cap2) // rc
    width = H * DH

    cond8 = jnp.zeros((8, D), F32).at[0].set(c_ctx).at[1:1 + b2].set(c)
    mod = _ada(cond8, w_ada, b_ada)

    x = None
    tabs1 = _dft_tables(n1)
    tabs2 = _dft_tables(n2)
    ctx_states = None

    for l in range(depth):
        wl = w_in[l]
        wm = jnp.concatenate([wl[:, 0:4 * width], wl[:, 4 * width + 4 * H:]], axis=1).astype(BF16)
        wg = jnp.pad(wl[:, 4 * width:4 * width + 4 * H], ((0, 0), (0, 128 - 4 * H))).astype(BF16)
        al = jnp.pad(dn_a_log[l].reshape(1, 2 * H), ((0, 0), (2 * H, 128 - 4 * H)))
        dt = jnp.pad(dn_dt_bias[l].reshape(1, 2 * H), ((0, 0), (2 * H, 128 - 4 * H)))
        xs = (x,) if l else (x_prompt.reshape(t1, D), x_sample.reshape(t2, D), _grid_pos_embed(n2, D))
        res = _in_proj(xs, mod[l], norm1_w[l].reshape(1, D), wm, wg, al, dt, t1, n2, tm)
        if not l:
            x, res = res[0], res[1:]
        p, g, gt = res

        nwd = dn_norm_w[l].reshape(1, DH)
        dn1, ctx_states = _deltanet(p, g, gt, dn_conv_w[l], nwd, None, ctx_states, b1, n1, 0)
        dn2, _ = _deltanet(p, g, gt, dn_conv_w[l], nwd, state_delta[:, l], None, b2, n2, t1 // n2)
        mx1 = _mix2(p, sc_conv_w[l], tabs1[1], tabs1[2], tabs1[0], b1, n1, 0, MIX_GROUP)
        mx2 = _mix2(p, sc_conv_w[l], tabs2[1], tabs2[2], tabs2[0], b2, n2, t1 // n2, 1)

        wr = jnp.pad(w_router[l], ((0, 0), (0, 128 - E)))
        x1, h2, aff_t = _out_proj(dn1, dn2, mx1, mx2, x, mod[l], norm2_w[l].reshape(1, D),
                                  w_out[l].astype(BF16), wr, t1, n2, tm)

        slots1, lo1, hi1 = _select(aff_t, t1, cap1, 0)
        slots2, lo2, hi2 = _select(aff_t, t2, cap2, t1 // t2)
        idx1, wsel1, rank1 = _decode_slots(slots1)
        idx2, wsel2, rank2 = _decode_slots(slots2)
        idx = jnp.concatenate([idx1, idx2 + t1], axis=1).reshape(-1) * ROW_TILES
        dst = jnp.concatenate([rank1, rank2 + 2 * t1], axis=1).reshape(-1) * ROW_TILES
        wcol = jnp.concatenate([wsel1, wsel2], axis=1).reshape(-1, 1)
        z = _expert_ffn(idx, dst, h2, wcol, w_gate, w_up, w_down, l, rc, nch)

        lo = jnp.concatenate([lo1, lo2 + 2.0 * t1], axis=1)
        hi = jnp.concatenate([hi1, hi2 + 2.0 * t1], axis=1)
        x = _combine(lo, hi, x1, mod[l], final_norm_w.reshape(1, D), z, t1, n2, l == depth - 1)

    y_prompt, y_sample = x
    return y_prompt.reshape(b1, n1, D), y_sample.reshape(b2, n2, D), ctx_states
```

```python
import functools
import math

import jax
import jax.numpy as jnp
import numpy as np
from jax import lax
from jax.experimental import pallas as pl
from jax.experimental.pallas import tpu as pltpu

F32 = jnp.float32
BF16 = jnp.bfloat16
I32 = jnp.int32
HI = lax.Precision.HIGHEST

D = 1024
H = 4
DH = 128
CH = 64
E = 16
NMOD = 6
EPS = 1e-6
GRID_W = 64
POS_BASE = 10000.0
P_COLS = 3072
P_A_COLS = 2048
VMEM_LIMIT = 56 * 1024 * 1024

NN = (((1,), (0,)), ((), ()))
NT = (((1,), (1,)), ((), ()))
TN = (((0,), (0,)), ((), ()))


def _mm(a, b, dims=NN, prec=None):
    return lax.dot_general(a, b, dims, precision=prec, preferred_element_type=F32)


def _mmb(a, b, dims=NN):
    return lax.dot_general(a.astype(BF16), b.astype(BF16), dims, preferred_element_type=F32)


def _mm3(a, b, dims=NN):
    a_hi = a.astype(BF16)
    a_lo = (a - a_hi.astype(F32)).astype(BF16)
    b_hi = b.astype(BF16)
    b_lo = (b - b_hi.astype(F32)).astype(BF16)
    return _mm(a_hi, b_hi, dims) + _mm(a_hi, b_lo, dims) + _mm(a_lo, b_hi, dims)


def _mm_inv(a, b):
    return _mmb(a, b)


def _silu(x):
    return x / (1.0 + jnp.exp(-x))


def _sigmoid(x):
    return 1.0 / (1.0 + jnp.exp(-x))


def _softplus(x):
    return jnp.maximum(x, 0.0) + jnp.log1p(jnp.exp(-jnp.abs(x)))


def _iota(shape, dim):
    return lax.broadcasted_iota(I32, shape, dim)


ROW_TILES = D // 128


def _load_token_tiles(ref, n):
    return jnp.concatenate([ref[pl.ds(s, n, stride=ROW_TILES), :] for s in range(ROW_TILES)], axis=1)


def _store_token_tiles(ref, x):
    n = x.shape[0]
    for s in range(ROW_TILES):
        ref[pl.ds(s, n, stride=ROW_TILES), :] = x[:, s * 128:(s + 1) * 128]


def _mod_row(t0, t1, n2):
    return jnp.where(t0 < t1, 0, 1 + jnp.maximum(t0 - t1, 0) // n2)


def _params(sem):
    return pltpu.CompilerParams(dimension_semantics=sem, vmem_limit_bytes=VMEM_LIMIT)


def _ada_kernel(c_ref, w_ref, b_ref, o_ref):
    s = _silu(c_ref[...])
    o_ref[0] = _mm3(s, w_ref[0]) + b_ref[0]


def _ada(cond8, w_ada, b_ada):
    depth = w_ada.shape[0]
    tn = 1536
    return pl.pallas_call(
        _ada_kernel,
        grid=(depth, NMOD * D // tn),
        in_specs=[pl.BlockSpec((8, D), lambda l, j: (0, 0)),
                  pl.BlockSpec((1, D, tn), lambda l, j: (l, 0, j)),
                  pl.BlockSpec((1, 1, tn), lambda l, j: (l, 0, j))],
        out_specs=pl.BlockSpec((1, 8, tn), lambda l, j: (l, 0, j)),
        out_shape=jax.ShapeDtypeStruct((depth, 8, NMOD * D), F32),
        compiler_params=_params(("arbitrary", "arbitrary")),
        name="ada",
    )(cond8, w_ada, b_ada.reshape(depth, 1, NMOD * D))


IN_SUB = 2


def _in_core(tm, r, x, mod_ref, nw_ref, wa_ref, wb_ref, wg_ref, al_ref, dt_ref, p_ref, g_ref, gt_ref):
    sh = mod_ref[pl.ds(r, 1), 0:D]
    sc = mod_ref[pl.ds(r, 1), D:2 * D]
    sub = tm // IN_SUB
    lane = _iota((1, 128), 1)
    hbs = []
    for k in range(IN_SUB):
        xk = x[k * sub:(k + 1) * sub]
        y = xk * lax.rsqrt(jnp.mean(xk * xk, axis=-1, keepdims=True) + EPS) * nw_ref[...]
        hbs.append((y * (1.0 + sc) + sh).astype(BF16))
    for k in range(IN_SUB):
        rs = slice(k * sub, (k + 1) * sub)
        for j in range(P_A_COLS // D):
            p_ref[rs, j * D:(j + 1) * D] = _mm(hbs[k], wa_ref[:, j * D:(j + 1) * D])
        p_ref[rs, P_A_COLS:P_COLS] = _mm(hbs[k], wb_ref[...])
    for k in range(IN_SUB):
        raw = _mm(hbs[k], wg_ref[...])
        g = -jnp.exp(al_ref[...]) * _softplus(raw + dt_ref[...])
        act = jnp.where(lane < 2 * H, _sigmoid(raw), g)
        g_ref[k * sub:(k + 1) * sub, :] = act[:, 0:4 * H]
        act_t = act.T
        for j in range(sub // CH):
            gt_ref[k * (sub // CH) + j] = act_t[0:4 * H, j * CH:(j + 1) * CH]


def _in_kernel(t1, n2, tm, x_ref, *refs):
    r = _mod_row(pl.program_id(0) * tm, t1, n2)
    _in_core(tm, r, x_ref[...], *refs)


def _in_first_kernel(t1, n2, tm, xp_ref, xs_ref, pos_ref, mod_ref, nw_ref, wa_ref, wb_ref, wg_ref, al_ref,
                     dt_ref, x_ref, p_ref, g_ref, gt_ref):
    t0 = pl.program_id(0) * tm
    x = jnp.where(t0 < t1, xp_ref[...], xs_ref[...] + pos_ref[...])
    x_ref[...] = x
    _in_core(tm, _mod_row(t0, t1, n2), x, mod_ref, nw_ref, wa_ref, wb_ref, wg_ref, al_ref, dt_ref,
             p_ref, g_ref, gt_ref)


def _in_proj(xs, mod, nw, wa, wb, wg, al, dt, t1, n2, tm):
    first = len(xs) == 3
    t_all = t1 + xs[1].shape[0] if first else xs[0].shape[0]
    nt1 = t1 // tm
    wspecs = [pl.BlockSpec((8, NMOD * D), lambda i: (0, 0)),
              pl.BlockSpec((1, D), lambda i: (0, 0)),
              pl.BlockSpec((D, P_A_COLS), lambda i: (0, 0)),
              pl.BlockSpec((D, P_COLS - P_A_COLS), lambda i: (0, 0)),
              pl.BlockSpec((D, 128), lambda i: (0, 0)),
              pl.BlockSpec((1, 128), lambda i: (0, 0)),
              pl.BlockSpec((1, 128), lambda i: (0, 0))]
    out_specs = [pl.BlockSpec((tm, P_COLS), lambda i: (i, 0)),
                 pl.BlockSpec((tm, 4 * H), lambda i: (i, 0)),
                 pl.BlockSpec((tm // CH, 4 * H, CH), lambda i: (i, 0, 0))]
    out_shape = [jax.ShapeDtypeStruct((t_all, P_COLS), F32),
                 jax.ShapeDtypeStruct((t_all, 4 * H), F32),
                 jax.ShapeDtypeStruct((t_all // CH, 4 * H, CH), F32)]
    if first:
        kern = _in_first_kernel
        xspecs = [pl.BlockSpec((tm, D), lambda i: (jnp.minimum(i, nt1 - 1), 0)),
                  pl.BlockSpec((tm, D), lambda i: (jnp.maximum(i - nt1, 0), 0)),
                  pl.BlockSpec((tm, D), lambda i: (jnp.maximum(i - nt1, 0) % (n2 // tm), 0))]
        out_specs = [pl.BlockSpec((tm, D), lambda i: (i, 0))] + out_specs
        out_shape = [jax.ShapeDtypeStruct((t_all, D), F32)] + out_shape
    else:
        kern = _in_kernel
        xspecs = [pl.BlockSpec((tm, D), lambda i: (i, 0))]
    return pl.pallas_call(
        functools.partial(kern, t1, n2, tm),
        grid=(t_all // tm,),
        in_specs=xspecs + wspecs,
        out_specs=out_specs,
        out_shape=out_shape,
        compiler_params=_params(("arbitrary",)),
        name="in_proj_first" if first else "in_proj",
    )(*xs, mod, nw, wa, wb, wg, al, dt)


def _shift_conv(x, w_ref, c0, c1, period=None):
    n = x.shape[0]
    row = _iota((n, 1), 0) & ((period or n) - 1)
    prev = jnp.where(row == 0, 0.0, pltpu.roll(x, 1, 0))
    nxt = jnp.where(row == (period or n) - 1, 0.0, pltpu.roll(x, n - 1, 0))
    return prev * w_ref[0:1, c0:c1] + x * w_ref[1:2, c0:c1] + nxt * w_ref[2:3, c0:c1]


DN_POS = 4


def _dn_kernel(n, nprev, has_s0, q_ref, k_ref, v_ref, z_ref, g_ref, gt_ref, cw_ref, nw_ref, *refs):
    refs = list(refs)
    s0_ref = refs.pop(0) if has_s0 else None
    prev_ref = refs.pop(0) if nprev else None
    o_ref, sfin_ref, qs, ks, vs, gcc_s, gct_s, s_s, oacc = refs
    nc = n // CH
    width = H * DH

    for hh in range(H):
        c0, c1 = hh * DH, (hh + 1) * DH
        q = _silu(_shift_conv(q_ref[:, c0:c1], cw_ref, c0, c1))
        q = q * lax.rsqrt(jnp.sum(q * q, axis=-1, keepdims=True) + 1e-6)
        qs[:, c0:c1] = q * (DH ** -0.5)
        k = _silu(_shift_conv(k_ref[:, c0:c1], cw_ref, width + c0, width + c1))
        ks[:, c0:c1] = k * lax.rsqrt(jnp.sum(k * k, axis=-1, keepdims=True) + 1e-6)
        vs[:, c0:c1] = _silu(_shift_conv(v_ref[:, c0:c1], cw_ref, 2 * width + c0, 2 * width + c1))

    ii = _iota((CH, CH), 0)
    jj = _iota((CH, CH), 1)
    x = gt_ref[...].reshape(nc * 4 * H, CH)
    cf = _mm(x, (ii <= jj).astype(F32), prec=HI)
    cb = _mm(x, (ii >= jj).astype(F32), prec=HI)
    rsel = (_iota((nc * 4 * H, 1), 0) & (4 * H - 1)) < 3 * H
    gct_s[...] = jnp.where(rsel, cf, cb).reshape(nc, 4 * H, CH)
    i2 = _iota((2 * CH, 2 * CH), 0)
    j2 = _iota((2 * CH, 2 * CH), 1)
    same = (i2 >= CH) == (j2 >= CH)
    lf = (same & (i2 >= j2)).astype(F32)
    lb = (same & (i2 <= j2)).astype(F32)
    lsel = _iota((1, 4 * H), 1) < 3 * H
    for m in range(n // (2 * CH)):
        y = g_ref[m * 2 * CH:(m + 1) * 2 * CH, :]
        gcc_s[m * 2 * CH:(m + 1) * 2 * CH, :] = jnp.where(lsel, _mm(lf, y, prec=HI), _mm(lb, y, prec=HI))

    for hh in range(H):
        for d in range(2):
            s_s[hh * 2 + d] = s0_ref[0, d, hh] if has_s0 else jnp.zeros((DH, DH), F32)

    eye = (ii == jj).astype(F32)
    blk_same = [(ii >> s) == (jj >> s) for s in range(1, 7)]
    merge_mask = [blk_same[lvl] & ~blk_same[lvl - 1] for lvl in range(1, len(blk_same))]

    chains = [(hh, d) for hh in range(H) for d in range(2)]

    def chunk_step(cc, carry):
        st = []
        for p in range(DN_POS):
            for hh, d in chains:
                c = cc * DN_POS + p
                cidx = c if d == 0 else nc - 1 - c
                r0 = pl.multiple_of(cidx * CH, CH)
                c0, c1 = hh * DH, (hh + 1) * DH
                gcol = 2 * H + H * d + hh
                bcol = H * d + hh
                incl = (ii >= jj) if d == 0 else (ii <= jj)
                strict = (ii > jj) if d == 0 else (ii < jj)
                qc = qs[pl.ds(r0, CH), c0:c1]
                kc = ks[pl.ds(r0, CH), c0:c1]
                vc = vs[pl.ds(r0, CH), c0:c1]
                beta = g_ref[pl.ds(r0, CH), bcol:bcol + 1]
                gc = gcc_s[pl.ds(r0, CH), gcol:gcol + 1]
                gr = gct_s[cidx, gcol:gcol + 1, :]
                tot = gr[:, CH - 1:CH] if d == 0 else gr[:, 0:1]
                decay = jnp.where(incl, jnp.exp(jnp.where(incl, gc - gr, 0.0)), 0.0)
                kb = kc * beta
                egc = jnp.exp(gc)
                st.append(dict(r0=r0, c0=c0, c1=c1, d=d, strict=strict, kc=kc, kb=kb, decay=decay,
                               rhs=jnp.concatenate([vc * beta, kb * egc], axis=1),
                               qd=qc * egc, qc=qc, kd=kc * jnp.exp(tot - gc), gl=jnp.exp(tot)))
        for x in st:
            x["lmat"] = jnp.where(x["strict"], _mmb(x["kb"], x["kc"], NT) * x["decay"], 0.0)
            x["qk"] = _mmb(x["qc"], x["kc"], NT) * x["decay"]
            x["tinv"] = eye - jnp.where(blk_same[0], x["lmat"], 0.0)
        for mask in merge_mask:
            for x in st:
                x["ot"] = _mm_inv(jnp.where(mask, x["lmat"], 0.0), x["tinv"])
            for x in st:
                x["tinv"] = x["tinv"] - _mm_inv(x["tinv"], x["ot"])
        for x in st:
            x["sol"] = _mm_inv(x["tinv"], x["rhs"])
        state = [s_s[i] for i in range(len(chains))]
        for p in range(DN_POS):
            units = st[p * len(chains):(p + 1) * len(chains)]
            for i, x in enumerate(units):
                x["sb"] = state[i].astype(BF16)
                x["v_new"] = x["sol"][:, 0:DH] - _mmb(x["sol"][:, DH:2 * DH], x["sb"])
            for i, x in enumerate(units):
                o = _mmb(x["qd"], x["sb"]) + _mmb(x["qk"], x["v_new"])
                state[i] = state[i] * x["gl"] + _mmb(x["kd"], x["v_new"], TN)
                oacc[x["d"], pl.ds(x["r0"], CH), x["c0"]:x["c1"]] = o
        for i in range(len(chains)):
            s_s[i] = state[i]
        return carry

    lax.fori_loop(0, nc // DN_POS, chunk_step, 0)

    for hh in range(H):
        c0, c1 = hh * DH, (hh + 1) * DH
        o = oacc[0, :, c0:c1] + oacc[1, :, c0:c1]
        o = o * lax.rsqrt(jnp.mean(o * o, axis=-1, keepdims=True) + EPS)
        o_ref[:, c0:c1] = (o * nw_ref[...] * _silu(z_ref[:, c0:c1])).astype(o_ref.dtype)
        for d in range(2):
            sfin_ref[0, nprev, d, hh] = s_s[hh * 2 + d]
    for layer in range(nprev):
        sfin_ref[0, layer] = prev_ref[0, layer]


def _deltanet(p, g, gt, conv_w, norm_w, s0, prev, bsz, n, row_blk0):
    nc = n // CH
    width = H * DH
    nprev = 0 if prev is None else prev.shape[1]
    state_spec = lambda k: pl.BlockSpec((1, k, 2, H, DH, DH), lambda b: (b, 0, 0, 0, 0, 0))
    opt_args, opt_specs = [], []
    if s0 is not None:
        opt_args.append(s0)
        opt_specs.append(pl.BlockSpec((1, 2, H, DH, DH), lambda b: (b, 0, 0, 0, 0)))
    if prev is not None:
        opt_args.append(prev)
        opt_specs.append(state_spec(nprev))
    return pl.pallas_call(
        functools.partial(_dn_kernel, n, nprev, s0 is not None),
        grid=(bsz,),
        in_specs=[pl.BlockSpec((n, width), lambda b: (row_blk0 + b, 0)),
                  pl.BlockSpec((n, width), lambda b: (row_blk0 + b, 1)),
                  pl.BlockSpec((n, width), lambda b: (row_blk0 + b, 2)),
                  pl.BlockSpec((n, width), lambda b: (row_blk0 + b, 3)),
                  pl.BlockSpec((n, 4 * H), lambda b: (row_blk0 + b, 0)),
                  pl.BlockSpec((nc, 4 * H, CH), lambda b: (row_blk0 + b, 0, 0)),
                  pl.BlockSpec((3, 3 * width), lambda b: (0, 0)),
                  pl.BlockSpec((1, DH), lambda b: (0, 0))] + opt_specs,
        out_specs=[pl.BlockSpec((n, width), lambda b: (b, 0)), state_spec(nprev + 1)],
        out_shape=[jax.ShapeDtypeStruct((bsz * n, width), BF16),
                   jax.ShapeDtypeStruct((bsz, nprev + 1, 2, H, DH, DH), F32)],
        scratch_shapes=[pltpu.VMEM((n, width), F32), pltpu.VMEM((n, width), F32),
                        pltpu.VMEM((n, width), F32), pltpu.VMEM((n, 4 * H), F32),
                        pltpu.VMEM((nc, 4 * H, CH), F32), pltpu.VMEM((2 * H, DH, DH), F32),
                        pltpu.VMEM((2, n, width), F32)],
        compiler_params=_params(("arbitrary",)),
        name="deltanet_n%d" % n,
    )(p, p, p, p, g, gt, conv_w, norm_w, *opt_args)


MIX_GROUP = 4


def _mix2_kernel(n, sb_ref, sc_ref, su_ref, fu_ref, cw_ref, bdc_ref, bds_ref, cs_ref, o_ref):
    rows, w = sb_ref.shape
    cu = sc_ref[...] * su_ref[...]
    o_ref[:, 0:w] = (sb_ref[...] * _shift_conv(cu, cw_ref, 0, w, n)).astype(o_ref.dtype)
    fu = fu_ref[...]
    a = _mm3(fu, bdc_ref[...])
    b = _mm3(fu, bds_ref[...])
    for q in range(rows // n):
        rs = slice(q * n, (q + 1) * n)
        fo = _mmb(cs_ref[...], jnp.concatenate([a[rs], b[rs]], axis=0))
        o_ref[rs, w:2 * w] = fo.astype(o_ref.dtype)


def _mix2(p, conv_w, bdc, bds, cs, bsz, n, row_blk0, group):
    w = 256
    rows = group * n
    blk0 = row_blk0 // group
    assert bsz % group == 0 and row_blk0 % group == 0
    return pl.pallas_call(
        functools.partial(_mix2_kernel, n),
        grid=(bsz // group,),
        in_specs=[pl.BlockSpec((rows, w), lambda b: (blk0 + b, 8)),
                  pl.BlockSpec((rows, w), lambda b: (blk0 + b, 9)),
                  pl.BlockSpec((rows, w), lambda b: (blk0 + b, 10)),
                  pl.BlockSpec((rows, w), lambda b: (blk0 + b, 11)),
                  pl.BlockSpec((3, w), lambda b: (0, 0)),
                  pl.BlockSpec((w, w), lambda b: (0, 0)),
                  pl.BlockSpec((w, w), lambda b: (0, 0)),
                  pl.BlockSpec((n, 2 * n), lambda b: (0, 0))],
        out_specs=pl.BlockSpec((rows, 2 * w), lambda b: (b, 0)),
        out_shape=jax.ShapeDtypeStruct((bsz * n, 2 * w), BF16),
        compiler_params=_params(("arbitrary",)),
        name="conv_fourier_n%d" % n,
    )(p, p, p, p, conv_w, bdc, bds, cs)


def _dft_tables(n):
    gw = 64
    k = np.arange(n, dtype=np.int64)
    ang = ((k[:, None] * k[None, :]) % n) * (2.0 * math.pi / n)
    scale = 1.0 / math.sqrt(n * gw)
    cs = np.concatenate([np.cos(ang), -np.sin(ang)], axis=1) * scale
    c = np.arange(256, dtype=np.int64)
    angc = (((c[:, None] % gw) * (c[None, :] % gw)) % gw) * (2.0 * math.pi / gw)
    same = (c[:, None] // gw) == (c[None, :] // gw)
    bdc = np.where(same, np.cos(angc), 0.0)
    bds = np.where(same, np.sin(angc), 0.0)
    return jnp.asarray(cs, F32).astype(BF16), jnp.asarray(bdc, F32), jnp.asarray(bds, F32)


OUT_SUB = 2


def _out_kernel(t1, n2, tm, dn1_ref, dn2_ref, mx1_ref, mx2_ref, x_ref, mod_ref, nw_ref, wo_ref, wr_ref,
                x1_ref, h2_ref, aff_ref):
    t0 = pl.program_id(0) * tm
    r = _mod_row(t0, t1, n2)
    half = D // 2
    sub = tm // OUT_SUB
    mixes = []
    for k in range(OUT_SUB):
        rs = slice(k * sub, (k + 1) * sub)
        dn = jnp.where(t0 < t1, dn1_ref[rs, :], dn2_ref[rs, :])
        mx = jnp.where(t0 < t1, mx1_ref[rs, :], mx2_ref[rs, :])
        mixes.append(_mmb(dn, wo_ref[0:half, :]) + _mmb(mx, wo_ref[half:D, :]))
    h2s = []
    for k in range(OUT_SUB):
        rs = slice(k * sub, (k + 1) * sub)
        x1 = x_ref[rs, :] + mod_ref[pl.ds(r, 1), 2 * D:3 * D] * mixes[k]
        x1_ref[rs, :] = x1
        y = x1 * lax.rsqrt(jnp.mean(x1 * x1, axis=-1, keepdims=True) + EPS) * nw_ref[...]
        h2 = y * (1.0 + mod_ref[pl.ds(r, 1), 4 * D:5 * D]) + mod_ref[pl.ds(r, 1), 3 * D:4 * D]
        _store_token_tiles(h2_ref.at[pl.ds(k * sub * ROW_TILES, sub * ROW_TILES)], h2)
        h2s.append(h2)
    logits = [_mm3(h2, wr_ref[...]) for h2 in h2s]
    for k in range(OUT_SUB):
        lt = logits[k].T[0:E, :]
        ex = jnp.exp(lt - jnp.max(lt, axis=0, keepdims=True))
        aff_ref[:, k * sub:(k + 1) * sub] = ex / jnp.sum(ex, axis=0, keepdims=True)


def _out_proj(dn1, dn2, mx1, mx2, x, mod, nw, wo, wr, t1, n2, tm):
    t_all = x.shape[0]
    nt1 = t1 // tm
    spec1 = pl.BlockSpec((tm, D // 2), lambda i: (jnp.minimum(i, nt1 - 1), 0))
    spec2 = pl.BlockSpec((tm, D // 2), lambda i: (jnp.maximum(i - nt1, 0), 0))
    return pl.pallas_call(
        functools.partial(_out_kernel, t1, n2, tm),
        grid=(t_all // tm,),
        in_specs=[spec1, spec2, spec1, spec2,
                  pl.BlockSpec((tm, D), lambda i: (i, 0)),
                  pl.BlockSpec((8, NMOD * D), lambda i: (0, 0)),
                  pl.BlockSpec((1, D), lambda i: (0, 0)),
                  pl.BlockSpec((D, D), lambda i: (0, 0)),
                  pl.BlockSpec((D, 128), lambda i: (0, 0))],
        out_specs=[pl.BlockSpec((tm, D), lambda i: (i, 0)),
                   pl.BlockSpec((tm * ROW_TILES, 128), lambda i: (i, 0)),
                   pl.BlockSpec((E, tm), lambda i: (0, i))],
        out_shape=[jax.ShapeDtypeStruct((t_all, D), F32),
                   jax.ShapeDtypeStruct((t_all * ROW_TILES, 128), F32),
                   jax.ShapeDtypeStruct((E, t_all), F32)],
        compiler_params=_params(("arbitrary",)),
        name="out_proj",
    )(dn1, dn2, mx1, mx2, x, mod, nw, wo, wr)


SEL_TB = 128
N_SLOT_ROWS = 16
SPILL_BLOCKS = 2


def _sel_kernel(t, cap, aff_ref, slots_ref, lo_ref, hi_ref, a3, c3, starts_v, starts_s, sem):
    ntb = t // SEL_TB
    aff = aff_ref[...]

    def search(i, thr):
        cand = thr | jnp.left_shift(jnp.int32(1), 30 - i)
        cnt = jnp.sum((aff >= pltpu.bitcast(cand, F32)).astype(F32), axis=1, keepdims=True)
        return jnp.where(cnt >= cap, cand, thr)

    thr = lax.fori_loop(0, 31, search, jnp.zeros((E, 1), I32))
    gt = (aff >= pltpu.bitcast(thr + 1, F32)).astype(F32)
    eq = (aff >= pltpu.bitcast(thr, F32)).astype(F32) - gt
    need = cap - jnp.sum(gt, axis=1, keepdims=True)

    ui = _iota((SEL_TB, SEL_TB), 0)
    uj = _iota((SEL_TB, SEL_TB), 1)
    upper = (ui < uj).astype(BF16)
    blk_lane = _iota((1, 128), 1)

    def excl_cumsum(rows, dst, r0, r1):
        carry = jnp.zeros((rows.shape[0], 1), F32)
        starts = jnp.zeros((rows.shape[0], 128), F32)
        for j in range(ntb):
            blk = rows[:, j * SEL_TB:(j + 1) * SEL_TB]
            dst[j, r0:r1, :] = _mm(blk.astype(BF16), upper) + carry
            starts = jnp.where(blk_lane == j, carry, starts)
            carry = carry + jnp.sum(blk, axis=1, keepdims=True)
        return starts

    excl_cumsum(eq, c3, 0, E)
    rank_eq = jnp.concatenate([c3[j, 0:E, :] for j in range(ntb)], axis=1)
    sel = jnp.maximum(gt, jnp.where(rank_eq < need, eq, 0.0))
    n_tok = jnp.sum(sel, axis=0, keepdims=True)
    starts = excl_cumsum(jnp.concatenate([sel, jnp.broadcast_to(n_tok, (8, t))], axis=0), c3, 0, E + 8)
    starts_v[...] = starts.astype(I32)
    to_smem = pltpu.make_async_copy(starts_v, starts_s, sem)
    to_smem.start()
    ei = _iota((E, E), 0)
    ej = _iota((E, E), 1)
    below = _mm((ej < ei).astype(BF16), sel.astype(BF16))
    for j in range(ntb):
        sl = slice(j * SEL_TB, (j + 1) * SEL_TB)
        off = c3[j, E:E + 1, :]
        lo_ref[:, sl] = off
        hi_ref[:, sl] = off + n_tok[:, sl]
        rank = off + below[:, sl]
        a3[j, 0] = sel[:, sl]
        a3[j, 1] = aff[:, sl]
        a3[j, 2] = rank
    slots_ref[...] = jnp.zeros(slots_ref.shape, F32)
    to_smem.wait()

    win_iota = _iota((2 * SEL_TB, SEL_TB), 0)
    tok_iota = _iota((1, SEL_TB), 1)

    def per_expert(e, carry):
        def per_block(j, carry2):
            wb = starts_s[e, j] >> 7
            chosen = a3[j, 0, pl.ds(e, 1), :]
            pos = c3[j, pl.ds(e, 1), :].astype(I32) - wb * SEL_TB
            pos = jnp.where(chosen > 0.0, pos, -1)
            w = a3[j, 1, pl.ds(e, 1), :]
            rank = a3[j, 2, pl.ds(e, 1), :].astype(I32)
            tok = tok_iota + j * SEL_TB
            w_hi = w.astype(BF16).astype(F32)
            w_mid = (w - w_hi).astype(BF16).astype(F32)
            w_lo = w - w_hi - w_mid
            vals = jnp.concatenate(
                [(tok >> 7).astype(F32), (tok & 127).astype(F32), w_hi, w_mid, w_lo,
                 (rank >> 7).astype(F32), (rank & 127).astype(F32),
                 jnp.zeros((N_SLOT_ROWS - 7, SEL_TB), F32)], axis=0)
            onehot = jnp.where(win_iota == pos, 1.0, 0.0)
            placed = _mmb(vals, onehot, NT)
            slots_ref[e, wb] += placed[:, 0:SEL_TB]
            slots_ref[e, wb + 1] += placed[:, SEL_TB:2 * SEL_TB]
            return carry2
        return lax.fori_loop(0, ntb, per_block, carry, unroll=16)

    lax.fori_loop(0, E, per_expert, 0)


def _select(aff_t, t, cap, col_blk):
    ntb = t // SEL_TB
    nsb = cap // SEL_TB + SPILL_BLOCKS
    return pl.pallas_call(
        functools.partial(_sel_kernel, t, cap),
        grid=(1,),
        in_specs=[pl.BlockSpec((E, t), lambda i: (0, col_blk))],
        out_specs=[pl.BlockSpec((E, nsb, N_SLOT_ROWS, SEL_TB), lambda i: (0, 0, 0, 0)),
                   pl.BlockSpec((1, t), lambda i: (0, 0)),
                   pl.BlockSpec((1, t), lambda i: (0, 0))],
        out_shape=[jax.ShapeDtypeStruct((E, nsb, N_SLOT_ROWS, SEL_TB), F32),
                   jax.ShapeDtypeStruct((1, t), F32),
                   jax.ShapeDtypeStruct((1, t), F32)],
        scratch_shapes=[pltpu.VMEM((ntb, 3, E, SEL_TB), F32),
                        pltpu.VMEM((ntb, E + 8, SEL_TB), F32),
                        pltpu.VMEM((E + 8, 128), I32),
                        pltpu.SMEM((E + 8, 128), I32),
                        pltpu.SemaphoreType.DMA(())],
        compiler_params=_params(("arbitrary",)),
        name="select_t%d" % t,
    )(aff_t)


DMA_UNROLL = 16


def _ffn_kernel(rc, nch, idx_ref, dst_ref, h2_hbm, wcol_ref, wg_ref, wu_ref, wd_ref, z_hbm,
                xbuf, ybuf, wgb, wub, wdb, gsem, ssem):
    c = pl.program_id(1)
    step = pl.program_id(0) * nch + c
    nsteps = E * nch
    slot = step % 2

    def start_gather(s):
        buf = xbuf.at[s % 2]
        sem = gsem.at[s % 2]

        def issue(jb, carry):
            for u in range(DMA_UNROLL):
                j = jb * DMA_UNROLL + u
                src = pl.multiple_of(idx_ref[s * rc + j], ROW_TILES)
                pltpu.make_async_copy(h2_hbm.at[pl.ds(src, ROW_TILES)],
                                      buf.at[pl.ds(j * ROW_TILES, ROW_TILES)], sem).start(priority=u % 2)
            return carry

        lax.fori_loop(0, rc // DMA_UNROLL, issue, 0)

    @pl.when(step == 0)
    def _():
        start_gather(step)

    @pl.when(step + 1 < nsteps)
    def _():
        start_gather(step + 1)

    @pl.when(c == 0)
    def _():
        wgb[...] = wg_ref[0, 0].astype(BF16)
        wub[...] = wu_ref[0, 0].astype(BF16)
        wdb[...] = wd_ref[0, 0].astype(BF16)

    pltpu.make_async_copy(h2_hbm.at[pl.ds(0, rc * ROW_TILES)], xbuf.at[slot], gsem.at[slot]).wait()
    xb = _load_token_tiles(xbuf.at[slot], rc).astype(BF16)
    hid = _silu(_mm(xb, wgb[...])) * _mm(xb, wub[...])
    y = _mm(hid.astype(BF16), wdb[...]) * wcol_ref[...]

    @pl.when(step > 0)
    def _():
        pltpu.make_async_copy(ybuf, z_hbm.at[pl.ds(0, rc * ROW_TILES)], ssem).wait()

    _store_token_tiles(ybuf, y)

    def scatter(jb, carry):
        for u in range(DMA_UNROLL):
            j = jb * DMA_UNROLL + u
            row = pl.multiple_of(dst_ref[step * rc + j], ROW_TILES)
            pltpu.make_async_copy(ybuf.at[pl.ds(j * ROW_TILES, ROW_TILES)],
                                  z_hbm.at[pl.ds(row, ROW_TILES)], ssem).start(priority=u % 2)
        return carry

    lax.fori_loop(0, rc // DMA_UNROLL, scatter, 0)

    @pl.when(step == nsteps - 1)
    def _():
        pltpu.make_async_copy(ybuf, z_hbm.at[pl.ds(0, rc * ROW_TILES)], ssem).wait()


def _expert_ffn(idx, dst, h2, wcol, w_gate, w_up, w_down, layer, rc, nch):
    zrows = E * nch * rc
    wspec = pl.BlockSpec((1, 1, D, D), lambda e, c, *_: (layer, e, 0, 0))
    return pl.pallas_call(
        functools.partial(_ffn_kernel, rc, nch),
        grid_spec=pltpu.PrefetchScalarGridSpec(
            num_scalar_prefetch=2,
            grid=(E, nch),
            in_specs=[pl.BlockSpec(memory_space=pl.ANY),
                      pl.BlockSpec((rc, 1), lambda e, c, *_: (e * nch + c, 0)),
                      wspec, wspec, wspec],
            out_specs=pl.BlockSpec(memory_space=pl.ANY),
            scratch_shapes=[pltpu.VMEM((2, rc * ROW_TILES, 128), F32),
                            pltpu.VMEM((rc * ROW_TILES, 128), F32),
                            pltpu.VMEM((D, D), BF16), pltpu.VMEM((D, D), BF16),
                            pltpu.VMEM((D, D), BF16),
                            pltpu.SemaphoreType.DMA((2,)), pltpu.SemaphoreType.DMA(())]),
        out_shape=jax.ShapeDtypeStruct((zrows * ROW_TILES, 128), F32),
        compiler_params=_params(("arbitrary", "arbitrary")),
        name="expert_ffn",
    )(idx, dst, h2, wcol, w_gate, w_up, w_down)


CMB_TB = 256
CMB_RC = 512
CMB_BUFS = 4


def _comb_kernel(t1, n2, zrows, final, first_ref, nom_ref, x1_ref, lo_ref, hi_ref, mod_ref, fw_ref, z_hbm,
                 *refs):
    outs, (zbuf, sem) = refs[:-2], refs[-2:]
    i = pl.program_id(0)
    nsteps = pl.num_programs(0)
    r = _mod_row(i * CMB_TB, t1, n2)
    total = first_ref[nsteps]

    def chunk_copy(g):
        b = pl.multiple_of(jnp.minimum(nom_ref[g], zrows - CMB_RC) * ROW_TILES, 8 * ROW_TILES)
        slot = g % CMB_BUFS
        return pltpu.make_async_copy(z_hbm.at[pl.ds(b, CMB_RC * ROW_TILES)], zbuf.at[slot], sem.at[slot])

    @pl.when(i == 0)
    def _():
        for g in range(CMB_BUFS - 1):
            @pl.when(g < total)
            def _():
                chunk_copy(g).start()

    eye = _iota((CMB_TB, CMB_TB), 0) == _iota((CMB_TB, CMB_TB), 1)
    lo = jnp.sum(jnp.where(eye, lo_ref[...], 0.0), axis=1, keepdims=True)
    hi = jnp.sum(jnp.where(eye, hi_ref[...], 0.0), axis=1, keepdims=True)
    col = _iota((1, CMB_RC), 1)

    def chunk(g, acc):
        @pl.when(g + CMB_BUFS - 1 < total)
        def _():
            chunk_copy(g + CMB_BUFS - 1).start()

        chunk_copy(g).wait()
        nominal = nom_ref[g]
        rows = (col + jnp.minimum(nominal, zrows - CMB_RC)).astype(F32)
        lo_c = jnp.maximum(lo, nominal.astype(F32))
        s = jnp.where(rows >= lo_c, jnp.where(rows < hi, 1.0, 0.0), 0.0).astype(BF16)
        return acc + _mmb(s, _load_token_tiles(zbuf.at[g % CMB_BUFS], CMB_RC))

    moe = lax.fori_loop(first_ref[i], first_ref[i + 1], chunk, jnp.zeros((CMB_TB, D), F32))

    x2 = x1_ref[...] + mod_ref[pl.ds(r, 1), 5 * D:6 * D] * moe
    if final:
        y = x2 * lax.rsqrt(jnp.mean(x2 * x2, axis=-1, keepdims=True) + EPS) * fw_ref[...]

        @pl.when(i * CMB_TB < t1)
        def _():
            outs[0][...] = y

        @pl.when(i * CMB_TB >= t1)
        def _():
            outs[1][...] = y
    else:
        outs[0][...] = x2


def _combine(lo, hi, x1, mod, fw, z, t1, n2, final):
    t_all = x1.shape[0]
    zrows = z.shape[0] // ROW_TILES
    nsteps = t_all // CMB_TB
    base = (lo[0, ::CMB_TB].astype(I32) // 8) * 8
    end = hi[0, CMB_TB - 1::CMB_TB].astype(I32)
    nchunk = jnp.maximum((end - base + CMB_RC - 1) // CMB_RC, 1)
    first = jnp.concatenate([jnp.zeros((1,), I32), jnp.cumsum(nchunk)])
    max_chunks = zrows // CMB_RC + 2 * nsteps
    g = jnp.arange(max_chunks, dtype=I32)
    step_of = jnp.minimum(jnp.sum(first[None, 1:] <= g[:, None], axis=1), nsteps - 1)
    nominal = base[step_of] + (g - first[step_of]) * CMB_RC
    nb1 = t1 // CMB_TB
    if final:
        out_specs = [pl.BlockSpec((CMB_TB, D), lambda i, *_: (jnp.minimum(i, nb1 - 1), 0)),
                     pl.BlockSpec((CMB_TB, D), lambda i, *_: (jnp.maximum(i - nb1, 0), 0))]
        out_shape = [jax.ShapeDtypeStruct((t1, D), F32), jax.ShapeDtypeStruct((t_all - t1, D), F32)]
    else:
        out_specs = pl.BlockSpec((CMB_TB, D), lambda i, *_: (i, 0))
        out_shape = jax.ShapeDtypeStruct((t_all, D), F32)
    return pl.pallas_call(
        functools.partial(_comb_kernel, t1, n2, zrows, final),
        grid_spec=pltpu.PrefetchScalarGridSpec(
            num_scalar_prefetch=2,
            grid=(t_all // CMB_TB,),
            in_specs=[pl.BlockSpec((CMB_TB, D), lambda i, *_: (i, 0)),
                      pl.BlockSpec((1, CMB_TB), lambda i, *_: (0, i)),
                      pl.BlockSpec((1, CMB_TB), lambda i, *_: (0, i)),
                      pl.BlockSpec((8, NMOD * D), lambda i, *_: (0, 0)),
                      pl.BlockSpec((1, D), lambda i, *_: (0, 0)),
                      pl.BlockSpec(memory_space=pl.ANY)],
            out_specs=out_specs,
            scratch_shapes=[pltpu.VMEM((CMB_BUFS, CMB_RC * ROW_TILES, 128), F32),
                            pltpu.SemaphoreType.DMA((CMB_BUFS,))]),
        out_shape=out_shape,
        compiler_params=_params(("arbitrary",)),
        name="combine",
    )(first, nominal, x1, lo, hi, mod, fw, z)


def _grid_pos_embed(n, d):
    rows = n // GRID_W
    r = np.repeat(np.arange(rows, dtype=np.float64), GRID_W)
    col = np.tile(np.arange(GRID_W, dtype=np.float64), rows)
    quarter = d // 4
    omega = np.power(POS_BASE, -np.arange(quarter, dtype=np.float64) / quarter)
    ra = r[:, None] * omega
    ca = col[:, None] * omega
    return jnp.asarray(np.concatenate([np.sin(ra), np.cos(ra), np.sin(ca), np.cos(ca)], axis=-1), F32)


def _decode_slots(slots):
    e, nsb, rows, tb = slots.shape
    nsb -= SPILL_BLOCKS
    slots = slots[:, :nsb].transpose(0, 2, 1, 3).reshape(e, rows, nsb * tb)
    idx = (slots[:, 0] * 128.0 + slots[:, 1]).astype(I32)
    w = slots[:, 2] + slots[:, 3] + slots[:, 4]
    rank = (slots[:, 5] * 128.0 + slots[:, 6]).astype(I32)
    return idx, w, rank


def kernel(x_prompt, x_sample, state_delta, c, c_ctx, w_ada, b_ada, norm1_w, norm2_w, w_in, dn_conv_w,
           dn_a_log, dn_dt_bias, dn_norm_w, sc_conv_w, w_out, w_router, w_gate, w_up, w_down,
           final_norm_w):
    b1, n1, _ = x_prompt.shape
    b2, n2, _ = x_sample.shape
    depth = w_ada.shape[0]
    t1, t2 = b1 * n1, b2 * n2
    t_all = t1 + t2
    cap1 = max(1, 2 * t1 // E)
    cap2 = max(1, 2 * t2 // E)
    tm = 512
    rc = 512 if (cap1 % 512 == 0 and cap2 % 512 == 0) else 128
    assert t1 % n2 == 0 and t1 % tm == 0 and n2 % tm == 0
    assert n1 % (DN_POS * CH) == 0 and n2 % (DN_POS * CH) == 0
    assert t1 % SEL_TB == 0 and t2 % SEL_TB == 0 and t1 % t2 == 0
    assert cap1 % rc == 0 and cap2 % rc == 0 and 2 * t_all >= CMB_RC
    nch = (cap1 + cap2) // rc
    width = H * DH

    cond8 = jnp.zeros((8, D), F32).at[0].set(c_ctx).at[1:1 + b2].set(c)
    mod = _ada(cond8, w_ada, b_ada)

    x = None
    tabs1 = _dft_tables(n1)
    tabs2 = _dft_tables(n2)
    ctx_states = None

    for l in range(depth):
        wl = w_in[l]
        wa = wl[:, 0:P_A_COLS].astype(BF16)
        wb = wl[:, P_A_COLS + 4 * H:].astype(BF16)
        wg = jnp.pad(wl[:, 4 * width:4 * width + 4 * H], ((0, 0), (0, 128 - 4 * H))).astype(BF16)
        al = jnp.pad(dn_a_log[l].reshape(1, 2 * H), ((0, 0), (2 * H, 128 - 4 * H)))
        dt = jnp.pad(dn_dt_bias[l].reshape(1, 2 * H), ((0, 0), (2 * H, 128 - 4 * H)))
        xs = (x,) if l else (x_prompt.reshape(t1, D), x_sample.reshape(t2, D), _grid_pos_embed(n2, D))
        res = _in_proj(xs, mod[l], norm1_w[l].reshape(1, D), wa, wb, wg, al, dt, t1, n2, tm)
        if not l:
            x, res = res[0], res[1:]
        p, g, gt = res

        nwd = dn_norm_w[l].reshape(1, DH)
        dn1, ctx_states = _deltanet(p, g, gt, dn_conv_w[l], nwd, None, ctx_states, b1, n1, 0)
        dn2, _ = _deltanet(p, g, gt, dn_conv_w[l], nwd, state_delta[:, l], None, b2, n2, t1 // n2)
        mx1 = _mix2(p, sc_conv_w[l], tabs1[1], tabs1[2], tabs1[0], b1, n1, 0, MIX_GROUP)
        mx2 = _mix2(p, sc_conv_w[l], tabs2[1], tabs2[2], tabs2[0], b2, n2, t1 // n2, 1)

        wr = jnp.pad(w_router[l], ((0, 0), (0, 128 - E)))
        x1, h2, aff_t = _out_proj(dn1, dn2, mx1, mx2, x, mod[l], norm2_w[l].reshape(1, D),
                                  w_out[l].astype(BF16), wr, t1, n2, tm)

        slots1, lo1, hi1 = _select(aff_t, t1, cap1, 0)
        slots2, lo2, hi2 = _select(aff_t, t2, cap2, t1 // t2)
        idx1, wsel1, rank1 = _decode_slots(slots1)
        idx2, wsel2, rank2 = _decode_slots(slots2)
        idx = jnp.concatenate([idx1, idx2 + t1], axis=1).reshape(-1) * ROW_TILES
        dst = jnp.concatenate([rank1, rank2 + 2 * t1], axis=1).reshape(-1) * ROW_TILES
        wcol = jnp.concatenate([wsel1, wsel2], axis=1).reshape(-1, 1)
        z = _expert_ffn(idx, dst, h2, wcol, w_gate, w_up, w_down, l, rc, nch)

        lo = jnp.concatenate([lo1, lo2 + 2.0 * t1], axis=1)
        hi = jnp.concatenate([hi1, hi2 + 2.0 * t1], axis=1)
        x = _combine(lo, hi, x1, mod[l], final_norm_w.reshape(1, D), z, t1, n2, l == depth - 1)

    y_prompt, y_sample = x
    return y_prompt.reshape(b1, n1, D), y_sample.reshape(b2, n2, D), ctx_states
```

```python
import functools
import math

import jax
import jax.numpy as jnp
import numpy as np
from jax import lax
from jax.experimental import pallas as pl
from jax.experimental.pallas import tpu as pltpu

F32 = jnp.float32
BF16 = jnp.bfloat16
I32 = jnp.int32
HI = lax.Precision.HIGHEST

D = 1024
H = 4
DH = 128
CH = 64
E = 16
NMOD = 6
EPS = 1e-6
GRID_W = 64
POS_BASE = 10000.0
P_COLS = 3072
P_A_COLS = 2048
VMEM_LIMIT = 56 * 1024 * 1024

NN = (((1,), (0,)), ((), ()))
NT = (((1,), (1,)), ((), ()))
TN = (((0,), (0,)), ((), ()))


def _mm(a, b, dims=NN, prec=None):
    return lax.dot_general(a, b, dims, precision=prec, preferred_element_type=F32)


def _mmb(a, b, dims=NN):
    return lax.dot_general(a.astype(BF16), b.astype(BF16), dims, preferred_element_type=F32)


def _mm3(a, b, dims=NN):
    a_hi = a.astype(BF16)
    a_lo = (a - a_hi.astype(F32)).astype(BF16)
    b_hi = b.astype(BF16)
    b_lo = (b - b_hi.astype(F32)).astype(BF16)
    return _mm(a_hi, b_hi, dims) + _mm(a_hi, b_lo, dims) + _mm(a_lo, b_hi, dims)


def _mm_inv(a, b):
    return _mmb(a, b)


def _silu(x):
    return x / (1.0 + jnp.exp(-x))


def _sigmoid(x):
    return 1.0 / (1.0 + jnp.exp(-x))


def _softplus(x):
    return jnp.maximum(x, 0.0) + jnp.log1p(jnp.exp(-jnp.abs(x)))


def _iota(shape, dim):
    return lax.broadcasted_iota(I32, shape, dim)


ROW_TILES = D // 128


def _load_token_tiles(ref, n):
    return jnp.concatenate([ref[pl.ds(s, n, stride=ROW_TILES), :] for s in range(ROW_TILES)], axis=1)


def _store_token_tiles(ref, x):
    n = x.shape[0]
    for s in range(ROW_TILES):
        ref[pl.ds(s, n, stride=ROW_TILES), :] = x[:, s * 128:(s + 1) * 128]


def _mod_row(t0, t1, n2):
    return jnp.where(t0 < t1, 0, 1 + jnp.maximum(t0 - t1, 0) // n2)


def _params(sem):
    return pltpu.CompilerParams(dimension_semantics=sem, vmem_limit_bytes=VMEM_LIMIT)


def _ada_kernel(c_ref, w_ref, b_ref, o_ref):
    s = _silu(c_ref[...])
    o_ref[0] = _mm3(s, w_ref[0]) + b_ref[0]


def _ada(cond8, w_ada, b_ada):
    depth = w_ada.shape[0]
    tn = 1536
    return pl.pallas_call(
        _ada_kernel,
        grid=(depth, NMOD * D // tn),
        in_specs=[pl.BlockSpec((8, D), lambda l, j: (0, 0)),
                  pl.BlockSpec((1, D, tn), lambda l, j: (l, 0, j)),
                  pl.BlockSpec((1, 1, tn), lambda l, j: (l, 0, j))],
        out_specs=pl.BlockSpec((1, 8, tn), lambda l, j: (l, 0, j)),
        out_shape=jax.ShapeDtypeStruct((depth, 8, NMOD * D), F32),
        compiler_params=_params(("arbitrary", "arbitrary")),
        name="ada",
    )(cond8, w_ada, b_ada.reshape(depth, 1, NMOD * D))


IN_SUB = 2


def _in_core(tm, r, x, mod_ref, nw_ref, wa_ref, wb_ref, wg_ref, al_ref, dt_ref, p_ref, g_ref, gt_ref):
    sh = mod_ref[pl.ds(r, 1), 0:D]
    sc = mod_ref[pl.ds(r, 1), D:2 * D]
    sub = tm // IN_SUB
    lane = _iota((1, 128), 1)
    hbs = []
    for k in range(IN_SUB):
        xk = x[k * sub:(k + 1) * sub]
        y = xk * lax.rsqrt(jnp.mean(xk * xk, axis=-1, keepdims=True) + EPS) * nw_ref[...]
        hbs.append((y * (1.0 + sc) + sh).astype(BF16))
    for k in range(IN_SUB):
        rs = slice(k * sub, (k + 1) * sub)
        for j in range(P_A_COLS // D):
            p_ref[rs, j * D:(j + 1) * D] = _mm(hbs[k], wa_ref[:, j * D:(j + 1) * D])
        p_ref[rs, P_A_COLS:P_COLS] = _mm(hbs[k], wb_ref[...])
    for k in range(IN_SUB):
        raw = _mm(hbs[k], wg_ref[...])
        g = -jnp.exp(al_ref[...]) * _softplus(raw + dt_ref[...])
        act = jnp.where(lane < 2 * H, _sigmoid(raw), g)
        g_ref[k * sub:(k + 1) * sub, :] = act[:, 0:4 * H]
        act_t = act.T
        for j in range(sub // CH):
            gt_ref[k * (sub // CH) + j] = act_t[0:4 * H, j * CH:(j + 1) * CH]


def _in_kernel(t1, n2, tm, x_ref, *refs):
    r = _mod_row(pl.program_id(0) * tm, t1, n2)
    _in_core(tm, r, x_ref[...], *refs)


def _in_first_kernel(t1, n2, tm, xp_ref, xs_ref, pos_ref, mod_ref, nw_ref, wa_ref, wb_ref, wg_ref, al_ref,
                     dt_ref, x_ref, p_ref, g_ref, gt_ref):
    t0 = pl.program_id(0) * tm
    x = jnp.where(t0 < t1, xp_ref[...], xs_ref[...] + pos_ref[...])
    x_ref[...] = x
    _in_core(tm, _mod_row(t0, t1, n2), x, mod_ref, nw_ref, wa_ref, wb_ref, wg_ref, al_ref, dt_ref,
             p_ref, g_ref, gt_ref)


def _in_proj(xs, mod, nw, wa, wb, wg, al, dt, t1, n2, tm):
    first = len(xs) == 3
    t_all = t1 + xs[1].shape[0] if first else xs[0].shape[0]
    nt1 = t1 // tm
    wspecs = [pl.BlockSpec((8, NMOD * D), lambda i: (0, 0)),
              pl.BlockSpec((1, D), lambda i: (0, 0)),
              pl.BlockSpec((D, P_A_COLS), lambda i: (0, 0)),
              pl.BlockSpec((D, P_COLS - P_A_COLS), lambda i: (0, 0)),
              pl.BlockSpec((D, 128), lambda i: (0, 0)),
              pl.BlockSpec((1, 128), lambda i: (0, 0)),
              pl.BlockSpec((1, 128), lambda i: (0, 0))]
    out_specs = [pl.BlockSpec((tm, P_COLS), lambda i: (i, 0)),
                 pl.BlockSpec((tm, 4 * H), lambda i: (i, 0)),
                 pl.BlockSpec((tm // CH, 4 * H, CH), lambda i: (i, 0, 0))]
    out_shape = [jax.ShapeDtypeStruct((t_all, P_COLS), F32),
                 jax.ShapeDtypeStruct((t_all, 4 * H), F32),
                 jax.ShapeDtypeStruct((t_all // CH, 4 * H, CH), F32)]
    if first:
        kern = _in_first_kernel
        xspecs = [pl.BlockSpec((tm, D), lambda i: (jnp.minimum(i, nt1 - 1), 0)),
                  pl.BlockSpec((tm, D), lambda i: (jnp.maximum(i - nt1, 0), 0)),
                  pl.BlockSpec((tm, D), lambda i: (jnp.maximum(i - nt1, 0) % (n2 // tm), 0))]
        out_specs = [pl.BlockSpec((tm, D), lambda i: (i, 0))] + out_specs
        out_shape = [jax.ShapeDtypeStruct((t_all, D), F32)] + out_shape
    else:
        kern = _in_kernel
        xspecs = [pl.BlockSpec((tm, D), lambda i: (i, 0))]
    return pl.pallas_call(
        functools.partial(kern, t1, n2, tm),
        grid=(t_all // tm,),
        in_specs=xspecs + wspecs,
        out_specs=out_specs,
        out_shape=out_shape,
        compiler_params=_params(("arbitrary",)),
        name="in_proj_first" if first else "in_proj",
    )(*xs, mod, nw, wa, wb, wg, al, dt)


def _shift_conv(x, w_ref, c0, c1, period=None):
    n = x.shape[0]
    row = _iota((n, 1), 0) & ((period or n) - 1)
    prev = jnp.where(row == 0, 0.0, pltpu.roll(x, 1, 0))
    nxt = jnp.where(row == (period or n) - 1, 0.0, pltpu.roll(x, n - 1, 0))
    return prev * w_ref[0:1, c0:c1] + x * w_ref[1:2, c0:c1] + nxt * w_ref[2:3, c0:c1]


DN_POS = 4


def _dn_kernel(n, nprev, has_s0, q_ref, k_ref, v_ref, z_ref, g_ref, gt_ref, cw_ref, nw_ref, *refs):
    refs = list(refs)
    s0_ref = refs.pop(0) if has_s0 else None
    prev_ref = refs.pop(0) if nprev else None
    o_ref, sfin_ref, qs, ks, vs, gcc_s, gct_s, s_s, oacc = refs
    nc = n // CH
    width = H * DH

    for hh in range(H):
        c0, c1 = hh * DH, (hh + 1) * DH
        q = _silu(_shift_conv(q_ref[:, c0:c1], cw_ref, c0, c1))
        q = q * lax.rsqrt(jnp.sum(q * q, axis=-1, keepdims=True) + 1e-6)
        qs[:, c0:c1] = q * (DH ** -0.5)
        k = _silu(_shift_conv(k_ref[:, c0:c1], cw_ref, width + c0, width + c1))
        ks[:, c0:c1] = k * lax.rsqrt(jnp.sum(k * k, axis=-1, keepdims=True) + 1e-6)
        vs[:, c0:c1] = _silu(_shift_conv(v_ref[:, c0:c1], cw_ref, 2 * width + c0, 2 * width + c1))

    ii = _iota((CH, CH), 0)
    jj = _iota((CH, CH), 1)
    x = gt_ref[...].reshape(nc * 4 * H, CH)
    cf = _mm(x, (ii <= jj).astype(F32), prec=HI)
    cb = _mm(x, (ii >= jj).astype(F32), prec=HI)
    rsel = (_iota((nc * 4 * H, 1), 0) & (4 * H - 1)) < 3 * H
    gct_s[...] = jnp.where(rsel, cf, cb).reshape(nc, 4 * H, CH)
    i2 = _iota((2 * CH, 2 * CH), 0)
    j2 = _iota((2 * CH, 2 * CH), 1)
    same = (i2 >= CH) == (j2 >= CH)
    lf = (same & (i2 >= j2)).astype(F32)
    lb = (same & (i2 <= j2)).astype(F32)
    lsel = _iota((1, 4 * H), 1) < 3 * H
    for m in range(n // (2 * CH)):
        y = g_ref[m * 2 * CH:(m + 1) * 2 * CH, :]
        gcc_s[m * 2 * CH:(m + 1) * 2 * CH, :] = jnp.where(lsel, _mm(lf, y, prec=HI), _mm(lb, y, prec=HI))

    for hh in range(H):
        for d in range(2):
            s_s[hh * 2 + d] = s0_ref[0, d, hh] if has_s0 else jnp.zeros((DH, DH), F32)

    eye = (ii == jj).astype(F32)
    blk_same = [(ii >> s) == (jj >> s) for s in range(1, 7)]
    merge_mask = [blk_same[lvl] & ~blk_same[lvl - 1] for lvl in range(1, len(blk_same))]

    chains = [(hh, d) for hh in range(H) for d in range(2)]

    def chunk_step(cc, carry):
        st = []
        for p in range(DN_POS):
            for hh, d in chains:
                c = cc * DN_POS + p
                cidx = c if d == 0 else nc - 1 - c
                r0 = pl.multiple_of(cidx * CH, CH)
                c0, c1 = hh * DH, (hh + 1) * DH
                gcol = 2 * H + H * d + hh
                bcol = H * d + hh
                incl = (ii >= jj) if d == 0 else (ii <= jj)
                strict = (ii > jj) if d == 0 else (ii < jj)
                qc = qs[pl.ds(r0, CH), c0:c1]
                kc = ks[pl.ds(r0, CH), c0:c1]
                vc = vs[pl.ds(r0, CH), c0:c1]
                beta = g_ref[pl.ds(r0, CH), bcol:bcol + 1]
                gc = gcc_s[pl.ds(r0, CH), gcol:gcol + 1]
                gr = gct_s[cidx, gcol:gcol + 1, :]
                tot = gr[:, CH - 1:CH] if d == 0 else gr[:, 0:1]
                decay = jnp.where(incl, jnp.exp(jnp.where(incl, gc - gr, 0.0)), 0.0)
                kb = kc * beta
                egc = jnp.exp(gc)
                st.append(dict(r0=r0, c0=c0, c1=c1, d=d, strict=strict, kc=kc, kb=kb, decay=decay,
                               rhs=jnp.concatenate([vc * beta, kb * egc], axis=1),
                               qd=qc * egc, qc=qc, kd=kc * jnp.exp(tot - gc), gl=jnp.exp(tot)))
        for x in st:
            x["lmat"] = jnp.where(x["strict"], _mmb(x["kb"], x["kc"], NT) * x["decay"], 0.0)
            x["qk"] = _mmb(x["qc"], x["kc"], NT) * x["decay"]
            x["tinv"] = eye - jnp.where(blk_same[0], x["lmat"], 0.0)
        for mask in merge_mask:
            for x in st:
                x["ot"] = _mm_inv(jnp.where(mask, x["lmat"], 0.0), x["tinv"])
            for x in st:
                x["tinv"] = x["tinv"] - _mm_inv(x["tinv"], x["ot"])
        for x in st:
            x["sol"] = _mm_inv(x["tinv"], x["rhs"])
        state = [s_s[i] for i in range(len(chains))]
        for p in range(DN_POS):
            units = st[p * len(chains):(p + 1) * len(chains)]
            for i, x in enumerate(units):
                x["sb"] = state[i].astype(BF16)
                x["v_new"] = x["sol"][:, 0:DH] - _mmb(x["sol"][:, DH:2 * DH], x["sb"])
            for i, x in enumerate(units):
                o = _mmb(x["qd"], x["sb"]) + _mmb(x["qk"], x["v_new"])
                state[i] = state[i] * x["gl"] + _mmb(x["kd"], x["v_new"], TN)
                oacc[x["d"], pl.ds(x["r0"], CH), x["c0"]:x["c1"]] = o
        for i in range(len(chains)):
            s_s[i] = state[i]
        return carry

    lax.fori_loop(0, nc // DN_POS, chunk_step, 0)

    for hh in range(H):
        c0, c1 = hh * DH, (hh + 1) * DH
        o = oacc[0, :, c0:c1] + oacc[1, :, c0:c1]
        o = o * lax.rsqrt(jnp.mean(o * o, axis=-1, keepdims=True) + EPS)
        o_ref[:, c0:c1] = (o * nw_ref[...] * _silu(z_ref[:, c0:c1])).astype(o_ref.dtype)
        for d in range(2):
            sfin_ref[0, nprev, d, hh] = s_s[hh * 2 + d]
    for layer in range(nprev):
        sfin_ref[0, layer] = prev_ref[0, layer]


def _deltanet(p, g, gt, conv_w, norm_w, s0, prev, bsz, n, row_blk0):
    nc = n // CH
    width = H * DH
    nprev = 0 if prev is None else prev.shape[1]
    state_spec = lambda k: pl.BlockSpec((1, k, 2, H, DH, DH), lambda b: (b, 0, 0, 0, 0, 0))
    opt_args, opt_specs = [], []
    if s0 is not None:
        opt_args.append(s0)
        opt_specs.append(pl.BlockSpec((1, 2, H, DH, DH), lambda b: (b, 0, 0, 0, 0)))
    if prev is not None:
        opt_args.append(prev)
        opt_specs.append(state_spec(nprev))
    return pl.pallas_call(
        functools.partial(_dn_kernel, n, nprev, s0 is not None),
        grid=(bsz,),
        in_specs=[pl.BlockSpec((n, width), lambda b: (row_blk0 + b, 0)),
                  pl.BlockSpec((n, width), lambda b: (row_blk0 + b, 1)),
                  pl.BlockSpec((n, width), lambda b: (row_blk0 + b, 2)),
                  pl.BlockSpec((n, width), lambda b: (row_blk0 + b, 3)),
                  pl.BlockSpec((n, 4 * H), lambda b: (row_blk0 + b, 0)),
                  pl.BlockSpec((nc, 4 * H, CH), lambda b: (row_blk0 + b, 0, 0)),
                  pl.BlockSpec((3, 3 * width), lambda b: (0, 0)),
                  pl.BlockSpec((1, DH), lambda b: (0, 0))] + opt_specs,
        out_specs=[pl.BlockSpec((n, width), lambda b: (b, 0)), state_spec(nprev + 1)],
        out_shape=[jax.ShapeDtypeStruct((bsz * n, width), BF16),
                   jax.ShapeDtypeStruct((bsz, nprev + 1, 2, H, DH, DH), F32)],
        scratch_shapes=[pltpu.VMEM((n, width), F32), pltpu.VMEM((n, width), F32),
                        pltpu.VMEM((n, width), F32), pltpu.VMEM((n, 4 * H), F32),
                        pltpu.VMEM((nc, 4 * H, CH), F32), pltpu.VMEM((2 * H, DH, DH), F32),
                        pltpu.VMEM((2, n, width), F32)],
        compiler_params=_params(("arbitrary",)),
        name="deltanet_n%d" % n,
    )(p, p, p, p, g, gt, conv_w, norm_w, *opt_args)


MIX_GROUP = 4


def _mix2_kernel(n, sb_ref, sc_ref, su_ref, fu_ref, cw_ref, bdc_ref, bds_ref, cs_ref, o_ref):
    rows, w = sb_ref.shape
    cu = sc_ref[...] * su_ref[...]
    o_ref[:, 0:w] = (sb_ref[...] * _shift_conv(cu, cw_ref, 0, w, n)).astype(o_ref.dtype)
    fu = fu_ref[...]
    a = _mm3(fu, bdc_ref[...])
    b = _mm3(fu, bds_ref[...])
    for q in range(rows // n):
        rs = slice(q * n, (q + 1) * n)
        fo = _mmb(cs_ref[...], jnp.concatenate([a[rs], b[rs]], axis=0))
        o_ref[rs, w:2 * w] = fo.astype(o_ref.dtype)


def _mix2(p, conv_w, bdc, bds, cs, bsz, n, row_blk0, group):
    w = 256
    rows = group * n
    blk0 = row_blk0 // group
    assert bsz % group == 0 and row_blk0 % group == 0
    return pl.pallas_call(
        functools.partial(_mix2_kernel, n),
        grid=(bsz // group,),
        in_specs=[pl.BlockSpec((rows, w), lambda b: (blk0 + b, 8)),
                  pl.BlockSpec((rows, w), lambda b: (blk0 + b, 9)),
                  pl.BlockSpec((rows, w), lambda b: (blk0 + b, 10)),
                  pl.BlockSpec((rows, w), lambda b: (blk0 + b, 11)),
                  pl.BlockSpec((3, w), lambda b: (0, 0)),
                  pl.BlockSpec((w, w), lambda b: (0, 0)),
                  pl.BlockSpec((w, w), lambda b: (0, 0)),
                  pl.BlockSpec((n, 2 * n), lambda b: (0, 0))],
        out_specs=pl.BlockSpec((rows, 2 * w), lambda b: (b, 0)),
        out_shape=jax.ShapeDtypeStruct((bsz * n, 2 * w), BF16),
        compiler_params=_params(("arbitrary",)),
        name="conv_fourier_n%d" % n,
    )(p, p, p, p, conv_w, bdc, bds, cs)


def _dft_tables(n):
    gw = 64
    k = np.arange(n, dtype=np.int64)
    ang = ((k[:, None] * k[None, :]) % n) * (2.0 * math.pi / n)
    scale = 1.0 / math.sqrt(n * gw)
    cs = np.concatenate([np.cos(ang), -np.sin(ang)], axis=1) * scale
    c = np.arange(256, dtype=np.int64)
    angc = (((c[:, None] % gw) * (c[None, :] % gw)) % gw) * (2.0 * math.pi / gw)
    same = (c[:, None] // gw) == (c[None, :] // gw)
    bdc = np.where(same, np.cos(angc), 0.0)
    bds = np.where(same, np.sin(angc), 0.0)
    return jnp.asarray(cs, F32).astype(BF16), jnp.asarray(bdc, F32), jnp.asarray(bds, F32)


OUT_SUB = 2


def _out_kernel(t1, n2, tm, dn1_ref, dn2_ref, mx1_ref, mx2_ref, x_ref, mod_ref, nw_ref, wo_ref, wr_ref,
                x1_ref, h2_ref, aff_ref):
    t0 = pl.program_id(0) * tm
    r = _mod_row(t0, t1, n2)
    half = D // 2
    sub = tm // OUT_SUB
    mixes = []
    for k in range(OUT_SUB):
        rs = slice(k * sub, (k + 1) * sub)
        dn = jnp.where(t0 < t1, dn1_ref[rs, :], dn2_ref[rs, :])
        mx = jnp.where(t0 < t1, mx1_ref[rs, :], mx2_ref[rs, :])
        mixes.append(_mmb(dn, wo_ref[0:half, :]) + _mmb(mx, wo_ref[half:D, :]))
    h2s = []
    for k in range(OUT_SUB):
        rs = slice(k * sub, (k + 1) * sub)
        x1 = x_ref[rs, :] + mod_ref[pl.ds(r, 1), 2 * D:3 * D] * mixes[k]
        x1_ref[rs, :] = x1
        y = x1 * lax.rsqrt(jnp.mean(x1 * x1, axis=-1, keepdims=True) + EPS) * nw_ref[...]
        h2 = y * (1.0 + mod_ref[pl.ds(r, 1), 4 * D:5 * D]) + mod_ref[pl.ds(r, 1), 3 * D:4 * D]
        _store_token_tiles(h2_ref.at[pl.ds(k * sub * ROW_TILES, sub * ROW_TILES)], h2)
        h2s.append(h2)
    logits = [_mm3(h2, wr_ref[...]) for h2 in h2s]
    for k in range(OUT_SUB):
        lt = logits[k].T[0:E, :]
        ex = jnp.exp(lt - jnp.max(lt, axis=0, keepdims=True))
        aff_ref[:, k * sub:(k + 1) * sub] = ex / jnp.sum(ex, axis=0, keepdims=True)


def _out_proj(dn1, dn2, mx1, mx2, x, mod, nw, wo, wr, t1, n2, tm):
    t_all = x.shape[0]
    nt1 = t1 // tm
    spec1 = pl.BlockSpec((tm, D // 2), lambda i: (jnp.minimum(i, nt1 - 1), 0))
    spec2 = pl.BlockSpec((tm, D // 2), lambda i: (jnp.maximum(i - nt1, 0), 0))
    return pl.pallas_call(
        functools.partial(_out_kernel, t1, n2, tm),
        grid=(t_all // tm,),
        in_specs=[spec1, spec2, spec1, spec2,
                  pl.BlockSpec((tm, D), lambda i: (i, 0)),
                  pl.BlockSpec((8, NMOD * D), lambda i: (0, 0)),
                  pl.BlockSpec((1, D), lambda i: (0, 0)),
                  pl.BlockSpec((D, D), lambda i: (0, 0)),
                  pl.BlockSpec((D, 128), lambda i: (0, 0))],
        out_specs=[pl.BlockSpec((tm, D), lambda i: (i, 0)),
                   pl.BlockSpec((tm * ROW_TILES, 128), lambda i: (i, 0)),
                   pl.BlockSpec((E, tm), lambda i: (0, i))],
        out_shape=[jax.ShapeDtypeStruct((t_all, D), F32),
                   jax.ShapeDtypeStruct((t_all * ROW_TILES, 128), F32),
                   jax.ShapeDtypeStruct((E, t_all), F32)],
        compiler_params=_params(("arbitrary",)),
        name="out_proj",
    )(dn1, dn2, mx1, mx2, x, mod, nw, wo, wr)


SEL_TB = 128
N_SLOT_ROWS = 16
SPILL_BLOCKS = 2


def _sel_kernel(t, cap, aff_ref, slots_ref, lo_ref, hi_ref, a3, c3, starts_v, starts_s, sem):
    ntb = t // SEL_TB
    aff = aff_ref[...]

    def search(i, thr):
        cand = thr | jnp.left_shift(jnp.int32(1), 30 - i)
        cnt = jnp.sum((aff >= pltpu.bitcast(cand, F32)).astype(F32), axis=1, keepdims=True)
        return jnp.where(cnt >= cap, cand, thr)

    thr = lax.fori_loop(0, 31, search, jnp.zeros((E, 1), I32))
    gt = (aff >= pltpu.bitcast(thr + 1, F32)).astype(F32)
    eq = (aff >= pltpu.bitcast(thr, F32)).astype(F32) - gt
    need = cap - jnp.sum(gt, axis=1, keepdims=True)

    ui = _iota((SEL_TB, SEL_TB), 0)
    uj = _iota((SEL_TB, SEL_TB), 1)
    upper = (ui < uj).astype(BF16)
    blk_lane = _iota((1, 128), 1)

    def excl_cumsum(rows, dst, r0, r1):
        carry = jnp.zeros((rows.shape[0], 1), F32)
        starts = jnp.zeros((rows.shape[0], 128), F32)
        for j in range(ntb):
            blk = rows[:, j * SEL_TB:(j + 1) * SEL_TB]
            dst[j, r0:r1, :] = _mm(blk.astype(BF16), upper) + carry
            starts = jnp.where(blk_lane == j, carry, starts)
            carry = carry + jnp.sum(blk, axis=1, keepdims=True)
        return starts

    excl_cumsum(eq, c3, 0, E)
    rank_eq = jnp.concatenate([c3[j, 0:E, :] for j in range(ntb)], axis=1)
    sel = jnp.maximum(gt, jnp.where(rank_eq < need, eq, 0.0))
    n_tok = jnp.sum(sel, axis=0, keepdims=True)
    starts = excl_cumsum(jnp.concatenate([sel, jnp.broadcast_to(n_tok, (8, t))], axis=0), c3, 0, E + 8)
    starts_v[...] = starts.astype(I32)
    to_smem = pltpu.make_async_copy(starts_v, starts_s, sem)
    to_smem.start()
    ei = _iota((E, E), 0)
    ej = _iota((E, E), 1)
    below = _mm((ej < ei).astype(BF16), sel.astype(BF16))
    for j in range(ntb):
        sl = slice(j * SEL_TB, (j + 1) * SEL_TB)
        off = c3[j, E:E + 1, :]
        lo_ref[:, sl] = off
        hi_ref[:, sl] = off + n_tok[:, sl]
        rank = off + below[:, sl]
        a3[j, 0] = sel[:, sl]
        a3[j, 1] = aff[:, sl]
        a3[j, 2] = rank
    slots_ref[...] = jnp.zeros(slots_ref.shape, F32)
    to_smem.wait()

    win_iota = _iota((2 * SEL_TB, SEL_TB), 0)
    tok_iota = _iota((1, SEL_TB), 1)

    def per_expert(e, carry):
        def per_block(j, carry2):
            wb = starts_s[e, j] >> 7
            chosen = a3[j, 0, pl.ds(e, 1), :]
            pos = c3[j, pl.ds(e, 1), :].astype(I32) - wb * SEL_TB
            pos = jnp.where(chosen > 0.0, pos, -1)
            w = a3[j, 1, pl.ds(e, 1), :]
            rank = a3[j, 2, pl.ds(e, 1), :].astype(I32)
            tok = tok_iota + j * SEL_TB
            w_hi = w.astype(BF16).astype(F32)
            w_mid = (w - w_hi).astype(BF16).astype(F32)
            w_lo = w - w_hi - w_mid
            vals = jnp.concatenate(
                [(tok >> 7).astype(F32), (tok & 127).astype(F32), w_hi, w_mid, w_lo,
                 (rank >> 7).astype(F32), (rank & 127).astype(F32),
                 jnp.zeros((N_SLOT_ROWS - 7, SEL_TB), F32)], axis=0)
            onehot = jnp.where(win_iota == pos, 1.0, 0.0)
            placed = _mmb(vals, onehot, NT)
            slots_ref[e, wb] += placed[:, 0:SEL_TB]
            slots_ref[e, wb + 1] += placed[:, SEL_TB:2 * SEL_TB]
            return carry2
        return lax.fori_loop(0, ntb, per_block, carry, unroll=16)

    lax.fori_loop(0, E, per_expert, 0)


def _select(aff_t, t, cap, col_blk):
    ntb = t // SEL_TB
    nsb = cap // SEL_TB + SPILL_BLOCKS
    return pl.pallas_call(
        functools.partial(_sel_kernel, t, cap),
        grid=(1,),
        in_specs=[pl.BlockSpec((E, t), lambda i: (0, col_blk))],
        out_specs=[pl.BlockSpec((E, nsb, N_SLOT_ROWS, SEL_TB), lambda i: (0, 0, 0, 0)),
                   pl.BlockSpec((1, t), lambda i: (0, 0)),
                   pl.BlockSpec((1, t), lambda i: (0, 0))],
        out_shape=[jax.ShapeDtypeStruct((E, nsb, N_SLOT_ROWS, SEL_TB), F32),
                   jax.ShapeDtypeStruct((1, t), F32),
                   jax.ShapeDtypeStruct((1, t), F32)],
        scratch_shapes=[pltpu.VMEM((ntb, 3, E, SEL_TB), F32),
                        pltpu.VMEM((ntb, E + 8, SEL_TB), F32),
                        pltpu.VMEM((E + 8, 128), I32),
                        pltpu.SMEM((E + 8, 128), I32),
                        pltpu.SemaphoreType.DMA(())],
        compiler_params=_params(("arbitrary",)),
        name="select_t%d" % t,
    )(aff_t)


DMA_UNROLL = 16


def _ffn_kernel(layer, rc, nch, idx_ref, dst_ref, h2_hbm, wcol_ref, wg_hbm, wu_hbm, wd_hbm, z_hbm,
                xbuf, ybuf, wbuf, wgb, wub, wdb, gsem, ssem, wsem):
    e = pl.program_id(0)
    c = pl.program_id(1)
    step = e * nch + c

    def weight_copies(ex):
        slot = ex % 2
        return [pltpu.make_async_copy(w.at[layer, ex], wbuf.at[slot, k], wsem.at[slot, k])
                for k, w in enumerate((wg_hbm, wu_hbm, wd_hbm))]
    nsteps = E * nch
    slot = step % 2

    def start_gather(s):
        buf = xbuf.at[s % 2]
        sem = gsem.at[s % 2]

        def issue(jb, carry):
            for u in range(DMA_UNROLL):
                j = jb * DMA_UNROLL + u
                src = pl.multiple_of(idx_ref[s * rc + j], ROW_TILES)
                pltpu.make_async_copy(h2_hbm.at[pl.ds(src, ROW_TILES)],
                                      buf.at[pl.ds(j * ROW_TILES, ROW_TILES)], sem).start(priority=u % 2)
            return carry

        lax.fori_loop(0, rc // DMA_UNROLL, issue, 0)

    @pl.when(step == 0)
    def _():
        start_gather(step)

    @pl.when(step + 1 < nsteps)
    def _():
        start_gather(step + 1)

    @pl.when(step == 0)
    def _():
        for cp in weight_copies(e):
            cp.start()

    @pl.when((c == 0) & (e + 1 < E))
    def _():
        for cp in weight_copies(e + 1):
            cp.start()

    @pl.when(c == 0)
    def _():
        for cp in weight_copies(e):
            cp.wait()
        slot = e % 2
        wgb[...] = wbuf[slot, 0].astype(BF16)
        wub[...] = wbuf[slot, 1].astype(BF16)
        wdb[...] = wbuf[slot, 2].astype(BF16)

    pltpu.make_async_copy(h2_hbm.at[pl.ds(0, rc * ROW_TILES)], xbuf.at[slot], gsem.at[slot]).wait()
    xb = _load_token_tiles(xbuf.at[slot], rc).astype(BF16)
    hid = _silu(_mm(xb, wgb[...])) * _mm(xb, wub[...])
    y = _mm(hid.astype(BF16), wdb[...]) * wcol_ref[...]

    @pl.when(step > 0)
    def _():
        pltpu.make_async_copy(ybuf, z_hbm.at[pl.ds(0, rc * ROW_TILES)], ssem).wait()

    _store_token_tiles(ybuf, y)

    def scatter(jb, carry):
        for u in range(DMA_UNROLL):
            j = jb * DMA_UNROLL + u
            row = pl.multiple_of(dst_ref[step * rc + j], ROW_TILES)
            pltpu.make_async_copy(ybuf.at[pl.ds(j * ROW_TILES, ROW_TILES)],
                                  z_hbm.at[pl.ds(row, ROW_TILES)], ssem).start(priority=u % 2)
        return carry

    lax.fori_loop(0, rc // DMA_UNROLL, scatter, 0)

    @pl.when(step == nsteps - 1)
    def _():
        pltpu.make_async_copy(ybuf, z_hbm.at[pl.ds(0, rc * ROW_TILES)], ssem).wait()


def _expert_ffn(idx, dst, h2, wcol, w_gate, w_up, w_down, layer, rc, nch):
    zrows = E * nch * rc
    wspec = pl.BlockSpec(memory_space=pl.ANY)
    return pl.pallas_call(
        functools.partial(_ffn_kernel, layer, rc, nch),
        grid_spec=pltpu.PrefetchScalarGridSpec(
            num_scalar_prefetch=2,
            grid=(E, nch),
            in_specs=[pl.BlockSpec(memory_space=pl.ANY),
                      pl.BlockSpec((rc, 1), lambda e, c, *_: (e * nch + c, 0)),
                      wspec, wspec, wspec],
            out_specs=pl.BlockSpec(memory_space=pl.ANY),
            scratch_shapes=[pltpu.VMEM((2, rc * ROW_TILES, 128), F32),
                            pltpu.VMEM((rc * ROW_TILES, 128), F32),
                            pltpu.VMEM((2, 3, D, D), F32),
                            pltpu.VMEM((D, D), BF16), pltpu.VMEM((D, D), BF16),
                            pltpu.VMEM((D, D), BF16),
                            pltpu.SemaphoreType.DMA((2,)), pltpu.SemaphoreType.DMA(()),
                            pltpu.SemaphoreType.DMA((2, 3))]),
        out_shape=jax.ShapeDtypeStruct((zrows * ROW_TILES, 128), F32),
        compiler_params=_params(("arbitrary", "arbitrary")),
        name="expert_ffn",
    )(idx, dst, h2, wcol, w_gate, w_up, w_down)


CMB_TB = 256
CMB_RC = 512
CMB_BUFS = 4


def _comb_kernel(t1, n2, zrows, final, first_ref, nom_ref, x1_ref, lo_ref, hi_ref, mod_ref, fw_ref, z_hbm,
                 *refs):
    outs, (zbuf, sem) = refs[:-2], refs[-2:]
    i = pl.program_id(0)
    nsteps = pl.num_programs(0)
    r = _mod_row(i * CMB_TB, t1, n2)
    total = first_ref[nsteps]

    def chunk_copy(g):
        b = pl.multiple_of(jnp.minimum(nom_ref[g], zrows - CMB_RC) * ROW_TILES, 8 * ROW_TILES)
        slot = g % CMB_BUFS
        return pltpu.make_async_copy(z_hbm.at[pl.ds(b, CMB_RC * ROW_TILES)], zbuf.at[slot], sem.at[slot])

    @pl.when(i == 0)
    def _():
        for g in range(CMB_BUFS - 1):
            @pl.when(g < total)
            def _():
                chunk_copy(g).start()

    eye = _iota((CMB_TB, CMB_TB), 0) == _iota((CMB_TB, CMB_TB), 1)
    lo = jnp.sum(jnp.where(eye, lo_ref[...], 0.0), axis=1, keepdims=True)
    hi = jnp.sum(jnp.where(eye, hi_ref[...], 0.0), axis=1, keepdims=True)
    col = _iota((1, CMB_RC), 1)

    def chunk(g, acc):
        @pl.when(g + CMB_BUFS - 1 < total)
        def _():
            chunk_copy(g + CMB_BUFS - 1).start()

        chunk_copy(g).wait()
        nominal = nom_ref[g]
        rows = (col + jnp.minimum(nominal, zrows - CMB_RC)).astype(F32)
        lo_c = jnp.maximum(lo, nominal.astype(F32))
        s = jnp.where(rows >= lo_c, jnp.where(rows < hi, 1.0, 0.0), 0.0).astype(BF16)
        return acc + _mmb(s, _load_token_tiles(zbuf.at[g % CMB_BUFS], CMB_RC))

    moe = lax.fori_loop(first_ref[i], first_ref[i + 1], chunk, jnp.zeros((CMB_TB, D), F32))

    x2 = x1_ref[...] + mod_ref[pl.ds(r, 1), 5 * D:6 * D] * moe
    if final:
        y = x2 * lax.rsqrt(jnp.mean(x2 * x2, axis=-1, keepdims=True) + EPS) * fw_ref[...]

        @pl.when(i * CMB_TB < t1)
        def _():
            outs[0][...] = y

        @pl.when(i * CMB_TB >= t1)
        def _():
            outs[1][...] = y
    else:
        outs[0][...] = x2


def _combine(lo, hi, x1, mod, fw, z, t1, n2, final):
    t_all = x1.shape[0]
    zrows = z.shape[0] // ROW_TILES
    nsteps = t_all // CMB_TB
    base = (lo[0, ::CMB_TB].astype(I32) // 8) * 8
    end = hi[0, CMB_TB - 1::CMB_TB].astype(I32)
    nchunk = jnp.maximum((end - base + CMB_RC - 1) // CMB_RC, 1)
    first = jnp.concatenate([jnp.zeros((1,), I32), jnp.cumsum(nchunk)])
    max_chunks = zrows // CMB_RC + 2 * nsteps
    g = jnp.arange(max_chunks, dtype=I32)
    step_of = jnp.minimum(jnp.sum(first[None, 1:] <= g[:, None], axis=1), nsteps - 1)
    nominal = base[step_of] + (g - first[step_of]) * CMB_RC
    nb1 = t1 // CMB_TB
    if final:
        out_specs = [pl.BlockSpec((CMB_TB, D), lambda i, *_: (jnp.minimum(i, nb1 - 1), 0)),
                     pl.BlockSpec((CMB_TB, D), lambda i, *_: (jnp.maximum(i - nb1, 0), 0))]
        out_shape = [jax.ShapeDtypeStruct((t1, D), F32), jax.ShapeDtypeStruct((t_all - t1, D), F32)]
    else:
        out_specs = pl.BlockSpec((CMB_TB, D), lambda i, *_: (i, 0))
        out_shape = jax.ShapeDtypeStruct((t_all, D), F32)
    return pl.pallas_call(
        functools.partial(_comb_kernel, t1, n2, zrows, final),
        grid_spec=pltpu.PrefetchScalarGridSpec(
            num_scalar_prefetch=2,
            grid=(t_all // CMB_TB,),
            in_specs=[pl.BlockSpec((CMB_TB, D), lambda i, *_: (i, 0)),
                      pl.BlockSpec((1, CMB_TB), lambda i, *_: (0, i)),
                      pl.BlockSpec((1, CMB_TB), lambda i, *_: (0, i)),
                      pl.BlockSpec((8, NMOD * D), lambda i, *_: (0, 0)),
                      pl.BlockSpec((1, D), lambda i, *_: (0, 0)),
                      pl.BlockSpec(memory_space=pl.ANY)],
            out_specs=out_specs,
            scratch_shapes=[pltpu.VMEM((CMB_BUFS, CMB_RC * ROW_TILES, 128), F32),
                            pltpu.SemaphoreType.DMA((CMB_BUFS,))]),
        out_shape=out_shape,
        compiler_params=_params(("arbitrary",)),
        name="combine",
    )(first, nominal, x1, lo, hi, mod, fw, z)


def _grid_pos_embed(n, d):
    rows = n // GRID_W
    r = np.repeat(np.arange(rows, dtype=np.float64), GRID_W)
    col = np.tile(np.arange(GRID_W, dtype=np.float64), rows)
    quarter = d // 4
    omega = np.power(POS_BASE, -np.arange(quarter, dtype=np.float64) / quarter)
    ra = r[:, None] * omega
    ca = col[:, None] * omega
    return jnp.asarray(np.concatenate([np.sin(ra), np.cos(ra), np.sin(ca), np.cos(ca)], axis=-1), F32)


def _decode_slots(slots):
    e, nsb, rows, tb = slots.shape
    nsb -= SPILL_BLOCKS
    slots = slots[:, :nsb].transpose(0, 2, 1, 3).reshape(e, rows, nsb * tb)
    idx = (slots[:, 0] * 128.0 + slots[:, 1]).astype(I32)
    w = slots[:, 2] + slots[:, 3] + slots[:, 4]
    rank = (slots[:, 5] * 128.0 + slots[:, 6]).astype(I32)
    return idx, w, rank


def kernel(x_prompt, x_sample, state_delta, c, c_ctx, w_ada, b_ada, norm1_w, norm2_w, w_in, dn_conv_w,
           dn_a_log, dn_dt_bias, dn_norm_w, sc_conv_w, w_out, w_router, w_gate, w_up, w_down,
           final_norm_w):
    b1, n1, _ = x_prompt.shape
    b2, n2, _ = x_sample.shape
    depth = w_ada.shape[0]
    t1, t2 = b1 * n1, b2 * n2
    t_all = t1 + t2
    cap1 = max(1, 2 * t1 // E)
    cap2 = max(1, 2 * t2 // E)
    tm = 512
    rc = 512 if (cap1 % 512 == 0 and cap2 % 512 == 0) else 128
    assert t1 % n2 == 0 and t1 % tm == 0 and n2 % tm == 0
    assert n1 % (DN_POS * CH) == 0 and n2 % (DN_POS * CH) == 0
    assert t1 % SEL_TB == 0 and t2 % SEL_TB == 0 and t1 % t2 == 0
    assert cap1 % rc == 0 and cap2 % rc == 0 and 2 * t_all >= CMB_RC
    nch = (cap1 + cap2) // rc
    width = H * DH

    cond8 = jnp.zeros((8, D), F32).at[0].set(c_ctx).at[1:1 + b2].set(c)
    mod = _ada(cond8, w_ada, b_ada)

    x = None
    tabs1 = _dft_tables(n1)
    tabs2 = _dft_tables(n2)
    ctx_states = None

    for l in range(depth):
        wl = w_in[l]
        wa = wl[:, 0:P_A_COLS].astype(BF16)
        wb = wl[:, P_A_COLS + 4 * H:].astype(BF16)
        wg = jnp.pad(wl[:, 4 * width:4 * width + 4 * H], ((0, 0), (0, 128 - 4 * H))).astype(BF16)
        al = jnp.pad(dn_a_log[l].reshape(1, 2 * H), ((0, 0), (2 * H, 128 - 4 * H)))
        dt = jnp.pad(dn_dt_bias[l].reshape(1, 2 * H), ((0, 0), (2 * H, 128 - 4 * H)))
        xs = (x,) if l else (x_prompt.reshape(t1, D), x_sample.reshape(t2, D), _grid_pos_embed(n2, D))
        res = _in_proj(xs, mod[l], norm1_w[l].reshape(1, D), wa, wb, wg, al, dt, t1, n2, tm)
        if not l:
            x, res = res[0], res[1:]
        p, g, gt = res

        nwd = dn_norm_w[l].reshape(1, DH)
        dn1, ctx_states = _deltanet(p, g, gt, dn_conv_w[l], nwd, None, ctx_states, b1, n1, 0)
        dn2, _ = _deltanet(p, g, gt, dn_conv_w[l], nwd, state_delta[:, l], None, b2, n2, t1 // n2)
        mx1 = _mix2(p, sc_conv_w[l], tabs1[1], tabs1[2], tabs1[0], b1, n1, 0, MIX_GROUP)
        mx2 = _mix2(p, sc_conv_w[l], tabs2[1], tabs2[2], tabs2[0], b2, n2, t1 // n2, 1)

        wr = jnp.pad(w_router[l], ((0, 0), (0, 128 - E)))
        x1, h2, aff_t = _out_proj(dn1, dn2, mx1, mx2, x, mod[l], norm2_w[l].reshape(1, D),
                                  w_out[l].astype(BF16), wr, t1, n2, tm)

        slots1, lo1, hi1 = _select(aff_t, t1, cap1, 0)
        slots2, lo2, hi2 = _select(aff_t, t2, cap2, t1 // t2)
        idx1, wsel1, rank1 = _decode_slots(slots1)
        idx2, wsel2, rank2 = _decode_slots(slots2)
        idx = jnp.concatenate([idx1, idx2 + t1], axis=1).reshape(-1) * ROW_TILES
        dst = jnp.concatenate([rank1, rank2 + 2 * t1], axis=1).reshape(-1) * ROW_TILES
        wcol = jnp.concatenate([wsel1, wsel2], axis=1).reshape(-1, 1)
        z = _expert_ffn(idx, dst, h2, wcol, w_gate, w_up, w_down, l, rc, nch)

        lo = jnp.concatenate([lo1, lo2 + 2.0 * t1], axis=1)
        hi = jnp.concatenate([hi1, hi2 + 2.0 * t1], axis=1)
        x = _combine(lo, hi, x1, mod[l], final_norm_w.reshape(1, D), z, t1, n2, l == depth - 1)

    y_prompt, y_sample = x
    return y_prompt.reshape(b1, n1, D), y_sample.reshape(b2, n2, D), ctx_states
```

```python
import functools
import math

import jax
import jax.numpy as jnp
import numpy as np
from jax import lax
from jax.experimental import pallas as pl
from jax.experimental.pallas import tpu as pltpu

F32 = jnp.float32
BF16 = jnp.bfloat16
I32 = jnp.int32
HI = lax.Precision.HIGHEST

D = 1024
H = 4
DH = 128
CH = 64
E = 16
NMOD = 6
EPS = 1e-6
GRID_W = 64
POS_BASE = 10000.0
P_COLS = 3072
P_A_COLS = 2048
VMEM_LIMIT = 56 * 1024 * 1024

NN = (((1,), (0,)), ((), ()))
NT = (((1,), (1,)), ((), ()))
TN = (((0,), (0,)), ((), ()))


def _mm(a, b, dims=NN, prec=None):
    return lax.dot_general(a, b, dims, precision=prec, preferred_element_type=F32)


def _mmb(a, b, dims=NN):
    return lax.dot_general(a.astype(BF16), b.astype(BF16), dims, preferred_element_type=F32)


def _mm3(a, b, dims=NN):
    a_hi = a.astype(BF16)
    a_lo = (a - a_hi.astype(F32)).astype(BF16)
    b_hi = b.astype(BF16)
    b_lo = (b - b_hi.astype(F32)).astype(BF16)
    return _mm(a_hi, b_hi, dims) + _mm(a_hi, b_lo, dims) + _mm(a_lo, b_hi, dims)


def _mm_inv(a, b):
    return _mmb(a, b)


def _silu(x):
    return x / (1.0 + jnp.exp(-x))


def _sigmoid(x):
    return 1.0 / (1.0 + jnp.exp(-x))


def _softplus(x):
    return jnp.maximum(x, 0.0) + jnp.log1p(jnp.exp(-jnp.abs(x)))


def _iota(shape, dim):
    return lax.broadcasted_iota(I32, shape, dim)


ROW_TILES = D // 128


def _load_token_tiles(ref, n):
    return jnp.concatenate([ref[pl.ds(s, n, stride=ROW_TILES), :] for s in range(ROW_TILES)], axis=1)


def _store_token_tiles(ref, x):
    n = x.shape[0]
    for s in range(ROW_TILES):
        ref[pl.ds(s, n, stride=ROW_TILES), :] = x[:, s * 128:(s + 1) * 128]


def _mod_row(t0, t1, n2):
    return jnp.where(t0 < t1, 0, 1 + jnp.maximum(t0 - t1, 0) // n2)


def _params(sem):
    return pltpu.CompilerParams(dimension_semantics=sem, vmem_limit_bytes=VMEM_LIMIT)


def _ada_kernel(c_ref, w_ref, b_ref, o_ref):
    s = _silu(c_ref[...])
    o_ref[0] = _mm3(s, w_ref[0]) + b_ref[0]


def _ada(cond8, w_ada, b_ada):
    depth = w_ada.shape[0]
    tn = 1536
    return pl.pallas_call(
        _ada_kernel,
        grid=(depth, NMOD * D // tn),
        in_specs=[pl.BlockSpec((8, D), lambda l, j: (0, 0)),
                  pl.BlockSpec((1, D, tn), lambda l, j: (l, 0, j)),
                  pl.BlockSpec((1, 1, tn), lambda l, j: (l, 0, j))],
        out_specs=pl.BlockSpec((1, 8, tn), lambda l, j: (l, 0, j)),
        out_shape=jax.ShapeDtypeStruct((depth, 8, NMOD * D), F32),
        compiler_params=_params(("arbitrary", "arbitrary")),
        name="ada",
    )(cond8, w_ada, b_ada.reshape(depth, 1, NMOD * D))


IN_SUB = 2


def _in_core(tm, r, x, mod_ref, nw_ref, wa_ref, wb_ref, wg_ref, al_ref, dt_ref, p_ref, g_ref, gt_ref):
    sh = mod_ref[pl.ds(r, 1), 0:D]
    sc = mod_ref[pl.ds(r, 1), D:2 * D]
    sub = tm // IN_SUB
    lane = _iota((1, 128), 1)
    hbs = []
    for k in range(IN_SUB):
        xk = x[k * sub:(k + 1) * sub]
        y = xk * lax.rsqrt(jnp.mean(xk * xk, axis=-1, keepdims=True) + EPS) * nw_ref[...]
        hbs.append((y * (1.0 + sc) + sh).astype(BF16))
    for k in range(IN_SUB):
        rs = slice(k * sub, (k + 1) * sub)
        for j in range(P_A_COLS // D):
            p_ref[rs, j * D:(j + 1) * D] = _mm(hbs[k], wa_ref[:, j * D:(j + 1) * D])
        p_ref[rs, P_A_COLS:P_COLS] = _mm(hbs[k], wb_ref[...])
    for k in range(IN_SUB):
        raw = _mm(hbs[k], wg_ref[...])
        g = -jnp.exp(al_ref[...]) * _softplus(raw + dt_ref[...])
        act = jnp.where(lane < 2 * H, _sigmoid(raw), g)
        g_ref[k * sub:(k + 1) * sub, :] = act[:, 0:4 * H]
        act_t = act.T
        for j in range(sub // CH):
            gt_ref[k * (sub // CH) + j] = act_t[0:4 * H, j * CH:(j + 1) * CH]


def _in_kernel(t1, n2, tm, x_ref, *refs):
    r = _mod_row(pl.program_id(0) * tm, t1, n2)
    _in_core(tm, r, x_ref[...], *refs)


def _in_first_kernel(t1, n2, tm, xp_ref, xs_ref, pos_ref, mod_ref, nw_ref, wa_ref, wb_ref, wg_ref, al_ref,
                     dt_ref, x_ref, p_ref, g_ref, gt_ref):
    t0 = pl.program_id(0) * tm
    x = jnp.where(t0 < t1, xp_ref[...], xs_ref[...] + pos_ref[...])
    x_ref[...] = x
    _in_core(tm, _mod_row(t0, t1, n2), x, mod_ref, nw_ref, wa_ref, wb_ref, wg_ref, al_ref, dt_ref,
             p_ref, g_ref, gt_ref)


def _in_proj(xs, mod, nw, wa, wb, wg, al, dt, t1, n2, tm):
    first = len(xs) == 3
    t_all = t1 + xs[1].shape[0] if first else xs[0].shape[0]
    nt1 = t1 // tm
    wspecs = [pl.BlockSpec((8, NMOD * D), lambda i: (0, 0)),
              pl.BlockSpec((1, D), lambda i: (0, 0)),
              pl.BlockSpec((D, P_A_COLS), lambda i: (0, 0)),
              pl.BlockSpec((D, P_COLS - P_A_COLS), lambda i: (0, 0)),
              pl.BlockSpec((D, 128), lambda i: (0, 0)),
              pl.BlockSpec((1, 128), lambda i: (0, 0)),
              pl.BlockSpec((1, 128), lambda i: (0, 0))]
    out_specs = [pl.BlockSpec((tm, P_COLS), lambda i: (i, 0)),
                 pl.BlockSpec((tm, 4 * H), lambda i: (i, 0)),
                 pl.BlockSpec((tm // CH, 4 * H, CH), lambda i: (i, 0, 0))]
    out_shape = [jax.ShapeDtypeStruct((t_all, P_COLS), F32),
                 jax.ShapeDtypeStruct((t_all, 4 * H), F32),
                 jax.ShapeDtypeStruct((t_all // CH, 4 * H, CH), F32)]
    if first:
        kern = _in_first_kernel
        xspecs = [pl.BlockSpec((tm, D), lambda i: (jnp.minimum(i, nt1 - 1), 0)),
                  pl.BlockSpec((tm, D), lambda i: (jnp.maximum(i - nt1, 0), 0)),
                  pl.BlockSpec((tm, D), lambda i: (jnp.maximum(i - nt1, 0) % (n2 // tm), 0))]
        out_specs = [pl.BlockSpec((tm, D), lambda i: (i, 0))] + out_specs
        out_shape = [jax.ShapeDtypeStruct((t_all, D), F32)] + out_shape
    else:
        kern = _in_kernel
        xspecs = [pl.BlockSpec((tm, D), lambda i: (i, 0))]
    return pl.pallas_call(
        functools.partial(kern, t1, n2, tm),
        grid=(t_all // tm,),
        in_specs=xspecs + wspecs,
        out_specs=out_specs,
        out_shape=out_shape,
        compiler_params=_params(("arbitrary",)),
        name="in_proj_first" if first else "in_proj",
    )(*xs, mod, nw, wa, wb, wg, al, dt)


def _shift_conv(x, w_ref, c0, c1, period=None):
    n = x.shape[0]
    row = _iota((n, 1), 0) & ((period or n) - 1)
    prev = jnp.where(row == 0, 0.0, pltpu.roll(x, 1, 0))
    nxt = jnp.where(row == (period or n) - 1, 0.0, pltpu.roll(x, n - 1, 0))
    return prev * w_ref[0:1, c0:c1] + x * w_ref[1:2, c0:c1] + nxt * w_ref[2:3, c0:c1]


DN_POS = 4


def _dn_kernel(n, nprev, has_s0, q_ref, k_ref, v_ref, z_ref, g_ref, gt_ref, cw_ref, nw_ref, *refs):
    refs = list(refs)
    s0_ref = refs.pop(0) if has_s0 else None
    prev_ref = refs.pop(0) if nprev else None
    o_ref, sfin_ref, qs, ks, vs, gcc_s, gct_s, s_s, oacc = refs
    nc = n // CH
    width = H * DH

    for hh in range(H):
        c0, c1 = hh * DH, (hh + 1) * DH
        q = _silu(_shift_conv(q_ref[:, c0:c1], cw_ref, c0, c1))
        q = q * lax.rsqrt(jnp.sum(q * q, axis=-1, keepdims=True) + 1e-6)
        qs[:, c0:c1] = q * (DH ** -0.5)
        k = _silu(_shift_conv(k_ref[:, c0:c1], cw_ref, width + c0, width + c1))
        ks[:, c0:c1] = k * lax.rsqrt(jnp.sum(k * k, axis=-1, keepdims=True) + 1e-6)
        vs[:, c0:c1] = _silu(_shift_conv(v_ref[:, c0:c1], cw_ref, 2 * width + c0, 2 * width + c1))

    ii = _iota((CH, CH), 0)
    jj = _iota((CH, CH), 1)
    x = gt_ref[...].reshape(nc * 4 * H, CH)
    cf = _mm(x, (ii <= jj).astype(F32), prec=HI)
    cb = _mm(x, (ii >= jj).astype(F32), prec=HI)
    rsel = (_iota((nc * 4 * H, 1), 0) & (4 * H - 1)) < 3 * H
    gct_s[...] = jnp.where(rsel, cf, cb).reshape(nc, 4 * H, CH)
    i2 = _iota((2 * CH, 2 * CH), 0)
    j2 = _iota((2 * CH, 2 * CH), 1)
    same = (i2 >= CH) == (j2 >= CH)
    lf = (same & (i2 >= j2)).astype(F32)
    lb = (same & (i2 <= j2)).astype(F32)
    lsel = _iota((1, 4 * H), 1) < 3 * H
    for m in range(n // (2 * CH)):
        y = g_ref[m * 2 * CH:(m + 1) * 2 * CH, :]
        gcc_s[m * 2 * CH:(m + 1) * 2 * CH, :] = jnp.where(lsel, _mm(lf, y, prec=HI), _mm(lb, y, prec=HI))

    for hh in range(H):
        for d in range(2):
            s_s[hh * 2 + d] = s0_ref[0, d, hh] if has_s0 else jnp.zeros((DH, DH), F32)

    eye = (ii == jj).astype(F32)
    blk_same = [(ii >> s) == (jj >> s) for s in range(1, 7)]
    merge_mask = [blk_same[lvl] & ~blk_same[lvl - 1] for lvl in range(1, len(blk_same))]

    chains = [(hh, d) for hh in range(H) for d in range(2)]

    def chunk_step(cc, carry):
        st = []
        for p in range(DN_POS):
            for hh, d in chains:
                c = cc * DN_POS + p
                cidx = c if d == 0 else nc - 1 - c
                r0 = pl.multiple_of(cidx * CH, CH)
                c0, c1 = hh * DH, (hh + 1) * DH
                gcol = 2 * H + H * d + hh
                bcol = H * d + hh
                incl = (ii >= jj) if d == 0 else (ii <= jj)
                strict = (ii > jj) if d == 0 else (ii < jj)
                qc = qs[pl.ds(r0, CH), c0:c1]
                kc = ks[pl.ds(r0, CH), c0:c1]
                vc = vs[pl.ds(r0, CH), c0:c1]
                beta = g_ref[pl.ds(r0, CH), bcol:bcol + 1]
                gc = gcc_s[pl.ds(r0, CH), gcol:gcol + 1]
                gr = gct_s[cidx, gcol:gcol + 1, :]
                tot = gr[:, CH - 1:CH] if d == 0 else gr[:, 0:1]
                decay = jnp.where(incl, jnp.exp(jnp.where(incl, gc - gr, 0.0)), 0.0)
                kb = kc * beta
                egc = jnp.exp(gc)
                st.append(dict(r0=r0, c0=c0, c1=c1, d=d, strict=strict, kc=kc, kb=kb, decay=decay,
                               rhs=jnp.concatenate([vc * beta, kb * egc], axis=1),
                               qd=qc * egc, qc=qc, kd=kc * jnp.exp(tot - gc), gl=jnp.exp(tot)))
        for x in st:
            x["lmat"] = jnp.where(x["strict"], _mmb(x["kb"], x["kc"], NT) * x["decay"], 0.0)
            x["qk"] = _mmb(x["qc"], x["kc"], NT) * x["decay"]
            x["tinv"] = eye - jnp.where(blk_same[0], x["lmat"], 0.0)
        for mask in merge_mask:
            for x in st:
                x["ot"] = _mm_inv(jnp.where(mask, x["lmat"], 0.0), x["tinv"])
            for x in st:
                x["tinv"] = x["tinv"] - _mm_inv(x["tinv"], x["ot"])
        for x in st:
            x["sol"] = _mm_inv(x["tinv"], x["rhs"])
        state = [s_s[i] for i in range(len(chains))]
        for p in range(DN_POS):
            units = st[p * len(chains):(p + 1) * len(chains)]
            for i, x in enumerate(units):
                x["sb"] = state[i].astype(BF16)
                x["v_new"] = x["sol"][:, 0:DH] - _mmb(x["sol"][:, DH:2 * DH], x["sb"])
            for i, x in enumerate(units):
                o = _mmb(x["qd"], x["sb"]) + _mmb(x["qk"], x["v_new"])
                state[i] = state[i] * x["gl"] + _mmb(x["kd"], x["v_new"], TN)
                oacc[x["d"], pl.ds(x["r0"], CH), x["c0"]:x["c1"]] = o
        for i in range(len(chains)):
            s_s[i] = state[i]
        return carry

    lax.fori_loop(0, nc // DN_POS, chunk_step, 0)

    for hh in range(H):
        c0, c1 = hh * DH, (hh + 1) * DH
        o = oacc[0, :, c0:c1] + oacc[1, :, c0:c1]
        o = o * lax.rsqrt(jnp.mean(o * o, axis=-1, keepdims=True) + EPS)
        o_ref[:, c0:c1] = (o * nw_ref[...] * _silu(z_ref[:, c0:c1])).astype(o_ref.dtype)
        for d in range(2):
            sfin_ref[0, nprev, d, hh] = s_s[hh * 2 + d]
    for layer in range(nprev):
        sfin_ref[0, layer] = prev_ref[0, layer]


def _deltanet(p, g, gt, conv_w, norm_w, s0, prev, bsz, n, row_blk0):
    nc = n // CH
    width = H * DH
    nprev = 0 if prev is None else prev.shape[1]
    state_spec = lambda k: pl.BlockSpec((1, k, 2, H, DH, DH), lambda b: (b, 0, 0, 0, 0, 0))
    opt_args, opt_specs = [], []
    if s0 is not None:
        opt_args.append(s0)
        opt_specs.append(pl.BlockSpec((1, 2, H, DH, DH), lambda b: (b, 0, 0, 0, 0)))
    if prev is not None:
        opt_args.append(prev)
        opt_specs.append(state_spec(nprev))
    return pl.pallas_call(
        functools.partial(_dn_kernel, n, nprev, s0 is not None),
        grid=(bsz,),
        in_specs=[pl.BlockSpec((n, width), lambda b: (row_blk0 + b, 0)),
                  pl.BlockSpec((n, width), lambda b: (row_blk0 + b, 1)),
                  pl.BlockSpec((n, width), lambda b: (row_blk0 + b, 2)),
                  pl.BlockSpec((n, width), lambda b: (row_blk0 + b, 3)),
                  pl.BlockSpec((n, 4 * H), lambda b: (row_blk0 + b, 0)),
                  pl.BlockSpec((nc, 4 * H, CH), lambda b: (row_blk0 + b, 0, 0)),
                  pl.BlockSpec((3, 3 * width), lambda b: (0, 0)),
                  pl.BlockSpec((1, DH), lambda b: (0, 0))] + opt_specs,
        out_specs=[pl.BlockSpec((n, width), lambda b: (b, 0)), state_spec(nprev + 1)],
        out_shape=[jax.ShapeDtypeStruct((bsz * n, width), BF16),
                   jax.ShapeDtypeStruct((bsz, nprev + 1, 2, H, DH, DH), F32)],
        scratch_shapes=[pltpu.VMEM((n, width), F32), pltpu.VMEM((n, width), F32),
                        pltpu.VMEM((n, width), F32), pltpu.VMEM((n, 4 * H), F32),
                        pltpu.VMEM((nc, 4 * H, CH), F32), pltpu.VMEM((2 * H, DH, DH), F32),
                        pltpu.VMEM((2, n, width), F32)],
        compiler_params=_params(("arbitrary",)),
        name="deltanet_n%d" % n,
    )(p, p, p, p, g, gt, conv_w, norm_w, *opt_args)


MIX_GROUP = 4


def _mix2_kernel(n, sb_ref, sc_ref, su_ref, fu_ref, cw_ref, bdc_ref, bds_ref, cs_ref, o_ref):
    rows, w = sb_ref.shape
    cu = sc_ref[...] * su_ref[...]
    o_ref[:, 0:w] = (sb_ref[...] * _shift_conv(cu, cw_ref, 0, w, n)).astype(o_ref.dtype)
    fu = fu_ref[...]
    a = _mm3(fu, bdc_ref[...])
    b = _mm3(fu, bds_ref[...])
    for q in range(rows // n):
        rs = slice(q * n, (q + 1) * n)
        fo = _mmb(cs_ref[...], jnp.concatenate([a[rs], b[rs]], axis=0))
        o_ref[rs, w:2 * w] = fo.astype(o_ref.dtype)


def _mix2(p, conv_w, bdc, bds, cs, bsz, n, row_blk0, group):
    w = 256
    rows = group * n
    blk0 = row_blk0 // group
    assert bsz % group == 0 and row_blk0 % group == 0
    return pl.pallas_call(
        functools.partial(_mix2_kernel, n),
        grid=(bsz // group,),
        in_specs=[pl.BlockSpec((rows, w), lambda b: (blk0 + b, 8)),
                  pl.BlockSpec((rows, w), lambda b: (blk0 + b, 9)),
                  pl.BlockSpec((rows, w), lambda b: (blk0 + b, 10)),
                  pl.BlockSpec((rows, w), lambda b: (blk0 + b, 11)),
                  pl.BlockSpec((3, w), lambda b: (0, 0)),
                  pl.BlockSpec((w, w), lambda b: (0, 0)),
                  pl.BlockSpec((w, w), lambda b: (0, 0)),
                  pl.BlockSpec((n, 2 * n), lambda b: (0, 0))],
        out_specs=pl.BlockSpec((rows, 2 * w), lambda b: (b, 0)),
        out_shape=jax.ShapeDtypeStruct((bsz * n, 2 * w), BF16),
        compiler_params=_params(("arbitrary",)),
        name="conv_fourier_n%d" % n,
    )(p, p, p, p, conv_w, bdc, bds, cs)


def _dft_tables(n):
    gw = 64
    k = np.arange(n, dtype=np.int64)
    ang = ((k[:, None] * k[None, :]) % n) * (2.0 * math.pi / n)
    scale = 1.0 / math.sqrt(n * gw)
    cs = np.concatenate([np.cos(ang), -np.sin(ang)], axis=1) * scale
    c = np.arange(256, dtype=np.int64)
    angc = (((c[:, None] % gw) * (c[None, :] % gw)) % gw) * (2.0 * math.pi / gw)
    same = (c[:, None] // gw) == (c[None, :] // gw)
    bdc = np.where(same, np.cos(angc), 0.0)
    bds = np.where(same, np.sin(angc), 0.0)
    return jnp.asarray(cs, F32).astype(BF16), jnp.asarray(bdc, F32), jnp.asarray(bds, F32)


OUT_SUB = 2


def _out_kernel(t1, n2, tm, dn1_ref, dn2_ref, mx1_ref, mx2_ref, x_ref, mod_ref, nw_ref, wo_ref, wr_ref,
                x1_ref, h2_ref, aff_ref):
    t0 = pl.program_id(0) * tm
    r = _mod_row(t0, t1, n2)
    half = D // 2
    sub = tm // OUT_SUB
    mixes = []
    for k in range(OUT_SUB):
        rs = slice(k * sub, (k + 1) * sub)
        dn = jnp.where(t0 < t1, dn1_ref[rs, :], dn2_ref[rs, :])
        mx = jnp.where(t0 < t1, mx1_ref[rs, :], mx2_ref[rs, :])
        mixes.append(_mmb(dn, wo_ref[0:half, :]) + _mmb(mx, wo_ref[half:D, :]))
    h2s = []
    for k in range(OUT_SUB):
        rs = slice(k * sub, (k + 1) * sub)
        x1 = x_ref[rs, :] + mod_ref[pl.ds(r, 1), 2 * D:3 * D] * mixes[k]
        x1_ref[rs, :] = x1
        y = x1 * lax.rsqrt(jnp.mean(x1 * x1, axis=-1, keepdims=True) + EPS) * nw_ref[...]
        h2 = y * (1.0 + mod_ref[pl.ds(r, 1), 4 * D:5 * D]) + mod_ref[pl.ds(r, 1), 3 * D:4 * D]
        _store_token_tiles(h2_ref.at[pl.ds(k * sub * ROW_TILES, sub * ROW_TILES)], h2)
        h2s.append(h2)
    logits = [_mm3(h2, wr_ref[...]) for h2 in h2s]
    for k in range(OUT_SUB):
        lt = logits[k].T[0:E, :]
        ex = jnp.exp(lt - jnp.max(lt, axis=0, keepdims=True))
        aff_ref[:, k * sub:(k + 1) * sub] = ex / jnp.sum(ex, axis=0, keepdims=True)


def _out_proj(dn1, dn2, mx1, mx2, x, mod, nw, wo, wr, t1, n2, tm):
    t_all = x.shape[0]
    nt1 = t1 // tm
    spec1 = pl.BlockSpec((tm, D // 2), lambda i: (jnp.minimum(i, nt1 - 1), 0))
    spec2 = pl.BlockSpec((tm, D // 2), lambda i: (jnp.maximum(i - nt1, 0), 0))
    return pl.pallas_call(
        functools.partial(_out_kernel, t1, n2, tm),
        grid=(t_all // tm,),
        in_specs=[spec1, spec2, spec1, spec2,
                  pl.BlockSpec((tm, D), lambda i: (i, 0)),
                  pl.BlockSpec((8, NMOD * D), lambda i: (0, 0)),
                  pl.BlockSpec((1, D), lambda i: (0, 0)),
                  pl.BlockSpec((D, D), lambda i: (0, 0)),
                  pl.BlockSpec((D, 128), lambda i: (0, 0))],
        out_specs=[pl.BlockSpec((tm, D), lambda i: (i, 0)),
                   pl.BlockSpec((tm * ROW_TILES, 128), lambda i: (i, 0)),
                   pl.BlockSpec((E, tm), lambda i: (0, i))],
        out_shape=[jax.ShapeDtypeStruct((t_all, D), F32),
                   jax.ShapeDtypeStruct((t_all * ROW_TILES, 128), F32),
                   jax.ShapeDtypeStruct((E, t_all), F32)],
        compiler_params=_params(("arbitrary",)),
        name="out_proj",
    )(dn1, dn2, mx1, mx2, x, mod, nw, wo, wr)


SEL_TB = 128
N_SLOT_ROWS = 16
SPILL_BLOCKS = 2


def _sel_kernel(t, cap, aff_ref, slots_ref, lo_ref, hi_ref, a3, c3, starts_v, starts_s, sem):
    ntb = t // SEL_TB
    aff = aff_ref[...]

    def search(i, thr):
        cand = thr | jnp.left_shift(jnp.int32(1), 30 - i)
        cnt = jnp.sum((aff >= pltpu.bitcast(cand, F32)).astype(F32), axis=1, keepdims=True)
        return jnp.where(cnt >= cap, cand, thr)

    thr = lax.fori_loop(0, 31, search, jnp.zeros((E, 1), I32))
    gt = (aff >= pltpu.bitcast(thr + 1, F32)).astype(F32)
    eq = (aff >= pltpu.bitcast(thr, F32)).astype(F32) - gt
    need = cap - jnp.sum(gt, axis=1, keepdims=True)

    ui = _iota((SEL_TB, SEL_TB), 0)
    uj = _iota((SEL_TB, SEL_TB), 1)
    upper = (ui < uj).astype(BF16)
    blk_lane = _iota((1, 128), 1)

    def excl_cumsum(rows, dst, r0, r1):
        carry = jnp.zeros((rows.shape[0], 1), F32)
        starts = jnp.zeros((rows.shape[0], 128), F32)
        for j in range(ntb):
            blk = rows[:, j * SEL_TB:(j + 1) * SEL_TB]
            dst[j, r0:r1, :] = _mm(blk.astype(BF16), upper) + carry
            starts = jnp.where(blk_lane == j, carry, starts)
            carry = carry + jnp.sum(blk, axis=1, keepdims=True)
        return starts

    excl_cumsum(eq, c3, 0, E)
    rank_eq = jnp.concatenate([c3[j, 0:E, :] for j in range(ntb)], axis=1)
    sel = jnp.maximum(gt, jnp.where(rank_eq < need, eq, 0.0))
    n_tok = jnp.sum(sel, axis=0, keepdims=True)
    starts = excl_cumsum(jnp.concatenate([sel, jnp.broadcast_to(n_tok, (8, t))], axis=0), c3, 0, E + 8)
    starts_v[...] = starts.astype(I32)
    to_smem = pltpu.make_async_copy(starts_v, starts_s, sem)
    to_smem.start()
    ei = _iota((E, E), 0)
    ej = _iota((E, E), 1)
    below = _mm((ej < ei).astype(BF16), sel.astype(BF16))
    for j in range(ntb):
        sl = slice(j * SEL_TB, (j + 1) * SEL_TB)
        off = c3[j, E:E + 1, :]
        lo_ref[:, sl] = off
        hi_ref[:, sl] = off + n_tok[:, sl]
        rank = off + below[:, sl]
        a3[j, 0] = sel[:, sl]
        a3[j, 1] = aff[:, sl]
        a3[j, 2] = rank
    slots_ref[...] = jnp.zeros(slots_ref.shape, F32)
    to_smem.wait()

    win_iota = _iota((2 * SEL_TB, SEL_TB), 0)
    tok_iota = _iota((1, SEL_TB), 1)

    def per_expert(e, carry):
        def per_block(j, carry2):
            wb = starts_s[e, j] >> 7
            chosen = a3[j, 0, pl.ds(e, 1), :]
            pos = c3[j, pl.ds(e, 1), :].astype(I32) - wb * SEL_TB
            pos = jnp.where(chosen > 0.0, pos, -1)
            w = a3[j, 1, pl.ds(e, 1), :]
            rank = a3[j, 2, pl.ds(e, 1), :].astype(I32)
            tok = tok_iota + j * SEL_TB
            w_hi = w.astype(BF16).astype(F32)
            w_mid = (w - w_hi).astype(BF16).astype(F32)
            w_lo = w - w_hi - w_mid
            vals = jnp.concatenate(
                [(tok >> 7).astype(F32), (tok & 127).astype(F32), w_hi, w_mid, w_lo,
                 (rank >> 7).astype(F32), (rank & 127).astype(F32),
                 jnp.zeros((N_SLOT_ROWS - 7, SEL_TB), F32)], axis=0)
            onehot = jnp.where(win_iota == pos, 1.0, 0.0)
            placed = _mmb(vals, onehot, NT)
            slots_ref[e, wb] += placed[:, 0:SEL_TB]
            slots_ref[e, wb + 1] += placed[:, SEL_TB:2 * SEL_TB]
            return carry2
        return lax.fori_loop(0, ntb, per_block, carry, unroll=16)

    lax.fori_loop(0, E, per_expert, 0)


def _select(aff_t, t, cap, col_blk):
    ntb = t // SEL_TB
    nsb = cap // SEL_TB + SPILL_BLOCKS
    return pl.pallas_call(
        functools.partial(_sel_kernel, t, cap),
        grid=(1,),
        in_specs=[pl.BlockSpec((E, t), lambda i: (0, col_blk))],
        out_specs=[pl.BlockSpec((E, nsb, N_SLOT_ROWS, SEL_TB), lambda i: (0, 0, 0, 0)),
                   pl.BlockSpec((1, t), lambda i: (0, 0)),
                   pl.BlockSpec((1, t), lambda i: (0, 0))],
        out_shape=[jax.ShapeDtypeStruct((E, nsb, N_SLOT_ROWS, SEL_TB), F32),
                   jax.ShapeDtypeStruct((1, t), F32),
                   jax.ShapeDtypeStruct((1, t), F32)],
        scratch_shapes=[pltpu.VMEM((ntb, 3, E, SEL_TB), F32),
                        pltpu.VMEM((ntb, E + 8, SEL_TB), F32),
                        pltpu.VMEM((E + 8, 128), I32),
                        pltpu.SMEM((E + 8, 128), I32),
                        pltpu.SemaphoreType.DMA(())],
        compiler_params=_params(("arbitrary",)),
        name="select_t%d" % t,
    )(aff_t)


DMA_UNROLL = 16


def _ffn_kernel(rc, nch, idx_ref, dst_ref, h2_hbm, wcol_ref, wg_ref, wu_ref, wd_ref, z_hbm,
                xbuf, ybuf, wgb, wub, wdb, gsem, ssem):
    c = pl.program_id(1)
    step = pl.program_id(0) * nch + c
    nsteps = E * nch
    slot = step % 2

    def start_gather(s):
        buf = xbuf.at[s % 2]
        sem = gsem.at[s % 2]

        def issue(jb, carry):
            for u in range(DMA_UNROLL):
                j = jb * DMA_UNROLL + u
                src = pl.multiple_of(idx_ref[s * rc + j], ROW_TILES)
                pltpu.make_async_copy(h2_hbm.at[pl.ds(src, ROW_TILES)],
                                      buf.at[pl.ds(j * ROW_TILES, ROW_TILES)], sem).start(priority=u % 2)
            return carry

        lax.fori_loop(0, rc // DMA_UNROLL, issue, 0)

    @pl.when(step == 0)
    def _():
        start_gather(step)

    @pl.when(step + 1 < nsteps)
    def _():
        start_gather(step + 1)

    @pl.when(c == 0)
    def _():
        wgb[...] = wg_ref[0, 0].astype(BF16)
        wub[...] = wu_ref[0, 0].astype(BF16)
        wdb[...] = wd_ref[0, 0].astype(BF16)

    pltpu.make_async_copy(h2_hbm.at[pl.ds(0, rc * ROW_TILES)], xbuf.at[slot], gsem.at[slot]).wait()
    xb = _load_token_tiles(xbuf.at[slot], rc).astype(BF16)
    hid = _silu(_mm(xb, wgb[...])) * _mm(xb, wub[...])
    y = _mm(hid.astype(BF16), wdb[...]) * wcol_ref[...]

    @pl.when(step > 0)
    def _():
        pltpu.make_async_copy(ybuf, z_hbm.at[pl.ds(0, rc * ROW_TILES)], ssem).wait()

    _store_token_tiles(ybuf, y)

    def scatter(jb, carry):
        for u in range(DMA_UNROLL):
            j = jb * DMA_UNROLL + u
            row = pl.multiple_of(dst_ref[step * rc + j], ROW_TILES)
            pltpu.make_async_copy(ybuf.at[pl.ds(j * ROW_TILES, ROW_TILES)],
                                  z_hbm.at[pl.ds(row, ROW_TILES)], ssem).start(priority=u % 2)
        return carry

    lax.fori_loop(0, rc // DMA_UNROLL, scatter, 0)

    @pl.when(step == nsteps - 1)
    def _():
        pltpu.make_async_copy(ybuf, z_hbm.at[pl.ds(0, rc * ROW_TILES)], ssem).wait()


def _expert_ffn(idx, dst, h2, wcol, w_gate, w_up, w_down, layer, rc, nch):
    zrows = E * nch * rc
    wspec = pl.BlockSpec((1, 1, D, D), lambda e, c, *_: (layer, e, 0, 0))
    return pl.pallas_call(
        functools.partial(_ffn_kernel, rc, nch),
        grid_spec=pltpu.PrefetchScalarGridSpec(
            num_scalar_prefetch=2,
            grid=(E, nch),
            in_specs=[pl.BlockSpec(memory_space=pl.ANY),
                      pl.BlockSpec((rc, 1), lambda e, c, *_: (e * nch + c, 0)),
                      wspec, wspec, wspec],
            out_specs=pl.BlockSpec(memory_space=pl.ANY),
            scratch_shapes=[pltpu.VMEM((2, rc * ROW_TILES, 128), F32),
                            pltpu.VMEM((rc * ROW_TILES, 128), F32),
                            pltpu.VMEM((D, D), BF16), pltpu.VMEM((D, D), BF16),
                            pltpu.VMEM((D, D), BF16),
                            pltpu.SemaphoreType.DMA((2,)), pltpu.SemaphoreType.DMA(())]),
        out_shape=jax.ShapeDtypeStruct((zrows * ROW_TILES, 128), F32),
        compiler_params=_params(("arbitrary", "arbitrary")),
        name="expert_ffn",
    )(idx, dst, h2, wcol, w_gate, w_up, w_down)


CMB_TB = 256
CMB_RC = 512
CMB_BUFS = 4


def _pair_row_sums(first_block, nblocks, step0, first_ref, nom_ref, lo_ref, hi_ref, z_hbm, zbuf, sem, zrows):
    total = first_ref[pl.num_programs(0) * nblocks]

    def chunk_copy(g):
        b = pl.multiple_of(jnp.minimum(nom_ref[g], zrows - CMB_RC) * ROW_TILES, 8 * ROW_TILES)
        slot = g % CMB_BUFS
        return pltpu.make_async_copy(z_hbm.at[pl.ds(b, CMB_RC * ROW_TILES)], zbuf.at[slot], sem.at[slot])

    @pl.when(step0)
    def _():
        for g in range(CMB_BUFS - 1):
            @pl.when(g < total)
            def _():
                chunk_copy(g).start()

    eye = _iota((CMB_TB, CMB_TB), 0) == _iota((CMB_TB, CMB_TB), 1)
    col = _iota((1, CMB_RC), 1)
    sums = []
    for k in range(nblocks):
        cols = slice(k * CMB_TB, (k + 1) * CMB_TB)
        lo = jnp.sum(jnp.where(eye, lo_ref[:, cols], 0.0), axis=1, keepdims=True)
        hi = jnp.sum(jnp.where(eye, hi_ref[:, cols], 0.0), axis=1, keepdims=True)

        def chunk(g, acc, lo=lo, hi=hi):
            @pl.when(g + CMB_BUFS - 1 < total)
            def _():
                chunk_copy(g + CMB_BUFS - 1).start()

            chunk_copy(g).wait()
            nominal = nom_ref[g]
            rows = (col + jnp.minimum(nominal, zrows - CMB_RC)).astype(F32)
            lo_c = jnp.maximum(lo, nominal.astype(F32))
            s = jnp.where(rows >= lo_c, jnp.where(rows < hi, 1.0, 0.0), 0.0).astype(BF16)
            return acc + _mmb(s, _load_token_tiles(zbuf.at[g % CMB_BUFS], CMB_RC))

        blk = first_block + k
        sums.append(lax.fori_loop(first_ref[blk], first_ref[blk + 1], chunk, jnp.zeros((CMB_TB, D), F32)))
    return sums


def _comb_kernel(t1, n2, zrows, first_ref, nom_ref, x1_ref, lo_ref, hi_ref, mod_ref, fw_ref, z_hbm,
                 y1_ref, y2_ref, zbuf, sem):
    i = pl.program_id(0)
    r = _mod_row(i * CMB_TB, t1, n2)
    moe, = _pair_row_sums(i, 1, i == 0, first_ref, nom_ref, lo_ref, hi_ref, z_hbm, zbuf, sem, zrows)
    x2 = x1_ref[...] + mod_ref[pl.ds(r, 1), 5 * D:6 * D] * moe
    y = x2 * lax.rsqrt(jnp.mean(x2 * x2, axis=-1, keepdims=True) + EPS) * fw_ref[...]

    @pl.when(i * CMB_TB < t1)
    def _():
        y1_ref[...] = y

    @pl.when(i * CMB_TB >= t1)
    def _():
        y2_ref[...] = y


def _comb_in_kernel(t1, n2, zrows, tm, first_ref, nom_ref, x1_ref, lo_ref, hi_ref, modp_ref, z_hbm,
                    modn_ref, nw_ref, wa_ref, wb_ref, wg_ref, al_ref, dt_ref,
                    x_ref, p_ref, g_ref, gt_ref, zbuf, sem):
    i = pl.program_id(0)
    r = _mod_row(i * tm, t1, n2)
    nblocks = tm // CMB_TB
    sums = _pair_row_sums(i * nblocks, nblocks, i == 0, first_ref, nom_ref, lo_ref, hi_ref, z_hbm,
                          zbuf, sem, zrows)
    gate = modp_ref[pl.ds(r, 1), 5 * D:6 * D]
    xs = []
    for k in range(nblocks):
        rs = slice(k * CMB_TB, (k + 1) * CMB_TB)
        xs.append(x1_ref[rs, :] + gate * sums[k])
        x_ref[rs, :] = xs[k]
    _in_core(tm, r, jnp.concatenate(xs, axis=0), modn_ref, nw_ref, wa_ref, wb_ref, wg_ref, al_ref, dt_ref,
             p_ref, g_ref, gt_ref)


def _chunk_table(lo, hi, zrows):
    nblk = lo.shape[1] // CMB_TB
    base = (lo[0, ::CMB_TB].astype(I32) // 8) * 8
    end = hi[0, CMB_TB - 1::CMB_TB].astype(I32)
    nchunk = jnp.maximum((end - base + CMB_RC - 1) // CMB_RC, 1)
    first = jnp.concatenate([jnp.zeros((1,), I32), jnp.cumsum(nchunk)])
    g = jnp.arange(zrows // CMB_RC + 2 * nblk, dtype=I32)
    blk_of = jnp.minimum(jnp.sum(first[None, 1:] <= g[:, None], axis=1), nblk - 1)
    return first, base[blk_of] + (g - first[blk_of]) * CMB_RC


def _ring_scratch():
    return [pltpu.VMEM((CMB_BUFS, CMB_RC * ROW_TILES, 128), F32), pltpu.SemaphoreType.DMA((CMB_BUFS,))]


def _combine(lo, hi, x1, mod, fw, z, t1, n2):
    t_all = x1.shape[0]
    zrows = z.shape[0] // ROW_TILES
    first, nominal = _chunk_table(lo, hi, zrows)
    nb1 = t1 // CMB_TB
    return pl.pallas_call(
        functools.partial(_comb_kernel, t1, n2, zrows),
        grid_spec=pltpu.PrefetchScalarGridSpec(
            num_scalar_prefetch=2,
            grid=(t_all // CMB_TB,),
            in_specs=[pl.BlockSpec((CMB_TB, D), lambda i, *_: (i, 0)),
                      pl.BlockSpec((1, CMB_TB), lambda i, *_: (0, i)),
                      pl.BlockSpec((1, CMB_TB), lambda i, *_: (0, i)),
                      pl.BlockSpec((8, NMOD * D), lambda i, *_: (0, 0)),
                      pl.BlockSpec((1, D), lambda i, *_: (0, 0)),
                      pl.BlockSpec(memory_space=pl.ANY)],
            out_specs=[pl.BlockSpec((CMB_TB, D), lambda i, *_: (jnp.minimum(i, nb1 - 1), 0)),
                       pl.BlockSpec((CMB_TB, D), lambda i, *_: (jnp.maximum(i - nb1, 0), 0))],
            scratch_shapes=_ring_scratch()),
        out_shape=[jax.ShapeDtypeStruct((t1, D), F32), jax.ShapeDtypeStruct((t_all - t1, D), F32)],
        compiler_params=_params(("arbitrary",)),
        name="combine",
    )(first, nominal, x1, lo, hi, mod, fw, z)


def _combine_in_proj(lo, hi, x1, mod_prev, z, mod_next, nw, wa, wb, wg, al, dt, t1, n2, tm):
    t_all = x1.shape[0]
    zrows = z.shape[0] // ROW_TILES
    first, nominal = _chunk_table(lo, hi, zrows)
    const = lambda shape: pl.BlockSpec(shape, lambda i, *_: (0,) * len(shape))
    return pl.pallas_call(
        functools.partial(_comb_in_kernel, t1, n2, zrows, tm),
        grid_spec=pltpu.PrefetchScalarGridSpec(
            num_scalar_prefetch=2,
            grid=(t_all // tm,),
            in_specs=[pl.BlockSpec((tm, D), lambda i, *_: (i, 0)),
                      pl.BlockSpec((1, tm), lambda i, *_: (0, i)),
                      pl.BlockSpec((1, tm), lambda i, *_: (0, i)),
                      const((8, NMOD * D)),
                      pl.BlockSpec(memory_space=pl.ANY),
                      const((8, NMOD * D)), const((1, D)), const((D, P_A_COLS)),
                      const((D, P_COLS - P_A_COLS)), const((D, 128)), const((1, 128)), const((1, 128))],
            out_specs=[pl.BlockSpec((tm, D), lambda i, *_: (i, 0)),
                       pl.BlockSpec((tm, P_COLS), lambda i, *_: (i, 0)),
                       pl.BlockSpec((tm, 4 * H), lambda i, *_: (i, 0)),
                       pl.BlockSpec((tm // CH, 4 * H, CH), lambda i, *_: (i, 0, 0))],
            scratch_shapes=_ring_scratch()),
        out_shape=[jax.ShapeDtypeStruct((t_all, D), F32),
                   jax.ShapeDtypeStruct((t_all, P_COLS), F32),
                   jax.ShapeDtypeStruct((t_all, 4 * H), F32),
                   jax.ShapeDtypeStruct((t_all // CH, 4 * H, CH), F32)],
        compiler_params=_params(("arbitrary",)),
        name="combine_in_proj",
    )(first, nominal, x1, lo, hi, mod_prev, z, mod_next, nw, wa, wb, wg, al, dt)


def _grid_pos_embed(n, d):
    rows = n // GRID_W
    r = np.repeat(np.arange(rows, dtype=np.float64), GRID_W)
    col = np.tile(np.arange(GRID_W, dtype=np.float64), rows)
    quarter = d // 4
    omega = np.power(POS_BASE, -np.arange(quarter, dtype=np.float64) / quarter)
    ra = r[:, None] * omega
    ca = col[:, None] * omega
    return jnp.asarray(np.concatenate([np.sin(ra), np.cos(ra), np.sin(ca), np.cos(ca)], axis=-1), F32)


def _decode_slots(slots):
    e, nsb, rows, tb = slots.shape
    nsb -= SPILL_BLOCKS
    slots = slots[:, :nsb].transpose(0, 2, 1, 3).reshape(e, rows, nsb * tb)
    idx = (slots[:, 0] * 128.0 + slots[:, 1]).astype(I32)
    w = slots[:, 2] + slots[:, 3] + slots[:, 4]
    rank = (slots[:, 5] * 128.0 + slots[:, 6]).astype(I32)
    return idx, w, rank


def kernel(x_prompt, x_sample, state_delta, c, c_ctx, w_ada, b_ada, norm1_w, norm2_w, w_in, dn_conv_w,
           dn_a_log, dn_dt_bias, dn_norm_w, sc_conv_w, w_out, w_router, w_gate, w_up, w_down,
           final_norm_w):
    b1, n1, _ = x_prompt.shape
    b2, n2, _ = x_sample.shape
    depth = w_ada.shape[0]
    t1, t2 = b1 * n1, b2 * n2
    t_all = t1 + t2
    cap1 = max(1, 2 * t1 // E)
    cap2 = max(1, 2 * t2 // E)
    tm = 512
    rc = 512 if (cap1 % 512 == 0 and cap2 % 512 == 0) else 128
    assert t1 % n2 == 0 and t1 % tm == 0 and n2 % tm == 0
    assert n1 % (DN_POS * CH) == 0 and n2 % (DN_POS * CH) == 0
    assert t1 % SEL_TB == 0 and t2 % SEL_TB == 0 and t1 % t2 == 0
    assert cap1 % rc == 0 and cap2 % rc == 0 and 2 * t_all >= CMB_RC
    nch = (cap1 + cap2) // rc
    width = H * DH

    cond8 = jnp.zeros((8, D), F32).at[0].set(c_ctx).at[1:1 + b2].set(c)
    mod = _ada(cond8, w_ada, b_ada)

    tabs1 = _dft_tables(n1)
    tabs2 = _dft_tables(n2)
    ctx_states = None

    def in_weights(l):
        wl = w_in[l]
        wa = wl[:, 0:P_A_COLS].astype(BF16)
        wb = wl[:, P_A_COLS + 4 * H:].astype(BF16)
        wg = jnp.pad(wl[:, 4 * width:4 * width + 4 * H], ((0, 0), (0, 128 - 4 * H))).astype(BF16)
        al = jnp.pad(dn_a_log[l].reshape(1, 2 * H), ((0, 0), (2 * H, 128 - 4 * H)))
        dt = jnp.pad(dn_dt_bias[l].reshape(1, 2 * H), ((0, 0), (2 * H, 128 - 4 * H)))
        return mod[l], norm1_w[l].reshape(1, D), wa, wb, wg, al, dt

    xs = (x_prompt.reshape(t1, D), x_sample.reshape(t2, D), _grid_pos_embed(n2, D))
    x, p, g, gt = _in_proj(xs, *in_weights(0), t1, n2, tm)
    for l in range(depth):
        nwd = dn_norm_w[l].reshape(1, DH)
        dn1, ctx_states = _deltanet(p, g, gt, dn_conv_w[l], nwd, None, ctx_states, b1, n1, 0)
        dn2, _ = _deltanet(p, g, gt, dn_conv_w[l], nwd, state_delta[:, l], None, b2, n2, t1 // n2)
        mx1 = _mix2(p, sc_conv_w[l], tabs1[1], tabs1[2], tabs1[0], b1, n1, 0, MIX_GROUP)
        mx2 = _mix2(p, sc_conv_w[l], tabs2[1], tabs2[2], tabs2[0], b2, n2, t1 // n2, 1)

        wr = jnp.pad(w_router[l], ((0, 0), (0, 128 - E)))
        x1, h2, aff_t = _out_proj(dn1, dn2, mx1, mx2, x, mod[l], norm2_w[l].reshape(1, D),
                                  w_out[l].astype(BF16), wr, t1, n2, tm)

        slots1, lo1, hi1 = _select(aff_t, t1, cap1, 0)
        slots2, lo2, hi2 = _select(aff_t, t2, cap2, t1 // t2)
        idx1, wsel1, rank1 = _decode_slots(slots1)
        idx2, wsel2, rank2 = _decode_slots(slots2)
        idx = jnp.concatenate([idx1, idx2 + t1], axis=1).reshape(-1) * ROW_TILES
        dst = jnp.concatenate([rank1, rank2 + 2 * t1], axis=1).reshape(-1) * ROW_TILES
        wcol = jnp.concatenate([wsel1, wsel2], axis=1).reshape(-1, 1)
        z = _expert_ffn(idx, dst, h2, wcol, w_gate, w_up, w_down, l, rc, nch)

        lo = jnp.concatenate([lo1, lo2 + 2.0 * t1], axis=1)
        hi = jnp.concatenate([hi1, hi2 + 2.0 * t1], axis=1)
        if l + 1 < depth:
            x, p, g, gt = _combine_in_proj(lo, hi, x1, mod[l], z, *in_weights(l + 1), t1, n2, tm)
        else:
            x = _combine(lo, hi, x1, mod[l], final_norm_w.reshape(1, D), z, t1, n2)

    y_prompt, y_sample = x
    return y_prompt.reshape(b1, n1, D), y_sample.reshape(b2, n2, D), ctx_states
```

```python
import functools
import math

import jax
import jax.numpy as jnp
import numpy as np
from jax import lax
from jax.experimental import pallas as pl
from jax.experimental.pallas import tpu as pltpu

F32 = jnp.float32
BF16 = jnp.bfloat16
I32 = jnp.int32
HI = lax.Precision.HIGHEST

D = 1024
H = 4
DH = 128
CH = 64
E = 16
NMOD = 6
EPS = 1e-6
GRID_W = 64
POS_BASE = 10000.0
P_COLS = 3072
P_A_COLS = 2048
VMEM_LIMIT = 56 * 1024 * 1024

NN = (((1,), (0,)), ((), ()))
NT = (((1,), (1,)), ((), ()))
TN = (((0,), (0,)), ((), ()))


def _mm(a, b, dims=NN, prec=None):
    return lax.dot_general(a, b, dims, precision=prec, preferred_element_type=F32)


def _mmb(a, b, dims=NN):
    return lax.dot_general(a.astype(BF16), b.astype(BF16), dims, preferred_element_type=F32)


def _mm3(a, b, dims=NN):
    a_hi = a.astype(BF16)
    a_lo = (a - a_hi.astype(F32)).astype(BF16)
    b_hi = b.astype(BF16)
    b_lo = (b - b_hi.astype(F32)).astype(BF16)
    return _mm(a_hi, b_hi, dims) + _mm(a_hi, b_lo, dims) + _mm(a_lo, b_hi, dims)


def _mm_inv(a, b):
    return _mmb(a, b)


def _silu(x):
    return x / (1.0 + jnp.exp(-x))


def _sigmoid(x):
    return 1.0 / (1.0 + jnp.exp(-x))


def _softplus(x):
    return jnp.maximum(x, 0.0) + jnp.log1p(jnp.exp(-jnp.abs(x)))


def _iota(shape, dim):
    return lax.broadcasted_iota(I32, shape, dim)


ROW_TILES = D // 128


def _load_token_tiles(ref, n):
    return jnp.concatenate([ref[pl.ds(s, n, stride=ROW_TILES), :] for s in range(ROW_TILES)], axis=1)


def _store_token_tiles(ref, x):
    n = x.shape[0]
    for s in range(ROW_TILES):
        ref[pl.ds(s, n, stride=ROW_TILES), :] = x[:, s * 128:(s + 1) * 128]


def _mod_row(t0, t1, n2):
    return jnp.where(t0 < t1, 0, 1 + jnp.maximum(t0 - t1, 0) // n2)


def _params(sem):
    return pltpu.CompilerParams(dimension_semantics=sem, vmem_limit_bytes=VMEM_LIMIT)


def _ada_kernel(c_ref, w_ref, b_ref, o_ref):
    s = _silu(c_ref[...])
    o_ref[0] = _mm3(s, w_ref[0]) + b_ref[0]


def _ada(cond8, w_ada, b_ada):
    depth = w_ada.shape[0]
    tn = 1536
    return pl.pallas_call(
        _ada_kernel,
        grid=(depth, NMOD * D // tn),
        in_specs=[pl.BlockSpec((8, D), lambda l, j: (0, 0)),
                  pl.BlockSpec((1, D, tn), lambda l, j: (l, 0, j)),
                  pl.BlockSpec((1, 1, tn), lambda l, j: (l, 0, j))],
        out_specs=pl.BlockSpec((1, 8, tn), lambda l, j: (l, 0, j)),
        out_shape=jax.ShapeDtypeStruct((depth, 8, NMOD * D), F32),
        compiler_params=_params(("arbitrary", "arbitrary")),
        name="ada",
    )(cond8, w_ada, b_ada.reshape(depth, 1, NMOD * D))


IN_SUB = 2


def _in_core(tm, r, x, mod_ref, nw_ref, wa_ref, wb_ref, wg_ref, al_ref, dt_ref, p_ref, g_ref, gt_ref):
    sh = mod_ref[pl.ds(r, 1), 0:D]
    sc = mod_ref[pl.ds(r, 1), D:2 * D]
    sub = tm // IN_SUB
    lane = _iota((1, 128), 1)
    hbs = []
    for k in range(IN_SUB):
        xk = x[k * sub:(k + 1) * sub]
        y = xk * lax.rsqrt(jnp.mean(xk * xk, axis=-1, keepdims=True) + EPS) * nw_ref[...]
        hbs.append((y * (1.0 + sc) + sh).astype(BF16))
    for k in range(IN_SUB):
        rs = slice(k * sub, (k + 1) * sub)
        for j in range(P_A_COLS // D):
            p_ref[rs, j * D:(j + 1) * D] = _mm(hbs[k], wa_ref[:, j * D:(j + 1) * D])
        p_ref[rs, P_A_COLS:P_COLS] = _mm(hbs[k], wb_ref[...])
    for k in range(IN_SUB):
        raw = _mm(hbs[k], wg_ref[...])
        g = -jnp.exp(al_ref[...]) * _softplus(raw + dt_ref[...])
        act = jnp.where(lane < 2 * H, _sigmoid(raw), g)
        g_ref[k * sub:(k + 1) * sub, :] = act[:, 0:4 * H]
        act_t = act.T
        for j in range(sub // CH):
            gt_ref[k * (sub // CH) + j] = act_t[0:4 * H, j * CH:(j + 1) * CH]


def _in_kernel(t1, n2, tm, x_ref, *refs):
    r = _mod_row(pl.program_id(0) * tm, t1, n2)
    _in_core(tm, r, x_ref[...], *refs)


def _in_first_kernel(t1, n2, tm, xp_ref, xs_ref, pos_ref, mod_ref, nw_ref, wa_ref, wb_ref, wg_ref, al_ref,
                     dt_ref, x_ref, p_ref, g_ref, gt_ref):
    t0 = pl.program_id(0) * tm
    x = jnp.where(t0 < t1, xp_ref[...], xs_ref[...] + pos_ref[...])
    x_ref[...] = x
    _in_core(tm, _mod_row(t0, t1, n2), x, mod_ref, nw_ref, wa_ref, wb_ref, wg_ref, al_ref, dt_ref,
             p_ref, g_ref, gt_ref)


def _in_proj(xs, mod, nw, wa, wb, wg, al, dt, t1, n2, tm):
    first = len(xs) == 3
    t_all = t1 + xs[1].shape[0] if first else xs[0].shape[0]
    nt1 = t1 // tm
    wspecs = [pl.BlockSpec((8, NMOD * D), lambda i: (0, 0)),
              pl.BlockSpec((1, D), lambda i: (0, 0)),
              pl.BlockSpec((D, P_A_COLS), lambda i: (0, 0)),
              pl.BlockSpec((D, P_COLS - P_A_COLS), lambda i: (0, 0)),
              pl.BlockSpec((D, 128), lambda i: (0, 0)),
              pl.BlockSpec((1, 128), lambda i: (0, 0)),
              pl.BlockSpec((1, 128), lambda i: (0, 0))]
    out_specs = [pl.BlockSpec((tm, P_COLS), lambda i: (i, 0)),
                 pl.BlockSpec((tm, 4 * H), lambda i: (i, 0)),
                 pl.BlockSpec((tm // CH, 4 * H, CH), lambda i: (i, 0, 0))]
    out_shape = [jax.ShapeDtypeStruct((t_all, P_COLS), F32),
                 jax.ShapeDtypeStruct((t_all, 4 * H), F32),
                 jax.ShapeDtypeStruct((t_all // CH, 4 * H, CH), F32)]
    if first:
        kern = _in_first_kernel
        xspecs = [pl.BlockSpec((tm, D), lambda i: (jnp.minimum(i, nt1 - 1), 0)),
                  pl.BlockSpec((tm, D), lambda i: (jnp.maximum(i - nt1, 0), 0)),
                  pl.BlockSpec((tm, D), lambda i: (jnp.maximum(i - nt1, 0) % (n2 // tm), 0))]
        out_specs = [pl.BlockSpec((tm, D), lambda i: (i, 0))] + out_specs
        out_shape = [jax.ShapeDtypeStruct((t_all, D), F32)] + out_shape
    else:
        kern = _in_kernel
        xspecs = [pl.BlockSpec((tm, D), lambda i: (i, 0))]
    return pl.pallas_call(
        functools.partial(kern, t1, n2, tm),
        grid=(t_all // tm,),
        in_specs=xspecs + wspecs,
        out_specs=out_specs,
        out_shape=out_shape,
        compiler_params=_params(("arbitrary",)),
        name="in_proj_first" if first else "in_proj",
    )(*xs, mod, nw, wa, wb, wg, al, dt)


def _shift_conv(x, w_ref, c0, c1, period=None):
    n = x.shape[0]
    row = _iota((n, 1), 0) & ((period or n) - 1)
    prev = jnp.where(row == 0, 0.0, pltpu.roll(x, 1, 0))
    nxt = jnp.where(row == (period or n) - 1, 0.0, pltpu.roll(x, n - 1, 0))
    return prev * w_ref[0:1, c0:c1] + x * w_ref[1:2, c0:c1] + nxt * w_ref[2:3, c0:c1]


DN_POS = 4


def _dn_kernel(n, nprev, has_s0, q_ref, k_ref, v_ref, z_ref, g_ref, gt_ref, cw_ref, nw_ref, *refs):
    refs = list(refs)
    s0_ref = refs.pop(0) if has_s0 else None
    prev_ref = refs.pop(0) if nprev else None
    o_ref, sfin_ref, qs, ks, vs, gcc_s, gct_s, s_s, oacc = refs
    nc = n // CH
    width = H * DH

    for hh in range(H):
        c0, c1 = hh * DH, (hh + 1) * DH
        q = _silu(_shift_conv(q_ref[:, c0:c1], cw_ref, c0, c1))
        q = q * lax.rsqrt(jnp.sum(q * q, axis=-1, keepdims=True) + 1e-6)
        qs[:, c0:c1] = q * (DH ** -0.5)
        k = _silu(_shift_conv(k_ref[:, c0:c1], cw_ref, width + c0, width + c1))
        ks[:, c0:c1] = k * lax.rsqrt(jnp.sum(k * k, axis=-1, keepdims=True) + 1e-6)
        vs[:, c0:c1] = _silu(_shift_conv(v_ref[:, c0:c1], cw_ref, 2 * width + c0, 2 * width + c1))

    ii = _iota((CH, CH), 0)
    jj = _iota((CH, CH), 1)
    x = gt_ref[...].reshape(nc * 4 * H, CH)
    cf = _mm(x, (ii <= jj).astype(F32), prec=HI)
    cb = _mm(x, (ii >= jj).astype(F32), prec=HI)
    rsel = (_iota((nc * 4 * H, 1), 0) & (4 * H - 1)) < 3 * H
    gct_s[...] = jnp.where(rsel, cf, cb).reshape(nc, 4 * H, CH)
    i2 = _iota((2 * CH, 2 * CH), 0)
    j2 = _iota((2 * CH, 2 * CH), 1)
    same = (i2 >= CH) == (j2 >= CH)
    lf = (same & (i2 >= j2)).astype(F32)
    lb = (same & (i2 <= j2)).astype(F32)
    lsel = _iota((1, 4 * H), 1) < 3 * H
    for m in range(n // (2 * CH)):
        y = g_ref[m * 2 * CH:(m + 1) * 2 * CH, :]
        gcc_s[m * 2 * CH:(m + 1) * 2 * CH, :] = jnp.where(lsel, _mm(lf, y, prec=HI), _mm(lb, y, prec=HI))

    for hh in range(H):
        for d in range(2):
            s_s[hh * 2 + d] = s0_ref[0, d, hh] if has_s0 else jnp.zeros((DH, DH), F32)

    eye = (ii == jj).astype(F32)
    blk_same = [(ii >> s) == (jj >> s) for s in range(1, 7)]
    merge_mask = [blk_same[lvl] & ~blk_same[lvl - 1] for lvl in range(1, len(blk_same))]

    chains = [(hh, d) for hh in range(H) for d in range(2)]

    def chunk_step(cc, carry):
        st = []
        for p in range(DN_POS):
            for hh, d in chains:
                c = cc * DN_POS + p
                cidx = c if d == 0 else nc - 1 - c
                r0 = pl.multiple_of(cidx * CH, CH)
                c0, c1 = hh * DH, (hh + 1) * DH
                gcol = 2 * H + H * d + hh
                bcol = H * d + hh
                incl = (ii >= jj) if d == 0 else (ii <= jj)
                strict = (ii > jj) if d == 0 else (ii < jj)
                qc = qs[pl.ds(r0, CH), c0:c1]
                kc = ks[pl.ds(r0, CH), c0:c1]
                vc = vs[pl.ds(r0, CH), c0:c1]
                beta = g_ref[pl.ds(r0, CH), bcol:bcol + 1]
                gc = gcc_s[pl.ds(r0, CH), gcol:gcol + 1]
                gr = gct_s[cidx, gcol:gcol + 1, :]
                tot = gr[:, CH - 1:CH] if d == 0 else gr[:, 0:1]
                decay = jnp.where(incl, jnp.exp(jnp.where(incl, gc - gr, 0.0)), 0.0)
                kb = kc * beta
                egc = jnp.exp(gc)
                st.append(dict(r0=r0, c0=c0, c1=c1, d=d, strict=strict, kc=kc, kb=kb, decay=decay,
                               rhs=jnp.concatenate([vc * beta, kb * egc], axis=1),
                               qd=qc * egc, qc=qc, kd=kc * jnp.exp(tot - gc), gl=jnp.exp(tot)))
        for x in st:
            x["lmat"] = jnp.where(x["strict"], _mmb(x["kb"], x["kc"], NT) * x["decay"], 0.0)
            x["qk"] = _mmb(x["qc"], x["kc"], NT) * x["decay"]
            x["tinv"] = eye - jnp.where(blk_same[0], x["lmat"], 0.0)
        for mask in merge_mask:
            for x in st:
                x["ot"] = _mm_inv(jnp.where(mask, x["lmat"], 0.0), x["tinv"])
            for x in st:
                x["tinv"] = x["tinv"] - _mm_inv(x["tinv"], x["ot"])
        for x in st:
            x["sol"] = _mm_inv(x["tinv"], x["rhs"])
        state = [s_s[i] for i in range(len(chains))]
        for p in range(DN_POS):
            units = st[p * len(chains):(p + 1) * len(chains)]
            for i, x in enumerate(units):
                x["sb"] = state[i].astype(BF16)
                x["v_new"] = x["sol"][:, 0:DH] - _mmb(x["sol"][:, DH:2 * DH], x["sb"])
            for i, x in enumerate(units):
                o = _mmb(x["qd"], x["sb"]) + _mmb(x["qk"], x["v_new"])
                state[i] = state[i] * x["gl"] + _mmb(x["kd"], x["v_new"], TN)
                oacc[x["d"], pl.ds(x["r0"], CH), x["c0"]:x["c1"]] = o
        for i in range(len(chains)):
            s_s[i] = state[i]
        return carry

    lax.fori_loop(0, nc // DN_POS, chunk_step, 0)

    for hh in range(H):
        c0, c1 = hh * DH, (hh + 1) * DH
        o = oacc[0, :, c0:c1] + oacc[1, :, c0:c1]
        o = o * lax.rsqrt(jnp.mean(o * o, axis=-1, keepdims=True) + EPS)
        o_ref[:, c0:c1] = (o * nw_ref[...] * _silu(z_ref[:, c0:c1])).astype(o_ref.dtype)
        for d in range(2):
            sfin_ref[0, nprev, d, hh] = s_s[hh * 2 + d]
    for layer in range(nprev):
        sfin_ref[0, layer] = prev_ref[0, layer]


def _deltanet(p, g, gt, conv_w, norm_w, s0, prev, bsz, n, row_blk0):
    nc = n // CH
    width = H * DH
    nprev = 0 if prev is None else prev.shape[1]
    state_spec = lambda k: pl.BlockSpec((1, k, 2, H, DH, DH), lambda b: (b, 0, 0, 0, 0, 0))
    opt_args, opt_specs = [], []
    if s0 is not None:
        opt_args.append(s0)
        opt_specs.append(pl.BlockSpec((1, 2, H, DH, DH), lambda b: (b, 0, 0, 0, 0)))
    if prev is not None:
        opt_args.append(prev)
        opt_specs.append(state_spec(nprev))
    return pl.pallas_call(
        functools.partial(_dn_kernel, n, nprev, s0 is not None),
        grid=(bsz,),
        in_specs=[pl.BlockSpec((n, width), lambda b: (row_blk0 + b, 0)),
                  pl.BlockSpec((n, width), lambda b: (row_blk0 + b, 1)),
                  pl.BlockSpec((n, width), lambda b: (row_blk0 + b, 2)),
                  pl.BlockSpec((n, width), lambda b: (row_blk0 + b, 3)),
                  pl.BlockSpec((n, 4 * H), lambda b: (row_blk0 + b, 0)),
                  pl.BlockSpec((nc, 4 * H, CH), lambda b: (row_blk0 + b, 0, 0)),
                  pl.BlockSpec((3, 3 * width), lambda b: (0, 0)),
                  pl.BlockSpec((1, DH), lambda b: (0, 0))] + opt_specs,
        out_specs=[pl.BlockSpec((n, width), lambda b: (b, 0)), state_spec(nprev + 1)],
        out_shape=[jax.ShapeDtypeStruct((bsz * n, width), BF16),
                   jax.ShapeDtypeStruct((bsz, nprev + 1, 2, H, DH, DH), F32)],
        scratch_shapes=[pltpu.VMEM((n, width), F32), pltpu.VMEM((n, width), F32),
                        pltpu.VMEM((n, width), F32), pltpu.VMEM((n, 4 * H), F32),
                        pltpu.VMEM((nc, 4 * H, CH), F32), pltpu.VMEM((2 * H, DH, DH), F32),
                        pltpu.VMEM((2, n, width), F32)],
        compiler_params=_params(("arbitrary",)),
        name="deltanet_n%d" % n,
    )(p, p, p, p, g, gt, conv_w, norm_w, *opt_args)


MIX_GROUP = 8


def _mix2_kernel(n, sb_ref, sc_ref, su_ref, fu_ref, cw_ref, bdc_ref, bds_ref, cs_ref, o_ref):
    rows, w = sb_ref.shape
    cu = sc_ref[...] * su_ref[...]
    o_ref[:, 0:w] = (sb_ref[...] * _shift_conv(cu, cw_ref, 0, w, n)).astype(o_ref.dtype)
    fu = fu_ref[...]
    a = _mm3(fu, bdc_ref[...])
    b = _mm3(fu, bds_ref[...])
    for q in range(rows // n):
        rs = slice(q * n, (q + 1) * n)
        fo = _mmb(cs_ref[...], jnp.concatenate([a[rs], b[rs]], axis=0))
        o_ref[rs, w:2 * w] = fo.astype(o_ref.dtype)


def _mix2(p, conv_w, bdc, bds, cs, bsz, n, row_blk0, group):
    w = 256
    rows = group * n
    blk0 = row_blk0 // group
    assert bsz % group == 0 and row_blk0 % group == 0
    return pl.pallas_call(
        functools.partial(_mix2_kernel, n),
        grid=(bsz // group,),
        in_specs=[pl.BlockSpec((rows, w), lambda b: (blk0 + b, 8)),
                  pl.BlockSpec((rows, w), lambda b: (blk0 + b, 9)),
                  pl.BlockSpec((rows, w), lambda b: (blk0 + b, 10)),
                  pl.BlockSpec((rows, w), lambda b: (blk0 + b, 11)),
                  pl.BlockSpec((3, w), lambda b: (0, 0)),
                  pl.BlockSpec((w, w), lambda b: (0, 0)),
                  pl.BlockSpec((w, w), lambda b: (0, 0)),
                  pl.BlockSpec((n, 2 * n), lambda b: (0, 0))],
        out_specs=pl.BlockSpec((rows, 2 * w), lambda b: (b, 0)),
        out_shape=jax.ShapeDtypeStruct((bsz * n, 2 * w), BF16),
        compiler_params=_params(("arbitrary",)),
        name="conv_fourier_n%d" % n,
    )(p, p, p, p, conv_w, bdc, bds, cs)


def _dft_tables(n):
    gw = 64
    k = np.arange(n, dtype=np.int64)
    ang = ((k[:, None] * k[None, :]) % n) * (2.0 * math.pi / n)
    scale = 1.0 / math.sqrt(n * gw)
    cs = np.concatenate([np.cos(ang), -np.sin(ang)], axis=1) * scale
    c = np.arange(256, dtype=np.int64)
    angc = (((c[:, None] % gw) * (c[None, :] % gw)) % gw) * (2.0 * math.pi / gw)
    same = (c[:, None] // gw) == (c[None, :] // gw)
    bdc = np.where(same, np.cos(angc), 0.0)
    bds = np.where(same, np.sin(angc), 0.0)
    return jnp.asarray(cs, F32).astype(BF16), jnp.asarray(bdc, F32), jnp.asarray(bds, F32)


OUT_SUB = 2


def _out_kernel(t1, n2, tm, dn1_ref, dn2_ref, mx1_ref, mx2_ref, x_ref, mod_ref, nw_ref, wo_ref, wr_ref,
                x1_ref, h2_ref, aff_ref):
    t0 = pl.program_id(0) * tm
    r = _mod_row(t0, t1, n2)
    half = D // 2
    sub = tm // OUT_SUB
    mixes = []
    for k in range(OUT_SUB):
        rs = slice(k * sub, (k + 1) * sub)
        dn = jnp.where(t0 < t1, dn1_ref[rs, :], dn2_ref[rs, :])
        mx = jnp.where(t0 < t1, mx1_ref[rs, :], mx2_ref[rs, :])
        mixes.append(_mmb(dn, wo_ref[0:half, :]) + _mmb(mx, wo_ref[half:D, :]))
    h2s = []
    for k in range(OUT_SUB):
        rs = slice(k * sub, (k + 1) * sub)
        x1 = x_ref[rs, :] + mod_ref[pl.ds(r, 1), 2 * D:3 * D] * mixes[k]
        x1_ref[rs, :] = x1
        y = x1 * lax.rsqrt(jnp.mean(x1 * x1, axis=-1, keepdims=True) + EPS) * nw_ref[...]
        h2 = y * (1.0 + mod_ref[pl.ds(r, 1), 4 * D:5 * D]) + mod_ref[pl.ds(r, 1), 3 * D:4 * D]
        _store_token_tiles(h2_ref.at[pl.ds(k * sub * ROW_TILES, sub * ROW_TILES)], h2)
        h2s.append(h2)
    logits = [_mm3(h2, wr_ref[...]) for h2 in h2s]
    for k in range(OUT_SUB):
        lt = logits[k].T[0:E, :]
        ex = jnp.exp(lt - jnp.max(lt, axis=0, keepdims=True))
        aff_ref[:, k * sub:(k + 1) * sub] = ex / jnp.sum(ex, axis=0, keepdims=True)


def _out_proj(dn1, dn2, mx1, mx2, x, mod, nw, wo, wr, t1, n2, tm):
    t_all = x.shape[0]
    nt1 = t1 // tm
    spec1 = pl.BlockSpec((tm, D // 2), lambda i: (jnp.minimum(i, nt1 - 1), 0))
    spec2 = pl.BlockSpec((tm, D // 2), lambda i: (jnp.maximum(i - nt1, 0), 0))
    return pl.pallas_call(
        functools.partial(_out_kernel, t1, n2, tm),
        grid=(t_all // tm,),
        in_specs=[spec1, spec2, spec1, spec2,
                  pl.BlockSpec((tm, D), lambda i: (i, 0)),
                  pl.BlockSpec((8, NMOD * D), lambda i: (0, 0)),
                  pl.BlockSpec((1, D), lambda i: (0, 0)),
                  pl.BlockSpec((D, D), lambda i: (0, 0)),
                  pl.BlockSpec((D, 128), lambda i: (0, 0))],
        out_specs=[pl.BlockSpec((tm, D), lambda i: (i, 0)),
                   pl.BlockSpec((tm * ROW_TILES, 128), lambda i: (i, 0)),
                   pl.BlockSpec((E, tm), lambda i: (0, i))],
        out_shape=[jax.ShapeDtypeStruct((t_all, D), F32),
                   jax.ShapeDtypeStruct((t_all * ROW_TILES, 128), F32),
                   jax.ShapeDtypeStruct((E, t_all), F32)],
        compiler_params=_params(("arbitrary",)),
        name="out_proj",
    )(dn1, dn2, mx1, mx2, x, mod, nw, wo, wr)


SEL_TB = 128
N_SLOT_ROWS = 16
SPILL_BLOCKS = 2


def _sel_kernel(t, cap, aff_ref, slots_ref, lo_ref, hi_ref, a3, c3, starts_v, starts_s, sem):
    ntb = t // SEL_TB
    aff = aff_ref[...]

    def search(i, thr):
        cand = thr | jnp.left_shift(jnp.int32(1), 30 - i)
        cnt = jnp.sum((aff >= pltpu.bitcast(cand, F32)).astype(F32), axis=1, keepdims=True)
        return jnp.where(cnt >= cap, cand, thr)

    thr = lax.fori_loop(0, 31, search, jnp.zeros((E, 1), I32))
    gt = (aff >= pltpu.bitcast(thr + 1, F32)).astype(F32)
    eq = (aff >= pltpu.bitcast(thr, F32)).astype(F32) - gt
    need = cap - jnp.sum(gt, axis=1, keepdims=True)

    ui = _iota((SEL_TB, SEL_TB), 0)
    uj = _iota((SEL_TB, SEL_TB), 1)
    upper = (ui < uj).astype(BF16)
    blk_lane = _iota((1, 128), 1)

    def excl_cumsum(rows, dst, r0, r1):
        carry = jnp.zeros((rows.shape[0], 1), F32)
        starts = jnp.zeros((rows.shape[0], 128), F32)
        for j in range(ntb):
            blk = rows[:, j * SEL_TB:(j + 1) * SEL_TB]
            dst[j, r0:r1, :] = _mm(blk.astype(BF16), upper) + carry
            starts = jnp.where(blk_lane == j, carry, starts)
            carry = carry + jnp.sum(blk, axis=1, keepdims=True)
        return starts

    excl_cumsum(eq, c3, 0, E)
    rank_eq = jnp.concatenate([c3[j, 0:E, :] for j in range(ntb)], axis=1)
    sel = jnp.maximum(gt, jnp.where(rank_eq < need, eq, 0.0))
    n_tok = jnp.sum(sel, axis=0, keepdims=True)
    starts = excl_cumsum(jnp.concatenate([sel, jnp.broadcast_to(n_tok, (8, t))], axis=0), c3, 0, E + 8)
    starts_v[...] = starts.astype(I32)
    to_smem = pltpu.make_async_copy(starts_v, starts_s, sem)
    to_smem.start()
    ei = _iota((E, E), 0)
    ej = _iota((E, E), 1)
    below = _mm((ej < ei).astype(BF16), sel.astype(BF16))
    for j in range(ntb):
        sl = slice(j * SEL_TB, (j + 1) * SEL_TB)
        off = c3[j, E:E + 1, :]
        lo_ref[:, sl] = off
        hi_ref[:, sl] = off + n_tok[:, sl]
        rank = off + below[:, sl]
        a3[j, 0] = sel[:, sl]
        a3[j, 1] = aff[:, sl]
        a3[j, 2] = rank
    slots_ref[...] = jnp.zeros(slots_ref.shape, F32)
    to_smem.wait()

    win_iota = _iota((2 * SEL_TB, SEL_TB), 0)
    tok_iota = _iota((1, SEL_TB), 1)

    def per_expert(e, carry):
        def per_block(j, carry2):
            wb = starts_s[e, j] >> 7
            chosen = a3[j, 0, pl.ds(e, 1), :]
            pos = c3[j, pl.ds(e, 1), :].astype(I32) - wb * SEL_TB
            pos = jnp.where(chosen > 0.0, pos, -1)
            w = a3[j, 1, pl.ds(e, 1), :]
            rank = a3[j, 2, pl.ds(e, 1), :].astype(I32)
            tok = tok_iota + j * SEL_TB
            w_hi = w.astype(BF16).astype(F32)
            w_mid = (w - w_hi).astype(BF16).astype(F32)
            w_lo = w - w_hi - w_mid
            vals = jnp.concatenate(
                [(tok >> 7).astype(F32), (tok & 127).astype(F32), w_hi, w_mid, w_lo,
                 (rank >> 7).astype(F32), (rank & 127).astype(F32),
                 jnp.zeros((N_SLOT_ROWS - 7, SEL_TB), F32)], axis=0)
            onehot = jnp.where(win_iota == pos, 1.0, 0.0)
            placed = _mmb(vals, onehot, NT)
            slots_ref[e, wb] += placed[:, 0:SEL_TB]
            slots_ref[e, wb + 1] += placed[:, SEL_TB:2 * SEL_TB]
            return carry2
        return lax.fori_loop(0, ntb, per_block, carry, unroll=16)

    lax.fori_loop(0, E, per_expert, 0)


def _select(aff_t, t, cap, col_blk):
    ntb = t // SEL_TB
    nsb = cap // SEL_TB + SPILL_BLOCKS
    return pl.pallas_call(
        functools.partial(_sel_kernel, t, cap),
        grid=(1,),
        in_specs=[pl.BlockSpec((E, t), lambda i: (0, col_blk))],
        out_specs=[pl.BlockSpec((E, nsb, N_SLOT_ROWS, SEL_TB), lambda i: (0, 0, 0, 0)),
                   pl.BlockSpec((1, t), lambda i: (0, 0)),
                   pl.BlockSpec((1, t), lambda i: (0, 0))],
        out_shape=[jax.ShapeDtypeStruct((E, nsb, N_SLOT_ROWS, SEL_TB), F32),
                   jax.ShapeDtypeStruct((1, t), F32),
                   jax.ShapeDtypeStruct((1, t), F32)],
        scratch_shapes=[pltpu.VMEM((ntb, 3, E, SEL_TB), F32),
                        pltpu.VMEM((ntb, E + 8, SEL_TB), F32),
                        pltpu.VMEM((E + 8, 128), I32),
                        pltpu.SMEM((E + 8, 128), I32),
                        pltpu.SemaphoreType.DMA(())],
        compiler_params=_params(("arbitrary",)),
        name="select_t%d" % t,
    )(aff_t)


DMA_UNROLL = 16


def _ffn_kernel(rc, nch, idx_ref, dst_ref, h2_hbm, wcol_ref, wg_ref, wu_ref, wd_ref, z_hbm,
                xbuf, ybuf, wgb, wub, wdb, gsem, ssem):
    c = pl.program_id(1)
    step = pl.program_id(0) * nch + c
    nsteps = E * nch
    slot = step % 2

    def start_gather(s):
        buf = xbuf.at[s % 2]
        sem = gsem.at[s % 2]

        def issue(jb, carry):
            for u in range(DMA_UNROLL):
                j = jb * DMA_UNROLL + u
                src = pl.multiple_of(idx_ref[s * rc + j], ROW_TILES)
                pltpu.make_async_copy(h2_hbm.at[pl.ds(src, ROW_TILES)],
                                      buf.at[pl.ds(j * ROW_TILES, ROW_TILES)], sem).start(priority=u % 2)
            return carry

        lax.fori_loop(0, rc // DMA_UNROLL, issue, 0)

    @pl.when(step == 0)
    def _():
        start_gather(step)

    @pl.when(step + 1 < nsteps)
    def _():
        start_gather(step + 1)

    @pl.when(c == 0)
    def _():
        wgb[...] = wg_ref[0, 0].astype(BF16)
        wub[...] = wu_ref[0, 0].astype(BF16)
        wdb[...] = wd_ref[0, 0].astype(BF16)

    pltpu.make_async_copy(h2_hbm.at[pl.ds(0, rc * ROW_TILES)], xbuf.at[slot], gsem.at[slot]).wait()
    xb = _load_token_tiles(xbuf.at[slot], rc).astype(BF16)
    hid = _silu(_mm(xb, wgb[...])) * _mm(xb, wub[...])
    y = _mm(hid.astype(BF16), wdb[...]) * wcol_ref[...]

    @pl.when(step > 0)
    def _():
        pltpu.make_async_copy(ybuf, z_hbm.at[pl.ds(0, rc * ROW_TILES)], ssem).wait()

    _store_token_tiles(ybuf, y)

    def scatter(jb, carry):
        for u in range(DMA_UNROLL):
            j = jb * DMA_UNROLL + u
            row = pl.multiple_of(dst_ref[step * rc + j], ROW_TILES)
            pltpu.make_async_copy(ybuf.at[pl.ds(j * ROW_TILES, ROW_TILES)],
                                  z_hbm.at[pl.ds(row, ROW_TILES)], ssem).start(priority=u % 2)
        return carry

    lax.fori_loop(0, rc // DMA_UNROLL, scatter, 0)

    @pl.when(step == nsteps - 1)
    def _():
        pltpu.make_async_copy(ybuf, z_hbm.at[pl.ds(0, rc * ROW_TILES)], ssem).wait()


def _expert_ffn(idx, dst, h2, wcol, w_gate, w_up, w_down, layer, rc, nch):
    zrows = E * nch * rc
    wspec = pl.BlockSpec((1, 1, D, D), lambda e, c, *_: (layer, e, 0, 0))
    return pl.pallas_call(
        functools.partial(_ffn_kernel, rc, nch),
        grid_spec=pltpu.PrefetchScalarGridSpec(
            num_scalar_prefetch=2,
            grid=(E, nch),
            in_specs=[pl.BlockSpec(memory_space=pl.ANY),
                      pl.BlockSpec((rc, 1), lambda e, c, *_: (e * nch + c, 0)),
                      wspec, wspec, wspec],
            out_specs=pl.BlockSpec(memory_space=pl.ANY),
            scratch_shapes=[pltpu.VMEM((2, rc * ROW_TILES, 128), F32),
                            pltpu.VMEM((rc * ROW_TILES, 128), F32),
                            pltpu.VMEM((D, D), BF16), pltpu.VMEM((D, D), BF16),
                            pltpu.VMEM((D, D), BF16),
                            pltpu.SemaphoreType.DMA((2,)), pltpu.SemaphoreType.DMA(())]),
        out_shape=jax.ShapeDtypeStruct((zrows * ROW_TILES, 128), F32),
        compiler_params=_params(("arbitrary", "arbitrary")),
        name="expert_ffn",
    )(idx, dst, h2, wcol, w_gate, w_up, w_down)


CMB_TB = 256
CMB_RC = 512
CMB_BUFS = 4


def _pair_row_sums(first_block, nblocks, step0, first_ref, nom_ref, lo_ref, hi_ref, z_hbm, zbuf, sem, zrows):
    total = first_ref[pl.num_programs(0) * nblocks]

    def chunk_copy(g):
        b = pl.multiple_of(jnp.minimum(nom_ref[g], zrows - CMB_RC) * ROW_TILES, 8 * ROW_TILES)
        slot = g % CMB_BUFS
        return pltpu.make_async_copy(z_hbm.at[pl.ds(b, CMB_RC * ROW_TILES)], zbuf.at[slot], sem.at[slot])

    @pl.when(step0)
    def _():
        for g in range(CMB_BUFS - 1):
            @pl.when(g < total)
            def _():
                chunk_copy(g).start()

    eye = _iota((CMB_TB, CMB_TB), 0) == _iota((CMB_TB, CMB_TB), 1)
    col = _iota((1, CMB_RC), 1)
    sums = []
    for k in range(nblocks):
        cols = slice(k * CMB_TB, (k + 1) * CMB_TB)
        lo = jnp.sum(jnp.where(eye, lo_ref[:, cols], 0.0), axis=1, keepdims=True)
        hi = jnp.sum(jnp.where(eye, hi_ref[:, cols], 0.0), axis=1, keepdims=True)

        def chunk(g, acc, lo=lo, hi=hi):
            @pl.when(g + CMB_BUFS - 1 < total)
            def _():
                chunk_copy(g + CMB_BUFS - 1).start()

            chunk_copy(g).wait()
            nominal = nom_ref[g]
            rows = (col + jnp.minimum(nominal, zrows - CMB_RC)).astype(F32)
            lo_c = jnp.maximum(lo, nominal.astype(F32))
            s = jnp.where(rows >= lo_c, jnp.where(rows < hi, 1.0, 0.0), 0.0).astype(BF16)
            return acc + _mmb(s, _load_token_tiles(zbuf.at[g % CMB_BUFS], CMB_RC))

        blk = first_block + k
        sums.append(lax.fori_loop(first_ref[blk], first_ref[blk + 1], chunk, jnp.zeros((CMB_TB, D), F32)))
    return sums


def _comb_kernel(t1, n2, zrows, first_ref, nom_ref, x1_ref, lo_ref, hi_ref, mod_ref, fw_ref, z_hbm,
                 y1_ref, y2_ref, zbuf, sem):
    i = pl.program_id(0)
    r = _mod_row(i * CMB_TB, t1, n2)
    moe, = _pair_row_sums(i, 1, i == 0, first_ref, nom_ref, lo_ref, hi_ref, z_hbm, zbuf, sem, zrows)
    x2 = x1_ref[...] + mod_ref[pl.ds(r, 1), 5 * D:6 * D] * moe
    y = x2 * lax.rsqrt(jnp.mean(x2 * x2, axis=-1, keepdims=True) + EPS) * fw_ref[...]

    @pl.when(i * CMB_TB < t1)
    def _():
        y1_ref[...] = y

    @pl.when(i * CMB_TB >= t1)
    def _():
        y2_ref[...] = y


def _comb_in_kernel(t1, n2, zrows, tm, first_ref, nom_ref, x1_ref, lo_ref, hi_ref, modp_ref, z_hbm,
                    modn_ref, nw_ref, wa_ref, wb_ref, wg_ref, al_ref, dt_ref,
                    x_ref, p_ref, g_ref, gt_ref, zbuf, sem):
    i = pl.program_id(0)
    r = _mod_row(i * tm, t1, n2)
    nblocks = tm // CMB_TB
    sums = _pair_row_sums(i * nblocks, nblocks, i == 0, first_ref, nom_ref, lo_ref, hi_ref, z_hbm,
                          zbuf, sem, zrows)
    gate = modp_ref[pl.ds(r, 1), 5 * D:6 * D]
    xs = []
    for k in range(nblocks):
        rs = slice(k * CMB_TB, (k + 1) * CMB_TB)
        xs.append(x1_ref[rs, :] + gate * sums[k])
        x_ref[rs, :] = xs[k]
    _in_core(tm, r, jnp.concatenate(xs, axis=0), modn_ref, nw_ref, wa_ref, wb_ref, wg_ref, al_ref, dt_ref,
             p_ref, g_ref, gt_ref)


def _chunk_table(lo, hi, zrows):
    nblk = lo.shape[1] // CMB_TB
    base = (lo[0, ::CMB_TB].astype(I32) // 8) * 8
    end = hi[0, CMB_TB - 1::CMB_TB].astype(I32)
    nchunk = jnp.maximum((end - base + CMB_RC - 1) // CMB_RC, 1)
    first = jnp.concatenate([jnp.zeros((1,), I32), jnp.cumsum(nchunk)])
    g = jnp.arange(zrows // CMB_RC + 2 * nblk, dtype=I32)
    blk_of = jnp.minimum(jnp.sum(first[None, 1:] <= g[:, None], axis=1), nblk - 1)
    return first, base[blk_of] + (g - first[blk_of]) * CMB_RC


def _ring_scratch():
    return [pltpu.VMEM((CMB_BUFS, CMB_RC * ROW_TILES, 128), F32), pltpu.SemaphoreType.DMA((CMB_BUFS,))]


def _combine(lo, hi, x1, mod, fw, z, t1, n2):
    t_all = x1.shape[0]
    zrows = z.shape[0] // ROW_TILES
    first, nominal = _chunk_table(lo, hi, zrows)
    nb1 = t1 // CMB_TB
    return pl.pallas_call(
        functools.partial(_comb_kernel, t1, n2, zrows),
        grid_spec=pltpu.PrefetchScalarGridSpec(
            num_scalar_prefetch=2,
            grid=(t_all // CMB_TB,),
            in_specs=[pl.BlockSpec((CMB_TB, D), lambda i, *_: (i, 0)),
                      pl.BlockSpec((1, CMB_TB), lambda i, *_: (0, i)),
                      pl.BlockSpec((1, CMB_TB), lambda i, *_: (0, i)),
                      pl.BlockSpec((8, NMOD * D), lambda i, *_: (0, 0)),
                      pl.BlockSpec((1, D), lambda i, *_: (0, 0)),
                      pl.BlockSpec(memory_space=pl.ANY)],
            out_specs=[pl.BlockSpec((CMB_TB, D), lambda i, *_: (jnp.minimum(i, nb1 - 1), 0)),
                       pl.BlockSpec((CMB_TB, D), lambda i, *_: (jnp.maximum(i - nb1, 0), 0))],
            scratch_shapes=_ring_scratch()),
        out_shape=[jax.ShapeDtypeStruct((t1, D), F32), jax.ShapeDtypeStruct((t_all - t1, D), F32)],
        compiler_params=_params(("arbitrary",)),
        name="combine",
    )(first, nominal, x1, lo, hi, mod, fw, z)


def _combine_in_proj(lo, hi, x1, mod_prev, z, mod_next, nw, wa, wb, wg, al, dt, t1, n2, tm):
    t_all = x1.shape[0]
    zrows = z.shape[0] // ROW_TILES
    first, nominal = _chunk_table(lo, hi, zrows)
    const = lambda shape: pl.BlockSpec(shape, lambda i, *_: (0,) * len(shape))
    return pl.pallas_call(
        functools.partial(_comb_in_kernel, t1, n2, zrows, tm),
        grid_spec=pltpu.PrefetchScalarGridSpec(
            num_scalar_prefetch=2,
            grid=(t_all // tm,),
            in_specs=[pl.BlockSpec((tm, D), lambda i, *_: (i, 0)),
                      pl.BlockSpec((1, tm), lambda i, *_: (0, i)),
                      pl.BlockSpec((1, tm), lambda i, *_: (0, i)),
                      const((8, NMOD * D)),
                      pl.BlockSpec(memory_space=pl.ANY),
                      const((8, NMOD * D)), const((1, D)), const((D, P_A_COLS)),
                      const((D, P_COLS - P_A_COLS)), const((D, 128)), const((1, 128)), const((1, 128))],
            out_specs=[pl.BlockSpec((tm, D), lambda i, *_: (i, 0)),
                       pl.BlockSpec((tm, P_COLS), lambda i, *_: (i, 0)),
                       pl.BlockSpec((tm, 4 * H), lambda i, *_: (i, 0)),
                       pl.BlockSpec((tm // CH, 4 * H, CH), lambda i, *_: (i, 0, 0))],
            scratch_shapes=_ring_scratch()),
        out_shape=[jax.ShapeDtypeStruct((t_all, D), F32),
                   jax.ShapeDtypeStruct((t_all, P_COLS), F32),
                   jax.ShapeDtypeStruct((t_all, 4 * H), F32),
                   jax.ShapeDtypeStruct((t_all // CH, 4 * H, CH), F32)],
        compiler_params=_params(("arbitrary",)),
        name="combine_in_proj",
    )(first, nominal, x1, lo, hi, mod_prev, z, mod_next, nw, wa, wb, wg, al, dt)


def _grid_pos_embed(n, d):
    rows = n // GRID_W
    r = np.repeat(np.arange(rows, dtype=np.float64), GRID_W)
    col = np.tile(np.arange(GRID_W, dtype=np.float64), rows)
    quarter = d // 4
    omega = np.power(POS_BASE, -np.arange(quarter, dtype=np.float64) / quarter)
    ra = r[:, None] * omega
    ca = col[:, None] * omega
    return jnp.asarray(np.concatenate([np.sin(ra), np.cos(ra), np.sin(ca), np.cos(ca)], axis=-1), F32)


def _decode_slots(slots):
    e, nsb, rows, tb = slots.shape
    nsb -= SPILL_BLOCKS
    slots = slots[:, :nsb].transpose(0, 2, 1, 3).reshape(e, rows, nsb * tb)
    idx = (slots[:, 0] * 128.0 + slots[:, 1]).astype(I32)
    w = slots[:, 2] + slots[:, 3] + slots[:, 4]
    rank = (slots[:, 5] * 128.0 + slots[:, 6]).astype(I32)
    return idx, w, rank


def kernel(x_prompt, x_sample, state_delta, c, c_ctx, w_ada, b_ada, norm1_w, norm2_w, w_in, dn_conv_w,
           dn_a_log, dn_dt_bias, dn_norm_w, sc_conv_w, w_out, w_router, w_gate, w_up, w_down,
           final_norm_w):
    b1, n1, _ = x_prompt.shape
    b2, n2, _ = x_sample.shape
    depth = w_ada.shape[0]
    t1, t2 = b1 * n1, b2 * n2
    t_all = t1 + t2
    cap1 = max(1, 2 * t1 // E)
    cap2 = max(1, 2 * t2 // E)
    tm = 512
    rc = 512 if (cap1 % 512 == 0 and cap2 % 512 == 0) else 128
    assert t1 % n2 == 0 and t1 % tm == 0 and n2 % tm == 0
    assert n1 % (DN_POS * CH) == 0 and n2 % (DN_POS * CH) == 0
    assert t1 % SEL_TB == 0 and t2 % SEL_TB == 0 and t1 % t2 == 0
    assert cap1 % rc == 0 and cap2 % rc == 0 and 2 * t_all >= CMB_RC
    nch = (cap1 + cap2) // rc
    width = H * DH

    cond8 = jnp.zeros((8, D), F32).at[0].set(c_ctx).at[1:1 + b2].set(c)
    mod = _ada(cond8, w_ada, b_ada)

    tabs1 = _dft_tables(n1)
    tabs2 = _dft_tables(n2)
    ctx_states = None

    def in_weights(l):
        wl = w_in[l]
        wa = wl[:, 0:P_A_COLS].astype(BF16)
        wb = wl[:, P_A_COLS + 4 * H:].astype(BF16)
        wg = jnp.pad(wl[:, 4 * width:4 * width + 4 * H], ((0, 0), (0, 128 - 4 * H))).astype(BF16)
        al = jnp.pad(dn_a_log[l].reshape(1, 2 * H), ((0, 0), (2 * H, 128 - 4 * H)))
        dt = jnp.pad(dn_dt_bias[l].reshape(1, 2 * H), ((0, 0), (2 * H, 128 - 4 * H)))
        return mod[l], norm1_w[l].reshape(1, D), wa, wb, wg, al, dt

    xs = (x_prompt.reshape(t1, D), x_sample.reshape(t2, D), _grid_pos_embed(n2, D))
    x, p, g, gt = _in_proj(xs, *in_weights(0), t1, n2, tm)
    for l in range(depth):
        nwd = dn_norm_w[l].reshape(1, DH)
        dn1, ctx_states = _deltanet(p, g, gt, dn_conv_w[l], nwd, None, ctx_states, b1, n1, 0)
        dn2, _ = _deltanet(p, g, gt, dn_conv_w[l], nwd, state_delta[:, l], None, b2, n2, t1 // n2)
        mx1 = _mix2(p, sc_conv_w[l], tabs1[1], tabs1[2], tabs1[0], b1, n1, 0, MIX_GROUP)
        mx2 = _mix2(p, sc_conv_w[l], tabs2[1], tabs2[2], tabs2[0], b2, n2, t1 // n2, 1)

        wr = jnp.pad(w_router[l], ((0, 0), (0, 128 - E)))
        x1, h2, aff_t = _out_proj(dn1, dn2, mx1, mx2, x, mod[l], norm2_w[l].reshape(1, D),
                                  w_out[l].astype(BF16), wr, t1, n2, tm)

        slots1, lo1, hi1 = _select(aff_t, t1, cap1, 0)
        slots2, lo2, hi2 = _select(aff_t, t2, cap2, t1 // t2)
        idx1, wsel1, rank1 = _decode_slots(slots1)
        idx2, wsel2, rank2 = _decode_slots(slots2)
        idx = jnp.concatenate([idx1, idx2 + t1], axis=1).reshape(-1) * ROW_TILES
        dst = jnp.concatenate([rank1, rank2 + 2 * t1], axis=1).reshape(-1) * ROW_TILES
        wcol = jnp.concatenate([wsel1, wsel2], axis=1).reshape(-1, 1)
        z = _expert_ffn(idx, dst, h2, wcol, w_gate, w_up, w_down, l, rc, nch)

        lo = jnp.concatenate([lo1, lo2 + 2.0 * t1], axis=1)
        hi = jnp.concatenate([hi1, hi2 + 2.0 * t1], axis=1)
        if l + 1 < depth:
            x, p, g, gt = _combine_in_proj(lo, hi, x1, mod[l], z, *in_weights(l + 1), t1, n2, tm)
        else:
            x = _combine(lo, hi, x1, mod[l], final_norm_w.reshape(1, D), z, t1, n2)

    y_prompt, y_sample = x
    return y_prompt.reshape(b1, n1, D), y_sample.reshape(b2, n2, D), ctx_states
```
